```python
import math
import jax, jax.numpy as jnp
from jax import lax
import numpy as np

D_MODEL = 1024
BATCH = 16
SEQ = 4096
DEPTH = 2
DEC_BATCH = 16
DEC_SEQ = 16
PAST_LEN = 1024

CHUNK = 64
QBLOCK = 128
N_GROUPS = 4
GROUP_W = D_MODEL // N_GROUPS
MIX_W = N_GROUPS * GROUP_W
EPS = 1e-6
MLA_H = 4
MLA_DN = 64
MLA_DR = 32
MLA_DV = GROUP_W // MLA_H
MLA_R = 128
ROPE_THETA = 10000.0
DIFF_H = 4
DIFF_DH = GROUP_W // (2 * DIFF_H)
SGU_H = 4
SGU_CHUNK = 128
SGU_DC = GROUP_W // SGU_H
GDN_H = 4
GDN_DK = GROUP_W // GDN_H
GDN_DV = GROUP_W // GDN_H
GDN_CONV = 4
GDN_CHUNK = 64
D_FF = 2816
FFN_CONV = 3
T5_BUCKETS = 32
T5_MAX_DIST = 128
IN_COLS = (MLA_H * (MLA_DN + MLA_DR), MLA_R, MLA_DR,
           2 * DIFF_H * DIFF_DH, 2 * DIFF_H * DIFF_DH, 2 * DIFF_H * DIFF_DH,
           GROUP_W, GROUP_W,
           3 * GROUP_W, GROUP_W, GDN_H, GDN_H)
D_IN = sum(IN_COLS)

kernel_name = 'hybrid_streaming_encoder_step'


def _rmsnorm(x, g, eps=EPS):
    xf = x.astype(jnp.float32)
    y = xf * lax.rsqrt(jnp.mean(xf * xf, axis=-1, keepdims=True) + eps)
    return (y * g.astype(jnp.float32)).astype(x.dtype)


def _l2norm(x):
    xf = x.astype(jnp.float32)
    return (xf * lax.rsqrt(jnp.sum(xf * xf, axis=-1, keepdims=True) + 1e-6)).astype(x.dtype)


def _rope(x, pos):
    d = x.shape[-1]
    inv = ROPE_THETA ** (-jnp.arange(0, d, 2, dtype=jnp.float32) / d)
    ang = pos.astype(jnp.float32)[:, None] * inv[None, :]
    ang = ang.reshape(ang.shape[0], *([1] * (x.ndim - 3)), d // 2)
    cos, sin = jnp.cos(ang), jnp.sin(ang)
    xf = x.astype(jnp.float32)
    x1, x2 = xf[..., : d // 2], xf[..., d // 2:]
    return jnp.concatenate([x1 * cos - x2 * sin, x2 * cos + x1 * sin], axis=-1).astype(x.dtype)


def _chunk_mask(q_pos, k_pos):
    return (k_pos // CHUNK)[None, :] <= (q_pos // CHUNK)[:, None]


def _masked_softmax(s, mask):
    return jax.nn.softmax(jnp.where(mask, s, -1e30), axis=-1)


def _t5_bucket(rel):
    nb = T5_BUCKETS // 2
    max_exact = nb // 2
    ret = jnp.where(rel > 0, nb, 0)
    n = jnp.abs(rel)
    nf = jnp.maximum(n, 1).astype(jnp.float32)
    large = max_exact + (jnp.log(nf / max_exact) / math.log(T5_MAX_DIST / max_exact)
                         * (nb - max_exact)).astype(jnp.int32)
    large = jnp.minimum(large, nb - 1)
    return ret + jnp.where(n < max_exact, n, large)


def _sweep(fn, q, q_pos):
    B, S = q.shape[:2]
    nb = S // QBLOCK
    qb = jnp.moveaxis(q.reshape(B, nb, QBLOCK, *q.shape[2:]), 1, 0)
    pb = q_pos.reshape(nb, QBLOCK)
    out = lax.map(lambda a: fn(a[0], a[1]), (qb, pb))
    return jnp.moveaxis(out, 0, 1).reshape(B, S, *out.shape[3:])


def _causal_dwconv(x, buf, w):
    width, C = w.shape
    xp = jnp.concatenate([buf.astype(x.dtype), x], axis=1)
    y = lax.conv_general_dilated(xp, w[:, None, :].astype(x.dtype), window_strides=(1,), padding='VALID',
                                 dimension_numbers=('NWC', 'WIO', 'NWC'), feature_group_count=C)
    return y, xp[:, xp.shape[1] - (width - 1):]


def _mla(xq, xlat, xkr, pos, past_lat, past_kr, lat_g, w_uk, w_uv):
    B, L, _ = xq.shape
    q = xq.reshape(B, L, MLA_H, MLA_DN + MLA_DR)
    q = jnp.concatenate([q[..., :MLA_DN], _rope(q[..., MLA_DN:], pos)], axis=-1)
    lat = _rmsnorm(xlat, lat_g)
    kr = _rope(xkr, pos)
    if past_lat is None:
        lat_all, kr_all, kpos = lat, kr, pos
    else:
        lat_all = jnp.concatenate([past_lat.astype(lat.dtype), lat], axis=1)
        kr_all = jnp.concatenate([past_kr.astype(kr.dtype), kr], axis=1)
        kpos = jnp.arange(lat_all.shape[1], dtype=jnp.int32)
    Sk = lat_all.shape[1]
    k_nope = jnp.einsum('bsr,rhd->bshd', lat_all, w_uk)
    v = jnp.einsum('bsr,rhd->bshd', lat_all, w_uv)
    k = jnp.concatenate([k_nope, jnp.broadcast_to(kr_all[:, :, None, :], (B, Sk, MLA_H, MLA_DR))], axis=-1)
    scale = (MLA_DN + MLA_DR) ** -0.5

    def core(qb, qp):
        s = jnp.einsum('bqhd,bkhd->bhqk', qb, k).astype(jnp.float32) * scale
        p = _masked_softmax(s, _chunk_mask(qp, kpos))
        return jnp.einsum('bhqk,bkhd->bqhd', p.astype(v.dtype), v)

    o = _sweep(core, q, pos) if past_lat is None else core(q, pos)
    return o.reshape(B, L, MLA_H * MLA_DV), lat, kr


def _diff(xq, xk, xv, pos, past_k, past_v, t5_table, lq1, lk1, lq2, lk2, sub_g, lam_init):
    B, L, _ = xq.shape
    q = xq.reshape(B, L, DIFF_H, 2, DIFF_DH)
    k = xk.reshape(B, L, DIFF_H, 2, DIFF_DH)
    v = xv.reshape(B, L, DIFF_H, 2 * DIFF_DH)
    if past_k is None:
        k_all, v_all, kpos = k, v, pos
    else:
        k_all = jnp.concatenate([past_k.astype(k.dtype), k], axis=1)
        v_all = jnp.concatenate([past_v.astype(v.dtype), v], axis=1)
        kpos = jnp.arange(k_all.shape[1], dtype=jnp.int32)
    f32 = jnp.float32
    lam = (jnp.exp(jnp.sum(lq1.astype(f32) * lk1.astype(f32)))
           - jnp.exp(jnp.sum(lq2.astype(f32) * lk2.astype(f32))) + lam_init)
    scale = DIFF_DH ** -0.5

    def core(qb, qp):
        s = jnp.einsum('bqhcd,bkhcd->bchqk', qb, k_all).astype(f32) * scale
        bias = jnp.take(t5_table, _t5_bucket(kpos[None, :] - qp[:, None]), axis=0)
        s = s + jnp.transpose(bias, (2, 0, 1)).astype(f32)
        p = _masked_softmax(s, _chunk_mask(qp, kpos))
        w = p[:, 0] - lam * p[:, 1]
        return jnp.einsum('bhqk,bkhd->bqhd', w.astype(v_all.dtype), v_all)

    o = _sweep(core, q, pos) if past_k is None else core(q, pos)
    o = _rmsnorm(o, sub_g, eps=1e-5) * (1.0 - lam_init)
    return o.reshape(B, L, DIFF_H * 2 * DIFF_DH), k, v


def _sgu(xu, xv, w_s, b_s, ln_g, ln_b):
    B, L, _ = xu.shape
    c = min(SGU_CHUNK, L)
    n = L // c
    vf = xv.astype(jnp.float32)
    mu = jnp.mean(vf, axis=-1, keepdims=True)
    var = jnp.mean(jnp.square(vf - mu), axis=-1, keepdims=True)
    vn = ((vf - mu) * lax.rsqrt(var + EPS) * ln_g.astype(jnp.float32)
          + ln_b.astype(jnp.float32)).astype(xv.dtype)
    w = jnp.tril(w_s[:, :c, :c])
    vb = vn.reshape(B, n, c, SGU_H, SGU_DC)
    mix = jnp.einsum('hts,bnshe->bnthe', w, vb) + b_s[:, :c].T[None, None, :, :, None]
    return xu * mix.reshape(B, L, GROUP_W), vn


def _gated_delta(q, k, v, g, beta, S0):
    f32 = jnp.float32
    B, L, H, DK = q.shape
    DV = v.shape[-1]
    C = min(GDN_CHUNK, L)
    N = L // C
    blk = lambda t: t.astype(f32).reshape(B, N, C, H, -1).transpose(1, 0, 3, 2, 4)
    blk_s = lambda t: t.astype(f32).reshape(B, N, C, H).transpose(1, 0, 3, 2)
    q, k, v, g, beta = blk(q), blk(k), blk(v), blk_s(g), blk_s(beta)
    decay = jnp.cumsum(g, axis=-1)
    tri = jnp.tril(jnp.ones((C, C), dtype=bool))
    strict = jnp.tril(jnp.ones((C, C), dtype=bool), -1)
    lmask = jnp.exp(jnp.where(tri, decay[..., :, None] - decay[..., None, :], -jnp.inf))
    kb = k * beta[..., None]
    A = jnp.where(strict, jnp.einsum('nbhid,nbhjd->nbhij', kb, k) * lmask, 0.0)
    rhs = jnp.concatenate([v * beta[..., None], kb * jnp.exp(decay)[..., None]], axis=-1)
    sol = lax.linalg.triangular_solve(jnp.eye(C, dtype=f32) + A, rhs, left_side=True, lower=True,
                                      unit_diagonal=True)
    u, w = sol[..., :DV], sol[..., DV:]
    qk = jnp.einsum('nbhid,nbhjd->nbhij', q, k) * lmask
    q_dec = q * jnp.exp(decay)[..., None]
    k_tail = k * jnp.exp(decay[..., -1:] - decay)[..., None]
    g_last = jnp.exp(decay[..., -1])

    def step(S, xs):
        u_c, w_c, qk_c, qd_c, kt_c, gl_c = xs
        v_new = u_c - jnp.einsum('bhcd,bhde->bhce', w_c, S)
        o = jnp.einsum('bhcd,bhde->bhce', qd_c, S) + jnp.einsum('bhij,bhje->bhie', qk_c, v_new)
        S = S * gl_c[..., None, None] + jnp.einsum('bhcd,bhce->bhde', kt_c, v_new)
        return S, o

    S, o = lax.scan(step, S0.astype(f32), (u, w, qk, q_dec, k_tail, g_last))
    o = o.transpose(1, 0, 3, 2, 4).reshape(B, L, H, DV)
    return o, S.astype(S0.dtype)


def _gdn(xqkv, xg, xa, xb, conv_buf, S0, conv_w, a_log, dt_bias, norm_g):
    B, L, _ = xqkv.shape
    f32 = jnp.float32
    y, conv_new = _causal_dwconv(xqkv, conv_buf, conv_w)
    y = jax.nn.silu(y)
    q, k, v = jnp.split(y, 3, axis=-1)
    q = _l2norm(q.reshape(B, L, GDN_H, GDN_DK)) * (GDN_DK ** -0.5)
    k = _l2norm(k.reshape(B, L, GDN_H, GDN_DK))
    v = v.reshape(B, L, GDN_H, GDN_DV)
    beta = jax.nn.sigmoid(xb.astype(f32))
    g = -jnp.exp(a_log.astype(f32)) * jax.nn.softplus(xa.astype(f32) + dt_bias.astype(f32))
    o, S = _gated_delta(q, k, v, g, beta, S0)
    o = _rmsnorm(o.astype(xqkv.dtype), norm_g) * jax.nn.silu(xg.reshape(B, L, GDN_H, GDN_DV))
    return o.reshape(B, L, GROUP_W), conv_new, S


def _convffn(h, buf, w_up, conv_w, w_down):
    a = jnp.einsum('bld,df->blf', h, w_up)
    a, buf_new = _causal_dwconv(a, buf, conv_w)
    gate, up = jnp.split(a, 2, axis=-1)
    return jnp.einsum('blf,fd->bld', jax.nn.silu(gate) * up, w_down), buf_new


def _layer(x, pos, past, lp, t5_table, l):
    B, L, _ = x.shape
    if past is None:
        past = dict(lat=None, kr=None, dk=None, dv=None,
                    gconv=jnp.zeros((B, GDN_CONV - 1, 3 * GROUP_W), x.dtype),
                    gs=jnp.zeros((B, GDN_H, GDN_DK, GDN_DV), x.dtype),
                    fconv=jnp.zeros((B, FFN_CONV - 1, 2 * D_FF), x.dtype))
    h = _rmsnorm(x, lp['norm1_g'])
    z = jnp.einsum('bld,dc->blc', h, lp['w_in'])
    pts = [int(c) for c in np.cumsum(IN_COLS)[:-1]]
    mq, mlat, mkr, dq, dk, dv, su, sv, gqkv, ggate, ga, gb = jnp.split(z, pts, axis=-1)
    o_mla, new_lat, new_kr = _mla(mq, mlat, mkr, pos, past['lat'], past['kr'],
                                  lp['mla_lat_g'], lp['mla_w_uk'], lp['mla_w_uv'])
    lam_init = 0.8 - 0.6 * math.exp(-0.3 * l)
    o_diff, new_dk, new_dv = _diff(dq, dk, dv, pos, past['dk'], past['dv'], t5_table,
                                   lp['diff_lam_q1'], lp['diff_lam_k1'], lp['diff_lam_q2'],
                                   lp['diff_lam_k2'], lp['diff_sub_g'], lam_init)
    o_sgu, new_sv = _sgu(su, sv, lp['sgu_w'], lp['sgu_b'], lp['sgu_ln_g'], lp['sgu_ln_b'])
    o_gdn, new_gconv, new_gs = _gdn(gqkv, ggate, ga, gb, past['gconv'], past['gs'], lp['gdn_conv_w'],
                                    lp['gdn_a_log'], lp['gdn_dt_bias'], lp['gdn_norm_g'])
    o = jnp.concatenate([o_mla, o_diff, o_sgu, o_gdn], axis=-1)
    x = x + jnp.einsum('blc,cd->bld', o, lp['w_out'])
    f, new_fconv = _convffn(_rmsnorm(x, lp['norm2_g']), past['fconv'],
                            lp['ffn_w_up'], lp['ffn_conv_w'], lp['ffn_w_down'])
    x = x + f
    return x, dict(lat=new_lat, kr=new_kr, dk=new_dk, dv=new_dv, sv=new_sv,
                   gconv=new_gconv, gs=new_gs, fconv=new_fconv)


def setup_inputs(seed: int = 0) -> dict:
    key = jax.random.key(seed)
    ks = iter(jax.random.split(key, 48))
    f32 = jnp.float32

    def nrm(shape, scale=1.0):
        return scale * jax.random.normal(next(ks), shape, f32)

    def gain(shape):
        return 1.0 + 0.05 * nrm(shape)

    dt = jnp.exp(jax.random.uniform(next(ks), (DEPTH, GDN_H), f32, math.log(1e-3), math.log(1e-1)))
    a_log = jnp.log(jax.random.uniform(next(ks), (DEPTH, GDN_H), f32, 1.0, 16.0))
    return {
        'x_prompt': nrm((BATCH, SEQ, D_MODEL)),
        'x_sample': nrm((DEC_BATCH, DEC_SEQ, D_MODEL)),
        'cache_mla_latent': nrm((DEPTH, DEC_BATCH, PAST_LEN, MLA_R)),
        'cache_mla_krope': nrm((DEPTH, DEC_BATCH, PAST_LEN, MLA_DR)),
        'cache_diff_k': nrm((DEPTH, DEC_BATCH, PAST_LEN, DIFF_H, 2, DIFF_DH)),
        'cache_diff_v': nrm((DEPTH, DEC_BATCH, PAST_LEN, DIFF_H, 2 * DIFF_DH)),
        'state_gdn_conv': nrm((DEPTH, DEC_BATCH, GDN_CONV - 1, 3 * GROUP_W)),
        'state_gdn_s': nrm((DEPTH, DEC_BATCH, GDN_H, GDN_DK, GDN_DV), 0.1),
        'state_ffn_conv': nrm((DEPTH, DEC_BATCH, FFN_CONV - 1, 2 * D_FF)),
        't5_table': nrm((T5_BUCKETS, DIFF_H), 0.5),
        'final_g': gain((D_MODEL,)),
        'norm1_g': gain((DEPTH, D_MODEL)),
        'w_in': nrm((DEPTH, D_MODEL, D_IN), D_MODEL ** -0.5),
        'mla_lat_g': gain((DEPTH, MLA_R)),
        'mla_w_uk': nrm((DEPTH, MLA_R, MLA_H, MLA_DN), MLA_R ** -0.5),
        'mla_w_uv': nrm((DEPTH, MLA_R, MLA_H, MLA_DV), MLA_R ** -0.5),
        'diff_lam_q1': nrm((DEPTH, DIFF_DH), 0.1),
        'diff_lam_k1': nrm((DEPTH, DIFF_DH), 0.1),
        'diff_lam_q2': nrm((DEPTH, DIFF_DH), 0.1),
        'diff_lam_k2': nrm((DEPTH, DIFF_DH), 0.1),
        'diff_sub_g': gain((DEPTH, 2 * DIFF_DH)),
        'sgu_ln_g': gain((DEPTH, GROUP_W)),
        'sgu_ln_b': nrm((DEPTH, GROUP_W), 0.02),
        'sgu_w': nrm((DEPTH, SGU_H, SGU_CHUNK, SGU_CHUNK), SGU_CHUNK ** -0.5),
        'sgu_b': 1.0 + nrm((DEPTH, SGU_H, SGU_CHUNK), 0.1),
        'gdn_conv_w': nrm((DEPTH, GDN_CONV, 3 * GROUP_W), GDN_CONV ** -0.5),
        'gdn_a_log': a_log,
        'gdn_dt_bias': dt + jnp.log(-jnp.expm1(-dt)),
        'gdn_norm_g': gain((DEPTH, GDN_DV)),
        'w_out': nrm((DEPTH, MIX_W, D_MODEL), MIX_W ** -0.5),
        'norm2_g': gain((DEPTH, D_MODEL)),
        'ffn_w_up': nrm((DEPTH, D_MODEL, 2 * D_FF), D_MODEL ** -0.5),
        'ffn_conv_w': nrm((DEPTH, FFN_CONV, 2 * D_FF), FFN_CONV ** -0.5),
        'ffn_w_down': nrm((DEPTH, D_FF, D_MODEL), D_FF ** -0.5),
    }


def reference(x_prompt, x_sample, cache_mla_latent, cache_mla_krope, cache_diff_k, cache_diff_v,
              state_gdn_conv, state_gdn_s, state_ffn_conv, t5_table, final_g,
              norm1_g, w_in, mla_lat_g, mla_w_uk, mla_w_uv,
              diff_lam_q1, diff_lam_k1, diff_lam_q2, diff_lam_k2, diff_sub_g,
              sgu_ln_g, sgu_ln_b, sgu_w, sgu_b,
              gdn_conv_w, gdn_a_log, gdn_dt_bias, gdn_norm_g,
              w_out, norm2_g, ffn_w_up, ffn_conv_w, ffn_w_down):
    past_len = cache_mla_latent.shape[2]
    pos_p = jnp.arange(x_prompt.shape[1], dtype=jnp.int32)
    pos_s = past_len + jnp.arange(x_sample.shape[1], dtype=jnp.int32)
    xp, xs = x_prompt, x_sample
    sp_list, ss_list = [], []
    for l in range(DEPTH):
        lp = dict(norm1_g=norm1_g[l], w_in=w_in[l], mla_lat_g=mla_lat_g[l], mla_w_uk=mla_w_uk[l],
                  mla_w_uv=mla_w_uv[l], diff_lam_q1=diff_lam_q1[l], diff_lam_k1=diff_lam_k1[l],
                  diff_lam_q2=diff_lam_q2[l], diff_lam_k2=diff_lam_k2[l], diff_sub_g=diff_sub_g[l],
                  sgu_ln_g=sgu_ln_g[l], sgu_ln_b=sgu_ln_b[l], sgu_w=sgu_w[l], sgu_b=sgu_b[l],
                  gdn_conv_w=gdn_conv_w[l], gdn_a_log=gdn_a_log[l], gdn_dt_bias=gdn_dt_bias[l],
                  gdn_norm_g=gdn_norm_g[l], w_out=w_out[l], norm2_g=norm2_g[l],
                  ffn_w_up=ffn_w_up[l], ffn_conv_w=ffn_conv_w[l], ffn_w_down=ffn_w_down[l])
        xp, sp = _layer(xp, pos_p, None, lp, t5_table, l)
        past = dict(lat=cache_mla_latent[l], kr=cache_mla_krope[l], dk=cache_diff_k[l], dv=cache_diff_v[l],
                    gconv=state_gdn_conv[l], gs=state_gdn_s[l], fconv=state_ffn_conv[l])
        xs, ss = _layer(xs, pos_s, past, lp, t5_table, l)
        sp_list.append(sp)
        ss_list.append(ss)
    y_prompt = _rmsnorm(xp, final_g)
    y_sample = _rmsnorm(xs, final_g)
    new_mla_latent_p = jnp.stack([d['lat'] for d in sp_list])
    new_mla_krope_p = jnp.stack([d['kr'] for d in sp_list])
    new_diff_k_p = jnp.stack([d['dk'] for d in sp_list])
    new_diff_v_p = jnp.stack([d['dv'] for d in sp_list])
    new_gdn_conv_p = jnp.stack([d['gconv'] for d in sp_list])
    new_gdn_s_p = jnp.stack([d['gs'] for d in sp_list])
    new_ffn_conv_p = jnp.stack([d['fconv'] for d in sp_list])
    new_mla_latent_s = jnp.stack([d['lat'] for d in ss_list])
    new_mla_krope_s = jnp.stack([d['kr'] for d in ss_list])
    new_diff_k_s = jnp.stack([d['dk'] for d in ss_list])
    new_diff_v_s = jnp.stack([d['dv'] for d in ss_list])
    new_gdn_conv_s = jnp.stack([d['gconv'] for d in ss_list])
    new_gdn_s_s = jnp.stack([d['gs'] for d in ss_list])
    new_ffn_conv_s = jnp.stack([d['fconv'] for d in ss_list])
    new_sgu_v_s = jnp.stack([d['sv'] for d in ss_list])
    return (y_prompt, y_sample,
            new_mla_latent_p, new_mla_krope_p, new_diff_k_p, new_diff_v_p,
            new_gdn_conv_p, new_gdn_s_p, new_ffn_conv_p,
            new_mla_latent_s, new_mla_krope_s, new_diff_k_s, new_diff_v_s,
            new_gdn_conv_s, new_gdn_s_s, new_ffn_conv_s, new_sgu_v_s)
```

```python
import functools
import math

import numpy as np
import jax
import jax.numpy as jnp
from jax import lax
from jax.experimental import pallas as pl
from jax.experimental.pallas import tpu as pltpu

F32 = jnp.float32
BF16 = jnp.bfloat16

CHUNK = 64
EPS = 1e-6
N_HEADS = 4
HEAD_W = 64
GROUP_W = 256
MLA_DN, MLA_DR, MLA_R = 64, 32, 128
DIFF_DH = 32
ROPE_THETA = 10000.0
SGU_CHUNK = 128
GDN_CHUNK = 64
GDN_CONV = 4
FFN_CONV = 3
T5_BUCKETS = 32
T5_MAX_DIST = 128
NEG = -1e30
SOLVE_BLOCK = 16

VMEM_LIMIT_BYTES = 56 * 1024 * 1024

C_QN, C_QR, C_QRR, C_LAT, C_KR, C_KRR = 0, 256, 384, 512, 640, 768
C_DQ, C_DK, C_DV, C_SU, C_SV = 896, 1152, 1408, 1664, 1920
C_GQKV, C_GG, C_GA, C_GB = 2176, 2944, 3200, 3456
C_TOTAL = 3712


def _cparams(sem):
    return pltpu.CompilerParams(dimension_semantics=sem, vmem_limit_bytes=VMEM_LIMIT_BYTES)


def _const_spec(shape):
    nd = len(shape)
    return pl.BlockSpec(shape, lambda *_: (0,) * nd, pipeline_mode=pl.Buffered(1))


def _dot(a, b):
    return jnp.dot(a, b, preferred_element_type=F32)


def _dot_nt(a, b):
    return lax.dot_general(a, b, (((1,), (1,)), ((), ())), preferred_element_type=F32)


def _dot_tn(a, b):
    return lax.dot_general(a, b, (((0,), (0,)), ((), ())), preferred_element_type=F32)


def _lane_group(shape, group):
    return lax.broadcasted_iota(jnp.int32, shape, len(shape) - 1) // group


def _silu(x):
    return x * (1.0 / (1.0 + jnp.exp(-x)))


def _shift_rows(x, prev8, s):
    if s == 0:
        return x
    r = pltpu.roll(x, s, axis=0)
    rp = pltpu.roll(prev8, s, axis=0)
    row8 = lax.broadcasted_iota(jnp.int32, rp.shape, 0)
    top = jnp.where(row8 < s, rp, r[0:8])
    if x.shape[0] == 8:
        return top
    return jnp.concatenate([top, r[8:]], axis=0)


def _inproj_kernel(x_ref, g1_ref, w_ref, cos_ref, sin_ref, wukt_ref, latg_ref,
                   sguw_ref, sgub_ref, lng_ref, lnb_ref,
                   qm_ref, km_ref, vm_ref, lat_ref, kr_ref,
                   dq_ref, dk_ref, dv_ref, kd_ref, vd_ref,
                   osgu_ref, vn_ref, gqkv_ref, ggate_ref, gab_ref, *, sgu_c):
    tm = x_ref.shape[0]
    x = x_ref[...]
    h = (x * lax.rsqrt(jnp.mean(x * x, axis=-1, keepdims=True) + EPS) * g1_ref[...]).astype(BF16)

    def sec(off, n):
        return _dot(h, w_ref[:, off:off + n])

    cos = cos_ref[...]
    sin = sin_ref[...]

    mla_scale = (MLA_DN + MLA_DR) ** -0.5
    zq = sec(C_QN, 512)
    qr = (zq[:, 256:384] * cos + zq[:, 384:512] * sin) * mla_scale
    head_of_lane = _lane_group((1, 128), MLA_DR)
    for hh in range(N_HEADS):
        qn = (zq[:, MLA_DN * hh:MLA_DN * (hh + 1)] * mla_scale).astype(BF16)
        qm_ref[hh, :, 0:128] = _dot(qn, wukt_ref[hh]).astype(BF16)
        qm_ref[hh, :, 128:256] = jnp.where(head_of_lane == hh, qr, 0.0).astype(BF16)

    zl = sec(C_LAT, 384)
    zlat = zl[:, 0:128]
    lat = zlat * lax.rsqrt(jnp.mean(zlat * zlat, axis=-1, keepdims=True) + EPS) * latg_ref[...]
    kr4 = zl[:, 128:256] * cos + zl[:, 256:384] * sin
    lat_ref[...] = lat
    kr_ref[...] = kr4[:, 0:MLA_DR]
    lat_b = lat.astype(BF16)
    km_ref[:, 0:128] = lat_b
    km_ref[:, 128:256] = kr4.astype(BF16)
    vm_ref[:, 0:128] = lat_b
    vm_ref[:, 128:256] = jnp.ones((tm, 128), BF16)

    zd = sec(C_DQ, 768)
    dq_ref[...] = (zd[:, 0:256] * (DIFF_DH ** -0.5)).astype(BF16)
    dk = zd[:, 256:512]
    dv = zd[:, 512:768]
    dk_ref[...] = dk
    dv_ref[...] = dv
    kd_ref[...] = dk.astype(BF16)
    ones64 = jnp.ones((tm, HEAD_W), F32)
    for hh in range(N_HEADS):
        vd_ref[hh] = jnp.concatenate([dv[:, HEAD_W * hh:HEAD_W * (hh + 1)], ones64], axis=1).astype(BF16)

    zs = sec(C_SU, 512)
    su = zs[:, 0:256]
    sv = zs[:, 256:512]
    mu = jnp.mean(sv, axis=-1, keepdims=True)
    svc = sv - mu
    var = jnp.mean(svc * svc, axis=-1, keepdims=True)
    vn = svc * lax.rsqrt(var + EPS) * lng_ref[...] + lnb_ref[...]
    vn_ref[...] = vn
    c = sgu_c
    wr = lax.broadcasted_iota(jnp.int32, (N_HEADS * c, c), 0) % c
    wc = lax.broadcasted_iota(jnp.int32, (N_HEADS * c, c), 1)
    w4 = jnp.where(wc <= wr, sguw_ref[...], 0.0).astype(BF16)
    hl = _lane_group((1, GROUP_W), HEAD_W)
    sgub = sgub_ref[...]
    for ci in range(tm // c):
        rows = slice(ci * c, (ci + 1) * c)
        m4 = _dot(w4, vn[rows].astype(BF16))
        mix = jnp.where(hl == 0, m4[0:c], 0.0)
        for hh in range(1, N_HEADS):
            mix = mix + jnp.where(hl == hh, m4[hh * c:(hh + 1) * c], 0.0)
        osgu_ref[rows, :] = (su[rows] * (mix + sgub)).astype(BF16)

    zg = sec(C_GQKV, 1536)
    gqkv_ref[...] = zg[:, 0:768]
    ggate_ref[...] = zg[:, 768:1024]
    gab_ref[...] = zg[:, 1024:1536]


def _inproj(x, g1, w, cos, sin, wukt, latg, sguw, sgub, lng, lnb, *, tm, sgu_c):
    t, d = x.shape
    n = t // tm
    npos = cos.shape[0] // tm
    row = lambda wd: pl.BlockSpec((tm, wd), lambda i: (i, 0))
    out_shape = [
        jax.ShapeDtypeStruct((N_HEADS, t, 256), BF16),
        jax.ShapeDtypeStruct((t, 256), BF16),
        jax.ShapeDtypeStruct((t, 256), BF16),
        jax.ShapeDtypeStruct((t, MLA_R), F32),
        jax.ShapeDtypeStruct((t, MLA_DR), F32),
        jax.ShapeDtypeStruct((t, 256), BF16),
        jax.ShapeDtypeStruct((t, 256), F32),
        jax.ShapeDtypeStruct((t, 256), F32),
        jax.ShapeDtypeStruct((t, 256), BF16),
        jax.ShapeDtypeStruct((N_HEADS, t, 128), BF16),
        jax.ShapeDtypeStruct((t, 256), BF16),
        jax.ShapeDtypeStruct((t, 256), F32),
        jax.ShapeDtypeStruct((t, 768), F32),
        jax.ShapeDtypeStruct((t, 256), F32),
        jax.ShapeDtypeStruct((t, 512), F32),
    ]
    out_specs = [
        pl.BlockSpec((N_HEADS, tm, 256), lambda i: (0, i, 0)),
        row(256), row(256), row(MLA_R), row(MLA_DR),
        row(256), row(256), row(256), row(256),
        pl.BlockSpec((N_HEADS, tm, 128), lambda i: (0, i, 0)),
        row(256), row(256), row(768), row(256), row(512),
    ]
    in_specs = [
        row(d), _const_spec(g1.shape), _const_spec(w.shape),
        pl.BlockSpec((tm, 128), lambda i: (i % npos, 0)),
        pl.BlockSpec((tm, 128), lambda i: (i % npos, 0)),
        _const_spec(wukt.shape), _const_spec(latg.shape),
        _const_spec(sguw.shape), _const_spec(sgub.shape),
        _const_spec(lng.shape), _const_spec(lnb.shape),
    ]
    return pl.pallas_call(
        functools.partial(_inproj_kernel, sgu_c=sgu_c),
        grid=(n,), in_specs=in_specs, out_specs=out_specs, out_shape=out_shape,
        compiler_params=_cparams(("parallel",)), name="inproj",
    )(x, g1, w, cos, sin, wukt, latg, sguw, sgub, lng, lnb)


def _t5_thresholds():
    nb = T5_BUCKETS // 2
    max_exact = nb // 2
    ratio = T5_MAX_DIST // max_exact
    thr = []
    for j in range(1, nb - max_exact):
        n = max_exact
        while (n ** (nb - max_exact)) * 1 < (ratio ** j) * (max_exact ** (nb - max_exact)):
            n += 1
        thr.append(n)
    return nb, max_exact, thr


def _bias_kernel(t_ref, o_ref, *, q0s, k0s):
    nb, max_exact, thr = _t5_thresholds()
    tq, tk = o_ref.shape[2], o_ref.shape[3]
    row = lax.broadcasted_iota(jnp.int32, (tq, tk), 0)
    col = lax.broadcasted_iota(jnp.int32, (tq, tk), 1)
    for di, (q0, k0) in enumerate(zip(q0s, k0s)):
        qpos = row + q0
        kpos = col + k0
        rel = kpos - qpos
        n = jnp.abs(rel)
        visible = (kpos // CHUNK) <= (qpos // CHUNK)
        for hh in range(N_HEADS):
            def side(base):
                val = jnp.full((tq, tk), t_ref[base + nb - 1, hh], F32)
                for j in range(len(thr) - 1, -1, -1):
                    val = jnp.where(n < thr[j], t_ref[base + max_exact + j, hh], val)
                for e in range(max_exact - 1, -1, -1):
                    val = jnp.where(n == e, t_ref[base + e, hh], val)
                return val
            b = jnp.where(rel > 0, side(nb), side(0)) - t_ref[nb - 1, hh]
            o_ref[di, hh] = jnp.where(visible, b, NEG)


def _t5_bias(t5_table, tq, tk, q0s, k0s):
    nd = len(q0s)
    return pl.pallas_call(
        functools.partial(_bias_kernel, q0s=tuple(q0s), k0s=tuple(k0s)),
        in_specs=[pl.BlockSpec(memory_space=pltpu.SMEM)],
        out_shape=jax.ShapeDtypeStruct((nd, N_HEADS, tq, tk), F32),
        name="t5_bias",
    )(t5_table)


def _mla_prompt_kernel(q_ref, k_ref, v_ref, wuv_ref, o_ref, *, tq, tk):
    qi = pl.program_id(1)
    r = N_HEADS * tq
    q = q_ref[...].reshape(r, 256)
    qpos = qi * tq + lax.broadcasted_iota(jnp.int32, (r, 1), 0) % tq
    qlim = (qpos // CHUNK + 1) * CHUNK
    col = lax.broadcasted_iota(jnp.int32, (1, tk), 1)

    def step(kb, carry, masked):
        m, acc = carry
        k0 = pl.multiple_of(kb * tk, tk)
        s = _dot_nt(q, k_ref[0, pl.ds(k0, tk), :])
        if masked:
            s = jnp.where(col < qlim - k0, s, NEG)
        mn = jnp.maximum(m, jnp.max(s, axis=1, keepdims=True))
        alpha = jnp.exp(m - mn)
        p = jnp.exp(s - mn).astype(BF16)
        acc = alpha * acc + _dot(p, v_ref[0, pl.ds(k0, tk), :])
        return mn, acc

    n_full = (qi * tq + CHUNK) // tk
    n_tot = ((qi + 1) * tq + tk - 1) // tk
    carry = (jnp.full((r, 1), NEG, F32), jnp.zeros((r, 256), F32))
    carry = lax.fori_loop(0, n_full, functools.partial(step, masked=False), carry)
    m, acc = lax.fori_loop(n_full, n_tot, functools.partial(step, masked=True), carry)
    on = (acc[:, 0:128] / acc[:, 128:129]).astype(BF16)
    o_ref[...] = jnp.concatenate(
        [_dot(on[hh * tq:(hh + 1) * tq], wuv_ref[hh]) for hh in range(N_HEADS)], axis=1).astype(BF16)


def _mla_prompt(qm, km, vm, wuv, *, batch, seq, tq, tk):
    nq = seq // tq
    t = batch * seq
    return pl.pallas_call(
        functools.partial(_mla_prompt_kernel, tq=tq, tk=tk),
        grid=(batch, nq),
        in_specs=[
            pl.BlockSpec((N_HEADS, tq, 256), lambda b, i: (0, b * nq + i, 0)),
            pl.BlockSpec((1, seq, 256), lambda b, i: (b, 0, 0)),
            pl.BlockSpec((1, seq, 256), lambda b, i: (b, 0, 0)),
            _const_spec(wuv.shape),
        ],
        out_specs=pl.BlockSpec((tq, 256), lambda b, i: (b * nq + i, 0)),
        out_shape=jax.ShapeDtypeStruct((t, 256), BF16),
        compiler_params=_cparams(("parallel", "arbitrary")), name="mla_prompt",
    )(qm, km.reshape(batch, seq, 256), vm.reshape(batch, seq, 256), wuv)


def _diff_lambda(lam_ref, lam_init):
    l = lam_ref[...]
    a = jnp.sum(l[0:1] * l[1:2], axis=-1, keepdims=True)
    b = jnp.sum(l[2:3] * l[3:4], axis=-1, keepdims=True)
    return jnp.exp(a) - jnp.exp(b) + lam_init


def _stack_q8(q):
    grp = _lane_group((1, GROUP_W), DIFF_DH)
    return jnp.concatenate([jnp.where(grp == g, q, jnp.zeros_like(q)) for g in range(2 * N_HEADS)], axis=0)


def _diff_finish(on, lam, subg, lam_init, tq):
    outs = []
    for hh in range(N_HEADS):
        o = on[(2 * hh) * tq:(2 * hh + 1) * tq] - lam * on[(2 * hh + 1) * tq:(2 * hh + 2) * tq]
        o = o * lax.rsqrt(jnp.mean(o * o, axis=-1, keepdims=True) + 1e-5) * subg
        outs.append(o * (1.0 - lam_init))
    return jnp.concatenate(outs, axis=1)


def _diff_prompt_kernel(q_ref, k_ref, v_ref, bias_ref, lam_ref, subg_ref, o_ref, *, tq, lam_init):
    qi = pl.program_id(1)
    r = 2 * N_HEADS * tq
    q8 = _stack_q8(q_ref[...])

    def step(kb, carry, bias_idx):
        m, acc = carry
        k0 = pl.multiple_of(kb * tq, tq)
        s = _dot_nt(q8, k_ref[0, pl.ds(k0, tq), :])
        if bias_idx is not None:
            s = s + jnp.concatenate(
                [bias_ref[bias_idx, hh // 2] for hh in range(2 * N_HEADS)], axis=0)
        mn = jnp.maximum(m, jnp.max(s, axis=1, keepdims=True))
        alpha = jnp.exp(m - mn)
        p = jnp.exp(s - mn).astype(BF16)
        pv = jnp.concatenate(
            [_dot(p[2 * hh * tq:2 * (hh + 1) * tq], v_ref[hh, 0, pl.ds(k0, tq), :])
             for hh in range(N_HEADS)], axis=0)
        return mn, alpha * acc + pv

    carry = (jnp.full((r, 1), NEG, F32), jnp.zeros((r, 128), F32))
    carry = lax.fori_loop(0, jnp.maximum(qi - 1, 0), functools.partial(step, bias_idx=None), carry)
    carry = lax.fori_loop(jnp.maximum(qi - 1, 0), qi, functools.partial(step, bias_idx=1), carry)
    m, acc = step(qi, carry, 0)
    on = acc[:, 0:HEAD_W] / acc[:, HEAD_W:HEAD_W + 1]
    lam = _diff_lambda(lam_ref, lam_init)
    o_ref[...] = _diff_finish(on, lam, subg_ref[...], lam_init, tq).astype(BF16)


def _diff_prompt(dq, kd, vd, bias, lamv, subg, *, batch, seq, tq, lam_init):
    nq = seq // tq
    t = batch * seq
    return pl.pallas_call(
        functools.partial(_diff_prompt_kernel, tq=tq, lam_init=lam_init),
        grid=(batch, nq),
        in_specs=[
            pl.BlockSpec((tq, 256), lambda b, i: (b * nq + i, 0)),
            pl.BlockSpec((1, seq, 256), lambda b, i: (b, 0, 0)),
            pl.BlockSpec((N_HEADS, 1, seq, 128), lambda b, i: (0, b, 0, 0)),
            _const_spec(bias.shape), _const_spec(lamv.shape), _const_spec(subg.shape),
        ],
        out_specs=pl.BlockSpec((tq, 256), lambda b, i: (b * nq + i, 0)),
        out_shape=jax.ShapeDtypeStruct((t, 256), BF16),
        compiler_params=_cparams(("parallel", "arbitrary")), name="diff_prompt",
    )(dq, kd.reshape(batch, seq, 256), vd.reshape(N_HEADS, batch, seq, 128), bias, lamv, subg)


def _mla_sample_kernel(q_ref, kc_ref, kn_ref, vn_ref, wuv_ref, o_ref, *, lq):
    r = N_HEADS * lq
    q = q_ref[...].reshape(r, 256)
    kc = kc_ref[0]
    sc = _dot_nt(q, kc)
    sn = _dot_nt(q, kn_ref[...])
    m = jnp.maximum(jnp.max(sc, axis=1, keepdims=True), jnp.max(sn, axis=1, keepdims=True))
    pc = jnp.exp(sc - m)
    pn = jnp.exp(sn - m)
    l = jnp.sum(pc, axis=1, keepdims=True) + jnp.sum(pn, axis=1, keepdims=True)
    pv = _dot(pc.astype(BF16), kc[:, 0:128]) + _dot(pn.astype(BF16), vn_ref[:, 0:128])
    on = (pv / l).astype(BF16)
    o_ref[...] = jnp.concatenate(
        [_dot(on[hh * lq:(hh + 1) * lq], wuv_ref[hh]) for hh in range(N_HEADS)], axis=1).astype(BF16)


def _mla_sample(qm, kc, km, vm, wuv, *, batch, lq):
    past = kc.shape[1]
    return pl.pallas_call(
        functools.partial(_mla_sample_kernel, lq=lq),
        grid=(batch,),
        in_specs=[
            pl.BlockSpec((N_HEADS, lq, 256), lambda b: (0, b, 0)),
            pl.BlockSpec((1, past, 256), lambda b: (b, 0, 0)),
            pl.BlockSpec((lq, 256), lambda b: (b, 0)),
            pl.BlockSpec((lq, 256), lambda b: (b, 0)),
            _const_spec(wuv.shape),
        ],
        out_specs=pl.BlockSpec((lq, 256), lambda b: (b, 0)),
        out_shape=jax.ShapeDtypeStruct((batch * lq, 256), BF16),
        compiler_params=_cparams(("parallel",)), name="mla_sample",
    )(qm, kc, km, vm, wuv)


def _diff_sample_kernel(q_ref, kc_ref, vc_ref, kn_ref, vn_ref, bc_ref, bn_ref, lam_ref, subg_ref, o_ref,
                        *, lq, lam_init):
    q8 = _stack_q8(q_ref[...])
    bc = jnp.concatenate([bc_ref[0, hh // 2] for hh in range(2 * N_HEADS)], axis=0)
    bn = jnp.concatenate([bn_ref[0, hh // 2] for hh in range(2 * N_HEADS)], axis=0)
    sc = _dot_nt(q8, kc_ref[0]) + bc
    sn = _dot_nt(q8, kn_ref[...].astype(BF16)) + bn
    m = jnp.maximum(jnp.max(sc, axis=1, keepdims=True), jnp.max(sn, axis=1, keepdims=True))
    pc = jnp.exp(sc - m)
    pn = jnp.exp(sn - m)
    l = jnp.sum(pc, axis=1, keepdims=True) + jnp.sum(pn, axis=1, keepdims=True)
    pv = (_dot(pc.astype(BF16), vc_ref[0]) + _dot(pn.astype(BF16), vn_ref[...].astype(BF16))) / l
    on = jnp.concatenate(
        [pv[g * lq:(g + 1) * lq, HEAD_W * (g // 2):HEAD_W * (g // 2 + 1)] for g in range(2 * N_HEADS)], axis=0)
    lam = _diff_lambda(lam_ref, lam_init)
    o_ref[...] = _diff_finish(on, lam, subg_ref[...], lam_init, lq).astype(BF16)


def _diff_sample(dq, kc, vc, dk, dv, bias_c, bias_n, lamv, subg, *, batch, lq, lam_init):
    past = kc.shape[1]
    return pl.pallas_call(
        functools.partial(_diff_sample_kernel, lq=lq, lam_init=lam_init),
        grid=(batch,),
        in_specs=[
            pl.BlockSpec((lq, 256), lambda b: (b, 0)),
            pl.BlockSpec((1, past, 256), lambda b: (b, 0, 0)),
            pl.BlockSpec((1, past, 256), lambda b: (b, 0, 0)),
            pl.BlockSpec((lq, 256), lambda b: (b, 0)),
            pl.BlockSpec((lq, 256), lambda b: (b, 0)),
            _const_spec(bias_c.shape), _const_spec(bias_n.shape),
            _const_spec(lamv.shape), _const_spec(subg.shape),
        ],
        out_specs=pl.BlockSpec((lq, 256), lambda b: (b, 0)),
        out_shape=jax.ShapeDtypeStruct((batch * lq, 256), BF16),
        compiler_params=_cparams(("parallel",)), name="diff_sample",
    )(dq, kc, vc, dk, dv, bias_c, bias_n, lamv, subg)


def _stack4(y, group):
    grp = _lane_group((1, y.shape[1]), group)
    return jnp.concatenate([jnp.where(grp == g, y, 0.0) for g in range(N_HEADS)], axis=0)


def _diag_sum(f, group):
    rr = f.shape[0] // N_HEADS
    grp = _lane_group((1, f.shape[1]), group)
    out = jnp.where(grp == 0, f[0:rr], 0.0)
    for g in range(1, N_HEADS):
        out = out + jnp.where(grp == g, f[g * rr:(g + 1) * rr], 0.0)
    return out


def _bmm(x, y, group):
    return _dot(x, _stack4(y, group))


def _split3(x):
    hi = x.astype(BF16)
    r1 = x - hi.astype(F32)
    mid = r1.astype(BF16)
    lo = (r1 - mid.astype(F32)).astype(BF16)
    return hi, mid, lo


def _unit_lower_inverse(a, c):
    i = lax.broadcasted_iota(jnp.int32, (c, N_HEADS * c), 0)
    j = lax.broadcasted_iota(jnp.int32, (c, N_HEADS * c), 1) % c
    eye = jnp.where(i == j, 1.0, 0.0)
    blockdiag = (i // SOLVE_BLOCK) == (j // SOLVE_BLOCK)
    ad = jnp.where(blockdiag, a, 0.0)
    p = _bmm(ad, ad, c)
    rinv = eye - ad
    rinv = rinv + _bmm(rinv, p, c)
    for _ in range(2):
        p = _bmm(p, p, c)
        rinv = rinv + _bmm(rinv, p, c)
    if c <= SOLVE_BLOCK:
        return rinv
    mm = _bmm(rinv, a - ad, c)
    t = eye - mm
    p = _bmm(mm, mm, c)
    t = t + _bmm(t, p, c)
    nblk = c // SOLVE_BLOCK
    span = 4
    while span < nblk:
        p = _bmm(p, p, c)
        t = t + _bmm(t, p, c)
        span *= 2
    return _bmm(t, rinv, c)


def _to_square(x, c):
    if c == HEAD_W:
        return x
    return jnp.concatenate([x[:, HEAD_W * hh:HEAD_W * hh + c] for hh in range(N_HEADS)], axis=1)


def _gdn_kernel(qkv_ref, gate_ref, gab_ref, buf_ref, s0_ref, cw_ref, alog_ref, dtb_ref, ng_ref,
                o_ref, conv_ref, s_ref, prev_scr, s_scr, *, tg, c):
    j = pl.program_id(1)
    nj = pl.num_programs(1)

    @pl.when(j == 0)
    def _():
        prev_scr[...] = jnp.zeros_like(prev_scr)
        prev_scr[8 - (GDN_CONV - 1):8, :] = buf_ref[0]
        for hh in range(N_HEADS):
            s_scr[:, HEAD_W * hh:HEAD_W * (hh + 1)] = s0_ref[0, hh]

    x = qkv_ref[...]
    prev8 = prev_scr[...]
    cw = cw_ref[...]
    y = x * cw[GDN_CONV - 1:GDN_CONV]
    for s in range(1, GDN_CONV):
        y = y + _shift_rows(x, prev8, s) * cw[GDN_CONV - 1 - s:GDN_CONV - s]
    prev_scr[...] = x[tg - 8:tg]

    @pl.when(j == nj - 1)
    def _():
        conv_ref[0] = x[tg - (GDN_CONV - 1):tg]

    y = _silu(y)
    q = y[:, 0:256]
    k = y[:, 256:512]
    v = y[:, 512:768]
    gones = jnp.where(_lane_group((GROUP_W, GROUP_W), HEAD_W)
                      == lax.broadcasted_iota(jnp.int32, (GROUP_W, GROUP_W), 0) // HEAD_W,
                      1.0, 0.0).astype(BF16)

    def head_sum(z):
        z_hi, z_mid, z_lo = _split3(z)
        return _dot(z_hi, gones) + _dot(z_mid, gones) + _dot(z_lo, gones)

    q = q * lax.rsqrt(head_sum(q * q) + 1e-6) * (HEAD_W ** -0.5)
    k = k * lax.rsqrt(head_sum(k * k) + 1e-6)
    gab = gab_ref[...]
    za = gab[:, 0:256] + dtb_ref[...]
    softplus = jnp.maximum(za, 0.0) + jnp.log(1.0 + jnp.exp(-jnp.abs(za)))
    g = -jnp.exp(alog_ref[...]) * softplus
    beta = 1.0 / (1.0 + jnp.exp(-gab[:, 256:512]))

    ri = lax.broadcasted_iota(jnp.int32, (c, N_HEADS * c), 0)
    cj = lax.broadcasted_iota(jnp.int32, (c, N_HEADS * c), 1) % c
    tri_cc = (lax.broadcasted_iota(jnp.int32, (c, c), 1) <= lax.broadcasted_iota(jnp.int32, (c, c), 0))
    tri_b = jnp.where(tri_cc, 1.0, 0.0).astype(BF16)

    outs = []
    for ci in range(tg // c):
        rows = slice(ci * c, (ci + 1) * c)
        qc, kc, vc, gc, bc = q[rows], k[rows], v[rows], g[rows], beta[rows]
        g_hi, g_mid, g_lo = _split3(gc)
        decay = _dot(tri_b, g_hi) + _dot(tri_b, g_mid) + _dot(tri_b, g_lo)
        dsq = _to_square(decay, c)
        drow = jnp.sum(jnp.where(ri == cj, dsq, 0.0), axis=0, keepdims=True)
        lm = jnp.where(cj <= ri, jnp.exp(jnp.where(cj <= ri, dsq - drow, 0.0)), 0.0)
        kb = kc * bc
        k4 = _stack4(kc, HEAD_W)
        a = jnp.where(cj < ri, _dot_nt(kb, k4) * lm, 0.0)
        qk = _dot_nt(qc, k4) * lm
        edec = jnp.exp(decay)
        tinv = _unit_lower_inverse(a, c)
        u = _bmm(tinv, vc * bc, HEAD_W)
        w = _bmm(tinv, kb * edec, HEAD_W)
        dlast = decay[c - 1:c]
        kt = kc * jnp.exp(dlast - decay)
        s_mat = s_scr[...]
        s4 = _stack4(s_mat, HEAD_W)
        vnew = u - _dot(w, s4)
        outs.append(_dot(qc * edec, s4) + _bmm(qk, vnew, HEAD_W))
        s_scr[...] = s_mat * jnp.exp(dlast) + _diag_sum(_dot_tn(kt, vnew), HEAD_W)

    o = outs[0] if len(outs) == 1 else jnp.concatenate(outs, axis=0)
    o = o * lax.rsqrt(head_sum(o * o) * (1.0 / HEAD_W) + EPS) * ng_ref[...]
    o_ref[...] = (o * _silu(gate_ref[...])).astype(BF16)

    @pl.when(j == nj - 1)
    def _():
        s_fin = s_scr[...]
        for hh in range(N_HEADS):
            s_ref[0, hh] = s_fin[:, HEAD_W * hh:HEAD_W * (hh + 1)]


def _gdn(gqkv, ggate, gab, buf, s0, cw, alog, dtb, ng, *, batch, seq, tg, c):
    nj = seq // tg
    t = batch * seq
    row = lambda wd: pl.BlockSpec((tg, wd), lambda b, j: (b * nj + j, 0))
    return pl.pallas_call(
        functools.partial(_gdn_kernel, tg=tg, c=c),
        grid=(batch, nj),
        in_specs=[
            row(768), row(256), row(512),
            pl.BlockSpec((1, GDN_CONV - 1, 768), lambda b, j: (b, 0, 0)),
            pl.BlockSpec((1, N_HEADS, HEAD_W, HEAD_W), lambda b, j: (b, 0, 0, 0)),
            _const_spec(cw.shape), _const_spec(alog.shape), _const_spec(dtb.shape), _const_spec(ng.shape),
        ],
        out_specs=[
            row(256),
            pl.BlockSpec((1, GDN_CONV - 1, 768), lambda b, j: (b, 0, 0)),
            pl.BlockSpec((1, N_HEADS, HEAD_W, HEAD_W), lambda b, j: (b, 0, 0, 0)),
        ],
        out_shape=[
            jax.ShapeDtypeStruct((t, 256), BF16),
            jax.ShapeDtypeStruct((batch, GDN_CONV - 1, 768), F32),
            jax.ShapeDtypeStruct((batch, N_HEADS, HEAD_W, HEAD_W), F32),
        ],
        scratch_shapes=[pltpu.VMEM((8, 768), F32), pltpu.VMEM((HEAD_W, GROUP_W), F32)],
        compiler_params=_cparams(("parallel", "arbitrary")), name="gdn",
    )(gqkv, ggate, gab, buf, s0, cw, alog, dtb, ng)


def _ffn_kernel(x_ref, om_ref, od_ref, os_ref, og_ref, buf_ref, wout_ref, g2_ref, wup_ref, cw_ref, wdn_ref,
                fg_ref, y_ref, conv_ref, prev_scr, *, tm, d_ff, ft, final):
    j = pl.program_id(1)
    nj = pl.num_programs(1)

    @pl.when(j == 0)
    def _():
        prev_scr[...] = jnp.zeros_like(prev_scr)
        prev_scr[8 - (FFN_CONV - 1):8, :] = buf_ref[0]

    mixed = jnp.concatenate([om_ref[...], od_ref[...], os_ref[...], og_ref[...]], axis=1)
    x1 = x_ref[...] + _dot(mixed, wout_ref[...])
    h2 =(x1 * lax.rsqrt(jnp.mean(x1 * x1, axis=-1, keepdims=True) + EPS) * g2_ref[...]).astype(BF16)

    def conv_cols(off):
        a = _dot(h2, wup_ref[:, off:off + ft])
        prev8 = prev_scr[:, off:off + ft]
        cw = cw_ref[:, off:off + ft]
        y = a * cw[FFN_CONV - 1:FFN_CONV]
        for s in range(1, FFN_CONV):
            y = y + _shift_rows(a, prev8, s) * cw[FFN_CONV - 1 - s:FFN_CONV - s]
        prev_scr[:, off:off + ft] = a[tm - 8:tm]
        return y

    acc = x1
    for fi in range(d_ff // ft):
        gate = conv_cols(fi * ft)
        up = conv_cols(d_ff + fi * ft)
        act = (_silu(gate) * up).astype(BF16)
        acc = acc + _dot(act, wdn_ref[fi * ft:(fi + 1) * ft, :])

    if final:
        acc = acc * lax.rsqrt(jnp.mean(acc * acc, axis=-1, keepdims=True) + EPS) * fg_ref[...]
    y_ref[...] = acc

    @pl.when(j == nj - 1)
    def _():
        conv_ref[0] = prev_scr[8 - (FFN_CONV - 1):8, :]


def _ffn(x, om, od, osg, og, buf, wout, g2, wup, cw, wdn, fg, *, batch, seq, tm, ft, final):
    nj = seq // tm
    t, d = x.shape
    d_ff = wdn.shape[0]
    row = lambda wd: pl.BlockSpec((tm, wd), lambda b, j: (b * nj + j, 0))
    return pl.pallas_call(
        functools.partial(_ffn_kernel, tm=tm, d_ff=d_ff, ft=ft, final=final),
        grid=(batch, nj),
        in_specs=[
            row(d), row(256), row(256), row(256), row(256),
            pl.BlockSpec((1, FFN_CONV - 1, 2 * d_ff), lambda b, j: (b, 0, 0)),
            _const_spec(wout.shape), _const_spec(g2.shape), _const_spec(wup.shape),
            _const_spec(cw.shape), _const_spec(wdn.shape), _const_spec(fg.shape),
        ],
        out_specs=[row(d), pl.BlockSpec((1, FFN_CONV - 1, 2 * d_ff), lambda b, j: (b, 0, 0))],
        out_shape=[jax.ShapeDtypeStruct((t, d), F32),
                   jax.ShapeDtypeStruct((batch, FFN_CONV - 1, 2 * d_ff), F32)],
        scratch_shapes=[pltpu.VMEM((8, 2 * d_ff), F32)],
        compiler_params=_cparams(("parallel", "arbitrary")), name="ffn",
    )(x, om, od, osg, og, buf, wout, g2, wup, cw, wdn, fg)


def _rot_cols(w):
    d = w.shape[0]
    wg = w.reshape(d, -1, 2, MLA_DR // 2)
    return jnp.concatenate([-wg[:, :, 1:2], wg[:, :, 0:1]], axis=2).reshape(d, -1)


def _permute_w_in(w_in):
    d = w_in.shape[0]
    pts = np.cumsum([384, 128, 32, 256, 256, 256, 256, 256, 768, 256, 4, 4])[:-1]
    mq, mlat, mkr, dq, dk, dv, su, sv, gqkv, ggate, ga, gb = jnp.split(w_in, pts, axis=1)
    mq = mq.reshape(d, N_HEADS, MLA_DN + MLA_DR)
    qn = mq[:, :, :MLA_DN].reshape(d, N_HEADS * MLA_DN)
    qr = mq[:, :, MLA_DN:].reshape(d, N_HEADS * MLA_DR)
    kr4 = jnp.tile(mkr, (1, N_HEADS))
    rep = lambda a: jnp.repeat(a, HEAD_W, axis=1)
    w = jnp.concatenate([qn, qr, _rot_cols(qr), mlat, kr4, _rot_cols(kr4),
                         dq, dk, dv, su, sv, gqkv, ggate, rep(ga), rep(gb)], axis=1)
    assert w.shape[1] == C_TOTAL
    return w.astype(BF16)


def _rope_tables(pos):
    inv = ROPE_THETA ** (-jnp.arange(0, MLA_DR, 2, dtype=F32) / MLA_DR)
    ang = pos.astype(F32)[:, None] * inv[None, :]
    cos = jnp.tile(jnp.cos(ang), (1, 2 * N_HEADS))
    sin = jnp.tile(jnp.sin(ang), (1, 2 * N_HEADS))
    return cos, sin


def _layer_params(l, p):
    row = lambda a: a.reshape(1, -1)
    rep = lambda a: jnp.repeat(a, HEAD_W).reshape(1, -1)
    sgu_w = p['sgu_w'][l]
    return dict(
        w_in=_permute_w_in(p['w_in'][l]),
        g1=row(p['norm1_g'][l]),
        wukt=jnp.transpose(p['mla_w_uk'][l], (1, 2, 0)).astype(BF16),
        wuv=jnp.transpose(p['mla_w_uv'][l], (1, 0, 2)).astype(BF16),
        latg=row(p['mla_lat_g'][l]),
        sgu_w=sgu_w,
        sgu_b=p['sgu_b'][l],
        lng=row(p['sgu_ln_g'][l]), lnb=row(p['sgu_ln_b'][l]),
        lamv=jnp.stack([p['diff_lam_q1'][l], p['diff_lam_k1'][l], p['diff_lam_q2'][l], p['diff_lam_k2'][l]]),
        subg=row(p['diff_sub_g'][l]),
        gcw=p['gdn_conv_w'][l],
        alog=rep(p['gdn_a_log'][l]), dtb=rep(p['gdn_dt_bias'][l]),
        ng=jnp.tile(p['gdn_norm_g'][l], N_HEADS).reshape(1, -1),
        wout=p['w_out'][l].astype(BF16),
        g2=row(p['norm2_g'][l]),
        wup=p['ffn_w_up'][l].astype(BF16),
        fcw=p['ffn_conv_w'][l],
        wdn=p['ffn_w_down'][l].astype(BF16),
        fg=row(p['final_g']),
    )


def _sgu_tables(lp, c):
    w = lp['sgu_w'][:, :c, :c].reshape(N_HEADS * c, c)
    b = jnp.repeat(jnp.transpose(lp['sgu_b'][:, :c]), HEAD_W, axis=1)
    return w, b


def _run_inproj(x, lp, cos, sin, *, tm, sgu_c):
    sw, sb = _sgu_tables(lp, sgu_c)
    return _inproj(x, lp['g1'], lp['w_in'], cos, sin, lp['wukt'], lp['latg'], sw, sb, lp['lng'], lp['lnb'],
                   tm=tm, sgu_c=sgu_c)


def kernel(x_prompt, x_sample, cache_mla_latent, cache_mla_krope, cache_diff_k, cache_diff_v, state_gdn_conv, state_gdn_s, state_ffn_conv, t5_table, final_g, norm1_g, w_in, mla_lat_g, mla_w_uk, mla_w_uv, diff_lam_q1, diff_lam_k1, diff_lam_q2, diff_lam_k2, diff_sub_g, sgu_ln_g, sgu_ln_b, sgu_w, sgu_b, gdn_conv_w, gdn_a_log, gdn_dt_bias, gdn_norm_g, w_out, norm2_g, ffn_w_up, ffn_conv_w, ffn_w_down):
    p = dict(final_g=final_g, norm1_g=norm1_g, w_in=w_in, mla_lat_g=mla_lat_g, mla_w_uk=mla_w_uk,
             mla_w_uv=mla_w_uv, diff_lam_q1=diff_lam_q1, diff_lam_k1=diff_lam_k1, diff_lam_q2=diff_lam_q2,
             diff_lam_k2=diff_lam_k2, diff_sub_g=diff_sub_g, sgu_ln_g=sgu_ln_g, sgu_ln_b=sgu_ln_b,
             sgu_w=sgu_w, sgu_b=sgu_b, gdn_conv_w=gdn_conv_w, gdn_a_log=gdn_a_log, gdn_dt_bias=gdn_dt_bias,
             gdn_norm_g=gdn_norm_g, w_out=w_out, norm2_g=norm2_g, ffn_w_up=ffn_w_up, ffn_conv_w=ffn_conv_w,
             ffn_w_down=ffn_w_down)
    depth = w_in.shape[0]
    bp, sp, d = x_prompt.shape
    bs, ls, _ = x_sample.shape
    past = cache_mla_latent.shape[2]
    d_ff = ffn_w_down.shape[1]
    assert past % CHUNK == 0 and ls <= CHUNK

    tm_p = min(512, sp)
    tq = min(256, sp)
    tg_p = min(256, sp)
    ft = 256
    sgu_cp = min(SGU_CHUNK, sp)
    gdn_cp = min(GDN_CHUNK, sp)

    cos_p, sin_p = _rope_tables(jnp.arange(sp, dtype=jnp.int32))
    pos_s = past + jnp.arange(ls, dtype=jnp.int32)
    cos_s, sin_s = _rope_tables(jnp.tile(pos_s, bs))

    bias_p = _t5_bias(t5_table, tq, tq, (0, tq), (0, 0))
    bias_sc = _t5_bias(t5_table, ls, past, (past,), (0,))
    bias_sn = _t5_bias(t5_table, ls, ls, (past,), (past,))

    xp = x_prompt.reshape(bp * sp, d)
    xs = x_sample.reshape(bs * ls, d)
    zeros_gconv = jnp.zeros((bp, GDN_CONV - 1, 3 * GROUP_W), F32)
    zeros_gs = jnp.zeros((bp, N_HEADS, HEAD_W, HEAD_W), F32)
    zeros_fconv = jnp.zeros((bp, FFN_CONV - 1, 2 * d_ff), F32)

    outs_p, outs_s = [], []
    for l in range(depth):
        lp = _layer_params(l, p)
        lam_init = 0.8 - 0.6 * math.exp(-0.3 * l)
        final = l == depth - 1

        (qm, km, vm, lat, kr, dq, dk, dv, kd, vd, osgu, _vn, gqkv, ggate, gab) = _run_inproj(
            xp, lp, cos_p, sin_p, tm=tm_p, sgu_c=sgu_cp)
        o_mla = _mla_prompt(qm, km, vm, lp['wuv'], batch=bp, seq=sp, tq=tq, tk=tq)
        o_diff = _diff_prompt(dq, kd, vd, bias_p, lp['lamv'], lp['subg'],
                              batch=bp, seq=sp, tq=tq, lam_init=lam_init)
        o_gdn, gconv, gs = _gdn(gqkv, ggate, gab, zeros_gconv, zeros_gs, lp['gcw'], lp['alog'], lp['dtb'],
                                lp['ng'], batch=bp, seq=sp, tg=tg_p, c=gdn_cp)
        xp, fconv = _ffn(xp, o_mla, o_diff, osgu, o_gdn, zeros_fconv, lp['wout'], lp['g2'], lp['wup'],
                         lp['fcw'], lp['wdn'], lp['fg'], batch=bp, seq=sp, tm=tm_p, ft=ft, final=final)
        outs_p.append(dict(
            lat=lat.reshape(bp, sp, MLA_R), kr=kr.reshape(bp, sp, MLA_DR),
            dk=dk.reshape(bp, sp, N_HEADS, 2, DIFF_DH), dv=dv.reshape(bp, sp, N_HEADS, 2 * DIFF_DH),
            gconv=gconv, gs=gs, fconv=fconv))

        (qm, km, vm, lat, kr, dq, dk, dv, kd, vd, osgu, vn, gqkv, ggate, gab) = _run_inproj(
            xs, lp, cos_s, sin_s, tm=bs * ls, sgu_c=min(SGU_CHUNK, ls))
        kc_m = jnp.concatenate([cache_mla_latent[l], jnp.tile(cache_mla_krope[l], (1, 1, N_HEADS))],
                               axis=-1).astype(BF16)
        o_mla = _mla_sample(qm, kc_m, km, vm, lp['wuv'], batch=bs, lq=ls)
        kc_d = cache_diff_k[l].reshape(bs, past, GROUP_W).astype(BF16)
        vc_d = cache_diff_v[l].reshape(bs, past, GROUP_W).astype(BF16)
        o_diff = _diff_sample(dq, kc_d, vc_d, dk, dv, bias_sc, bias_sn, lp['lamv'], lp['subg'],
                              batch=bs, lq=ls, lam_init=lam_init)
        o_gdn, gconv, gs = _gdn(gqkv, ggate, gab, state_gdn_conv[l], state_gdn_s[l], lp['gcw'], lp['alog'],
                                lp['dtb'], lp['ng'], batch=bs, seq=ls, tg=ls, c=min(GDN_CHUNK, ls))
        xs, fconv = _ffn(xs, o_mla, o_diff, osgu, o_gdn, state_ffn_conv[l], lp['wout'], lp['g2'], lp['wup'],
                         lp['fcw'], lp['wdn'], lp['fg'], batch=bs, seq=ls, tm=ls, ft=ft, final=final)
        outs_s.append(dict(
            lat=lat.reshape(bs, ls, MLA_R), kr=kr.reshape(bs, ls, MLA_DR),
            dk=dk.reshape(bs, ls, N_HEADS, 2, DIFF_DH), dv=dv.reshape(bs, ls, N_HEADS, 2 * DIFF_DH),
            gconv=gconv, gs=gs, fconv=fconv, sv=vn.reshape(bs, ls, GROUP_W)))

    st = lambda lst, key: jnp.stack([o[key] for o in lst])
    return (xp.reshape(bp, sp, d), xs.reshape(bs, ls, d),
            st(outs_p, 'lat'), st(outs_p, 'kr'), st(outs_p, 'dk'), st(outs_p, 'dv'),
            st(outs_p, 'gconv'), st(outs_p, 'gs'), st(outs_p, 'fconv'),
            st(outs_s, 'lat'), st(outs_s, 'kr'), st(outs_s, 'dk'), st(outs_s, 'dv'),
            st(outs_s, 'gconv'), st(outs_s, 'gs'), st(outs_s, 'fconv'), st(outs_s, 'sv'))
```

```python
import functools
import math

import numpy as np
import jax
import jax.numpy as jnp
from jax import lax
from jax.experimental import pallas as pl
from jax.experimental.pallas import tpu as pltpu

F32 = jnp.float32
BF16 = jnp.bfloat16

CHUNK = 64
EPS = 1e-6
N_HEADS = 4
HEAD_W = 64
GROUP_W = 256
MLA_DN, MLA_DR, MLA_R = 64, 32, 128
DIFF_DH = 32
ROPE_THETA = 10000.0
SGU_CHUNK = 128
GDN_CHUNK = 64
GDN_CONV = 4
FFN_CONV = 3
T5_BUCKETS = 32
T5_MAX_DIST = 128
NEG = -1e30
SOLVE_BLOCK = 16

VMEM_LIMIT_BYTES = 56 * 1024 * 1024

C_QN, C_QR, C_QRR, C_LAT, C_KR, C_KRR = 0, 256, 384, 512, 640, 768
C_DQ, C_DK, C_DV, C_SU, C_SV = 896, 1152, 1408, 1664, 1920
C_GQKV, C_GG, C_GA, C_GB = 2176, 2944, 3200, 3456
C_TOTAL = 3712


def _cparams(sem):
    return pltpu.CompilerParams(dimension_semantics=sem, vmem_limit_bytes=VMEM_LIMIT_BYTES)


def _const_spec(shape):
    nd = len(shape)
    return pl.BlockSpec(shape, lambda *_: (0,) * nd, pipeline_mode=pl.Buffered(1))


def _dot(a, b):
    return jnp.dot(a, b, preferred_element_type=F32)


def _dot_nt(a, b):
    return lax.dot_general(a, b, (((1,), (1,)), ((), ())), preferred_element_type=F32)


def _dot_tn(a, b):
    return lax.dot_general(a, b, (((0,), (0,)), ((), ())), preferred_element_type=F32)


def _lane_group(shape, group):
    return lax.broadcasted_iota(jnp.int32, shape, len(shape) - 1) // group


def _silu(x):
    return x * (1.0 / (1.0 + jnp.exp(-x)))


def _shift_rows(x, prev8, s):
    if s == 0:
        return x
    r = pltpu.roll(x, s, axis=0)
    rp = pltpu.roll(prev8, s, axis=0)
    row8 = lax.broadcasted_iota(jnp.int32, rp.shape, 0)
    top = jnp.where(row8 < s, rp, r[0:8])
    if x.shape[0] == 8:
        return top
    return jnp.concatenate([top, r[8:]], axis=0)


def _inproj_kernel(x_ref, g1_ref, w_ref, cos_ref, sin_ref, wukt_ref, latg_ref,
                   sguw_ref, sgub_ref, lng_ref, lnb_ref,
                   qm_ref, km_ref, vm_ref, lat_ref, kr_ref,
                   dq_ref, dk_ref, dv_ref, kd_ref, vd_ref,
                   osgu_ref, vn_ref, gqkv_ref, ggate_ref, gab_ref, *, sgu_c):
    tm = x_ref.shape[0]
    x = x_ref[...]
    h = (x * lax.rsqrt(jnp.mean(x * x, axis=-1, keepdims=True) + EPS) * g1_ref[...]).astype(BF16)

    def sec(off, n):
        return _dot(h, w_ref[:, off:off + n])

    cos = cos_ref[...]
    sin = sin_ref[...]

    mla_scale = (MLA_DN + MLA_DR) ** -0.5
    zq = sec(C_QN, 512)
    qr = (zq[:, 256:384] * cos + zq[:, 384:512] * sin) * mla_scale
    head_of_lane = _lane_group((1, 128), MLA_DR)
    for hh in range(N_HEADS):
        qn = (zq[:, MLA_DN * hh:MLA_DN * (hh + 1)] * mla_scale).astype(BF16)
        qm_ref[hh, :, 0:128] = _dot(qn, wukt_ref[hh]).astype(BF16)
        qm_ref[hh, :, 128:256] = jnp.where(head_of_lane == hh, qr, 0.0).astype(BF16)

    zl = sec(C_LAT, 384)
    zlat = zl[:, 0:128]
    lat = zlat * lax.rsqrt(jnp.mean(zlat * zlat, axis=-1, keepdims=True) + EPS) * latg_ref[...]
    kr4 = zl[:, 128:256] * cos + zl[:, 256:384] * sin
    lat_ref[...] = lat
    kr_ref[...] = kr4[:, 0:MLA_DR]
    lat_b = lat.astype(BF16)
    km_ref[:, 0:128] = lat_b
    km_ref[:, 128:256] = kr4.astype(BF16)
    vm_ref[:, 0:128] = lat_b
    vm_ref[:, 128:256] = jnp.ones((tm, 128), BF16)

    zd = sec(C_DQ, 768)
    dq_ref[...] = (zd[:, 0:256] * (DIFF_DH ** -0.5)).astype(BF16)
    dk = zd[:, 256:512]
    dv = zd[:, 512:768]
    dk_ref[...] = dk
    dv_ref[...] = dv
    kd_ref[...] = dk.astype(BF16)
    ones64 = jnp.ones((tm, HEAD_W), F32)
    for hh in range(N_HEADS):
        vd_ref[hh] = jnp.concatenate([dv[:, HEAD_W * hh:HEAD_W * (hh + 1)], ones64], axis=1).astype(BF16)

    zs = sec(C_SU, 512)
    su = zs[:, 0:256]
    sv = zs[:, 256:512]
    mu = jnp.mean(sv, axis=-1, keepdims=True)
    svc = sv - mu
    var = jnp.mean(svc * svc, axis=-1, keepdims=True)
    vn = svc * lax.rsqrt(var + EPS) * lng_ref[...] + lnb_ref[...]
    vn_ref[...] = vn
    c = sgu_c
    wr = lax.broadcasted_iota(jnp.int32, (N_HEADS * c, c), 0) % c
    wc = lax.broadcasted_iota(jnp.int32, (N_HEADS * c, c), 1)
    w4 = jnp.where(wc <= wr, sguw_ref[...], 0.0).astype(BF16)
    hl = _lane_group((1, GROUP_W), HEAD_W)
    sgub = sgub_ref[...]
    for ci in range(tm // c):
        rows = slice(ci * c, (ci + 1) * c)
        m4 = _dot(w4, vn[rows].astype(BF16))
        mix = jnp.where(hl == 0, m4[0:c], 0.0)
        for hh in range(1, N_HEADS):
            mix = mix + jnp.where(hl == hh, m4[hh * c:(hh + 1) * c], 0.0)
        osgu_ref[rows, :] = (su[rows] * (mix + sgub)).astype(BF16)

    zg = sec(C_GQKV, 1536)
    gqkv_ref[...] = zg[:, 0:768]
    ggate_ref[...] = zg[:, 768:1024]
    gab_ref[...] = zg[:, 1024:1536]


def _inproj(x, g1, w, cos, sin, wukt, latg, sguw, sgub, lng, lnb, *, tm, sgu_c):
    t, d = x.shape
    n = t // tm
    npos = cos.shape[0] // tm
    row = lambda wd: pl.BlockSpec((tm, wd), lambda i: (i, 0))
    out_shape = [
        jax.ShapeDtypeStruct((N_HEADS, t, 256), BF16),
        jax.ShapeDtypeStruct((t, 256), BF16),
        jax.ShapeDtypeStruct((t, 256), BF16),
        jax.ShapeDtypeStruct((t, MLA_R), F32),
        jax.ShapeDtypeStruct((t, MLA_DR), F32),
        jax.ShapeDtypeStruct((t, 256), BF16),
        jax.ShapeDtypeStruct((t, 256), F32),
        jax.ShapeDtypeStruct((t, 256), F32),
        jax.ShapeDtypeStruct((t, 256), BF16),
        jax.ShapeDtypeStruct((N_HEADS, t, 128), BF16),
        jax.ShapeDtypeStruct((t, 256), BF16),
        jax.ShapeDtypeStruct((t, 256), F32),
        jax.ShapeDtypeStruct((t, 768), F32),
        jax.ShapeDtypeStruct((t, 256), F32),
        jax.ShapeDtypeStruct((t, 512), F32),
    ]
    out_specs = [
        pl.BlockSpec((N_HEADS, tm, 256), lambda i: (0, i, 0)),
        row(256), row(256), row(MLA_R), row(MLA_DR),
        row(256), row(256), row(256), row(256),
        pl.BlockSpec((N_HEADS, tm, 128), lambda i: (0, i, 0)),
        row(256), row(256), row(768), row(256), row(512),
    ]
    in_specs = [
        row(d), _const_spec(g1.shape), _const_spec(w.shape),
        pl.BlockSpec((tm, 128), lambda i: (i % npos, 0)),
        pl.BlockSpec((tm, 128), lambda i: (i % npos, 0)),
        _const_spec(wukt.shape), _const_spec(latg.shape),
        _const_spec(sguw.shape), _const_spec(sgub.shape),
        _const_spec(lng.shape), _const_spec(lnb.shape),
    ]
    return pl.pallas_call(
        functools.partial(_inproj_kernel, sgu_c=sgu_c),
        grid=(n,), in_specs=in_specs, out_specs=out_specs, out_shape=out_shape,
        compiler_params=_cparams(("parallel",)), name="inproj",
    )(x, g1, w, cos, sin, wukt, latg, sguw, sgub, lng, lnb)


def _t5_thresholds():
    nb = T5_BUCKETS // 2
    max_exact = nb // 2
    ratio = T5_MAX_DIST // max_exact
    thr = []
    for j in range(1, nb - max_exact):
        n = max_exact
        while (n ** (nb - max_exact)) * 1 < (ratio ** j) * (max_exact ** (nb - max_exact)):
            n += 1
        thr.append(n)
    return nb, max_exact, thr


def _bias_kernel(t_ref, o_ref, *, q0s, k0s):
    nb, max_exact, thr = _t5_thresholds()
    tq, tk = o_ref.shape[2], o_ref.shape[3]
    row = lax.broadcasted_iota(jnp.int32, (tq, tk), 0)
    col = lax.broadcasted_iota(jnp.int32, (tq, tk), 1)
    for di, (q0, k0) in enumerate(zip(q0s, k0s)):
        qpos = row + q0
        kpos = col + k0
        rel = kpos - qpos
        n = jnp.abs(rel)
        visible = (kpos // CHUNK) <= (qpos // CHUNK)
        for hh in range(N_HEADS):
            def side(base):
                val = jnp.full((tq, tk), t_ref[base + nb - 1, hh], F32)
                for j in range(len(thr) - 1, -1, -1):
                    val = jnp.where(n < thr[j], t_ref[base + max_exact + j, hh], val)
                for e in range(max_exact - 1, -1, -1):
                    val = jnp.where(n == e, t_ref[base + e, hh], val)
                return val
            b = jnp.where(rel > 0, side(nb), side(0)) - t_ref[nb - 1, hh]
            o_ref[di, hh] = jnp.where(visible, b, NEG)


def _t5_bias(t5_table, tq, tk, q0s, k0s):
    nd = len(q0s)
    return pl.pallas_call(
        functools.partial(_bias_kernel, q0s=tuple(q0s), k0s=tuple(k0s)),
        in_specs=[pl.BlockSpec(memory_space=pltpu.SMEM)],
        out_shape=jax.ShapeDtypeStruct((nd, N_HEADS, tq, tk), F32),
        name="t5_bias",
    )(t5_table)


def _mla_prompt_kernel(q_ref, k_ref, v_ref, wuv_ref, o_ref, *, tq, tk):
    qi = pl.program_id(1)
    r = N_HEADS * tq
    q = q_ref[...].reshape(r, 256)
    qpos = qi * tq + lax.broadcasted_iota(jnp.int32, (r, 1), 0) % tq
    qlim = (qpos // CHUNK + 1) * CHUNK
    col = lax.broadcasted_iota(jnp.int32, (1, tk), 1)

    def step(kb, carry, masked):
        m, acc = carry
        k0 = pl.multiple_of(kb * tk, tk)
        s = _dot_nt(q, k_ref[0, pl.ds(k0, tk), :])
        if masked:
            s = jnp.where(col < qlim - k0, s, NEG)
        mn = jnp.maximum(m, jnp.max(s, axis=1, keepdims=True))
        alpha = jnp.exp(m - mn)
        p = jnp.exp(s - mn).astype(BF16)
        acc = alpha * acc + _dot(p, v_ref[0, pl.ds(k0, tk), :])
        return mn, acc

    n_full = (qi * tq + CHUNK) // tk
    n_tot = ((qi + 1) * tq + tk - 1) // tk
    carry = (jnp.full((r, 1), NEG, F32), jnp.zeros((r, 256), F32))
    carry = lax.fori_loop(0, n_full, functools.partial(step, masked=False), carry)
    m, acc = lax.fori_loop(n_full, n_tot, functools.partial(step, masked=True), carry)
    on = (acc[:, 0:128] / acc[:, 128:129]).astype(BF16)
    o_ref[...] = jnp.concatenate(
        [_dot(on[hh * tq:(hh + 1) * tq], wuv_ref[hh]) for hh in range(N_HEADS)], axis=1).astype(BF16)


def _mla_prompt(qm, km, vm, wuv, *, batch, seq, tq, tk):
    nq = seq // tq
    t = batch * seq
    return pl.pallas_call(
        functools.partial(_mla_prompt_kernel, tq=tq, tk=tk),
        grid=(batch, nq),
        in_specs=[
            pl.BlockSpec((N_HEADS, tq, 256), lambda b, i: (0, b * nq + i, 0)),
            pl.BlockSpec((1, seq, 256), lambda b, i: (b, 0, 0)),
            pl.BlockSpec((1, seq, 256), lambda b, i: (b, 0, 0)),
            _const_spec(wuv.shape),
        ],
        out_specs=pl.BlockSpec((tq, 256), lambda b, i: (b * nq + i, 0)),
        out_shape=jax.ShapeDtypeStruct((t, 256), BF16),
        compiler_params=_cparams(("parallel", "arbitrary")), name="mla_prompt",
    )(qm, km.reshape(batch, seq, 256), vm.reshape(batch, seq, 256), wuv)


def _diff_lambda(lam_ref, lam_init):
    l = lam_ref[...]
    a = jnp.sum(l[0:1] * l[1:2], axis=-1, keepdims=True)
    b = jnp.sum(l[2:3] * l[3:4], axis=-1, keepdims=True)
    return jnp.exp(a) - jnp.exp(b) + lam_init


def _stack_q8(q):
    grp = _lane_group((1, GROUP_W), DIFF_DH)
    return jnp.concatenate([jnp.where(grp == g, q, jnp.zeros_like(q)) for g in range(2 * N_HEADS)], axis=0)


def _diff_finish(on, lam, subg, lam_init, tq):
    outs = []
    for hh in range(N_HEADS):
        o = on[(2 * hh) * tq:(2 * hh + 1) * tq] - lam * on[(2 * hh + 1) * tq:(2 * hh + 2) * tq]
        o = o * lax.rsqrt(jnp.mean(o * o, axis=-1, keepdims=True) + 1e-5) * subg
        outs.append(o * (1.0 - lam_init))
    return jnp.concatenate(outs, axis=1)


def _diff_prompt_kernel(q_ref, k_ref, v_ref, bias_ref, lam_ref, subg_ref, o_ref, *, tq, lam_init):
    qi = pl.program_id(1)
    r = 2 * N_HEADS * tq
    q8 = _stack_q8(q_ref[...])

    def step(kb, carry, bias_idx):
        m, acc = carry
        k0 = pl.multiple_of(kb * tq, tq)
        s = _dot_nt(q8, k_ref[0, pl.ds(k0, tq), :])
        if bias_idx is not None:
            s = s + jnp.concatenate(
                [bias_ref[bias_idx, hh // 2] for hh in range(2 * N_HEADS)], axis=0)
        mn = jnp.maximum(m, jnp.max(s, axis=1, keepdims=True))
        alpha = jnp.exp(m - mn)
        p = jnp.exp(s - mn).astype(BF16)
        pv = jnp.concatenate(
            [_dot(p[2 * hh * tq:2 * (hh + 1) * tq], v_ref[hh, 0, pl.ds(k0, tq), :])
             for hh in range(N_HEADS)], axis=0)
        return mn, alpha * acc + pv

    carry = (jnp.full((r, 1), NEG, F32), jnp.zeros((r, 128), F32))
    carry = lax.fori_loop(0, jnp.maximum(qi - 1, 0), functools.partial(step, bias_idx=None), carry)
    carry = lax.fori_loop(jnp.maximum(qi - 1, 0), qi, functools.partial(step, bias_idx=1), carry)
    m, acc = step(qi, carry, 0)
    on = acc[:, 0:HEAD_W] / acc[:, HEAD_W:HEAD_W + 1]
    lam = _diff_lambda(lam_ref, lam_init)
    o_ref[...] = _diff_finish(on, lam, subg_ref[...], lam_init, tq).astype(BF16)


def _diff_prompt(dq, kd, vd, bias, lamv, subg, *, batch, seq, tq, lam_init):
    nq = seq // tq
    t = batch * seq
    return pl.pallas_call(
        functools.partial(_diff_prompt_kernel, tq=tq, lam_init=lam_init),
        grid=(batch, nq),
        in_specs=[
            pl.BlockSpec((tq, 256), lambda b, i: (b * nq + i, 0)),
            pl.BlockSpec((1, seq, 256), lambda b, i: (b, 0, 0)),
            pl.BlockSpec((N_HEADS, 1, seq, 128), lambda b, i: (0, b, 0, 0)),
            _const_spec(bias.shape), _const_spec(lamv.shape), _const_spec(subg.shape),
        ],
        out_specs=pl.BlockSpec((tq, 256), lambda b, i: (b * nq + i, 0)),
        out_shape=jax.ShapeDtypeStruct((t, 256), BF16),
        compiler_params=_cparams(("parallel", "arbitrary")), name="diff_prompt",
    )(dq, kd.reshape(batch, seq, 256), vd.reshape(N_HEADS, batch, seq, 128), bias, lamv, subg)


def _mla_sample_kernel(q_ref, kc_ref, kn_ref, vn_ref, wuv_ref, o_ref, *, lq):
    r = N_HEADS * lq
    q = q_ref[...].reshape(r, 256)
    kc = kc_ref[0]
    sc = _dot_nt(q, kc)
    sn = _dot_nt(q, kn_ref[...])
    m = jnp.maximum(jnp.max(sc, axis=1, keepdims=True), jnp.max(sn, axis=1, keepdims=True))
    pc = jnp.exp(sc - m)
    pn = jnp.exp(sn - m)
    l = jnp.sum(pc, axis=1, keepdims=True) + jnp.sum(pn, axis=1, keepdims=True)
    pv = _dot(pc.astype(BF16), kc[:, 0:128]) + _dot(pn.astype(BF16), vn_ref[:, 0:128])
    on = (pv / l).astype(BF16)
    o_ref[...] = jnp.concatenate(
        [_dot(on[hh * lq:(hh + 1) * lq], wuv_ref[hh]) for hh in range(N_HEADS)], axis=1).astype(BF16)


def _mla_sample(qm, kc, km, vm, wuv, *, batch, lq):
    past = kc.shape[1]
    return pl.pallas_call(
        functools.partial(_mla_sample_kernel, lq=lq),
        grid=(batch,),
        in_specs=[
            pl.BlockSpec((N_HEADS, lq, 256), lambda b: (0, b, 0)),
            pl.BlockSpec((1, past, 256), lambda b: (b, 0, 0)),
            pl.BlockSpec((lq, 256), lambda b: (b, 0)),
            pl.BlockSpec((lq, 256), lambda b: (b, 0)),
            _const_spec(wuv.shape),
        ],
        out_specs=pl.BlockSpec((lq, 256), lambda b: (b, 0)),
        out_shape=jax.ShapeDtypeStruct((batch * lq, 256), BF16),
        compiler_params=_cparams(("parallel",)), name="mla_sample",
    )(qm, kc, km, vm, wuv)


def _diff_sample_kernel(q_ref, kc_ref, vc_ref, kn_ref, vn_ref, bc_ref, bn_ref, lam_ref, subg_ref, o_ref,
                        *, lq, lam_init):
    q8 = _stack_q8(q_ref[...])
    bc = jnp.concatenate([bc_ref[0, hh // 2] for hh in range(2 * N_HEADS)], axis=0)
    bn = jnp.concatenate([bn_ref[0, hh // 2] for hh in range(2 * N_HEADS)], axis=0)
    sc = _dot_nt(q8, kc_ref[0]) + bc
    sn = _dot_nt(q8, kn_ref[...].astype(BF16)) + bn
    m = jnp.maximum(jnp.max(sc, axis=1, keepdims=True), jnp.max(sn, axis=1, keepdims=True))
    pc = jnp.exp(sc - m)
    pn = jnp.exp(sn - m)
    l = jnp.sum(pc, axis=1, keepdims=True) + jnp.sum(pn, axis=1, keepdims=True)
    pv = (_dot(pc.astype(BF16), vc_ref[0]) + _dot(pn.astype(BF16), vn_ref[...].astype(BF16))) / l
    on = jnp.concatenate(
        [pv[g * lq:(g + 1) * lq, HEAD_W * (g // 2):HEAD_W * (g // 2 + 1)] for g in range(2 * N_HEADS)], axis=0)
    lam = _diff_lambda(lam_ref, lam_init)
    o_ref[...] = _diff_finish(on, lam, subg_ref[...], lam_init, lq).astype(BF16)


def _diff_sample(dq, kc, vc, dk, dv, bias_c, bias_n, lamv, subg, *, batch, lq, lam_init):
    past = kc.shape[1]
    return pl.pallas_call(
        functools.partial(_diff_sample_kernel, lq=lq, lam_init=lam_init),
        grid=(batch,),
        in_specs=[
            pl.BlockSpec((lq, 256), lambda b: (b, 0)),
            pl.BlockSpec((1, past, 256), lambda b: (b, 0, 0)),
            pl.BlockSpec((1, past, 256), lambda b: (b, 0, 0)),
            pl.BlockSpec((lq, 256), lambda b: (b, 0)),
            pl.BlockSpec((lq, 256), lambda b: (b, 0)),
            _const_spec(bias_c.shape), _const_spec(bias_n.shape),
            _const_spec(lamv.shape), _const_spec(subg.shape),
        ],
        out_specs=pl.BlockSpec((lq, 256), lambda b: (b, 0)),
        out_shape=jax.ShapeDtypeStruct((batch * lq, 256), BF16),
        compiler_params=_cparams(("parallel",)), name="diff_sample",
    )(dq, kc, vc, dk, dv, bias_c, bias_n, lamv, subg)


def _stack4(y, group):
    grp = _lane_group((1, y.shape[1]), group)
    return jnp.concatenate([jnp.where(grp == g, y, 0.0) for g in range(N_HEADS)], axis=0)


def _diag_sum(f, group):
    rr = f.shape[0] // N_HEADS
    grp = _lane_group((1, f.shape[1]), group)
    out = jnp.where(grp == 0, f[0:rr], 0.0)
    for g in range(1, N_HEADS):
        out = out + jnp.where(grp == g, f[g * rr:(g + 1) * rr], 0.0)
    return out


def _bmm(x, y, group):
    return _dot(x, _stack4(y, group))


def _split3(x):
    hi = x.astype(BF16)
    r1 = x - hi.astype(F32)
    mid = r1.astype(BF16)
    lo = (r1 - mid.astype(F32)).astype(BF16)
    return hi, mid, lo


def _unit_lower_inverse(a, c):
    i = lax.broadcasted_iota(jnp.int32, (c, N_HEADS * c), 0)
    j = lax.broadcasted_iota(jnp.int32, (c, N_HEADS * c), 1) % c
    eye = jnp.where(i == j, 1.0, 0.0)
    blockdiag = (i // SOLVE_BLOCK) == (j // SOLVE_BLOCK)
    ad = jnp.where(blockdiag, a, 0.0)
    p = _bmm(ad, ad, c)
    rinv = eye - ad
    rinv = rinv + _bmm(rinv, p, c)
    for _ in range(2):
        p = _bmm(p, p, c)
        rinv = rinv + _bmm(rinv, p, c)
    if c <= SOLVE_BLOCK:
        return rinv
    mm = _bmm(rinv, a - ad, c)
    t = eye - mm
    p = _bmm(mm, mm, c)
    t = t + _bmm(t, p, c)
    nblk = c // SOLVE_BLOCK
    span = 4
    while span < nblk:
        p = _bmm(p, p, c)
        t = t + _bmm(t, p, c)
        span *= 2
    return _bmm(t, rinv, c)


def _to_square(x, c):
    if c == HEAD_W:
        return x
    return jnp.concatenate([x[:, HEAD_W * hh:HEAD_W * hh + c] for hh in range(N_HEADS)], axis=1)


def _gdn_kernel(qkv_ref, gate_ref, gab_ref, buf_ref, s0_ref, cw_ref, alog_ref, dtb_ref, ng_ref,
                o_ref, conv_ref, s_ref, prev_scr, s_scr, *, tg, c):
    j = pl.program_id(1)
    nj = pl.num_programs(1)

    @pl.when(j == 0)
    def _():
        prev_scr[...] = jnp.zeros_like(prev_scr)
        prev_scr[8 - (GDN_CONV - 1):8, :] = buf_ref[0]
        for hh in range(N_HEADS):
            s_scr[:, HEAD_W * hh:HEAD_W * (hh + 1)] = s0_ref[0, hh]

    x = qkv_ref[...]
    prev8 = prev_scr[...]
    cw = cw_ref[...]
    y = x * cw[GDN_CONV - 1:GDN_CONV]
    for s in range(1, GDN_CONV):
        y = y + _shift_rows(x, prev8, s) * cw[GDN_CONV - 1 - s:GDN_CONV - s]
    prev_scr[...] = x[tg - 8:tg]

    @pl.when(j == nj - 1)
    def _():
        conv_ref[0] = x[tg - (GDN_CONV - 1):tg]

    y = _silu(y)
    q = y[:, 0:256]
    k = y[:, 256:512]
    v = y[:, 512:768]
    gones = jnp.where(_lane_group((GROUP_W, GROUP_W), HEAD_W)
                      == lax.broadcasted_iota(jnp.int32, (GROUP_W, GROUP_W), 0) // HEAD_W,
                      1.0, 0.0).astype(BF16)

    def head_sum(z):
        z_hi, z_mid, z_lo = _split3(z)
        return _dot(z_hi, gones) + _dot(z_mid, gones) + _dot(z_lo, gones)

    q = q * lax.rsqrt(head_sum(q * q) + 1e-6) * (HEAD_W ** -0.5)
    k = k * lax.rsqrt(head_sum(k * k) + 1e-6)
    gab = gab_ref[...]
    za = gab[:, 0:256] + dtb_ref[...]
    softplus = jnp.maximum(za, 0.0) + jnp.log(1.0 + jnp.exp(-jnp.abs(za)))
    g = -jnp.exp(alog_ref[...]) * softplus
    beta = 1.0 / (1.0 + jnp.exp(-gab[:, 256:512]))

    ri = lax.broadcasted_iota(jnp.int32, (c, N_HEADS * c), 0)
    cj = lax.broadcasted_iota(jnp.int32, (c, N_HEADS * c), 1) % c
    tri_cc = (lax.broadcasted_iota(jnp.int32, (c, c), 1) <= lax.broadcasted_iota(jnp.int32, (c, c), 0))
    tri_b = jnp.where(tri_cc, 1.0, 0.0).astype(BF16)

    outs = []
    for ci in range(tg // c):
        rows = slice(ci * c, (ci + 1) * c)
        qc, kc, vc, gc, bc = q[rows], k[rows], v[rows], g[rows], beta[rows]
        g_hi, g_mid, g_lo = _split3(gc)
        decay = _dot(tri_b, g_hi) + _dot(tri_b, g_mid) + _dot(tri_b, g_lo)
        dsq = _to_square(decay, c)
        drow = jnp.sum(jnp.where(ri == cj, dsq, 0.0), axis=0, keepdims=True)
        lm = jnp.where(cj <= ri, jnp.exp(jnp.where(cj <= ri, dsq - drow, 0.0)), 0.0)
        kb = kc * bc
        k4 = _stack4(kc, HEAD_W)
        a = jnp.where(cj < ri, _dot_nt(kb, k4) * lm, 0.0)
        qk = _dot_nt(qc, k4) * lm
        edec = jnp.exp(decay)
        tinv = _unit_lower_inverse(a, c)
        u = _bmm(tinv, vc * bc, HEAD_W)
        w = _bmm(tinv, kb * edec, HEAD_W)
        dlast = decay[c - 1:c]
        kt = kc * jnp.exp(dlast - decay)
        s_mat = s_scr[...]
        s4 = _stack4(s_mat, HEAD_W)
        vnew = u - _dot(w, s4)
        outs.append(_dot(qc * edec, s4) + _bmm(qk, vnew, HEAD_W))
        s_scr[...] = s_mat * jnp.exp(dlast) + _diag_sum(_dot_tn(kt, vnew), HEAD_W)

    o = outs[0] if len(outs) == 1 else jnp.concatenate(outs, axis=0)
    o = o * lax.rsqrt(head_sum(o * o) * (1.0 / HEAD_W) + EPS) * ng_ref[...]
    o_ref[...] = (o * _silu(gate_ref[...])).astype(BF16)

    @pl.when(j == nj - 1)
    def _():
        s_fin = s_scr[...]
        for hh in range(N_HEADS):
            s_ref[0, hh] = s_fin[:, HEAD_W * hh:HEAD_W * (hh + 1)]


def _gdn(gqkv, ggate, gab, buf, s0, cw, alog, dtb, ng, *, batch, seq, tg, c):
    nj = seq // tg
    t = batch * seq
    row = lambda wd: pl.BlockSpec((tg, wd), lambda b, j: (b * nj + j, 0))
    return pl.pallas_call(
        functools.partial(_gdn_kernel, tg=tg, c=c),
        grid=(batch, nj),
        in_specs=[
            row(768), row(256), row(512),
            pl.BlockSpec((1, GDN_CONV - 1, 768), lambda b, j: (b, 0, 0)),
            pl.BlockSpec((1, N_HEADS, HEAD_W, HEAD_W), lambda b, j: (b, 0, 0, 0)),
            _const_spec(cw.shape), _const_spec(alog.shape), _const_spec(dtb.shape), _const_spec(ng.shape),
        ],
        out_specs=[
            row(256),
            pl.BlockSpec((1, GDN_CONV - 1, 768), lambda b, j: (b, 0, 0)),
            pl.BlockSpec((1, N_HEADS, HEAD_W, HEAD_W), lambda b, j: (b, 0, 0, 0)),
        ],
        out_shape=[
            jax.ShapeDtypeStruct((t, 256), BF16),
            jax.ShapeDtypeStruct((batch, GDN_CONV - 1, 768), F32),
            jax.ShapeDtypeStruct((batch, N_HEADS, HEAD_W, HEAD_W), F32),
        ],
        scratch_shapes=[pltpu.VMEM((8, 768), F32), pltpu.VMEM((HEAD_W, GROUP_W), F32)],
        compiler_params=_cparams(("parallel", "arbitrary")), name="gdn",
    )(gqkv, ggate, gab, buf, s0, cw, alog, dtb, ng)


def _ffn_kernel(x_ref, om_ref, od_ref, os_ref, og_ref, buf_ref, wout_ref, g2_ref, wup_ref, cw_ref, wdn_ref,
                fg_ref, y_ref, conv_ref, prev_scr, *, tm, d_ff, ft, final):
    j = pl.program_id(1)
    nj = pl.num_programs(1)

    @pl.when(j == 0)
    def _():
        prev_scr[...] = jnp.zeros_like(prev_scr)
        prev_scr[8 - (FFN_CONV - 1):8, :] = buf_ref[0]

    mixed = jnp.concatenate([om_ref[...], od_ref[...], os_ref[...], og_ref[...]], axis=1)
    x1 = x_ref[...] + _dot(mixed, wout_ref[...])
    h2 =(x1 * lax.rsqrt(jnp.mean(x1 * x1, axis=-1, keepdims=True) + EPS) * g2_ref[...]).astype(BF16)

    def conv_cols(off):
        a = _dot(h2, wup_ref[:, off:off + ft])
        prev8 = prev_scr[:, off:off + ft]
        cw = cw_ref[:, off:off + ft]
        y = a * cw[FFN_CONV - 1:FFN_CONV]
        for s in range(1, FFN_CONV):
            y = y + _shift_rows(a, prev8, s) * cw[FFN_CONV - 1 - s:FFN_CONV - s]
        prev_scr[:, off:off + ft] = a[tm - 8:tm]
        return y

    acc = x1
    for fi in range(d_ff // ft):
        gate = conv_cols(fi * ft)
        up = conv_cols(d_ff + fi * ft)
        act = (_silu(gate) * up).astype(BF16)
        acc = acc + _dot(act, wdn_ref[fi * ft:(fi + 1) * ft, :])

    if final:
        acc = acc * lax.rsqrt(jnp.mean(acc * acc, axis=-1, keepdims=True) + EPS) * fg_ref[...]
    y_ref[...] = acc

    @pl.when(j == nj - 1)
    def _():
        conv_ref[0] = prev_scr[8 - (FFN_CONV - 1):8, :]


def _ffn(x, om, od, osg, og, buf, wout, g2, wup, cw, wdn, fg, *, batch, seq, tm, ft, final):
    nj = seq // tm
    t, d = x.shape
    d_ff = wdn.shape[0]
    row = lambda wd: pl.BlockSpec((tm, wd), lambda b, j: (b * nj + j, 0))
    return pl.pallas_call(
        functools.partial(_ffn_kernel, tm=tm, d_ff=d_ff, ft=ft, final=final),
        grid=(batch, nj),
        in_specs=[
            row(d), row(256), row(256), row(256), row(256),
            pl.BlockSpec((1, FFN_CONV - 1, 2 * d_ff), lambda b, j: (b, 0, 0)),
            _const_spec(wout.shape), _const_spec(g2.shape), _const_spec(wup.shape),
            _const_spec(cw.shape), _const_spec(wdn.shape), _const_spec(fg.shape),
        ],
        out_specs=[row(d), pl.BlockSpec((1, FFN_CONV - 1, 2 * d_ff), lambda b, j: (b, 0, 0))],
        out_shape=[jax.ShapeDtypeStruct((t, d), F32),
                   jax.ShapeDtypeStruct((batch, FFN_CONV - 1, 2 * d_ff), F32)],
        scratch_shapes=[pltpu.VMEM((8, 2 * d_ff), F32)],
        compiler_params=_cparams(("parallel", "arbitrary")), name="ffn",
    )(x, om, od, osg, og, buf, wout, g2, wup, cw, wdn, fg)


def _rot_cols(w):
    d = w.shape[0]
    wg = w.reshape(d, -1, 2, MLA_DR // 2)
    return jnp.concatenate([-wg[:, :, 1:2], wg[:, :, 0:1]], axis=2).reshape(d, -1)


def _permute_w_in(w_in):
    d = w_in.shape[0]
    pts = np.cumsum([384, 128, 32, 256, 256, 256, 256, 256, 768, 256, 4, 4])[:-1]
    mq, mlat, mkr, dq, dk, dv, su, sv, gqkv, ggate, ga, gb = jnp.split(w_in, pts, axis=1)
    mq = mq.reshape(d, N_HEADS, MLA_DN + MLA_DR)
    qn = mq[:, :, :MLA_DN].reshape(d, N_HEADS * MLA_DN)
    qr = mq[:, :, MLA_DN:].reshape(d, N_HEADS * MLA_DR)
    kr4 = jnp.tile(mkr, (1, N_HEADS))
    rep = lambda a: jnp.repeat(a, HEAD_W, axis=1)
    w = jnp.concatenate([qn, qr, _rot_cols(qr), mlat, kr4, _rot_cols(kr4),
                         dq, dk, dv, su, sv, gqkv, ggate, rep(ga), rep(gb)], axis=1)
    assert w.shape[1] == C_TOTAL
    return w.astype(BF16)


def _rope_tables(pos):
    inv = ROPE_THETA ** (-jnp.arange(0, MLA_DR, 2, dtype=F32) / MLA_DR)
    ang = pos.astype(F32)[:, None] * inv[None, :]
    cos = jnp.tile(jnp.cos(ang), (1, 2 * N_HEADS))
    sin = jnp.tile(jnp.sin(ang), (1, 2 * N_HEADS))
    return cos, sin


def _layer_params(l, p):
    row = lambda a: a.reshape(1, -1)
    rep = lambda a: jnp.repeat(a, HEAD_W).reshape(1, -1)
    sgu_w = p['sgu_w'][l]
    return dict(
        w_in=_permute_w_in(p['w_in'][l]),
        g1=row(p['norm1_g'][l]),
        wukt=jnp.transpose(p['mla_w_uk'][l], (1, 2, 0)).astype(BF16),
        wuv=jnp.transpose(p['mla_w_uv'][l], (1, 0, 2)).astype(BF16),
        latg=row(p['mla_lat_g'][l]),
        sgu_w=sgu_w,
        sgu_b=p['sgu_b'][l],
        lng=row(p['sgu_ln_g'][l]), lnb=row(p['sgu_ln_b'][l]),
        lamv=jnp.stack([p['diff_lam_q1'][l], p['diff_lam_k1'][l], p['diff_lam_q2'][l], p['diff_lam_k2'][l]]),
        subg=row(p['diff_sub_g'][l]),
        gcw=p['gdn_conv_w'][l],
        alog=rep(p['gdn_a_log'][l]), dtb=rep(p['gdn_dt_bias'][l]),
        ng=jnp.tile(p['gdn_norm_g'][l], N_HEADS).reshape(1, -1),
        wout=p['w_out'][l].astype(BF16),
        g2=row(p['norm2_g'][l]),
        wup=p['ffn_w_up'][l].astype(BF16),
        fcw=p['ffn_conv_w'][l],
        wdn=p['ffn_w_down'][l].astype(BF16),
        fg=row(p['final_g']),
    )


def _sgu_tables(lp, c):
    w = lp['sgu_w'][:, :c, :c].reshape(N_HEADS * c, c)
    b = jnp.repeat(jnp.transpose(lp['sgu_b'][:, :c]), HEAD_W, axis=1)
    return w, b


def _run_inproj(x, lp, cos, sin, *, tm, sgu_c):
    sw, sb = _sgu_tables(lp, sgu_c)
    return _inproj(x, lp['g1'], lp['w_in'], cos, sin, lp['wukt'], lp['latg'], sw, sb, lp['lng'], lp['lnb'],
                   tm=tm, sgu_c=sgu_c)


def kernel(x_prompt, x_sample, cache_mla_latent, cache_mla_krope, cache_diff_k, cache_diff_v, state_gdn_conv, state_gdn_s, state_ffn_conv, t5_table, final_g, norm1_g, w_in, mla_lat_g, mla_w_uk, mla_w_uv, diff_lam_q1, diff_lam_k1, diff_lam_q2, diff_lam_k2, diff_sub_g, sgu_ln_g, sgu_ln_b, sgu_w, sgu_b, gdn_conv_w, gdn_a_log, gdn_dt_bias, gdn_norm_g, w_out, norm2_g, ffn_w_up, ffn_conv_w, ffn_w_down):
    p = dict(final_g=final_g, norm1_g=norm1_g, w_in=w_in, mla_lat_g=mla_lat_g, mla_w_uk=mla_w_uk,
             mla_w_uv=mla_w_uv, diff_lam_q1=diff_lam_q1, diff_lam_k1=diff_lam_k1, diff_lam_q2=diff_lam_q2,
             diff_lam_k2=diff_lam_k2, diff_sub_g=diff_sub_g, sgu_ln_g=sgu_ln_g, sgu_ln_b=sgu_ln_b,
             sgu_w=sgu_w, sgu_b=sgu_b, gdn_conv_w=gdn_conv_w, gdn_a_log=gdn_a_log, gdn_dt_bias=gdn_dt_bias,
             gdn_norm_g=gdn_norm_g, w_out=w_out, norm2_g=norm2_g, ffn_w_up=ffn_w_up, ffn_conv_w=ffn_conv_w,
             ffn_w_down=ffn_w_down)
    depth = w_in.shape[0]
    bp, sp, d = x_prompt.shape
    bs, ls, _ = x_sample.shape
    past = cache_mla_latent.shape[2]
    d_ff = ffn_w_down.shape[1]
    assert past % CHUNK == 0 and ls <= CHUNK

    tm_p = min(512, sp)
    tq = min(512, sp)
    tg_p = min(256, sp)
    ft = 256
    sgu_cp = min(SGU_CHUNK, sp)
    gdn_cp = min(GDN_CHUNK, sp)

    cos_p, sin_p = _rope_tables(jnp.arange(sp, dtype=jnp.int32))
    pos_s = past + jnp.arange(ls, dtype=jnp.int32)
    cos_s, sin_s = _rope_tables(jnp.tile(pos_s, bs))

    bias_p = _t5_bias(t5_table, tq, tq, (0, tq), (0, 0))
    bias_sc = _t5_bias(t5_table, ls, past, (past,), (0,))
    bias_sn = _t5_bias(t5_table, ls, ls, (past,), (past,))

    xp = x_prompt.reshape(bp * sp, d)
    xs = x_sample.reshape(bs * ls, d)
    zeros_gconv = jnp.zeros((bp, GDN_CONV - 1, 3 * GROUP_W), F32)
    zeros_gs = jnp.zeros((bp, N_HEADS, HEAD_W, HEAD_W), F32)
    zeros_fconv = jnp.zeros((bp, FFN_CONV - 1, 2 * d_ff), F32)

    outs_p, outs_s = [], []
    for l in range(depth):
        lp = _layer_params(l, p)
        lam_init = 0.8 - 0.6 * math.exp(-0.3 * l)
        final = l == depth - 1

        (qm, km, vm, lat, kr, dq, dk, dv, kd, vd, osgu, _vn, gqkv, ggate, gab) = _run_inproj(
            xp, lp, cos_p, sin_p, tm=tm_p, sgu_c=sgu_cp)
        o_mla = _mla_prompt(qm, km, vm, lp['wuv'], batch=bp, seq=sp, tq=tq, tk=tq)
        o_diff = _diff_prompt(dq, kd, vd, bias_p, lp['lamv'], lp['subg'],
                              batch=bp, seq=sp, tq=tq, lam_init=lam_init)
        o_gdn, gconv, gs = _gdn(gqkv, ggate, gab, zeros_gconv, zeros_gs, lp['gcw'], lp['alog'], lp['dtb'],
                                lp['ng'], batch=bp, seq=sp, tg=tg_p, c=gdn_cp)
        xp, fconv = _ffn(xp, o_mla, o_diff, osgu, o_gdn, zeros_fconv, lp['wout'], lp['g2'], lp['wup'],
                         lp['fcw'], lp['wdn'], lp['fg'], batch=bp, seq=sp, tm=tm_p, ft=ft, final=final)
        outs_p.append(dict(
            lat=lat.reshape(bp, sp, MLA_R), kr=kr.reshape(bp, sp, MLA_DR),
            dk=dk.reshape(bp, sp, N_HEADS, 2, DIFF_DH), dv=dv.reshape(bp, sp, N_HEADS, 2 * DIFF_DH),
            gconv=gconv, gs=gs, fconv=fconv))

        (qm, km, vm, lat, kr, dq, dk, dv, kd, vd, osgu, vn, gqkv, ggate, gab) = _run_inproj(
            xs, lp, cos_s, sin_s, tm=bs * ls, sgu_c=min(SGU_CHUNK, ls))
        kc_m = jnp.concatenate([cache_mla_latent[l], jnp.tile(cache_mla_krope[l], (1, 1, N_HEADS))],
                               axis=-1).astype(BF16)
        o_mla = _mla_sample(qm, kc_m, km, vm, lp['wuv'], batch=bs, lq=ls)
        kc_d = cache_diff_k[l].reshape(bs, past, GROUP_W).astype(BF16)
        vc_d = cache_diff_v[l].reshape(bs, past, GROUP_W).astype(BF16)
        o_diff = _diff_sample(dq, kc_d, vc_d, dk, dv, bias_sc, bias_sn, lp['lamv'], lp['subg'],
                              batch=bs, lq=ls, lam_init=lam_init)
        o_gdn, gconv, gs = _gdn(gqkv, ggate, gab, state_gdn_conv[l], state_gdn_s[l], lp['gcw'], lp['alog'],
                                lp['dtb'], lp['ng'], batch=bs, seq=ls, tg=ls, c=min(GDN_CHUNK, ls))
        xs, fconv = _ffn(xs, o_mla, o_diff, osgu, o_gdn, state_ffn_conv[l], lp['wout'], lp['g2'], lp['wup'],
                         lp['fcw'], lp['wdn'], lp['fg'], batch=bs, seq=ls, tm=ls, ft=ft, final=final)
        outs_s.append(dict(
            lat=lat.reshape(bs, ls, MLA_R), kr=kr.reshape(bs, ls, MLA_DR),
            dk=dk.reshape(bs, ls, N_HEADS, 2, DIFF_DH), dv=dv.reshape(bs, ls, N_HEADS, 2 * DIFF_DH),
            gconv=gconv, gs=gs, fconv=fconv, sv=vn.reshape(bs, ls, GROUP_W)))

    st = lambda lst, key: jnp.stack([o[key] for o in lst])
    return (xp.reshape(bp, sp, d), xs.reshape(bs, ls, d),
            st(outs_p, 'lat'), st(outs_p, 'kr'), st(outs_p, 'dk'), st(outs_p, 'dv'),
            st(outs_p, 'gconv'), st(outs_p, 'gs'), st(outs_p, 'fconv'),
            st(outs_s, 'lat'), st(outs_s, 'kr'), st(outs_s, 'dk'), st(outs_s, 'dv'),
            st(outs_s, 'gconv'), st(outs_s, 'gs'), st(outs_s, 'fconv'), st(outs_s, 'sv'))
```

```python
import functools
import math

import numpy as np
import jax
import jax.numpy as jnp
from jax import lax
from jax.experimental import pallas as pl
from jax.experimental.pallas import tpu as pltpu

F32 = jnp.float32
BF16 = jnp.bfloat16

CHUNK = 64
EPS = 1e-6
N_HEADS = 4
HEAD_W = 64
GROUP_W = 256
MLA_DN, MLA_DR, MLA_R = 64, 32, 128
DIFF_DH = 32
ROPE_THETA = 10000.0
SGU_CHUNK = 128
GDN_CHUNK = 64
GDN_CONV = 4
FFN_CONV = 3
T5_BUCKETS = 32
T5_MAX_DIST = 128
NEG = -1e30
LOG2E = math.log2(math.e)
SOLVE_BLOCK = 16

VMEM_LIMIT_BYTES = 56 * 1024 * 1024

C_QN, C_QR, C_QRR, C_LAT, C_KR, C_KRR = 0, 256, 384, 512, 640, 768
C_DQ, C_DK, C_DV, C_SU, C_SV = 896, 1152, 1408, 1664, 1920
C_GQKV, C_GG, C_GA, C_GB = 2176, 2944, 3200, 3456
C_TOTAL = 3712


def _cparams(sem):
    return pltpu.CompilerParams(dimension_semantics=sem, vmem_limit_bytes=VMEM_LIMIT_BYTES)


def _const_spec(shape):
    nd = len(shape)
    return pl.BlockSpec(shape, lambda *_: (0,) * nd, pipeline_mode=pl.Buffered(1))


def _dot(a, b):
    return jnp.dot(a, b, preferred_element_type=F32)


def _dot_nt(a, b):
    return lax.dot_general(a, b, (((1,), (1,)), ((), ())), preferred_element_type=F32)


def _dot_tn(a, b):
    return lax.dot_general(a, b, (((0,), (0,)), ((), ())), preferred_element_type=F32)


def _lane_group(shape, group):
    return lax.broadcasted_iota(jnp.int32, shape, len(shape) - 1) // group


def _silu(x):
    return x * (1.0 / (1.0 + jnp.exp(-x)))


def _shift_rows(x, prev8, s):
    if s == 0:
        return x
    r = pltpu.roll(x, s, axis=0)
    rp = pltpu.roll(prev8, s, axis=0)
    row8 = lax.broadcasted_iota(jnp.int32, rp.shape, 0)
    top = jnp.where(row8 < s, rp, r[0:8])
    if x.shape[0] == 8:
        return top
    return jnp.concatenate([top, r[8:]], axis=0)


def _inproj_kernel(x_ref, g1_ref, w_ref, cos_ref, sin_ref, wukt_ref, latg_ref,
                   sguw_ref, sgub_ref, lng_ref, lnb_ref,
                   qm_ref, km_ref, vm_ref, lat_ref, kr_ref,
                   dq_ref, dk_ref, dv_ref, kd_ref, vd_ref,
                   osgu_ref, vn_ref, gqkv_ref, ggate_ref, gab_ref, *, sgu_c):
    tm = x_ref.shape[0]
    x = x_ref[...]
    h = (x * lax.rsqrt(jnp.mean(x * x, axis=-1, keepdims=True) + EPS) * g1_ref[...]).astype(BF16)

    def sec(off, n):
        return _dot(h, w_ref[:, off:off + n])

    cos = cos_ref[...]
    sin = sin_ref[...]

    mla_scale = (MLA_DN + MLA_DR) ** -0.5 * LOG2E
    zq = sec(C_QN, 512)
    qr = (zq[:, 256:384] * cos + zq[:, 384:512] * sin) * mla_scale
    head_of_lane = _lane_group((1, 128), MLA_DR)
    for hh in range(N_HEADS):
        qn = (zq[:, MLA_DN * hh:MLA_DN * (hh + 1)] * mla_scale).astype(BF16)
        qm_ref[hh, :, 0:128] = _dot(qn, wukt_ref[hh]).astype(BF16)
        qm_ref[hh, :, 128:256] = jnp.where(head_of_lane == hh, qr, 0.0).astype(BF16)

    zl = sec(C_LAT, 384)
    zlat = zl[:, 0:128]
    lat = zlat * lax.rsqrt(jnp.mean(zlat * zlat, axis=-1, keepdims=True) + EPS) * latg_ref[...]
    kr4 = zl[:, 128:256] * cos + zl[:, 256:384] * sin
    lat_ref[...] = lat
    kr_ref[...] = kr4[:, 0:MLA_DR]
    lat_b = lat.astype(BF16)
    km_ref[:, 0:128] = lat_b
    km_ref[:, 128:256] = kr4.astype(BF16)
    vm_ref[:, 0:128] = lat_b
    vm_ref[:, 128:256] = jnp.ones((tm, 128), BF16)

    zd = sec(C_DQ, 768)
    dq_ref[...] = (zd[:, 0:256] * (DIFF_DH ** -0.5 * LOG2E)).astype(BF16)
    dk = zd[:, 256:512]
    dv = zd[:, 512:768]
    dk_ref[...] = dk
    dv_ref[...] = dv
    kd_ref[...] = dk.astype(BF16)
    ones64 = jnp.ones((tm, HEAD_W), F32)
    for hh in range(N_HEADS):
        vd_ref[hh] = jnp.concatenate([dv[:, HEAD_W * hh:HEAD_W * (hh + 1)], ones64], axis=1).astype(BF16)

    zs = sec(C_SU, 512)
    su = zs[:, 0:256]
    sv = zs[:, 256:512]
    mu = jnp.mean(sv, axis=-1, keepdims=True)
    svc = sv - mu
    var = jnp.mean(svc * svc, axis=-1, keepdims=True)
    vn = svc * lax.rsqrt(var + EPS) * lng_ref[...] + lnb_ref[...]
    vn_ref[...] = vn
    c = sgu_c
    wr = lax.broadcasted_iota(jnp.int32, (N_HEADS * c, c), 0) % c
    wc = lax.broadcasted_iota(jnp.int32, (N_HEADS * c, c), 1)
    w4 = jnp.where(wc <= wr, sguw_ref[...], 0.0).astype(BF16)
    hl = _lane_group((1, GROUP_W), HEAD_W)
    sgub = sgub_ref[...]
    for ci in range(tm // c):
        rows = slice(ci * c, (ci + 1) * c)
        m4 = _dot(w4, vn[rows].astype(BF16))
        mix = jnp.where(hl == 0, m4[0:c], 0.0)
        for hh in range(1, N_HEADS):
            mix = mix + jnp.where(hl == hh, m4[hh * c:(hh + 1) * c], 0.0)
        osgu_ref[rows, :] = (su[rows] * (mix + sgub)).astype(BF16)

    zg = sec(C_GQKV, 1536)
    gqkv_ref[...] = zg[:, 0:768]
    ggate_ref[...] = zg[:, 768:1024]
    gab_ref[...] = zg[:, 1024:1536]


def _inproj(x, g1, w, cos, sin, wukt, latg, sguw, sgub, lng, lnb, *, tm, sgu_c):
    t, d = x.shape
    n = t // tm
    npos = cos.shape[0] // tm
    row = lambda wd: pl.BlockSpec((tm, wd), lambda i: (i, 0))
    out_shape = [
        jax.ShapeDtypeStruct((N_HEADS, t, 256), BF16),
        jax.ShapeDtypeStruct((t, 256), BF16),
        jax.ShapeDtypeStruct((t, 256), BF16),
        jax.ShapeDtypeStruct((t, MLA_R), F32),
        jax.ShapeDtypeStruct((t, MLA_DR), F32),
        jax.ShapeDtypeStruct((t, 256), BF16),
        jax.ShapeDtypeStruct((t, 256), F32),
        jax.ShapeDtypeStruct((t, 256), F32),
        jax.ShapeDtypeStruct((t, 256), BF16),
        jax.ShapeDtypeStruct((N_HEADS, t, 128), BF16),
        jax.ShapeDtypeStruct((t, 256), BF16),
        jax.ShapeDtypeStruct((t, 256), F32),
        jax.ShapeDtypeStruct((t, 768), F32),
        jax.ShapeDtypeStruct((t, 256), F32),
        jax.ShapeDtypeStruct((t, 512), F32),
    ]
    out_specs = [
        pl.BlockSpec((N_HEADS, tm, 256), lambda i: (0, i, 0)),
        row(256), row(256), row(MLA_R), row(MLA_DR),
        row(256), row(256), row(256), row(256),
        pl.BlockSpec((N_HEADS, tm, 128), lambda i: (0, i, 0)),
        row(256), row(256), row(768), row(256), row(512),
    ]
    in_specs = [
        row(d), _const_spec(g1.shape), _const_spec(w.shape),
        pl.BlockSpec((tm, 128), lambda i: (i % npos, 0)),
        pl.BlockSpec((tm, 128), lambda i: (i % npos, 0)),
        _const_spec(wukt.shape), _const_spec(latg.shape),
        _const_spec(sguw.shape), _const_spec(sgub.shape),
        _const_spec(lng.shape), _const_spec(lnb.shape),
    ]
    return pl.pallas_call(
        functools.partial(_inproj_kernel, sgu_c=sgu_c),
        grid=(n,), in_specs=in_specs, out_specs=out_specs, out_shape=out_shape,
        compiler_params=_cparams(("parallel",)), name="inproj",
    )(x, g1, w, cos, sin, wukt, latg, sguw, sgub, lng, lnb)


def _t5_thresholds():
    nb = T5_BUCKETS // 2
    max_exact = nb // 2
    ratio = T5_MAX_DIST // max_exact
    thr = []
    for j in range(1, nb - max_exact):
        n = max_exact
        while (n ** (nb - max_exact)) * 1 < (ratio ** j) * (max_exact ** (nb - max_exact)):
            n += 1
        thr.append(n)
    return nb, max_exact, thr


def _bias_kernel(t_ref, o_ref, *, q0s, k0s):
    nb, max_exact, thr = _t5_thresholds()
    tq, tk = o_ref.shape[2], o_ref.shape[3]
    row = lax.broadcasted_iota(jnp.int32, (tq, tk), 0)
    col = lax.broadcasted_iota(jnp.int32, (tq, tk), 1)
    for di, (q0, k0) in enumerate(zip(q0s, k0s)):
        qpos = row + q0
        kpos = col + k0
        rel = kpos - qpos
        n = jnp.abs(rel)
        visible = (kpos // CHUNK) <= (qpos // CHUNK)
        for hh in range(N_HEADS):
            def side(base):
                val = jnp.full((tq, tk), t_ref[base + nb - 1, hh], F32)
                for j in range(len(thr) - 1, -1, -1):
                    val = jnp.where(n < thr[j], t_ref[base + max_exact + j, hh], val)
                for e in range(max_exact - 1, -1, -1):
                    val = jnp.where(n == e, t_ref[base + e, hh], val)
                return val
            b = jnp.where(rel > 0, side(nb), side(0)) - t_ref[nb - 1, hh]
            o_ref[di, hh] = jnp.where(visible, b * LOG2E, NEG)


def _t5_bias(t5_table, tq, tk, q0s, k0s):
    nd = len(q0s)
    return pl.pallas_call(
        functools.partial(_bias_kernel, q0s=tuple(q0s), k0s=tuple(k0s)),
        in_specs=[pl.BlockSpec(memory_space=pltpu.SMEM)],
        out_shape=jax.ShapeDtypeStruct((nd, N_HEADS, tq, tk), F32),
        name="t5_bias",
    )(t5_table)


def _mla_prompt_kernel(q_ref, k_ref, v_ref, wuv_ref, o_ref, *, tq, tk):
    qi = pl.program_id(1)
    r = N_HEADS * tq
    q = q_ref[...].reshape(r, 256)
    qpos = qi * tq + lax.broadcasted_iota(jnp.int32, (r, 1), 0) % tq
    qlim = (qpos // CHUNK + 1) * CHUNK
    col = lax.broadcasted_iota(jnp.int32, (1, tk), 1)

    def step(kb, carry, masked):
        m, acc = carry
        k0 = pl.multiple_of(kb * tk, tk)
        s = _dot_nt(q, k_ref[0, pl.ds(k0, tk), :])
        if masked:
            s = jnp.where(col < qlim - k0, s, NEG)
        mn = jnp.maximum(m, jnp.max(s, axis=1, keepdims=True))
        alpha = jnp.exp2(m - mn)
        p = jnp.exp2(s - mn).astype(BF16)
        acc = alpha * acc + _dot(p, v_ref[0, pl.ds(k0, tk), :])
        return mn, acc

    n_full = (qi * tq + CHUNK) // tk
    n_tot = ((qi + 1) * tq + tk - 1) // tk
    carry = (jnp.full((r, 1), NEG, F32), jnp.zeros((r, 256), F32))
    carry = lax.fori_loop(0, n_full, functools.partial(step, masked=False), carry)
    m, acc = lax.fori_loop(n_full, n_tot, functools.partial(step, masked=True), carry)
    on = (acc[:, 0:128] / acc[:, 128:129]).astype(BF16)
    o_ref[...] = jnp.concatenate(
        [_dot(on[hh * tq:(hh + 1) * tq], wuv_ref[hh]) for hh in range(N_HEADS)], axis=1).astype(BF16)


def _mla_prompt(qm, km, vm, wuv, *, batch, seq, tq, tk):
    nq = seq // tq
    t = batch * seq
    return pl.pallas_call(
        functools.partial(_mla_prompt_kernel, tq=tq, tk=tk),
        grid=(batch, nq),
        in_specs=[
            pl.BlockSpec((N_HEADS, tq, 256), lambda b, i: (0, b * nq + i, 0)),
            pl.BlockSpec((1, seq, 256), lambda b, i: (b, 0, 0)),
            pl.BlockSpec((1, seq, 256), lambda b, i: (b, 0, 0)),
            _const_spec(wuv.shape),
        ],
        out_specs=pl.BlockSpec((tq, 256), lambda b, i: (b * nq + i, 0)),
        out_shape=jax.ShapeDtypeStruct((t, 256), BF16),
        compiler_params=_cparams(("parallel", "arbitrary")), name="mla_prompt",
    )(qm, km.reshape(batch, seq, 256), vm.reshape(batch, seq, 256), wuv)


def _diff_lambda(lam_ref, lam_init):
    l = lam_ref[...]
    a = jnp.sum(l[0:1] * l[1:2], axis=-1, keepdims=True)
    b = jnp.sum(l[2:3] * l[3:4], axis=-1, keepdims=True)
    return jnp.exp(a) - jnp.exp(b) + lam_init


def _stack_q8(q):
    grp = _lane_group((1, GROUP_W), DIFF_DH)
    return jnp.concatenate([jnp.where(grp == g, q, jnp.zeros_like(q)) for g in range(2 * N_HEADS)], axis=0)


def _diff_finish(on, lam, subg, lam_init, tq):
    outs = []
    for hh in range(N_HEADS):
        o = on[(2 * hh) * tq:(2 * hh + 1) * tq] - lam * on[(2 * hh + 1) * tq:(2 * hh + 2) * tq]
        o = o * lax.rsqrt(jnp.mean(o * o, axis=-1, keepdims=True) + 1e-5) * subg
        outs.append(o * (1.0 - lam_init))
    return jnp.concatenate(outs, axis=1)


DIFF_CORNER = 128


def _diff_prompt_kernel(q_ref, k_ref, v_ref, bias_ref, cbias_ref, lam_ref, subg_ref, o_ref, *, tq, lam_init):
    qi = pl.program_id(1)
    r = 2 * N_HEADS * tq
    q8 = _stack_q8(q_ref[...])

    def step(kb, carry, bias):
        m, acc = carry
        k0 = pl.multiple_of(kb * tq, tq)
        s = _dot_nt(q8, k_ref[0, pl.ds(k0, tq), :])
        if bias == "diag":
            s = s + bias_ref[...]
        elif bias == "corner":
            s = jnp.concatenate([s[:, 0:tq - DIFF_CORNER], s[:, tq - DIFF_CORNER:tq] + cbias_ref[...]], axis=1)
        mn = jnp.maximum(m, jnp.max(s, axis=1, keepdims=True))
        alpha = jnp.exp2(m - mn)
        p = jnp.exp2(s - mn).astype(BF16)
        pv = jnp.concatenate(
            [_dot(p[2 * hh * tq:2 * (hh + 1) * tq], v_ref[hh, 0, pl.ds(k0, tq), :])
             for hh in range(N_HEADS)], axis=0)
        return mn, alpha * acc + pv

    carry = (jnp.full((r, 1), NEG, F32), jnp.zeros((r, 128), F32))
    carry = lax.fori_loop(0, jnp.maximum(qi - 1, 0), functools.partial(step, bias=None), carry)
    carry = lax.fori_loop(jnp.maximum(qi - 1, 0), qi, functools.partial(step, bias="corner"), carry)
    m, acc = step(qi, carry, "diag")
    on = acc * (1.0 / pltpu.roll(acc, HEAD_W, axis=1))
    lam = _diff_lambda(lam_ref, lam_init)
    low = lax.broadcasted_iota(jnp.int32, (1, 2 * HEAD_W), 1) < HEAD_W
    halves = []
    for hh in range(N_HEADS):
        o = on[(2 * hh) * tq:(2 * hh + 1) * tq] - lam * on[(2 * hh + 1) * tq:(2 * hh + 2) * tq]
        ms = jnp.sum(jnp.where(low, o * o, 0.0), axis=1, keepdims=True) * (1.0 / HEAD_W)
        halves.append(o * lax.rsqrt(ms + 1e-5))
    pair = lambda a, b: jnp.where(low, a, pltpu.roll(b, HEAD_W, axis=1))
    out = jnp.concatenate([pair(halves[0], halves[1]), pair(halves[2], halves[3])], axis=1)
    o_ref[...] = (out * (subg_ref[...] * (1.0 - lam_init))).astype(BF16)


def _diff_prompt(dq, kd, vd, bias, cbias, lamv, subg, *, batch, seq, tq, lam_init):
    nq = seq // tq
    t = batch * seq
    return pl.pallas_call(
        functools.partial(_diff_prompt_kernel, tq=tq, lam_init=lam_init),
        grid=(batch, nq),
        in_specs=[
            pl.BlockSpec((tq, 256), lambda b, i: (b * nq + i, 0)),
            pl.BlockSpec((1, seq, 256), lambda b, i: (b, 0, 0)),
            pl.BlockSpec((N_HEADS, 1, seq, 128), lambda b, i: (0, b, 0, 0)),
            _const_spec(bias.shape), _const_spec(cbias.shape), _const_spec(lamv.shape), _const_spec(subg.shape),
        ],
        out_specs=pl.BlockSpec((tq, 256), lambda b, i: (b * nq + i, 0)),
        out_shape=jax.ShapeDtypeStruct((t, 256), BF16),
        compiler_params=_cparams(("parallel", "arbitrary")), name="diff_prompt",
    )(dq, kd.reshape(batch, seq, 256), vd.reshape(N_HEADS, batch, seq, 128), bias, cbias, lamv, subg)


def _mla_sample_kernel(q_ref, kc_ref, kn_ref, vn_ref, wuv_ref, o_ref, *, lq):
    r = N_HEADS * lq
    q = q_ref[...].reshape(r, 256)
    kc = kc_ref[0]
    sc = _dot_nt(q, kc)
    sn = _dot_nt(q, kn_ref[...])
    m = jnp.maximum(jnp.max(sc, axis=1, keepdims=True), jnp.max(sn, axis=1, keepdims=True))
    pc = jnp.exp2(sc - m)
    pn = jnp.exp2(sn - m)
    l = jnp.sum(pc, axis=1, keepdims=True) + jnp.sum(pn, axis=1, keepdims=True)
    pv = _dot(pc.astype(BF16), kc[:, 0:128]) + _dot(pn.astype(BF16), vn_ref[:, 0:128])
    on = (pv / l).astype(BF16)
    o_ref[...] = jnp.concatenate(
        [_dot(on[hh * lq:(hh + 1) * lq], wuv_ref[hh]) for hh in range(N_HEADS)], axis=1).astype(BF16)


def _mla_sample(qm, kc, km, vm, wuv, *, batch, lq):
    past = kc.shape[1]
    return pl.pallas_call(
        functools.partial(_mla_sample_kernel, lq=lq),
        grid=(batch,),
        in_specs=[
            pl.BlockSpec((N_HEADS, lq, 256), lambda b: (0, b, 0)),
            pl.BlockSpec((1, past, 256), lambda b: (b, 0, 0)),
            pl.BlockSpec((lq, 256), lambda b: (b, 0)),
            pl.BlockSpec((lq, 256), lambda b: (b, 0)),
            _const_spec(wuv.shape),
        ],
        out_specs=pl.BlockSpec((lq, 256), lambda b: (b, 0)),
        out_shape=jax.ShapeDtypeStruct((batch * lq, 256), BF16),
        compiler_params=_cparams(("parallel",)), name="mla_sample",
    )(qm, kc, km, vm, wuv)


def _diff_sample_kernel(q_ref, kc_ref, vc_ref, kn_ref, vn_ref, bc_ref, bn_ref, lam_ref, subg_ref, o_ref,
                        *, lq, lam_init):
    q8 = _stack_q8(q_ref[...])
    bc = jnp.concatenate([bc_ref[0, hh // 2] for hh in range(2 * N_HEADS)], axis=0)
    bn = jnp.concatenate([bn_ref[0, hh // 2] for hh in range(2 * N_HEADS)], axis=0)
    sc = _dot_nt(q8, kc_ref[0]) + bc
    sn = _dot_nt(q8, kn_ref[...].astype(BF16)) + bn
    m = jnp.maximum(jnp.max(sc, axis=1, keepdims=True), jnp.max(sn, axis=1, keepdims=True))
    pc = jnp.exp2(sc - m)
    pn = jnp.exp2(sn - m)
    l = jnp.sum(pc, axis=1, keepdims=True) + jnp.sum(pn, axis=1, keepdims=True)
    pv = (_dot(pc.astype(BF16), vc_ref[0]) + _dot(pn.astype(BF16), vn_ref[...].astype(BF16))) / l
    on = jnp.concatenate(
        [pv[g * lq:(g + 1) * lq, HEAD_W * (g // 2):HEAD_W * (g // 2 + 1)] for g in range(2 * N_HEADS)], axis=0)
    lam = _diff_lambda(lam_ref, lam_init)
    o_ref[...] = _diff_finish(on, lam, subg_ref[...], lam_init, lq).astype(BF16)


def _diff_sample(dq, kc, vc, dk, dv, bias_c, bias_n, lamv, subg, *, batch, lq, lam_init):
    past = kc.shape[1]
    return pl.pallas_call(
        functools.partial(_diff_sample_kernel, lq=lq, lam_init=lam_init),
        grid=(batch,),
        in_specs=[
            pl.BlockSpec((lq, 256), lambda b: (b, 0)),
            pl.BlockSpec((1, past, 256), lambda b: (b, 0, 0)),
            pl.BlockSpec((1, past, 256), lambda b: (b, 0, 0)),
            pl.BlockSpec((lq, 256), lambda b: (b, 0)),
            pl.BlockSpec((lq, 256), lambda b: (b, 0)),
            _const_spec(bias_c.shape), _const_spec(bias_n.shape),
            _const_spec(lamv.shape), _const_spec(subg.shape),
        ],
        out_specs=pl.BlockSpec((lq, 256), lambda b: (b, 0)),
        out_shape=jax.ShapeDtypeStruct((batch * lq, 256), BF16),
        compiler_params=_cparams(("parallel",)), name="diff_sample",
    )(dq, kc, vc, dk, dv, bias_c, bias_n, lamv, subg)


def _stack4(y, group):
    grp = _lane_group((1,) * (y.ndim - 1) + (y.shape[-1],), group)
    return jnp.concatenate([jnp.where(grp == g, y, 0.0) for g in range(N_HEADS)], axis=-2)


def _diag_sum(f, group):
    rr = f.shape[0] // N_HEADS
    grp = _lane_group((1, f.shape[1]), group)
    out = jnp.where(grp == 0, f[0:rr], 0.0)
    for g in range(1, N_HEADS):
        out = out + jnp.where(grp == g, f[g * rr:(g + 1) * rr], 0.0)
    return out


def _bmm(x, y, group):
    return _dot(x, _stack4(y, group))


def _bdot(x, w):
    return lax.dot_general(x, w, (((2,), (1,)), ((0,), (0,))), preferred_element_type=F32)


def _bdot_nt(x, w):
    return lax.dot_general(x, w, (((2,), (2,)), ((0,), (0,))), preferred_element_type=F32)


def _bbmm(x, y, group):
    return _bdot(x, _stack4(y, group))


def _split3(x):
    hi = x.astype(BF16)
    r1 = x - hi.astype(F32)
    mid = r1.astype(BF16)
    lo = (r1 - mid.astype(F32)).astype(BF16)
    return hi, mid, lo


def _unit_lower_inverse(a, c):
    shape = (1, c, N_HEADS * c)
    i = lax.broadcasted_iota(jnp.int32, shape, 1)
    j = lax.broadcasted_iota(jnp.int32, shape, 2) % c
    eye = jnp.where(i == j, 1.0, 0.0)
    blockdiag = (i // SOLVE_BLOCK) == (j // SOLVE_BLOCK)
    ad = jnp.where(blockdiag, a, 0.0)
    p = _bbmm(ad, ad, c)
    rinv = eye - ad
    rinv = rinv + _bbmm(rinv, p, c)
    for _ in range(2):
        p = _bbmm(p, p, c)
        rinv = rinv + _bbmm(rinv, p, c)
    if c <= SOLVE_BLOCK:
        return rinv
    mm = _bbmm(rinv, a - ad, c)
    t = eye - mm
    p = _bbmm(mm, mm, c)
    t = t + _bbmm(t, p, c)
    nblk = c // SOLVE_BLOCK
    span = 4
    while span < nblk:
        p = _bbmm(p, p, c)
        t = t + _bbmm(t, p, c)
        span *= 2
    return _bbmm(t, rinv, c)


def _to_square(x, c):
    if c == HEAD_W:
        return x
    return jnp.concatenate([x[..., HEAD_W * hh:HEAD_W * hh + c] for hh in range(N_HEADS)], axis=-1)


def _gdn_kernel(qkv_ref, gate_ref, gab_ref, buf_ref, s0_ref, cw_ref, alog_ref, dtb_ref, ng_ref,
                o_ref, conv_ref, s_ref, prev_scr, s_scr, *, tg, c):
    j = pl.program_id(1)
    nj = pl.num_programs(1)

    @pl.when(j == 0)
    def _():
        prev_scr[...] = jnp.zeros_like(prev_scr)
        prev_scr[8 - (GDN_CONV - 1):8, :] = buf_ref[0]
        for hh in range(N_HEADS):
            s_scr[:, HEAD_W * hh:HEAD_W * (hh + 1)] = s0_ref[0, hh]

    x = qkv_ref[...]
    prev8 = prev_scr[...]
    cw = cw_ref[...]
    y = x * cw[GDN_CONV - 1:GDN_CONV]
    for s in range(1, GDN_CONV):
        y = y + _shift_rows(x, prev8, s) * cw[GDN_CONV - 1 - s:GDN_CONV - s]
    prev_scr[...] = x[tg - 8:tg]

    @pl.when(j == nj - 1)
    def _():
        conv_ref[0] = x[tg - (GDN_CONV - 1):tg]

    y = _silu(y)
    q = y[:, 0:256]
    k = y[:, 256:512]
    v = y[:, 512:768]
    gones = jnp.where(_lane_group((GROUP_W, GROUP_W), HEAD_W)
                      == lax.broadcasted_iota(jnp.int32, (GROUP_W, GROUP_W), 0) // HEAD_W,
                      1.0, 0.0).astype(BF16)

    def head_sum(z):
        z_hi, z_mid, z_lo = _split3(z)
        return _dot(z_hi, gones) + _dot(z_mid, gones) + _dot(z_lo, gones)

    q = q * lax.rsqrt(head_sum(q * q) + 1e-6) * (HEAD_W ** -0.5)
    k = k * lax.rsqrt(head_sum(k * k) + 1e-6)
    gab = gab_ref[...]
    za = gab[:, 0:256] + dtb_ref[...]
    softplus = jnp.maximum(za, 0.0) + jnp.log(1.0 + jnp.exp(-jnp.abs(za)))
    g = -jnp.exp(alog_ref[...]) * softplus
    beta = 1.0 / (1.0 + jnp.exp(-gab[:, 256:512]))

    nc = tg // c
    sq = (1, c, N_HEADS * c)
    ri = lax.broadcasted_iota(jnp.int32, sq, 1)
    cj = lax.broadcasted_iota(jnp.int32, sq, 2) % c
    tr = lax.broadcasted_iota(jnp.int32, (tg, tg), 0)
    tc = lax.broadcasted_iota(jnp.int32, (tg, tg), 1)
    tri_b = jnp.where((tc <= tr) & (tc // c == tr // c), 1.0, 0.0).astype(BF16)
    g_hi, g_mid, g_lo = _split3(g)
    decay2 = _dot(tri_b, g_hi) + _dot(tri_b, g_mid) + _dot(tri_b, g_lo)
    chunked = lambda z: z.reshape(nc, c, z.shape[-1])
    qb, kb_, vb, bb, decay = chunked(q), chunked(k), chunked(v), chunked(beta), chunked(decay2)
    dsq = _to_square(decay, c)
    drow = jnp.sum(jnp.where(ri == cj, dsq, 0.0), axis=1, keepdims=True)
    lm = jnp.where(cj <= ri, jnp.exp(jnp.where(cj <= ri, dsq - drow, 0.0)), 0.0)
    kbeta = kb_ * bb
    kq = _bdot_nt(jnp.concatenate([kbeta, qb], axis=1), _stack4(kb_, HEAD_W))
    a_mat = jnp.where(cj < ri, kq[:, 0:c] * lm, 0.0)
    qk_all = kq[:, c:2 * c] * lm
    edec = jnp.exp(decay)
    tinv = _unit_lower_inverse(a_mat, c)
    uw = _bdot(tinv, jnp.concatenate([_stack4(vb * bb, HEAD_W), _stack4(kbeta * edec, HEAD_W)], axis=2))
    dlast = decay[:, c - 1:c]
    qd_all = qb * edec
    kt_all = kb_ * jnp.exp(dlast - decay)
    gl_all = jnp.exp(dlast)
    pre = [(uw[n, :, 0:GROUP_W], uw[n, :, GROUP_W:2 * GROUP_W], qk_all[n], qd_all[n], kt_all[n], gl_all[n])
           for n in range(nc)]

    s_mat = s_scr[...]
    outs = []
    for (u, w, qk, qd, kt, glast) in pre:
        ws = _dot(jnp.concatenate([w, qd], axis=0), _stack4(s_mat, HEAD_W))
        vnew = u - ws[0:c]
        outs.append(ws[c:2 * c] + _bmm(qk, vnew, HEAD_W))
        s_mat = s_mat * glast + _diag_sum(_dot_tn(kt, vnew), HEAD_W)
    s_scr[...] = s_mat

    o = outs[0] if len(outs) == 1 else jnp.concatenate(outs, axis=0)
    o = o * lax.rsqrt(head_sum(o * o) * (1.0 / HEAD_W) + EPS) * ng_ref[...]
    o_ref[...] = (o * _silu(gate_ref[...])).astype(BF16)

    @pl.when(j == nj - 1)
    def _():
        s_fin = s_scr[...]
        for hh in range(N_HEADS):
            s_ref[0, hh] = s_fin[:, HEAD_W * hh:HEAD_W * (hh + 1)]


def _gdn(gqkv, ggate, gab, buf, s0, cw, alog, dtb, ng, *, batch, seq, tg, c):
    nj = seq // tg
    t = batch * seq
    row = lambda wd: pl.BlockSpec((tg, wd), lambda b, j: (b * nj + j, 0))
    return pl.pallas_call(
        functools.partial(_gdn_kernel, tg=tg, c=c),
        grid=(batch, nj),
        in_specs=[
            row(768), row(256), row(512),
            pl.BlockSpec((1, GDN_CONV - 1, 768), lambda b, j: (b, 0, 0)),
            pl.BlockSpec((1, N_HEADS, HEAD_W, HEAD_W), lambda b, j: (b, 0, 0, 0)),
            _const_spec(cw.shape), _const_spec(alog.shape), _const_spec(dtb.shape), _const_spec(ng.shape),
        ],
        out_specs=[
            row(256),
            pl.BlockSpec((1, GDN_CONV - 1, 768), lambda b, j: (b, 0, 0)),
            pl.BlockSpec((1, N_HEADS, HEAD_W, HEAD_W), lambda b, j: (b, 0, 0, 0)),
        ],
        out_shape=[
            jax.ShapeDtypeStruct((t, 256), BF16),
            jax.ShapeDtypeStruct((batch, GDN_CONV - 1, 768), F32),
            jax.ShapeDtypeStruct((batch, N_HEADS, HEAD_W, HEAD_W), F32),
        ],
        scratch_shapes=[pltpu.VMEM((8, 768), F32), pltpu.VMEM((HEAD_W, GROUP_W), F32)],
        compiler_params=_cparams(("parallel", "arbitrary")), name="gdn",
    )(gqkv, ggate, gab, buf, s0, cw, alog, dtb, ng)


def _ffn_kernel(x_ref, om_ref, od_ref, os_ref, og_ref, buf_ref, wout_ref, g2_ref, wup_ref, cw_ref, wdn_ref,
                fg_ref, y_ref, conv_ref, prev_scr, *, tm, d_ff, ft, final):
    j = pl.program_id(1)
    nj = pl.num_programs(1)

    @pl.when(j == 0)
    def _():
        prev_scr[...] = jnp.zeros_like(prev_scr)
        prev_scr[8 - (FFN_CONV - 1):8, :] = buf_ref[0]

    mixed = jnp.concatenate([om_ref[...], od_ref[...], os_ref[...], og_ref[...]], axis=1)
    x1 = x_ref[...] + _dot(mixed, wout_ref[...])
    h2 =(x1 * lax.rsqrt(jnp.mean(x1 * x1, axis=-1, keepdims=True) + EPS) * g2_ref[...]).astype(BF16)

    def conv_cols(off):
        a = _dot(h2, wup_ref[:, off:off + ft])
        prev8 = prev_scr[:, off:off + ft]
        cw = cw_ref[:, off:off + ft]
        y = a * cw[FFN_CONV - 1:FFN_CONV]
        for s in range(1, FFN_CONV):
            y = y + _shift_rows(a, prev8, s) * cw[FFN_CONV - 1 - s:FFN_CONV - s]
        prev_scr[:, off:off + ft] = a[tm - 8:tm]
        return y

    acc = x1
    for fi in range(d_ff // ft):
        gate = conv_cols(fi * ft)
        up = conv_cols(d_ff + fi * ft)
        act = (_silu(gate) * up).astype(BF16)
        acc = acc + _dot(act, wdn_ref[fi * ft:(fi + 1) * ft, :])

    if final:
        acc = acc * lax.rsqrt(jnp.mean(acc * acc, axis=-1, keepdims=True) + EPS) * fg_ref[...]
    y_ref[...] = acc

    @pl.when(j == nj - 1)
    def _():
        conv_ref[0] = prev_scr[8 - (FFN_CONV - 1):8, :]


def _ffn(x, om, od, osg, og, buf, wout, g2, wup, cw, wdn, fg, *, batch, seq, tm, ft, final):
    nj = seq // tm
    t, d = x.shape
    d_ff = wdn.shape[0]
    row = lambda wd: pl.BlockSpec((tm, wd), lambda b, j: (b * nj + j, 0))
    return pl.pallas_call(
        functools.partial(_ffn_kernel, tm=tm, d_ff=d_ff, ft=ft, final=final),
        grid=(batch, nj),
        in_specs=[
            row(d), row(256), row(256), row(256), row(256),
            pl.BlockSpec((1, FFN_CONV - 1, 2 * d_ff), lambda b, j: (b, 0, 0)),
            _const_spec(wout.shape), _const_spec(g2.shape), _const_spec(wup.shape),
            _const_spec(cw.shape), _const_spec(wdn.shape), _const_spec(fg.shape),
        ],
        out_specs=[row(d), pl.BlockSpec((1, FFN_CONV - 1, 2 * d_ff), lambda b, j: (b, 0, 0))],
        out_shape=[jax.ShapeDtypeStruct((t, d), F32),
                   jax.ShapeDtypeStruct((batch, FFN_CONV - 1, 2 * d_ff), F32)],
        scratch_shapes=[pltpu.VMEM((8, 2 * d_ff), F32)],
        compiler_params=_cparams(("parallel", "arbitrary")), name="ffn",
    )(x, om, od, osg, og, buf, wout, g2, wup, cw, wdn, fg)


def _rot_cols(w):
    d = w.shape[0]
    wg = w.reshape(d, -1, 2, MLA_DR // 2)
    return jnp.concatenate([-wg[:, :, 1:2], wg[:, :, 0:1]], axis=2).reshape(d, -1)


def _permute_w_in(w_in):
    d = w_in.shape[0]
    pts = np.cumsum([384, 128, 32, 256, 256, 256, 256, 256, 768, 256, 4, 4])[:-1]
    mq, mlat, mkr, dq, dk, dv, su, sv, gqkv, ggate, ga, gb = jnp.split(w_in, pts, axis=1)
    mq = mq.reshape(d, N_HEADS, MLA_DN + MLA_DR)
    qn = mq[:, :, :MLA_DN].reshape(d, N_HEADS * MLA_DN)
    qr = mq[:, :, MLA_DN:].reshape(d, N_HEADS * MLA_DR)
    kr4 = jnp.tile(mkr, (1, N_HEADS))
    rep = lambda a: jnp.repeat(a, HEAD_W, axis=1)
    w = jnp.concatenate([qn, qr, _rot_cols(qr), mlat, kr4, _rot_cols(kr4),
                         dq, dk, dv, su, sv, gqkv, ggate, rep(ga), rep(gb)], axis=1)
    assert w.shape[1] == C_TOTAL
    return w.astype(BF16)


def _rope_tables(pos):
    inv = ROPE_THETA ** (-jnp.arange(0, MLA_DR, 2, dtype=F32) / MLA_DR)
    ang = pos.astype(F32)[:, None] * inv[None, :]
    cos = jnp.tile(jnp.cos(ang), (1, 2 * N_HEADS))
    sin = jnp.tile(jnp.sin(ang), (1, 2 * N_HEADS))
    return cos, sin


def _layer_params(l, p):
    row = lambda a: a.reshape(1, -1)
    rep = lambda a: jnp.repeat(a, HEAD_W).reshape(1, -1)
    sgu_w = p['sgu_w'][l]
    return dict(
        w_in=_permute_w_in(p['w_in'][l]),
        g1=row(p['norm1_g'][l]),
        wukt=jnp.transpose(p['mla_w_uk'][l], (1, 2, 0)).astype(BF16),
        wuv=jnp.transpose(p['mla_w_uv'][l], (1, 0, 2)).astype(BF16),
        latg=row(p['mla_lat_g'][l]),
        sgu_w=sgu_w,
        sgu_b=p['sgu_b'][l],
        lng=row(p['sgu_ln_g'][l]), lnb=row(p['sgu_ln_b'][l]),
        lamv=jnp.stack([p['diff_lam_q1'][l], p['diff_lam_k1'][l], p['diff_lam_q2'][l], p['diff_lam_k2'][l]]),
        subg=row(p['diff_sub_g'][l]),
        gcw=p['gdn_conv_w'][l],
        alog=rep(p['gdn_a_log'][l]), dtb=rep(p['gdn_dt_bias'][l]),
        ng=jnp.tile(p['gdn_norm_g'][l], N_HEADS).reshape(1, -1),
        wout=p['w_out'][l].astype(BF16),
        g2=row(p['norm2_g'][l]),
        wup=p['ffn_w_up'][l].astype(BF16),
        fcw=p['ffn_conv_w'][l],
        wdn=p['ffn_w_down'][l].astype(BF16),
        fg=row(p['final_g']),
    )


def _sgu_tables(lp, c):
    w = lp['sgu_w'][:, :c, :c].reshape(N_HEADS * c, c)
    b = jnp.repeat(jnp.transpose(lp['sgu_b'][:, :c]), HEAD_W, axis=1)
    return w, b


def _run_inproj(x, lp, cos, sin, *, tm, sgu_c):
    sw, sb = _sgu_tables(lp, sgu_c)
    return _inproj(x, lp['g1'], lp['w_in'], cos, sin, lp['wukt'], lp['latg'], sw, sb, lp['lng'], lp['lnb'],
                   tm=tm, sgu_c=sgu_c)


def kernel(x_prompt, x_sample, cache_mla_latent, cache_mla_krope, cache_diff_k, cache_diff_v, state_gdn_conv, state_gdn_s, state_ffn_conv, t5_table, final_g, norm1_g, w_in, mla_lat_g, mla_w_uk, mla_w_uv, diff_lam_q1, diff_lam_k1, diff_lam_q2, diff_lam_k2, diff_sub_g, sgu_ln_g, sgu_ln_b, sgu_w, sgu_b, gdn_conv_w, gdn_a_log, gdn_dt_bias, gdn_norm_g, w_out, norm2_g, ffn_w_up, ffn_conv_w, ffn_w_down):
    p = dict(final_g=final_g, norm1_g=norm1_g, w_in=w_in, mla_lat_g=mla_lat_g, mla_w_uk=mla_w_uk,
             mla_w_uv=mla_w_uv, diff_lam_q1=diff_lam_q1, diff_lam_k1=diff_lam_k1, diff_lam_q2=diff_lam_q2,
             diff_lam_k2=diff_lam_k2, diff_sub_g=diff_sub_g, sgu_ln_g=sgu_ln_g, sgu_ln_b=sgu_ln_b,
             sgu_w=sgu_w, sgu_b=sgu_b, gdn_conv_w=gdn_conv_w, gdn_a_log=gdn_a_log, gdn_dt_bias=gdn_dt_bias,
             gdn_norm_g=gdn_norm_g, w_out=w_out, norm2_g=norm2_g, ffn_w_up=ffn_w_up, ffn_conv_w=ffn_conv_w,
             ffn_w_down=ffn_w_down)
    depth = w_in.shape[0]
    bp, sp, d = x_prompt.shape
    bs, ls, _ = x_sample.shape
    past = cache_mla_latent.shape[2]
    d_ff = ffn_w_down.shape[1]
    assert past % CHUNK == 0 and ls <= CHUNK

    tm_p = min(512, sp)
    tq = min(512, sp)
    tg_p = min(512, sp)
    ft = 256
    sgu_cp = min(SGU_CHUNK, sp)
    gdn_cp = min(GDN_CHUNK, sp)

    cos_p, sin_p = _rope_tables(jnp.arange(sp, dtype=jnp.int32))
    pos_s = past + jnp.arange(ls, dtype=jnp.int32)
    cos_s, sin_s = _rope_tables(jnp.tile(pos_s, bs))

    assert tq >= DIFF_CORNER
    bias_p = jnp.repeat(_t5_bias(t5_table, tq, tq, (0,), (0,))[0], 2, axis=0).reshape(2 * N_HEADS * tq, tq)
    corner = _t5_bias(t5_table, DIFF_CORNER, DIFF_CORNER, (DIFF_CORNER,), (0,))[0]
    bias_pc = jnp.repeat(jnp.pad(corner, ((0, 0), (0, tq - DIFF_CORNER), (0, 0))), 2, axis=0).reshape(
        2 * N_HEADS * tq, DIFF_CORNER)
    bias_sc = _t5_bias(t5_table, ls, past, (past,), (0,))
    bias_sn = _t5_bias(t5_table, ls, ls, (past,), (past,))

    xp = x_prompt.reshape(bp * sp, d)
    xs = x_sample.reshape(bs * ls, d)
    zeros_gconv = jnp.zeros((bp, GDN_CONV - 1, 3 * GROUP_W), F32)
    zeros_gs = jnp.zeros((bp, N_HEADS, HEAD_W, HEAD_W), F32)
    zeros_fconv = jnp.zeros((bp, FFN_CONV - 1, 2 * d_ff), F32)

    outs_p, outs_s = [], []
    for l in range(depth):
        lp = _layer_params(l, p)
        lam_init = 0.8 - 0.6 * math.exp(-0.3 * l)
        final = l == depth - 1

        (qm, km, vm, lat, kr, dq, dk, dv, kd, vd, osgu, _vn, gqkv, ggate, gab) = _run_inproj(
            xp, lp, cos_p, sin_p, tm=tm_p, sgu_c=sgu_cp)
        o_mla = _mla_prompt(qm, km, vm, lp['wuv'], batch=bp, seq=sp, tq=tq, tk=tq)
        o_diff = _diff_prompt(dq, kd, vd, bias_p, bias_pc, lp['lamv'], jnp.tile(lp['subg'], (1, N_HEADS)),
                              batch=bp, seq=sp, tq=tq, lam_init=lam_init)
        o_gdn, gconv, gs = _gdn(gqkv, ggate, gab, zeros_gconv, zeros_gs, lp['gcw'], lp['alog'], lp['dtb'],
                                lp['ng'], batch=bp, seq=sp, tg=tg_p, c=gdn_cp)
        xp, fconv = _ffn(xp, o_mla, o_diff, osgu, o_gdn, zeros_fconv, lp['wout'], lp['g2'], lp['wup'],
                         lp['fcw'], lp['wdn'], lp['fg'], batch=bp, seq=sp, tm=tm_p, ft=ft, final=final)
        outs_p.append(dict(
            lat=lat.reshape(bp, sp, MLA_R), kr=kr.reshape(bp, sp, MLA_DR),
            dk=dk.reshape(bp, sp, N_HEADS, 2, DIFF_DH), dv=dv.reshape(bp, sp, N_HEADS, 2 * DIFF_DH),
            gconv=gconv, gs=gs, fconv=fconv))

        (qm, km, vm, lat, kr, dq, dk, dv, kd, vd, osgu, vn, gqkv, ggate, gab) = _run_inproj(
            xs, lp, cos_s, sin_s, tm=bs * ls, sgu_c=min(SGU_CHUNK, ls))
        kc_m = jnp.concatenate([cache_mla_latent[l], jnp.tile(cache_mla_krope[l], (1, 1, N_HEADS))],
                               axis=-1).astype(BF16)
        o_mla = _mla_sample(qm, kc_m, km, vm, lp['wuv'], batch=bs, lq=ls)
        kc_d = cache_diff_k[l].reshape(bs, past, GROUP_W).astype(BF16)
        vc_d = cache_diff_v[l].reshape(bs, past, GROUP_W).astype(BF16)
        o_diff = _diff_sample(dq, kc_d, vc_d, dk, dv, bias_sc, bias_sn, lp['lamv'], lp['subg'],
                              batch=bs, lq=ls, lam_init=lam_init)
        o_gdn, gconv, gs = _gdn(gqkv, ggate, gab, state_gdn_conv[l], state_gdn_s[l], lp['gcw'], lp['alog'],
                                lp['dtb'], lp['ng'], batch=bs, seq=ls, tg=ls, c=min(GDN_CHUNK, ls))
        xs, fconv = _ffn(xs, o_mla, o_diff, osgu, o_gdn, state_ffn_conv[l], lp['wout'], lp['g2'], lp['wup'],
                         lp['fcw'], lp['wdn'], lp['fg'], batch=bs, seq=ls, tm=ls, ft=ft, final=final)
        outs_s.append(dict(
            lat=lat.reshape(bs, ls, MLA_R), kr=kr.reshape(bs, ls, MLA_DR),
            dk=dk.reshape(bs, ls, N_HEADS, 2, DIFF_DH), dv=dv.reshape(bs, ls, N_HEADS, 2 * DIFF_DH),
            gconv=gconv, gs=gs, fconv=fconv, sv=vn.reshape(bs, ls, GROUP_W)))

    st = lambda lst, key: jnp.stack([o[key] for o in lst])
    return (xp.reshape(bp, sp, d), xs.reshape(bs, ls, d),
            st(outs_p, 'lat'), st(outs_p, 'kr'), st(outs_p, 'dk'), st(outs_p, 'dv'),
            st(outs_p, 'gconv'), st(outs_p, 'gs'), st(outs_p, 'fconv'),
            st(outs_s, 'lat'), st(outs_s, 'kr'), st(outs_s, 'dk'), st(outs_s, 'dv'),
            st(outs_s, 'gconv'), st(outs_s, 'gs'), st(outs_s, 'fconv'), st(outs_s, 'sv'))
```

```python
import functools
import math

import numpy as np
import jax
import jax.numpy as jnp
from jax import lax
from jax.experimental import pallas as pl
from jax.experimental.pallas import tpu as pltpu

F32 = jnp.float32
BF16 = jnp.bfloat16

CHUNK = 64
EPS = 1e-6
N_HEADS = 4
HEAD_W = 64
GROUP_W = 256
MLA_DN, MLA_DR, MLA_R = 64, 32, 128
DIFF_DH = 32
ROPE_THETA = 10000.0
SGU_CHUNK = 128
GDN_CHUNK = 64
GDN_CONV = 4
FFN_CONV = 3
T5_BUCKETS = 32
T5_MAX_DIST = 128
NEG = -1e30
LOG2E = math.log2(math.e)
SOLVE_BLOCK = 16

VMEM_LIMIT_BYTES = 56 * 1024 * 1024

C_QN, C_QR, C_QRR, C_LAT, C_KR, C_KRR = 0, 256, 384, 512, 640, 768
C_DQ, C_DK, C_DV, C_SU, C_SV = 896, 1152, 1408, 1664, 1920
C_GQKV, C_GG, C_GA, C_GB = 2176, 2944, 3200, 3456
C_TOTAL = 3712


def _cparams(sem):
    return pltpu.CompilerParams(dimension_semantics=sem, vmem_limit_bytes=VMEM_LIMIT_BYTES)


def _const_spec(shape):
    nd = len(shape)
    return pl.BlockSpec(shape, lambda *_: (0,) * nd, pipeline_mode=pl.Buffered(1))


def _dot(a, b):
    return jnp.dot(a, b, preferred_element_type=F32)


def _dot_nt(a, b):
    return lax.dot_general(a, b, (((1,), (1,)), ((), ())), preferred_element_type=F32)


def _dot_tn(a, b):
    return lax.dot_general(a, b, (((0,), (0,)), ((), ())), preferred_element_type=F32)


def _lane_group(shape, group):
    return lax.broadcasted_iota(jnp.int32, shape, len(shape) - 1) // group


def _silu(x):
    return x * (1.0 / (1.0 + jnp.exp(-x)))


def _shift_rows(x, prev8, s):
    if s == 0:
        return x
    r = pltpu.roll(x, s, axis=0)
    rp = pltpu.roll(prev8, s, axis=0)
    row8 = lax.broadcasted_iota(jnp.int32, rp.shape, 0)
    top = jnp.where(row8 < s, rp, r[0:8])
    if x.shape[0] == 8:
        return top
    return jnp.concatenate([top, r[8:]], axis=0)


def _inproj_kernel(x_ref, g1_ref, w_ref, cos_ref, sin_ref, wukt_ref, latg_ref,
                   sguw_ref, sgub_ref, lng_ref, lnb_ref,
                   qm_ref, km_ref, vm_ref, lat_ref, kr_ref,
                   dq_ref, dk_ref, dv_ref, kd_ref, vd_ref,
                   osgu_ref, vn_ref, gqkv_ref, ggate_ref, gab_ref, *, sgu_c):
    tm = x_ref.shape[0]
    x = x_ref[...]
    h = (x * lax.rsqrt(jnp.mean(x * x, axis=-1, keepdims=True) + EPS) * g1_ref[...]).astype(BF16)

    def sec(off, n):
        return _dot(h, w_ref[:, off:off + n])

    cos = cos_ref[...]
    sin = sin_ref[...]

    mla_scale = (MLA_DN + MLA_DR) ** -0.5 * LOG2E
    zq = sec(C_QN, 512)
    qr = (zq[:, 256:384] * cos + zq[:, 384:512] * sin) * mla_scale
    head_of_lane = _lane_group((1, 128), MLA_DR)
    for hh in range(N_HEADS):
        qn = (zq[:, MLA_DN * hh:MLA_DN * (hh + 1)] * mla_scale).astype(BF16)
        qm_ref[hh, :, 0:128] = _dot(qn, wukt_ref[hh]).astype(BF16)
        qm_ref[hh, :, 128:256] = jnp.where(head_of_lane == hh, qr, 0.0).astype(BF16)

    zl = sec(C_LAT, 384)
    zlat = zl[:, 0:128]
    lat = zlat * lax.rsqrt(jnp.mean(zlat * zlat, axis=-1, keepdims=True) + EPS) * latg_ref[...]
    kr4 = zl[:, 128:256] * cos + zl[:, 256:384] * sin
    lat_ref[...] = lat
    kr_ref[...] = kr4[:, 0:MLA_DR]
    lat_b = lat.astype(BF16)
    km_ref[:, 0:128] = lat_b
    km_ref[:, 128:256] = kr4.astype(BF16)
    vm_ref[:, 0:128] = lat_b
    vm_ref[:, 128:256] = jnp.ones((tm, 128), BF16)

    zd = sec(C_DQ, 768)
    dq_ref[...] = (zd[:, 0:256] * (DIFF_DH ** -0.5 * LOG2E)).astype(BF16)
    dk = zd[:, 256:512]
    dv = zd[:, 512:768]
    dk_ref[...] = dk
    dv_ref[...] = dv
    kd_ref[...] = dk.astype(BF16)
    ones64 = jnp.ones((tm, HEAD_W), F32)
    for hh in range(N_HEADS):
        vd_ref[hh] = jnp.concatenate([dv[:, HEAD_W * hh:HEAD_W * (hh + 1)], ones64], axis=1).astype(BF16)

    zs = sec(C_SU, 512)
    su = zs[:, 0:256]
    sv = zs[:, 256:512]
    mu = jnp.mean(sv, axis=-1, keepdims=True)
    svc = sv - mu
    var = jnp.mean(svc * svc, axis=-1, keepdims=True)
    vn = svc * lax.rsqrt(var + EPS) * lng_ref[...] + lnb_ref[...]
    vn_ref[...] = vn
    c = sgu_c
    wr = lax.broadcasted_iota(jnp.int32, (N_HEADS * c, c), 0) % c
    wc = lax.broadcasted_iota(jnp.int32, (N_HEADS * c, c), 1)
    w4 = jnp.where(wc <= wr, sguw_ref[...], 0.0).astype(BF16)
    hl = _lane_group((1, GROUP_W), HEAD_W)
    sgub = sgub_ref[...]
    for ci in range(tm // c):
        rows = slice(ci * c, (ci + 1) * c)
        m4 = _dot(w4, vn[rows].astype(BF16))
        mix = jnp.where(hl == 0, m4[0:c], 0.0)
        for hh in range(1, N_HEADS):
            mix = mix + jnp.where(hl == hh, m4[hh * c:(hh + 1) * c], 0.0)
        osgu_ref[rows, :] = (su[rows] * (mix + sgub)).astype(BF16)

    zg = sec(C_GQKV, 1536)
    gqkv_ref[...] = zg[:, 0:768]
    ggate_ref[...] = zg[:, 768:1024]
    gab_ref[...] = zg[:, 1024:1536]


def _inproj(x, g1, w, cos, sin, wukt, latg, sguw, sgub, lng, lnb, *, tm, sgu_c):
    t, d = x.shape
    n = t // tm
    npos = cos.shape[0] // tm
    row = lambda wd: pl.BlockSpec((tm, wd), lambda i: (i, 0))
    out_shape = [
        jax.ShapeDtypeStruct((N_HEADS, t, 256), BF16),
        jax.ShapeDtypeStruct((t, 256), BF16),
        jax.ShapeDtypeStruct((t, 256), BF16),
        jax.ShapeDtypeStruct((t, MLA_R), F32),
        jax.ShapeDtypeStruct((t, MLA_DR), F32),
        jax.ShapeDtypeStruct((t, 256), BF16),
        jax.ShapeDtypeStruct((t, 256), F32),
        jax.ShapeDtypeStruct((t, 256), F32),
        jax.ShapeDtypeStruct((t, 256), BF16),
        jax.ShapeDtypeStruct((N_HEADS, t, 128), BF16),
        jax.ShapeDtypeStruct((t, 256), BF16),
        jax.ShapeDtypeStruct((t, 256), F32),
        jax.ShapeDtypeStruct((t, 768), F32),
        jax.ShapeDtypeStruct((t, 256), F32),
        jax.ShapeDtypeStruct((t, 512), F32),
    ]
    out_specs = [
        pl.BlockSpec((N_HEADS, tm, 256), lambda i: (0, i, 0)),
        row(256), row(256), row(MLA_R), row(MLA_DR),
        row(256), row(256), row(256), row(256),
        pl.BlockSpec((N_HEADS, tm, 128), lambda i: (0, i, 0)),
        row(256), row(256), row(768), row(256), row(512),
    ]
    in_specs = [
        row(d), _const_spec(g1.shape), _const_spec(w.shape),
        pl.BlockSpec((tm, 128), lambda i: (i % npos, 0)),
        pl.BlockSpec((tm, 128), lambda i: (i % npos, 0)),
        _const_spec(wukt.shape), _const_spec(latg.shape),
        _const_spec(sguw.shape), _const_spec(sgub.shape),
        _const_spec(lng.shape), _const_spec(lnb.shape),
    ]
    return pl.pallas_call(
        functools.partial(_inproj_kernel, sgu_c=sgu_c),
        grid=(n,), in_specs=in_specs, out_specs=out_specs, out_shape=out_shape,
        compiler_params=_cparams(("parallel",)), name="inproj",
    )(x, g1, w, cos, sin, wukt, latg, sguw, sgub, lng, lnb)


def _t5_thresholds():
    nb = T5_BUCKETS // 2
    max_exact = nb // 2
    ratio = T5_MAX_DIST // max_exact
    thr = []
    for j in range(1, nb - max_exact):
        n = max_exact
        while (n ** (nb - max_exact)) * 1 < (ratio ** j) * (max_exact ** (nb - max_exact)):
            n += 1
        thr.append(n)
    return nb, max_exact, thr


def _bias_kernel(t_ref, o_ref, *, q0s, k0s):
    nb, max_exact, thr = _t5_thresholds()
    tq, tk = o_ref.shape[2], o_ref.shape[3]
    row = lax.broadcasted_iota(jnp.int32, (tq, tk), 0)
    col = lax.broadcasted_iota(jnp.int32, (tq, tk), 1)
    for di, (q0, k0) in enumerate(zip(q0s, k0s)):
        qpos = row + q0
        kpos = col + k0
        rel = kpos - qpos
        n = jnp.abs(rel)
        visible = (kpos // CHUNK) <= (qpos // CHUNK)
        for hh in range(N_HEADS):
            def side(base):
                val = jnp.full((tq, tk), t_ref[base + nb - 1, hh], F32)
                for j in range(len(thr) - 1, -1, -1):
                    val = jnp.where(n < thr[j], t_ref[base + max_exact + j, hh], val)
                for e in range(max_exact - 1, -1, -1):
                    val = jnp.where(n == e, t_ref[base + e, hh], val)
                return val
            b = jnp.where(rel > 0, side(nb), side(0)) - t_ref[nb - 1, hh]
            o_ref[di, hh] = jnp.where(visible, b * LOG2E, NEG)


def _t5_bias(t5_table, tq, tk, q0s, k0s):
    nd = len(q0s)
    return pl.pallas_call(
        functools.partial(_bias_kernel, q0s=tuple(q0s), k0s=tuple(k0s)),
        in_specs=[pl.BlockSpec(memory_space=pltpu.SMEM)],
        out_shape=jax.ShapeDtypeStruct((nd, N_HEADS, tq, tk), F32),
        name="t5_bias",
    )(t5_table)


def _mla_prompt_kernel(q_ref, k_ref, v_ref, wuv_ref, o_ref, *, tq, tk):
    qi = pl.program_id(1)
    r = N_HEADS * tq
    q = q_ref[...].reshape(r, 256)
    qpos = qi * tq + lax.broadcasted_iota(jnp.int32, (r, 1), 0) % tq
    qlim = (qpos // CHUNK + 1) * CHUNK
    col = lax.broadcasted_iota(jnp.int32, (1, tk), 1)

    def step(kb, carry, masked):
        m, acc = carry
        k0 = pl.multiple_of(kb * tk, tk)
        s = _dot_nt(q, k_ref[0, pl.ds(k0, tk), :])
        if masked:
            s = jnp.where(col < qlim - k0, s, NEG)
        mn = jnp.maximum(m, jnp.max(s, axis=1, keepdims=True))
        alpha = jnp.exp2(m - mn)
        p = jnp.exp2(s - mn).astype(BF16)
        acc = alpha * acc + _dot(p, v_ref[0, pl.ds(k0, tk), :])
        return mn, acc

    n_full = (qi * tq + CHUNK) // tk
    n_tot = ((qi + 1) * tq + tk - 1) // tk
    carry = (jnp.full((r, 1), NEG, F32), jnp.zeros((r, 256), F32))
    carry = lax.fori_loop(0, n_full, functools.partial(step, masked=False), carry)
    m, acc = lax.fori_loop(n_full, n_tot, functools.partial(step, masked=True), carry)
    on = (acc[:, 0:128] / acc[:, 128:129]).astype(BF16)
    o_ref[...] = jnp.concatenate(
        [_dot(on[hh * tq:(hh + 1) * tq], wuv_ref[hh]) for hh in range(N_HEADS)], axis=1).astype(BF16)


def _mla_prompt(qm, km, vm, wuv, *, batch, seq, tq, tk):
    nq = seq // tq
    t = batch * seq
    return pl.pallas_call(
        functools.partial(_mla_prompt_kernel, tq=tq, tk=tk),
        grid=(batch, nq),
        in_specs=[
            pl.BlockSpec((N_HEADS, tq, 256), lambda b, i: (0, b * nq + i, 0)),
            pl.BlockSpec((1, seq, 256), lambda b, i: (b, 0, 0)),
            pl.BlockSpec((1, seq, 256), lambda b, i: (b, 0, 0)),
            _const_spec(wuv.shape),
        ],
        out_specs=pl.BlockSpec((tq, 256), lambda b, i: (b * nq + i, 0)),
        out_shape=jax.ShapeDtypeStruct((t, 256), BF16),
        compiler_params=_cparams(("parallel", "arbitrary")), name="mla_prompt",
    )(qm, km.reshape(batch, seq, 256), vm.reshape(batch, seq, 256), wuv)


def _diff_lambda(lam_ref, lam_init):
    l = lam_ref[...]
    a = jnp.sum(l[0:1] * l[1:2], axis=-1, keepdims=True)
    b = jnp.sum(l[2:3] * l[3:4], axis=-1, keepdims=True)
    return jnp.exp(a) - jnp.exp(b) + lam_init


def _stack_q8(q):
    grp = _lane_group((1, GROUP_W), DIFF_DH)
    return jnp.concatenate([jnp.where(grp == g, q, jnp.zeros_like(q)) for g in range(2 * N_HEADS)], axis=0)


def _diff_finish(on, lam, subg, lam_init, tq):
    outs = []
    for hh in range(N_HEADS):
        o = on[(2 * hh) * tq:(2 * hh + 1) * tq] - lam * on[(2 * hh + 1) * tq:(2 * hh + 2) * tq]
        o = o * lax.rsqrt(jnp.mean(o * o, axis=-1, keepdims=True) + 1e-5) * subg
        outs.append(o * (1.0 - lam_init))
    return jnp.concatenate(outs, axis=1)


DIFF_CORNER = 128


def _diff_prompt_kernel(q_ref, k_ref, v_ref, bias_ref, cbias_ref, lam_ref, subg_ref, o_ref, *, tq, lam_init):
    qi = pl.program_id(1)
    r = 2 * N_HEADS * tq
    q8 = _stack_q8(q_ref[...])

    def step(kb, carry, bias):
        m, acc = carry
        k0 = pl.multiple_of(kb * tq, tq)
        s = _dot_nt(q8, k_ref[0, pl.ds(k0, tq), :])
        if bias == "diag":
            s = s + bias_ref[...]
        elif bias == "corner":
            s = jnp.concatenate([s[:, 0:tq - DIFF_CORNER], s[:, tq - DIFF_CORNER:tq] + cbias_ref[...]], axis=1)
        mn = jnp.maximum(m, jnp.max(s, axis=1, keepdims=True))
        alpha = jnp.exp2(m - mn)
        p = jnp.exp2(s - mn).astype(BF16)
        pv = jnp.concatenate(
            [_dot(p[2 * hh * tq:2 * (hh + 1) * tq], v_ref[hh, 0, pl.ds(k0, tq), :])
             for hh in range(N_HEADS)], axis=0)
        return mn, alpha * acc + pv

    carry = (jnp.full((r, 1), NEG, F32), jnp.zeros((r, 128), F32))
    carry = lax.fori_loop(0, jnp.maximum(qi - 1, 0), functools.partial(step, bias=None), carry)
    carry = lax.fori_loop(jnp.maximum(qi - 1, 0), qi, functools.partial(step, bias="corner"), carry)
    m, acc = step(qi, carry, "diag")
    on = acc * (1.0 / pltpu.roll(acc, HEAD_W, axis=1))
    lam = _diff_lambda(lam_ref, lam_init)
    low = lax.broadcasted_iota(jnp.int32, (1, 2 * HEAD_W), 1) < HEAD_W
    halves = []
    for hh in range(N_HEADS):
        o = on[(2 * hh) * tq:(2 * hh + 1) * tq] - lam * on[(2 * hh + 1) * tq:(2 * hh + 2) * tq]
        ms = jnp.sum(jnp.where(low, o * o, 0.0), axis=1, keepdims=True) * (1.0 / HEAD_W)
        halves.append(o * lax.rsqrt(ms + 1e-5))
    pair = lambda a, b: jnp.where(low, a, pltpu.roll(b, HEAD_W, axis=1))
    out = jnp.concatenate([pair(halves[0], halves[1]), pair(halves[2], halves[3])], axis=1)
    o_ref[...] = (out * (subg_ref[...] * (1.0 - lam_init))).astype(BF16)


def _diff_prompt(dq, kd, vd, bias, cbias, lamv, subg, *, batch, seq, tq, lam_init):
    nq = seq // tq
    t = batch * seq
    return pl.pallas_call(
        functools.partial(_diff_prompt_kernel, tq=tq, lam_init=lam_init),
        grid=(batch, nq),
        in_specs=[
            pl.BlockSpec((tq, 256), lambda b, i: (b * nq + i, 0)),
            pl.BlockSpec((1, seq, 256), lambda b, i: (b, 0, 0)),
            pl.BlockSpec((N_HEADS, 1, seq, 128), lambda b, i: (0, b, 0, 0)),
            _const_spec(bias.shape), _const_spec(cbias.shape), _const_spec(lamv.shape), _const_spec(subg.shape),
        ],
        out_specs=pl.BlockSpec((tq, 256), lambda b, i: (b * nq + i, 0)),
        out_shape=jax.ShapeDtypeStruct((t, 256), BF16),
        compiler_params=_cparams(("parallel", "arbitrary")), name="diff_prompt",
    )(dq, kd.reshape(batch, seq, 256), vd.reshape(N_HEADS, batch, seq, 128), bias, cbias, lamv, subg)


def _mla_sample_kernel(q_ref, kc_ref, kn_ref, vn_ref, wuv_ref, o_ref, *, lq):
    r = N_HEADS * lq
    q = q_ref[...].reshape(r, 256)
    kc = kc_ref[0]
    sc = _dot_nt(q, kc)
    sn = _dot_nt(q, kn_ref[...])
    m = jnp.maximum(jnp.max(sc, axis=1, keepdims=True), jnp.max(sn, axis=1, keepdims=True))
    pc = jnp.exp2(sc - m)
    pn = jnp.exp2(sn - m)
    l = jnp.sum(pc, axis=1, keepdims=True) + jnp.sum(pn, axis=1, keepdims=True)
    pv = _dot(pc.astype(BF16), kc[:, 0:128]) + _dot(pn.astype(BF16), vn_ref[:, 0:128])
    on = (pv / l).astype(BF16)
    o_ref[...] = jnp.concatenate(
        [_dot(on[hh * lq:(hh + 1) * lq], wuv_ref[hh]) for hh in range(N_HEADS)], axis=1).astype(BF16)


def _mla_sample(qm, kc, km, vm, wuv, *, batch, lq):
    past = kc.shape[1]
    return pl.pallas_call(
        functools.partial(_mla_sample_kernel, lq=lq),
        grid=(batch,),
        in_specs=[
            pl.BlockSpec((N_HEADS, lq, 256), lambda b: (0, b, 0)),
            pl.BlockSpec((1, past, 256), lambda b: (b, 0, 0)),
            pl.BlockSpec((lq, 256), lambda b: (b, 0)),
            pl.BlockSpec((lq, 256), lambda b: (b, 0)),
            _const_spec(wuv.shape),
        ],
        out_specs=pl.BlockSpec((lq, 256), lambda b: (b, 0)),
        out_shape=jax.ShapeDtypeStruct((batch * lq, 256), BF16),
        compiler_params=_cparams(("parallel",)), name="mla_sample",
    )(qm, kc, km, vm, wuv)


def _diff_sample_kernel(q_ref, kc_ref, vc_ref, kn_ref, vn_ref, bc_ref, bn_ref, lam_ref, subg_ref, o_ref,
                        *, lq, lam_init):
    q8 = _stack_q8(q_ref[...])
    bc = jnp.concatenate([bc_ref[0, hh // 2] for hh in range(2 * N_HEADS)], axis=0)
    bn = jnp.concatenate([bn_ref[0, hh // 2] for hh in range(2 * N_HEADS)], axis=0)
    sc = _dot_nt(q8, kc_ref[0]) + bc
    sn = _dot_nt(q8, kn_ref[...].astype(BF16)) + bn
    m = jnp.maximum(jnp.max(sc, axis=1, keepdims=True), jnp.max(sn, axis=1, keepdims=True))
    pc = jnp.exp2(sc - m)
    pn = jnp.exp2(sn - m)
    l = jnp.sum(pc, axis=1, keepdims=True) + jnp.sum(pn, axis=1, keepdims=True)
    pv = (_dot(pc.astype(BF16), vc_ref[0]) + _dot(pn.astype(BF16), vn_ref[...].astype(BF16))) / l
    on = jnp.concatenate(
        [pv[g * lq:(g + 1) * lq, HEAD_W * (g // 2):HEAD_W * (g // 2 + 1)] for g in range(2 * N_HEADS)], axis=0)
    lam = _diff_lambda(lam_ref, lam_init)
    o_ref[...] = _diff_finish(on, lam, subg_ref[...], lam_init, lq).astype(BF16)


def _diff_sample(dq, kc, vc, dk, dv, bias_c, bias_n, lamv, subg, *, batch, lq, lam_init):
    past = kc.shape[1]
    return pl.pallas_call(
        functools.partial(_diff_sample_kernel, lq=lq, lam_init=lam_init),
        grid=(batch,),
        in_specs=[
            pl.BlockSpec((lq, 256), lambda b: (b, 0)),
            pl.BlockSpec((1, past, 256), lambda b: (b, 0, 0)),
            pl.BlockSpec((1, past, 256), lambda b: (b, 0, 0)),
            pl.BlockSpec((lq, 256), lambda b: (b, 0)),
            pl.BlockSpec((lq, 256), lambda b: (b, 0)),
            _const_spec(bias_c.shape), _const_spec(bias_n.shape),
            _const_spec(lamv.shape), _const_spec(subg.shape),
        ],
        out_specs=pl.BlockSpec((lq, 256), lambda b: (b, 0)),
        out_shape=jax.ShapeDtypeStruct((batch * lq, 256), BF16),
        compiler_params=_cparams(("parallel",)), name="diff_sample",
    )(dq, kc, vc, dk, dv, bias_c, bias_n, lamv, subg)


def _stack4(y, group):
    grp = _lane_group((1,) * (y.ndim - 1) + (y.shape[-1],), group)
    return jnp.concatenate([jnp.where(grp == g, y, 0.0) for g in range(N_HEADS)], axis=-2)


def _diag_sum(f, group):
    rr = f.shape[0] // N_HEADS
    grp = _lane_group((1, f.shape[1]), group)
    out = jnp.where(grp == 0, f[0:rr], 0.0)
    for g in range(1, N_HEADS):
        out = out + jnp.where(grp == g, f[g * rr:(g + 1) * rr], 0.0)
    return out


def _bmm(x, y, group):
    return _dot(x, _stack4(y, group))


def _bdot(x, w):
    return lax.dot_general(x, w, (((2,), (1,)), ((0,), (0,))), preferred_element_type=F32)


def _bdot_nt(x, w):
    return lax.dot_general(x, w, (((2,), (2,)), ((0,), (0,))), preferred_element_type=F32)


def _bbmm(x, y, group):
    return _bdot(x, _stack4(y, group))


def _split3(x):
    hi = x.astype(BF16)
    r1 = x - hi.astype(F32)
    mid = r1.astype(BF16)
    lo = (r1 - mid.astype(F32)).astype(BF16)
    return hi, mid, lo


def _unit_lower_inverse(a, c):
    shape = (1, c, N_HEADS * c)
    i = lax.broadcasted_iota(jnp.int32, shape, 1)
    j = lax.broadcasted_iota(jnp.int32, shape, 2) % c
    eye = jnp.where(i == j, 1.0, 0.0)
    blockdiag = (i // SOLVE_BLOCK) == (j // SOLVE_BLOCK)
    ad = jnp.where(blockdiag, a, 0.0)
    p = _bbmm(ad, ad, c)
    rinv = eye - ad
    rinv = rinv + _bbmm(rinv, p, c)
    for _ in range(2):
        p = _bbmm(p, p, c)
        rinv = rinv + _bbmm(rinv, p, c)
    if c <= SOLVE_BLOCK:
        return rinv
    mm = _bbmm(rinv, a - ad, c)
    t = eye - mm
    p = _bbmm(mm, mm, c)
    t = t + _bbmm(t, p, c)
    nblk = c // SOLVE_BLOCK
    span = 4
    while span < nblk:
        p = _bbmm(p, p, c)
        t = t + _bbmm(t, p, c)
        span *= 2
    return _bbmm(t, rinv, c)


def _to_square(x, c):
    if c == HEAD_W:
        return x
    return jnp.concatenate([x[..., HEAD_W * hh:HEAD_W * hh + c] for hh in range(N_HEADS)], axis=-1)


def _gdn_kernel(qkv_ref, gate_ref, gab_ref, buf_ref, s0_ref, cw_ref, alog_ref, dtb_ref, ng_ref,
                o_ref, conv_ref, s_ref, prev_scr, s_scr, *, tg, c):
    j = pl.program_id(1)
    nj = pl.num_programs(1)

    @pl.when(j == 0)
    def _():
        prev_scr[...] = jnp.zeros_like(prev_scr)
        prev_scr[8 - (GDN_CONV - 1):8, :] = buf_ref[0]
        for hh in range(N_HEADS):
            s_scr[:, HEAD_W * hh:HEAD_W * (hh + 1)] = s0_ref[0, hh]

    x = qkv_ref[...]
    prev8 = prev_scr[...]
    cw = cw_ref[...]
    y = x * cw[GDN_CONV - 1:GDN_CONV]
    for s in range(1, GDN_CONV):
        y = y + _shift_rows(x, prev8, s) * cw[GDN_CONV - 1 - s:GDN_CONV - s]
    prev_scr[...] = x[tg - 8:tg]

    @pl.when(j == nj - 1)
    def _():
        conv_ref[0] = x[tg - (GDN_CONV - 1):tg]

    y = _silu(y)
    q = y[:, 0:256]
    k = y[:, 256:512]
    v = y[:, 512:768]
    gones = jnp.where(_lane_group((GROUP_W, GROUP_W), HEAD_W)
                      == lax.broadcasted_iota(jnp.int32, (GROUP_W, GROUP_W), 0) // HEAD_W,
                      1.0, 0.0).astype(BF16)

    def head_sum(z):
        z_hi, z_mid, z_lo = _split3(z)
        return _dot(z_hi, gones) + _dot(z_mid, gones) + _dot(z_lo, gones)

    q = q * lax.rsqrt(head_sum(q * q) + 1e-6) * (HEAD_W ** -0.5)
    k = k * lax.rsqrt(head_sum(k * k) + 1e-6)
    gab = gab_ref[...]
    za = gab[:, 0:256] + dtb_ref[...]
    softplus = jnp.maximum(za, 0.0) + jnp.log(1.0 + jnp.exp(-jnp.abs(za)))
    g = -jnp.exp(alog_ref[...]) * softplus
    beta = 1.0 / (1.0 + jnp.exp(-gab[:, 256:512]))

    nc = tg // c
    sq = (1, c, N_HEADS * c)
    ri = lax.broadcasted_iota(jnp.int32, sq, 1)
    cj = lax.broadcasted_iota(jnp.int32, sq, 2) % c
    tr = lax.broadcasted_iota(jnp.int32, (tg, tg), 0)
    tc = lax.broadcasted_iota(jnp.int32, (tg, tg), 1)
    tri_b = jnp.where((tc <= tr) & (tc // c == tr // c), 1.0, 0.0).astype(BF16)
    g_hi, g_mid, g_lo = _split3(g)
    decay2 = _dot(tri_b, g_hi) + _dot(tri_b, g_mid) + _dot(tri_b, g_lo)
    chunked = lambda z: z.reshape(nc, c, z.shape[-1])
    qb, kb_, vb, bb, decay = chunked(q), chunked(k), chunked(v), chunked(beta), chunked(decay2)
    dsq = _to_square(decay, c)
    drow = jnp.sum(jnp.where(ri == cj, dsq, 0.0), axis=1, keepdims=True)
    lm = jnp.where(cj <= ri, jnp.exp(jnp.where(cj <= ri, dsq - drow, 0.0)), 0.0)
    kbeta = kb_ * bb
    kq = _bdot_nt(jnp.concatenate([kbeta, qb], axis=1), _stack4(kb_, HEAD_W))
    a_mat = jnp.where(cj < ri, kq[:, 0:c] * lm, 0.0)
    qk_all = kq[:, c:2 * c] * lm
    edec = jnp.exp(decay)
    tinv = _unit_lower_inverse(a_mat, c)
    uw = _bdot(tinv, jnp.concatenate([_stack4(vb * bb, HEAD_W), _stack4(kbeta * edec, HEAD_W)], axis=2))
    dlast = decay[:, c - 1:c]
    qd_all = qb * edec
    kt_all = kb_ * jnp.exp(dlast - decay)
    gl_all = jnp.exp(dlast)
    pre = [(uw[n, :, 0:GROUP_W], uw[n, :, GROUP_W:2 * GROUP_W], qk_all[n], qd_all[n], kt_all[n], gl_all[n])
           for n in range(nc)]

    s_mat = s_scr[...]
    outs = []
    for (u, w, qk, qd, kt, glast) in pre:
        ws = _dot(jnp.concatenate([w, qd], axis=0), _stack4(s_mat, HEAD_W))
        vnew = u - ws[0:c]
        outs.append(ws[c:2 * c] + _bmm(qk, vnew, HEAD_W))
        s_mat = s_mat * glast + _diag_sum(_dot_tn(kt, vnew), HEAD_W)
    s_scr[...] = s_mat

    o = outs[0] if len(outs) == 1 else jnp.concatenate(outs, axis=0)
    o = o * lax.rsqrt(head_sum(o * o) * (1.0 / HEAD_W) + EPS) * ng_ref[...]
    o_ref[...] = (o * _silu(gate_ref[...])).astype(BF16)

    @pl.when(j == nj - 1)
    def _():
        s_fin = s_scr[...]
        for hh in range(N_HEADS):
            s_ref[0, hh] = s_fin[:, HEAD_W * hh:HEAD_W * (hh + 1)]


def _gdn(gqkv, ggate, gab, buf, s0, cw, alog, dtb, ng, *, batch, seq, tg, c):
    nj = seq // tg
    t = batch * seq
    row = lambda wd: pl.BlockSpec((tg, wd), lambda b, j: (b * nj + j, 0))
    return pl.pallas_call(
        functools.partial(_gdn_kernel, tg=tg, c=c),
        grid=(batch, nj),
        in_specs=[
            row(768), row(256), row(512),
            pl.BlockSpec((1, GDN_CONV - 1, 768), lambda b, j: (b, 0, 0)),
            pl.BlockSpec((1, N_HEADS, HEAD_W, HEAD_W), lambda b, j: (b, 0, 0, 0)),
            _const_spec(cw.shape), _const_spec(alog.shape), _const_spec(dtb.shape), _const_spec(ng.shape),
        ],
        out_specs=[
            row(256),
            pl.BlockSpec((1, GDN_CONV - 1, 768), lambda b, j: (b, 0, 0)),
            pl.BlockSpec((1, N_HEADS, HEAD_W, HEAD_W), lambda b, j: (b, 0, 0, 0)),
        ],
        out_shape=[
            jax.ShapeDtypeStruct((t, 256), BF16),
            jax.ShapeDtypeStruct((batch, GDN_CONV - 1, 768), F32),
            jax.ShapeDtypeStruct((batch, N_HEADS, HEAD_W, HEAD_W), F32),
        ],
        scratch_shapes=[pltpu.VMEM((8, 768), F32), pltpu.VMEM((HEAD_W, GROUP_W), F32)],
        compiler_params=_cparams(("parallel", "arbitrary")), name="gdn",
    )(gqkv, ggate, gab, buf, s0, cw, alog, dtb, ng)


def _ffn_kernel(x_ref, om_ref, od_ref, os_ref, og_ref, buf_ref, wout_ref, g2_ref, wup_ref, cw_ref, wdn_ref,
                fg_ref, y_ref, conv_ref, prev_scr, act_scr, *, tm, d_ff, ft, final):
    j = pl.program_id(1)
    nj = pl.num_programs(1)

    @pl.when(j == 0)
    def _():
        prev_scr[...] = jnp.zeros_like(prev_scr)
        prev_scr[8 - (FFN_CONV - 1):8, :] = buf_ref[0]

    mixed = jnp.concatenate([om_ref[...], od_ref[...], os_ref[...], og_ref[...]], axis=1)
    x1 = x_ref[...] + _dot(mixed, wout_ref[...])
    h2 =(x1 * lax.rsqrt(jnp.mean(x1 * x1, axis=-1, keepdims=True) + EPS) * g2_ref[...]).astype(BF16)

    def conv_cols(off):
        a = _dot(h2, wup_ref[:, off:off + ft])
        prev8 = prev_scr[:, off:off + ft]
        cw = cw_ref[:, off:off + ft]
        y = a * cw[FFN_CONV - 1:FFN_CONV]
        for s in range(1, FFN_CONV):
            y = y + _shift_rows(a, prev8, s) * cw[FFN_CONV - 1 - s:FFN_CONV - s]
        prev_scr[:, off:off + ft] = a[tm - 8:tm]
        return y

    for fi in range(d_ff // ft):
        gate = conv_cols(fi * ft)
        up = conv_cols(d_ff + fi * ft)
        act_scr[:, fi * ft:(fi + 1) * ft] = (_silu(gate) * up).astype(BF16)
    acc = x1 + _dot(act_scr[...], wdn_ref[...])

    if final:
        acc = acc * lax.rsqrt(jnp.mean(acc * acc, axis=-1, keepdims=True) + EPS) * fg_ref[...]
    y_ref[...] = acc

    @pl.when(j == nj - 1)
    def _():
        conv_ref[0] = prev_scr[8 - (FFN_CONV - 1):8, :]


def _ffn(x, om, od, osg, og, buf, wout, g2, wup, cw, wdn, fg, *, batch, seq, tm, ft, final):
    nj = seq // tm
    t, d = x.shape
    d_ff = wdn.shape[0]
    row = lambda wd: pl.BlockSpec((tm, wd), lambda b, j: (b * nj + j, 0))
    return pl.pallas_call(
        functools.partial(_ffn_kernel, tm=tm, d_ff=d_ff, ft=ft, final=final),
        grid=(batch, nj),
        in_specs=[
            row(d), row(256), row(256), row(256), row(256),
            pl.BlockSpec((1, FFN_CONV - 1, 2 * d_ff), lambda b, j: (b, 0, 0)),
            _const_spec(wout.shape), _const_spec(g2.shape), _const_spec(wup.shape),
            _const_spec(cw.shape), _const_spec(wdn.shape), _const_spec(fg.shape),
        ],
        out_specs=[row(d), pl.BlockSpec((1, FFN_CONV - 1, 2 * d_ff), lambda b, j: (b, 0, 0))],
        out_shape=[jax.ShapeDtypeStruct((t, d), F32),
                   jax.ShapeDtypeStruct((batch, FFN_CONV - 1, 2 * d_ff), F32)],
        scratch_shapes=[pltpu.VMEM((8, 2 * d_ff), F32), pltpu.VMEM((tm, d_ff), BF16)],
        compiler_params=_cparams(("parallel", "arbitrary")), name="ffn",
    )(x, om, od, osg, og, buf, wout, g2, wup, cw, wdn, fg)


def _rot_cols(w):
    d = w.shape[0]
    wg = w.reshape(d, -1, 2, MLA_DR // 2)
    return jnp.concatenate([-wg[:, :, 1:2], wg[:, :, 0:1]], axis=2).reshape(d, -1)


def _permute_w_in(w_in):
    d = w_in.shape[0]
    pts = np.cumsum([384, 128, 32, 256, 256, 256, 256, 256, 768, 256, 4, 4])[:-1]
    mq, mlat, mkr, dq, dk, dv, su, sv, gqkv, ggate, ga, gb = jnp.split(w_in, pts, axis=1)
    mq = mq.reshape(d, N_HEADS, MLA_DN + MLA_DR)
    qn = mq[:, :, :MLA_DN].reshape(d, N_HEADS * MLA_DN)
    qr = mq[:, :, MLA_DN:].reshape(d, N_HEADS * MLA_DR)
    kr4 = jnp.tile(mkr, (1, N_HEADS))
    rep = lambda a: jnp.repeat(a, HEAD_W, axis=1)
    w = jnp.concatenate([qn, qr, _rot_cols(qr), mlat, kr4, _rot_cols(kr4),
                         dq, dk, dv, su, sv, gqkv, ggate, rep(ga), rep(gb)], axis=1)
    assert w.shape[1] == C_TOTAL
    return w.astype(BF16)


def _rope_tables(pos):
    inv = ROPE_THETA ** (-jnp.arange(0, MLA_DR, 2, dtype=F32) / MLA_DR)
    ang = pos.astype(F32)[:, None] * inv[None, :]
    cos = jnp.tile(jnp.cos(ang), (1, 2 * N_HEADS))
    sin = jnp.tile(jnp.sin(ang), (1, 2 * N_HEADS))
    return cos, sin


def _layer_params(l, p):
    row = lambda a: a.reshape(1, -1)
    rep = lambda a: jnp.repeat(a, HEAD_W).reshape(1, -1)
    sgu_w = p['sgu_w'][l]
    return dict(
        w_in=_permute_w_in(p['w_in'][l]),
        g1=row(p['norm1_g'][l]),
        wukt=jnp.transpose(p['mla_w_uk'][l], (1, 2, 0)).astype(BF16),
        wuv=jnp.transpose(p['mla_w_uv'][l], (1, 0, 2)).astype(BF16),
        latg=row(p['mla_lat_g'][l]),
        sgu_w=sgu_w,
        sgu_b=p['sgu_b'][l],
        lng=row(p['sgu_ln_g'][l]), lnb=row(p['sgu_ln_b'][l]),
        lamv=jnp.stack([p['diff_lam_q1'][l], p['diff_lam_k1'][l], p['diff_lam_q2'][l], p['diff_lam_k2'][l]]),
        subg=row(p['diff_sub_g'][l]),
        gcw=p['gdn_conv_w'][l],
        alog=rep(p['gdn_a_log'][l]), dtb=rep(p['gdn_dt_bias'][l]),
        ng=jnp.tile(p['gdn_norm_g'][l], N_HEADS).reshape(1, -1),
        wout=p['w_out'][l].astype(BF16),
        g2=row(p['norm2_g'][l]),
        wup=p['ffn_w_up'][l].astype(BF16),
        fcw=p['ffn_conv_w'][l],
        wdn=p['ffn_w_down'][l].astype(BF16),
        fg=row(p['final_g']),
    )


def _sgu_tables(lp, c):
    w = lp['sgu_w'][:, :c, :c].reshape(N_HEADS * c, c)
    b = jnp.repeat(jnp.transpose(lp['sgu_b'][:, :c]), HEAD_W, axis=1)
    return w, b


def _run_inproj(x, lp, cos, sin, *, tm, sgu_c):
    sw, sb = _sgu_tables(lp, sgu_c)
    return _inproj(x, lp['g1'], lp['w_in'], cos, sin, lp['wukt'], lp['latg'], sw, sb, lp['lng'], lp['lnb'],
                   tm=tm, sgu_c=sgu_c)


def kernel(x_prompt, x_sample, cache_mla_latent, cache_mla_krope, cache_diff_k, cache_diff_v, state_gdn_conv, state_gdn_s, state_ffn_conv, t5_table, final_g, norm1_g, w_in, mla_lat_g, mla_w_uk, mla_w_uv, diff_lam_q1, diff_lam_k1, diff_lam_q2, diff_lam_k2, diff_sub_g, sgu_ln_g, sgu_ln_b, sgu_w, sgu_b, gdn_conv_w, gdn_a_log, gdn_dt_bias, gdn_norm_g, w_out, norm2_g, ffn_w_up, ffn_conv_w, ffn_w_down):
    p = dict(final_g=final_g, norm1_g=norm1_g, w_in=w_in, mla_lat_g=mla_lat_g, mla_w_uk=mla_w_uk,
             mla_w_uv=mla_w_uv, diff_lam_q1=diff_lam_q1, diff_lam_k1=diff_lam_k1, diff_lam_q2=diff_lam_q2,
             diff_lam_k2=diff_lam_k2, diff_sub_g=diff_sub_g, sgu_ln_g=sgu_ln_g, sgu_ln_b=sgu_ln_b,
             sgu_w=sgu_w, sgu_b=sgu_b, gdn_conv_w=gdn_conv_w, gdn_a_log=gdn_a_log, gdn_dt_bias=gdn_dt_bias,
             gdn_norm_g=gdn_norm_g, w_out=w_out, norm2_g=norm2_g, ffn_w_up=ffn_w_up, ffn_conv_w=ffn_conv_w,
             ffn_w_down=ffn_w_down)
    depth = w_in.shape[0]
    bp, sp, d = x_prompt.shape
    bs, ls, _ = x_sample.shape
    past = cache_mla_latent.shape[2]
    d_ff = ffn_w_down.shape[1]
    assert past % CHUNK == 0 and ls <= CHUNK

    tm_p = min(512, sp)
    tq = min(512, sp)
    tg_p = min(512, sp)
    ft = 256
    sgu_cp = min(SGU_CHUNK, sp)
    gdn_cp = min(GDN_CHUNK, sp)

    cos_p, sin_p = _rope_tables(jnp.arange(sp, dtype=jnp.int32))
    pos_s = past + jnp.arange(ls, dtype=jnp.int32)
    cos_s, sin_s = _rope_tables(jnp.tile(pos_s, bs))

    assert tq >= DIFF_CORNER
    bias_p = jnp.repeat(_t5_bias(t5_table, tq, tq, (0,), (0,))[0], 2, axis=0).reshape(2 * N_HEADS * tq, tq)
    corner = _t5_bias(t5_table, DIFF_CORNER, DIFF_CORNER, (DIFF_CORNER,), (0,))[0]
    bias_pc = jnp.repeat(jnp.pad(corner, ((0, 0), (0, tq - DIFF_CORNER), (0, 0))), 2, axis=0).reshape(
        2 * N_HEADS * tq, DIFF_CORNER)
    bias_sc = _t5_bias(t5_table, ls, past, (past,), (0,))
    bias_sn = _t5_bias(t5_table, ls, ls, (past,), (past,))

    xp = x_prompt.reshape(bp * sp, d)
    xs = x_sample.reshape(bs * ls, d)
    zeros_gconv = jnp.zeros((bp, GDN_CONV - 1, 3 * GROUP_W), F32)
    zeros_gs = jnp.zeros((bp, N_HEADS, HEAD_W, HEAD_W), F32)
    zeros_fconv = jnp.zeros((bp, FFN_CONV - 1, 2 * d_ff), F32)

    outs_p, outs_s = [], []
    for l in range(depth):
        lp = _layer_params(l, p)
        lam_init = 0.8 - 0.6 * math.exp(-0.3 * l)
        final = l == depth - 1

        (qm, km, vm, lat, kr, dq, dk, dv, kd, vd, osgu, _vn, gqkv, ggate, gab) = _run_inproj(
            xp, lp, cos_p, sin_p, tm=tm_p, sgu_c=sgu_cp)
        o_mla = _mla_prompt(qm, km, vm, lp['wuv'], batch=bp, seq=sp, tq=tq, tk=tq)
        o_diff = _diff_prompt(dq, kd, vd, bias_p, bias_pc, lp['lamv'], jnp.tile(lp['subg'], (1, N_HEADS)),
                              batch=bp, seq=sp, tq=tq, lam_init=lam_init)
        o_gdn, gconv, gs = _gdn(gqkv, ggate, gab, zeros_gconv, zeros_gs, lp['gcw'], lp['alog'], lp['dtb'],
                                lp['ng'], batch=bp, seq=sp, tg=tg_p, c=gdn_cp)
        xp, fconv = _ffn(xp, o_mla, o_diff, osgu, o_gdn, zeros_fconv, lp['wout'], lp['g2'], lp['wup'],
                         lp['fcw'], lp['wdn'], lp['fg'], batch=bp, seq=sp, tm=tm_p, ft=ft, final=final)
        outs_p.append(dict(
            lat=lat.reshape(bp, sp, MLA_R), kr=kr.reshape(bp, sp, MLA_DR),
            dk=dk.reshape(bp, sp, N_HEADS, 2, DIFF_DH), dv=dv.reshape(bp, sp, N_HEADS, 2 * DIFF_DH),
            gconv=gconv, gs=gs, fconv=fconv))

        (qm, km, vm, lat, kr, dq, dk, dv, kd, vd, osgu, vn, gqkv, ggate, gab) = _run_inproj(
            xs, lp, cos_s, sin_s, tm=bs * ls, sgu_c=min(SGU_CHUNK, ls))
        kc_m = jnp.concatenate([cache_mla_latent[l], jnp.tile(cache_mla_krope[l], (1, 1, N_HEADS))],
                               axis=-1).astype(BF16)
        o_mla = _mla_sample(qm, kc_m, km, vm, lp['wuv'], batch=bs, lq=ls)
        kc_d = cache_diff_k[l].reshape(bs, past, GROUP_W).astype(BF16)
        vc_d = cache_diff_v[l].reshape(bs, past, GROUP_W).astype(BF16)
        o_diff = _diff_sample(dq, kc_d, vc_d, dk, dv, bias_sc, bias_sn, lp['lamv'], lp['subg'],
                              batch=bs, lq=ls, lam_init=lam_init)
        o_gdn, gconv, gs = _gdn(gqkv, ggate, gab, state_gdn_conv[l], state_gdn_s[l], lp['gcw'], lp['alog'],
                                lp['dtb'], lp['ng'], batch=bs, seq=ls, tg=ls, c=min(GDN_CHUNK, ls))
        xs, fconv = _ffn(xs, o_mla, o_diff, osgu, o_gdn, state_ffn_conv[l], lp['wout'], lp['g2'], lp['wup'],
                         lp['fcw'], lp['wdn'], lp['fg'], batch=bs, seq=ls, tm=ls, ft=ft, final=final)
        outs_s.append(dict(
            lat=lat.reshape(bs, ls, MLA_R), kr=kr.reshape(bs, ls, MLA_DR),
            dk=dk.reshape(bs, ls, N_HEADS, 2, DIFF_DH), dv=dv.reshape(bs, ls, N_HEADS, 2 * DIFF_DH),
            gconv=gconv, gs=gs, fconv=fconv, sv=vn.reshape(bs, ls, GROUP_W)))

    st = lambda lst, key: jnp.stack([o[key] for o in lst])
    return (xp.reshape(bp, sp, d), xs.reshape(bs, ls, d),
            st(outs_p, 'lat'), st(outs_p, 'kr'), st(outs_p, 'dk'), st(outs_p, 'dv'),
            st(outs_p, 'gconv'), st(outs_p, 'gs'), st(outs_p, 'fconv'),
            st(outs_s, 'lat'), st(outs_s, 'kr'), st(outs_s, 'dk'), st(outs_s, 'dv'),
            st(outs_s, 'gconv'), st(outs_s, 'gs'), st(outs_s, 'fconv'), st(outs_s, 'sv'))
```

```python
import functools
import math

import numpy as np
import jax
import jax.numpy as jnp
from jax import lax
from jax.experimental import pallas as pl
from jax.experimental.pallas import tpu as pltpu

F32 = jnp.float32
BF16 = jnp.bfloat16

CHUNK = 64
EPS = 1e-6
N_HEADS = 4
HEAD_W = 64
GROUP_W = 256
MLA_DN, MLA_DR, MLA_R = 64, 32, 128
DIFF_DH = 32
ROPE_THETA = 10000.0
SGU_CHUNK = 128
GDN_CHUNK = 64
GDN_CONV = 4
FFN_CONV = 3
T5_BUCKETS = 32
T5_MAX_DIST = 128
NEG = -1e30
LOG2E = math.log2(math.e)
SOLVE_BLOCK = 16

VMEM_LIMIT_BYTES = 56 * 1024 * 1024

C_QN, C_QR, C_QRR, C_LAT, C_KR, C_KRR = 0, 256, 384, 512, 640, 768
C_DQ, C_DK, C_DV, C_SU, C_SV = 896, 1152, 1408, 1664, 1920
C_GQKV, C_GG, C_GA, C_GB = 2176, 2944, 3200, 3456
C_TOTAL = 3712


def _cparams(sem):
    return pltpu.CompilerParams(dimension_semantics=sem, vmem_limit_bytes=VMEM_LIMIT_BYTES)


def _const_spec(shape):
    nd = len(shape)
    return pl.BlockSpec(shape, lambda *_: (0,) * nd, pipeline_mode=pl.Buffered(1))


def _dot(a, b):
    return jnp.dot(a, b, preferred_element_type=F32)


def _dot_nt(a, b):
    return lax.dot_general(a, b, (((1,), (1,)), ((), ())), preferred_element_type=F32)


def _dot_tn(a, b):
    return lax.dot_general(a, b, (((0,), (0,)), ((), ())), preferred_element_type=F32)


def _lane_group(shape, group):
    return lax.broadcasted_iota(jnp.int32, shape, len(shape) - 1) // group


def _silu(x):
    return x * (1.0 / (1.0 + jnp.exp(-x)))


def _shift_rows(x, prev8, s):
    if s == 0:
        return x
    r = pltpu.roll(x, s, axis=0)
    rp = pltpu.roll(prev8, s, axis=0)
    row8 = lax.broadcasted_iota(jnp.int32, rp.shape, 0)
    top = jnp.where(row8 < s, rp, r[0:8])
    if x.shape[0] == 8:
        return top
    return jnp.concatenate([top, r[8:]], axis=0)


def _inproj_kernel(x_ref, g1_ref, w_ref, cos_ref, sin_ref, wukt_ref, latg_ref,
                   sguw_ref, sgub_ref, lng_ref, lnb_ref,
                   qm_ref, km_ref, vm_ref, lat_ref, kr_ref,
                   dq_ref, dk_ref, dv_ref, kd_ref, vd_ref,
                   osgu_ref, vn_ref, gqkv_ref, ggate_ref, gab_ref, *, sgu_c):
    tm = x_ref.shape[0]
    x = x_ref[...]
    h = (x * lax.rsqrt(jnp.mean(x * x, axis=-1, keepdims=True) + EPS) * g1_ref[...]).astype(BF16)

    def sec(off, n):
        return _dot(h, w_ref[:, off:off + n])

    cos = cos_ref[...]
    sin = sin_ref[...]

    mla_scale = (MLA_DN + MLA_DR) ** -0.5 * LOG2E
    zq = sec(C_QN, 512)
    qr = (zq[:, 256:384] * cos + zq[:, 384:512] * sin) * mla_scale
    head_of_lane = _lane_group((1, 128), MLA_DR)
    for hh in range(N_HEADS):
        qn = (zq[:, MLA_DN * hh:MLA_DN * (hh + 1)] * mla_scale).astype(BF16)
        qm_ref[hh, :, 0:128] = _dot(qn, wukt_ref[hh]).astype(BF16)
        qm_ref[hh, :, 128:256] = jnp.where(head_of_lane == hh, qr, 0.0).astype(BF16)

    zl = sec(C_LAT, 384)
    zlat = zl[:, 0:128]
    lat = zlat * lax.rsqrt(jnp.mean(zlat * zlat, axis=-1, keepdims=True) + EPS) * latg_ref[...]
    kr4 = zl[:, 128:256] * cos + zl[:, 256:384] * sin
    lat_ref[...] = lat
    kr_ref[...] = kr4[:, 0:MLA_DR]
    lat_b = lat.astype(BF16)
    km_ref[:, 0:128] = lat_b
    km_ref[:, 128:256] = kr4.astype(BF16)
    vm_ref[:, 0:128] = lat_b
    vm_ref[:, 128:256] = jnp.ones((tm, 128), BF16)

    zd = sec(C_DQ, 768)
    dq_ref[...] = (zd[:, 0:256] * (DIFF_DH ** -0.5 * LOG2E)).astype(BF16)
    dk = zd[:, 256:512]
    dv = zd[:, 512:768]
    dk_ref[...] = dk
    dv_ref[...] = dv
    kd_ref[...] = dk.astype(BF16)
    ones64 = jnp.ones((tm, HEAD_W), F32)
    for hh in range(N_HEADS):
        vd_ref[hh] = jnp.concatenate([dv[:, HEAD_W * hh:HEAD_W * (hh + 1)], ones64], axis=1).astype(BF16)

    zs = sec(C_SU, 512)
    su = zs[:, 0:256]
    sv = zs[:, 256:512]
    mu = jnp.mean(sv, axis=-1, keepdims=True)
    svc = sv - mu
    var = jnp.mean(svc * svc, axis=-1, keepdims=True)
    vn = svc * lax.rsqrt(var + EPS) * lng_ref[...] + lnb_ref[...]
    vn_ref[...] = vn
    c = sgu_c
    wr = lax.broadcasted_iota(jnp.int32, (N_HEADS * c, c), 0) % c
    wc = lax.broadcasted_iota(jnp.int32, (N_HEADS * c, c), 1)
    w4 = jnp.where(wc <= wr, sguw_ref[...], 0.0).astype(BF16)
    hl = _lane_group((1, GROUP_W), HEAD_W)
    sgub = sgub_ref[...]
    for ci in range(tm // c):
        rows = slice(ci * c, (ci + 1) * c)
        m4 = _dot(w4, vn[rows].astype(BF16))
        mix = jnp.where(hl == 0, m4[0:c], 0.0)
        for hh in range(1, N_HEADS):
            mix = mix + jnp.where(hl == hh, m4[hh * c:(hh + 1) * c], 0.0)
        osgu_ref[rows, :] = (su[rows] * (mix + sgub)).astype(BF16)

    zg = sec(C_GQKV, 1536)
    gqkv_ref[...] = zg[:, 0:768]
    ggate_ref[...] = zg[:, 768:1024]
    gab_ref[...] = zg[:, 1024:1536]


def _inproj(x, g1, w, cos, sin, wukt, latg, sguw, sgub, lng, lnb, *, tm, sgu_c):
    t, d = x.shape
    n = t // tm
    npos = cos.shape[0] // tm
    row = lambda wd: pl.BlockSpec((tm, wd), lambda i: (i, 0))
    out_shape = [
        jax.ShapeDtypeStruct((N_HEADS, t, 256), BF16),
        jax.ShapeDtypeStruct((t, 256), BF16),
        jax.ShapeDtypeStruct((t, 256), BF16),
        jax.ShapeDtypeStruct((t, MLA_R), F32),
        jax.ShapeDtypeStruct((t, MLA_DR), F32),
        jax.ShapeDtypeStruct((t, 256), BF16),
        jax.ShapeDtypeStruct((t, 256), F32),
        jax.ShapeDtypeStruct((t, 256), F32),
        jax.ShapeDtypeStruct((t, 256), BF16),
        jax.ShapeDtypeStruct((N_HEADS, t, 128), BF16),
        jax.ShapeDtypeStruct((t, 256), BF16),
        jax.ShapeDtypeStruct((t, 256), F32),
        jax.ShapeDtypeStruct((t, 768), F32),
        jax.ShapeDtypeStruct((t, 256), F32),
        jax.ShapeDtypeStruct((t, 512), F32),
    ]
    out_specs = [
        pl.BlockSpec((N_HEADS, tm, 256), lambda i: (0, i, 0)),
        row(256), row(256), row(MLA_R), row(MLA_DR),
        row(256), row(256), row(256), row(256),
        pl.BlockSpec((N_HEADS, tm, 128), lambda i: (0, i, 0)),
        row(256), row(256), row(768), row(256), row(512),
    ]
    in_specs = [
        row(d), _const_spec(g1.shape), _const_spec(w.shape),
        pl.BlockSpec((tm, 128), lambda i: (i % npos, 0)),
        pl.BlockSpec((tm, 128), lambda i: (i % npos, 0)),
        _const_spec(wukt.shape), _const_spec(latg.shape),
        _const_spec(sguw.shape), _const_spec(sgub.shape),
        _const_spec(lng.shape), _const_spec(lnb.shape),
    ]
    return pl.pallas_call(
        functools.partial(_inproj_kernel, sgu_c=sgu_c),
        grid=(n,), in_specs=in_specs, out_specs=out_specs, out_shape=out_shape,
        compiler_params=_cparams(("parallel",)), name="inproj",
    )(x, g1, w, cos, sin, wukt, latg, sguw, sgub, lng, lnb)


def _t5_thresholds():
    nb = T5_BUCKETS // 2
    max_exact = nb // 2
    ratio = T5_MAX_DIST // max_exact
    thr = []
    for j in range(1, nb - max_exact):
        n = max_exact
        while (n ** (nb - max_exact)) * 1 < (ratio ** j) * (max_exact ** (nb - max_exact)):
            n += 1
        thr.append(n)
    return nb, max_exact, thr


def _bias_kernel(t_ref, o_ref, *, q0s, k0s):
    nb, max_exact, thr = _t5_thresholds()
    tq, tk = o_ref.shape[2], o_ref.shape[3]
    row = lax.broadcasted_iota(jnp.int32, (tq, tk), 0)
    col = lax.broadcasted_iota(jnp.int32, (tq, tk), 1)
    for di, (q0, k0) in enumerate(zip(q0s, k0s)):
        qpos = row + q0
        kpos = col + k0
        rel = kpos - qpos
        n = jnp.abs(rel)
        visible = (kpos // CHUNK) <= (qpos // CHUNK)
        for hh in range(N_HEADS):
            def side(base):
                val = jnp.full((tq, tk), t_ref[base + nb - 1, hh], F32)
                for j in range(len(thr) - 1, -1, -1):
                    val = jnp.where(n < thr[j], t_ref[base + max_exact + j, hh], val)
                for e in range(max_exact - 1, -1, -1):
                    val = jnp.where(n == e, t_ref[base + e, hh], val)
                return val
            b = jnp.where(rel > 0, side(nb), side(0)) - t_ref[nb - 1, hh]
            o_ref[di, hh] = jnp.where(visible, b * LOG2E, NEG)


def _t5_bias(t5_table, tq, tk, q0s, k0s):
    nd = len(q0s)
    return pl.pallas_call(
        functools.partial(_bias_kernel, q0s=tuple(q0s), k0s=tuple(k0s)),
        in_specs=[pl.BlockSpec(memory_space=pltpu.SMEM)],
        out_shape=jax.ShapeDtypeStruct((nd, N_HEADS, tq, tk), F32),
        name="t5_bias",
    )(t5_table)


def _mla_prompt_kernel(q_ref, k_ref, v_ref, wuv_ref, o_ref, *, tq, tk):
    qi = pl.program_id(1)
    qpos = qi * tq + lax.broadcasted_iota(jnp.int32, (tq, 1), 0)
    qlim = (qpos // CHUNK + 1) * CHUNK
    col = lax.broadcasted_iota(jnp.int32, (1, tk), 1)

    def step(kb, carry, masked):
        k0 = pl.multiple_of(kb * tk, tk)
        kblk = k_ref[0, pl.ds(k0, tk), :]
        vblk = v_ref[0, pl.ds(k0, tk), :]
        scores = [_dot_nt(q_ref[hh], kblk) for hh in range(N_HEADS)]
        ps, mns, alphas = [], [], []
        for hh in range(N_HEADS):
            m = carry[2 * hh]
            s = scores[hh]
            if masked:
                s = jnp.where(col < qlim - k0, s, NEG)
            mn = jnp.maximum(m, jnp.max(s, axis=1, keepdims=True))
            alphas.append(jnp.exp2(m - mn))
            ps.append(jnp.exp2(s - mn).astype(BF16))
            mns.append(mn)
        out = []
        for hh in range(N_HEADS):
            out += [mns[hh], alphas[hh] * carry[2 * hh + 1] + _dot(ps[hh], vblk)]
        return tuple(out)

    n_full = (qi * tq + CHUNK) // tk
    n_tot = ((qi + 1) * tq + tk - 1) // tk
    carry = (jnp.full((tq, 1), NEG, F32), jnp.zeros((tq, 256), F32)) * N_HEADS
    carry = lax.fori_loop(0, n_full, functools.partial(step, masked=False), carry)
    carry = lax.fori_loop(n_full, n_tot, functools.partial(step, masked=True), carry)
    outs = []
    for hh in range(N_HEADS):
        acc = carry[2 * hh + 1]
        on = (acc[:, 0:128] / acc[:, 128:129]).astype(BF16)
        outs.append(_dot(on, wuv_ref[hh]))
    o_ref[...] = jnp.concatenate(outs, axis=1).astype(BF16)


def _mla_prompt(qm, km, vm, wuv, *, batch, seq, tq, tk):
    nq = seq // tq
    t = batch * seq
    return pl.pallas_call(
        functools.partial(_mla_prompt_kernel, tq=tq, tk=tk),
        grid=(batch, nq),
        in_specs=[
            pl.BlockSpec((N_HEADS, tq, 256), lambda b, i: (0, b * nq + i, 0)),
            pl.BlockSpec((1, seq, 256), lambda b, i: (b, 0, 0)),
            pl.BlockSpec((1, seq, 256), lambda b, i: (b, 0, 0)),
            _const_spec(wuv.shape),
        ],
        out_specs=pl.BlockSpec((tq, 256), lambda b, i: (b * nq + i, 0)),
        out_shape=jax.ShapeDtypeStruct((t, 256), BF16),
        compiler_params=_cparams(("parallel", "arbitrary")), name="mla_prompt",
    )(qm, km.reshape(batch, seq, 256), vm.reshape(batch, seq, 256), wuv)


def _diff_lambda(lam_ref, lam_init):
    l = lam_ref[...]
    a = jnp.sum(l[0:1] * l[1:2], axis=-1, keepdims=True)
    b = jnp.sum(l[2:3] * l[3:4], axis=-1, keepdims=True)
    return jnp.exp(a) - jnp.exp(b) + lam_init


def _stack_q8(q):
    grp = _lane_group((1, GROUP_W), DIFF_DH)
    return jnp.concatenate([jnp.where(grp == g, q, jnp.zeros_like(q)) for g in range(2 * N_HEADS)], axis=0)


def _diff_finish(on, lam, subg, lam_init, tq):
    outs = []
    for hh in range(N_HEADS):
        o = on[(2 * hh) * tq:(2 * hh + 1) * tq] - lam * on[(2 * hh + 1) * tq:(2 * hh + 2) * tq]
        o = o * lax.rsqrt(jnp.mean(o * o, axis=-1, keepdims=True) + 1e-5) * subg
        outs.append(o * (1.0 - lam_init))
    return jnp.concatenate(outs, axis=1)


DIFF_CORNER = 128


def _diff_prompt_kernel(q_ref, k_ref, v_ref, bias_ref, cbias_ref, lam_ref, subg_ref, o_ref, *, tq, lam_init):
    qi = pl.program_id(1)
    r = 2 * N_HEADS * tq
    q8 = _stack_q8(q_ref[...])

    def step(kb, carry, bias):
        m, acc = carry
        k0 = pl.multiple_of(kb * tq, tq)
        s = _dot_nt(q8, k_ref[0, pl.ds(k0, tq), :])
        if bias == "diag":
            s = s + bias_ref[...]
        elif bias == "corner":
            s = jnp.concatenate([s[:, 0:tq - DIFF_CORNER], s[:, tq - DIFF_CORNER:tq] + cbias_ref[...]], axis=1)
        mn = jnp.maximum(m, jnp.max(s, axis=1, keepdims=True))
        alpha = jnp.exp2(m - mn)
        p = jnp.exp2(s - mn).astype(BF16)
        pv = jnp.concatenate(
            [_dot(p[2 * hh * tq:2 * (hh + 1) * tq], v_ref[hh, 0, pl.ds(k0, tq), :])
             for hh in range(N_HEADS)], axis=0)
        return mn, alpha * acc + pv

    carry = (jnp.full((r, 1), NEG, F32), jnp.zeros((r, 128), F32))
    carry = lax.fori_loop(0, jnp.maximum(qi - 1, 0), functools.partial(step, bias=None), carry)
    carry = lax.fori_loop(jnp.maximum(qi - 1, 0), qi, functools.partial(step, bias="corner"), carry)
    m, acc = step(qi, carry, "diag")
    on = acc * (1.0 / pltpu.roll(acc, HEAD_W, axis=1))
    lam = _diff_lambda(lam_ref, lam_init)
    low = lax.broadcasted_iota(jnp.int32, (1, 2 * HEAD_W), 1) < HEAD_W
    halves = []
    for hh in range(N_HEADS):
        o = on[(2 * hh) * tq:(2 * hh + 1) * tq] - lam * on[(2 * hh + 1) * tq:(2 * hh + 2) * tq]
        ms = jnp.sum(jnp.where(low, o * o, 0.0), axis=1, keepdims=True) * (1.0 / HEAD_W)
        halves.append(o * lax.rsqrt(ms + 1e-5))
    pair = lambda a, b: jnp.where(low, a, pltpu.roll(b, HEAD_W, axis=1))
    out = jnp.concatenate([pair(halves[0], halves[1]), pair(halves[2], halves[3])], axis=1)
    o_ref[...] = (out * (subg_ref[...] * (1.0 - lam_init))).astype(BF16)


def _diff_prompt(dq, kd, vd, bias, cbias, lamv, subg, *, batch, seq, tq, lam_init):
    nq = seq // tq
    t = batch * seq
    return pl.pallas_call(
        functools.partial(_diff_prompt_kernel, tq=tq, lam_init=lam_init),
        grid=(batch, nq),
        in_specs=[
            pl.BlockSpec((tq, 256), lambda b, i: (b * nq + i, 0)),
            pl.BlockSpec((1, seq, 256), lambda b, i: (b, 0, 0)),
            pl.BlockSpec((N_HEADS, 1, seq, 128), lambda b, i: (0, b, 0, 0)),
            _const_spec(bias.shape), _const_spec(cbias.shape), _const_spec(lamv.shape), _const_spec(subg.shape),
        ],
        out_specs=pl.BlockSpec((tq, 256), lambda b, i: (b * nq + i, 0)),
        out_shape=jax.ShapeDtypeStruct((t, 256), BF16),
        compiler_params=_cparams(("parallel", "arbitrary")), name="diff_prompt",
    )(dq, kd.reshape(batch, seq, 256), vd.reshape(N_HEADS, batch, seq, 128), bias, cbias, lamv, subg)


def _mla_sample_kernel(q_ref, kc_ref, kn_ref, vn_ref, wuv_ref, o_ref, *, lq):
    r = N_HEADS * lq
    q = q_ref[...].reshape(r, 256)
    kc = kc_ref[0]
    sc = _dot_nt(q, kc)
    sn = _dot_nt(q, kn_ref[...])
    m = jnp.maximum(jnp.max(sc, axis=1, keepdims=True), jnp.max(sn, axis=1, keepdims=True))
    pc = jnp.exp2(sc - m)
    pn = jnp.exp2(sn - m)
    l = jnp.sum(pc, axis=1, keepdims=True) + jnp.sum(pn, axis=1, keepdims=True)
    pv = _dot(pc.astype(BF16), kc[:, 0:128]) + _dot(pn.astype(BF16), vn_ref[:, 0:128])
    on = (pv / l).astype(BF16)
    o_ref[...] = jnp.concatenate(
        [_dot(on[hh * lq:(hh + 1) * lq], wuv_ref[hh]) for hh in range(N_HEADS)], axis=1).astype(BF16)


def _mla_sample(qm, kc, km, vm, wuv, *, batch, lq):
    past = kc.shape[1]
    return pl.pallas_call(
        functools.partial(_mla_sample_kernel, lq=lq),
        grid=(batch,),
        in_specs=[
            pl.BlockSpec((N_HEADS, lq, 256), lambda b: (0, b, 0)),
            pl.BlockSpec((1, past, 256), lambda b: (b, 0, 0)),
            pl.BlockSpec((lq, 256), lambda b: (b, 0)),
            pl.BlockSpec((lq, 256), lambda b: (b, 0)),
            _const_spec(wuv.shape),
        ],
        out_specs=pl.BlockSpec((lq, 256), lambda b: (b, 0)),
        out_shape=jax.ShapeDtypeStruct((batch * lq, 256), BF16),
        compiler_params=_cparams(("parallel",)), name="mla_sample",
    )(qm, kc, km, vm, wuv)


def _diff_sample_kernel(q_ref, kc_ref, vc_ref, kn_ref, vn_ref, bc_ref, bn_ref, lam_ref, subg_ref, o_ref,
                        *, lq, lam_init):
    q8 = _stack_q8(q_ref[...])
    bc = jnp.concatenate([bc_ref[0, hh // 2] for hh in range(2 * N_HEADS)], axis=0)
    bn = jnp.concatenate([bn_ref[0, hh // 2] for hh in range(2 * N_HEADS)], axis=0)
    sc = _dot_nt(q8, kc_ref[0]) + bc
    sn = _dot_nt(q8, kn_ref[...].astype(BF16)) + bn
    m = jnp.maximum(jnp.max(sc, axis=1, keepdims=True), jnp.max(sn, axis=1, keepdims=True))
    pc = jnp.exp2(sc - m)
    pn = jnp.exp2(sn - m)
    l = jnp.sum(pc, axis=1, keepdims=True) + jnp.sum(pn, axis=1, keepdims=True)
    pv = (_dot(pc.astype(BF16), vc_ref[0]) + _dot(pn.astype(BF16), vn_ref[...].astype(BF16))) / l
    on = jnp.concatenate(
        [pv[g * lq:(g + 1) * lq, HEAD_W * (g // 2):HEAD_W * (g // 2 + 1)] for g in range(2 * N_HEADS)], axis=0)
    lam = _diff_lambda(lam_ref, lam_init)
    o_ref[...] = _diff_finish(on, lam, subg_ref[...], lam_init, lq).astype(BF16)


def _diff_sample(dq, kc, vc, dk, dv, bias_c, bias_n, lamv, subg, *, batch, lq, lam_init):
    past = kc.shape[1]
    return pl.pallas_call(
        functools.partial(_diff_sample_kernel, lq=lq, lam_init=lam_init),
        grid=(batch,),
        in_specs=[
            pl.BlockSpec((lq, 256), lambda b: (b, 0)),
            pl.BlockSpec((1, past, 256), lambda b: (b, 0, 0)),
            pl.BlockSpec((1, past, 256), lambda b: (b, 0, 0)),
            pl.BlockSpec((lq, 256), lambda b: (b, 0)),
            pl.BlockSpec((lq, 256), lambda b: (b, 0)),
            _const_spec(bias_c.shape), _const_spec(bias_n.shape),
            _const_spec(lamv.shape), _const_spec(subg.shape),
        ],
        out_specs=pl.BlockSpec((lq, 256), lambda b: (b, 0)),
        out_shape=jax.ShapeDtypeStruct((batch * lq, 256), BF16),
        compiler_params=_cparams(("parallel",)), name="diff_sample",
    )(dq, kc, vc, dk, dv, bias_c, bias_n, lamv, subg)


def _stack4(y, group):
    grp = _lane_group((1,) * (y.ndim - 1) + (y.shape[-1],), group)
    return jnp.concatenate([jnp.where(grp == g, y, 0.0) for g in range(N_HEADS)], axis=-2)


def _diag_sum(f, group):
    rr = f.shape[0] // N_HEADS
    grp = _lane_group((1, f.shape[1]), group)
    out = jnp.where(grp == 0, f[0:rr], 0.0)
    for g in range(1, N_HEADS):
        out = out + jnp.where(grp == g, f[g * rr:(g + 1) * rr], 0.0)
    return out


def _bmm(x, y, group):
    return _dot(x, _stack4(y, group))


def _bdot(x, w):
    return lax.dot_general(x, w, (((2,), (1,)), ((0,), (0,))), preferred_element_type=F32)


def _bdot_nt(x, w):
    return lax.dot_general(x, w, (((2,), (2,)), ((0,), (0,))), preferred_element_type=F32)


def _bbmm(x, y, group):
    return _bdot(x, _stack4(y, group))


def _split3(x):
    hi = x.astype(BF16)
    r1 = x - hi.astype(F32)
    mid = r1.astype(BF16)
    lo = (r1 - mid.astype(F32)).astype(BF16)
    return hi, mid, lo


def _unit_lower_inverse(a, c):
    shape = (1, c, N_HEADS * c)
    i = lax.broadcasted_iota(jnp.int32, shape, 1)
    j = lax.broadcasted_iota(jnp.int32, shape, 2) % c
    eye = jnp.where(i == j, 1.0, 0.0)
    blockdiag = (i // SOLVE_BLOCK) == (j // SOLVE_BLOCK)
    ad = jnp.where(blockdiag, a, 0.0)
    p = _bbmm(ad, ad, c)
    rinv = eye - ad
    rinv = rinv + _bbmm(rinv, p, c)
    for _ in range(2):
        p = _bbmm(p, p, c)
        rinv = rinv + _bbmm(rinv, p, c)
    if c <= SOLVE_BLOCK:
        return rinv
    mm = _bbmm(rinv, a - ad, c)
    t = eye - mm
    p = _bbmm(mm, mm, c)
    t = t + _bbmm(t, p, c)
    nblk = c // SOLVE_BLOCK
    span = 4
    while span < nblk:
        p = _bbmm(p, p, c)
        t = t + _bbmm(t, p, c)
        span *= 2
    return _bbmm(t, rinv, c)


def _to_square(x, c):
    if c == HEAD_W:
        return x
    return jnp.concatenate([x[..., HEAD_W * hh:HEAD_W * hh + c] for hh in range(N_HEADS)], axis=-1)


def _gdn_kernel(qkv_ref, gate_ref, gab_ref, buf_ref, s0_ref, cw_ref, alog_ref, dtb_ref, ng_ref,
                o_ref, conv_ref, s_ref, prev_scr, s_scr, *, tg, c):
    j = pl.program_id(1)
    nj = pl.num_programs(1)

    @pl.when(j == 0)
    def _():
        prev_scr[...] = jnp.zeros_like(prev_scr)
        prev_scr[8 - (GDN_CONV - 1):8, :] = buf_ref[0]
        for hh in range(N_HEADS):
            s_scr[:, HEAD_W * hh:HEAD_W * (hh + 1)] = s0_ref[0, hh]

    x = qkv_ref[...]
    prev8 = prev_scr[...]
    cw = cw_ref[...]
    y = x * cw[GDN_CONV - 1:GDN_CONV]
    for s in range(1, GDN_CONV):
        y = y + _shift_rows(x, prev8, s) * cw[GDN_CONV - 1 - s:GDN_CONV - s]
    prev_scr[...] = x[tg - 8:tg]

    @pl.when(j == nj - 1)
    def _():
        conv_ref[0] = x[tg - (GDN_CONV - 1):tg]

    y = _silu(y)
    q = y[:, 0:256]
    k = y[:, 256:512]
    v = y[:, 512:768]
    gones = jnp.where(_lane_group((GROUP_W, GROUP_W), HEAD_W)
                      == lax.broadcasted_iota(jnp.int32, (GROUP_W, GROUP_W), 0) // HEAD_W,
                      1.0, 0.0).astype(BF16)

    def head_sum(z):
        z_hi, z_mid, z_lo = _split3(z)
        return _dot(z_hi, gones) + _dot(z_mid, gones) + _dot(z_lo, gones)

    q = q * lax.rsqrt(head_sum(q * q) + 1e-6) * (HEAD_W ** -0.5)
    k = k * lax.rsqrt(head_sum(k * k) + 1e-6)
    gab = gab_ref[...]
    za = gab[:, 0:256] + dtb_ref[...]
    softplus = jnp.maximum(za, 0.0) + jnp.log(1.0 + jnp.exp(-jnp.abs(za)))
    g = -jnp.exp(alog_ref[...]) * softplus
    beta = 1.0 / (1.0 + jnp.exp(-gab[:, 256:512]))

    nc = tg // c
    sq = (1, c, N_HEADS * c)
    ri = lax.broadcasted_iota(jnp.int32, sq, 1)
    cj = lax.broadcasted_iota(jnp.int32, sq, 2) % c
    tr = lax.broadcasted_iota(jnp.int32, (tg, tg), 0)
    tc = lax.broadcasted_iota(jnp.int32, (tg, tg), 1)
    tri_b = jnp.where((tc <= tr) & (tc // c == tr // c), 1.0, 0.0).astype(BF16)
    g_hi, g_mid, g_lo = _split3(g)
    decay2 = _dot(tri_b, g_hi) + _dot(tri_b, g_mid) + _dot(tri_b, g_lo)
    chunked = lambda z: z.reshape(nc, c, z.shape[-1])
    qb, kb_, vb, bb, decay = chunked(q), chunked(k), chunked(v), chunked(beta), chunked(decay2)
    dsq = _to_square(decay, c)
    drow = jnp.sum(jnp.where(ri == cj, dsq, 0.0), axis=1, keepdims=True)
    lm = jnp.where(cj <= ri, jnp.exp(jnp.where(cj <= ri, dsq - drow, 0.0)), 0.0)
    kbeta = kb_ * bb
    kq = _bdot_nt(jnp.concatenate([kbeta, qb], axis=1), _stack4(kb_, HEAD_W))
    a_mat = jnp.where(cj < ri, kq[:, 0:c] * lm, 0.0)
    qk_all = kq[:, c:2 * c] * lm
    edec = jnp.exp(decay)
    tinv = _unit_lower_inverse(a_mat, c)
    uw = _bdot(tinv, jnp.concatenate([_stack4(vb * bb, HEAD_W), _stack4(kbeta * edec, HEAD_W)], axis=2))
    dlast = decay[:, c - 1:c]
    qd_all = qb * edec
    kt_all = kb_ * jnp.exp(dlast - decay)
    gl_all = jnp.exp(dlast)
    pre = [(uw[n, :, 0:GROUP_W], uw[n, :, GROUP_W:2 * GROUP_W], qk_all[n], qd_all[n], kt_all[n], gl_all[n])
           for n in range(nc)]

    s_mat = s_scr[...]
    outs = []
    for (u, w, qk, qd, kt, glast) in pre:
        ws = _dot(jnp.concatenate([w, qd], axis=0), _stack4(s_mat, HEAD_W))
        vnew = u - ws[0:c]
        outs.append(ws[c:2 * c] + _bmm(qk, vnew, HEAD_W))
        s_mat = s_mat * glast + _diag_sum(_dot_tn(kt, vnew), HEAD_W)
    s_scr[...] = s_mat

    o = outs[0] if len(outs) == 1 else jnp.concatenate(outs, axis=0)
    o = o * lax.rsqrt(head_sum(o * o) * (1.0 / HEAD_W) + EPS) * ng_ref[...]
    o_ref[...] = (o * _silu(gate_ref[...])).astype(BF16)

    @pl.when(j == nj - 1)
    def _():
        s_fin = s_scr[...]
        for hh in range(N_HEADS):
            s_ref[0, hh] = s_fin[:, HEAD_W * hh:HEAD_W * (hh + 1)]


def _gdn(gqkv, ggate, gab, buf, s0, cw, alog, dtb, ng, *, batch, seq, tg, c):
    nj = seq // tg
    t = batch * seq
    row = lambda wd: pl.BlockSpec((tg, wd), lambda b, j: (b * nj + j, 0))
    return pl.pallas_call(
        functools.partial(_gdn_kernel, tg=tg, c=c),
        grid=(batch, nj),
        in_specs=[
            row(768), row(256), row(512),
            pl.BlockSpec((1, GDN_CONV - 1, 768), lambda b, j: (b, 0, 0)),
            pl.BlockSpec((1, N_HEADS, HEAD_W, HEAD_W), lambda b, j: (b, 0, 0, 0)),
            _const_spec(cw.shape), _const_spec(alog.shape), _const_spec(dtb.shape), _const_spec(ng.shape),
        ],
        out_specs=[
            row(256),
            pl.BlockSpec((1, GDN_CONV - 1, 768), lambda b, j: (b, 0, 0)),
            pl.BlockSpec((1, N_HEADS, HEAD_W, HEAD_W), lambda b, j: (b, 0, 0, 0)),
        ],
        out_shape=[
            jax.ShapeDtypeStruct((t, 256), BF16),
            jax.ShapeDtypeStruct((batch, GDN_CONV - 1, 768), F32),
            jax.ShapeDtypeStruct((batch, N_HEADS, HEAD_W, HEAD_W), F32),
        ],
        scratch_shapes=[pltpu.VMEM((8, 768), F32), pltpu.VMEM((HEAD_W, GROUP_W), F32)],
        compiler_params=_cparams(("parallel", "arbitrary")), name="gdn",
    )(gqkv, ggate, gab, buf, s0, cw, alog, dtb, ng)


def _ffn_kernel(x_ref, om_ref, od_ref, os_ref, og_ref, buf_ref, wout_ref, g2_ref, wup_ref, cw_ref, wdn_ref,
                fg_ref, y_ref, conv_ref, prev_scr, act_scr, *, tm, d_ff, ft, final):
    j = pl.program_id(1)
    nj = pl.num_programs(1)

    @pl.when(j == 0)
    def _():
        prev_scr[...] = jnp.zeros_like(prev_scr)
        prev_scr[8 - (FFN_CONV - 1):8, :] = buf_ref[0]

    mixed = jnp.concatenate([om_ref[...], od_ref[...], os_ref[...], og_ref[...]], axis=1)
    x1 = x_ref[...] + _dot(mixed, wout_ref[...])
    h2 =(x1 * lax.rsqrt(jnp.mean(x1 * x1, axis=-1, keepdims=True) + EPS) * g2_ref[...]).astype(BF16)

    def conv_cols(off):
        a = _dot(h2, wup_ref[:, off:off + ft])
        prev8 = prev_scr[:, off:off + ft]
        cw = cw_ref[:, off:off + ft]
        y = a * cw[FFN_CONV - 1:FFN_CONV]
        for s in range(1, FFN_CONV):
            y = y + _shift_rows(a, prev8, s) * cw[FFN_CONV - 1 - s:FFN_CONV - s]
        prev_scr[:, off:off + ft] = a[tm - 8:tm]
        return y

    for fi in range(d_ff // ft):
        gate = conv_cols(fi * ft)
        up = conv_cols(d_ff + fi * ft)
        act_scr[:, fi * ft:(fi + 1) * ft] = (_silu(gate) * up).astype(BF16)
    acc = x1 + _dot(act_scr[...], wdn_ref[...])

    if final:
        acc = acc * lax.rsqrt(jnp.mean(acc * acc, axis=-1, keepdims=True) + EPS) * fg_ref[...]
    y_ref[...] = acc

    @pl.when(j == nj - 1)
    def _():
        conv_ref[0] = prev_scr[8 - (FFN_CONV - 1):8, :]


def _ffn(x, om, od, osg, og, buf, wout, g2, wup, cw, wdn, fg, *, batch, seq, tm, ft, final):
    nj = seq // tm
    t, d = x.shape
    d_ff = wdn.shape[0]
    row = lambda wd: pl.BlockSpec((tm, wd), lambda b, j: (b * nj + j, 0))
    return pl.pallas_call(
        functools.partial(_ffn_kernel, tm=tm, d_ff=d_ff, ft=ft, final=final),
        grid=(batch, nj),
        in_specs=[
            row(d), row(256), row(256), row(256), row(256),
            pl.BlockSpec((1, FFN_CONV - 1, 2 * d_ff), lambda b, j: (b, 0, 0)),
            _const_spec(wout.shape), _const_spec(g2.shape), _const_spec(wup.shape),
            _const_spec(cw.shape), _const_spec(wdn.shape), _const_spec(fg.shape),
        ],
        out_specs=[row(d), pl.BlockSpec((1, FFN_CONV - 1, 2 * d_ff), lambda b, j: (b, 0, 0))],
        out_shape=[jax.ShapeDtypeStruct((t, d), F32),
                   jax.ShapeDtypeStruct((batch, FFN_CONV - 1, 2 * d_ff), F32)],
        scratch_shapes=[pltpu.VMEM((8, 2 * d_ff), F32), pltpu.VMEM((tm, d_ff), BF16)],
        compiler_params=_cparams(("parallel", "arbitrary")), name="ffn",
    )(x, om, od, osg, og, buf, wout, g2, wup, cw, wdn, fg)


def _rot_cols(w):
    d = w.shape[0]
    wg = w.reshape(d, -1, 2, MLA_DR // 2)
    return jnp.concatenate([-wg[:, :, 1:2], wg[:, :, 0:1]], axis=2).reshape(d, -1)


def _permute_w_in(w_in):
    d = w_in.shape[0]
    pts = np.cumsum([384, 128, 32, 256, 256, 256, 256, 256, 768, 256, 4, 4])[:-1]
    mq, mlat, mkr, dq, dk, dv, su, sv, gqkv, ggate, ga, gb = jnp.split(w_in, pts, axis=1)
    mq = mq.reshape(d, N_HEADS, MLA_DN + MLA_DR)
    qn = mq[:, :, :MLA_DN].reshape(d, N_HEADS * MLA_DN)
    qr = mq[:, :, MLA_DN:].reshape(d, N_HEADS * MLA_DR)
    kr4 = jnp.tile(mkr, (1, N_HEADS))
    rep = lambda a: jnp.repeat(a, HEAD_W, axis=1)
    w = jnp.concatenate([qn, qr, _rot_cols(qr), mlat, kr4, _rot_cols(kr4),
                         dq, dk, dv, su, sv, gqkv, ggate, rep(ga), rep(gb)], axis=1)
    assert w.shape[1] == C_TOTAL
    return w.astype(BF16)


def _rope_tables(pos):
    inv = ROPE_THETA ** (-jnp.arange(0, MLA_DR, 2, dtype=F32) / MLA_DR)
    ang = pos.astype(F32)[:, None] * inv[None, :]
    cos = jnp.tile(jnp.cos(ang), (1, 2 * N_HEADS))
    sin = jnp.tile(jnp.sin(ang), (1, 2 * N_HEADS))
    return cos, sin


def _layer_params(l, p):
    row = lambda a: a.reshape(1, -1)
    rep = lambda a: jnp.repeat(a, HEAD_W).reshape(1, -1)
    sgu_w = p['sgu_w'][l]
    return dict(
        w_in=_permute_w_in(p['w_in'][l]),
        g1=row(p['norm1_g'][l]),
        wukt=jnp.transpose(p['mla_w_uk'][l], (1, 2, 0)).astype(BF16),
        wuv=jnp.transpose(p['mla_w_uv'][l], (1, 0, 2)).astype(BF16),
        latg=row(p['mla_lat_g'][l]),
        sgu_w=sgu_w,
        sgu_b=p['sgu_b'][l],
        lng=row(p['sgu_ln_g'][l]), lnb=row(p['sgu_ln_b'][l]),
        lamv=jnp.stack([p['diff_lam_q1'][l], p['diff_lam_k1'][l], p['diff_lam_q2'][l], p['diff_lam_k2'][l]]),
        subg=row(p['diff_sub_g'][l]),
        gcw=p['gdn_conv_w'][l],
        alog=rep(p['gdn_a_log'][l]), dtb=rep(p['gdn_dt_bias'][l]),
        ng=jnp.tile(p['gdn_norm_g'][l], N_HEADS).reshape(1, -1),
        wout=p['w_out'][l].astype(BF16),
        g2=row(p['norm2_g'][l]),
        wup=p['ffn_w_up'][l].astype(BF16),
        fcw=p['ffn_conv_w'][l],
        wdn=p['ffn_w_down'][l].astype(BF16),
        fg=row(p['final_g']),
    )


def _sgu_tables(lp, c):
    w = lp['sgu_w'][:, :c, :c].reshape(N_HEADS * c, c)
    b = jnp.repeat(jnp.transpose(lp['sgu_b'][:, :c]), HEAD_W, axis=1)
    return w, b


def _run_inproj(x, lp, cos, sin, *, tm, sgu_c):
    sw, sb = _sgu_tables(lp, sgu_c)
    return _inproj(x, lp['g1'], lp['w_in'], cos, sin, lp['wukt'], lp['latg'], sw, sb, lp['lng'], lp['lnb'],
                   tm=tm, sgu_c=sgu_c)


def kernel(x_prompt, x_sample, cache_mla_latent, cache_mla_krope, cache_diff_k, cache_diff_v, state_gdn_conv, state_gdn_s, state_ffn_conv, t5_table, final_g, norm1_g, w_in, mla_lat_g, mla_w_uk, mla_w_uv, diff_lam_q1, diff_lam_k1, diff_lam_q2, diff_lam_k2, diff_sub_g, sgu_ln_g, sgu_ln_b, sgu_w, sgu_b, gdn_conv_w, gdn_a_log, gdn_dt_bias, gdn_norm_g, w_out, norm2_g, ffn_w_up, ffn_conv_w, ffn_w_down):
    p = dict(final_g=final_g, norm1_g=norm1_g, w_in=w_in, mla_lat_g=mla_lat_g, mla_w_uk=mla_w_uk,
             mla_w_uv=mla_w_uv, diff_lam_q1=diff_lam_q1, diff_lam_k1=diff_lam_k1, diff_lam_q2=diff_lam_q2,
             diff_lam_k2=diff_lam_k2, diff_sub_g=diff_sub_g, sgu_ln_g=sgu_ln_g, sgu_ln_b=sgu_ln_b,
             sgu_w=sgu_w, sgu_b=sgu_b, gdn_conv_w=gdn_conv_w, gdn_a_log=gdn_a_log, gdn_dt_bias=gdn_dt_bias,
             gdn_norm_g=gdn_norm_g, w_out=w_out, norm2_g=norm2_g, ffn_w_up=ffn_w_up, ffn_conv_w=ffn_conv_w,
             ffn_w_down=ffn_w_down)
    depth = w_in.shape[0]
    bp, sp, d = x_prompt.shape
    bs, ls, _ = x_sample.shape
    past = cache_mla_latent.shape[2]
    d_ff = ffn_w_down.shape[1]
    assert past % CHUNK == 0 and ls <= CHUNK

    tm_p = min(512, sp)
    tq = min(512, sp)
    tg_p = min(512, sp)
    ft = 256
    sgu_cp = min(SGU_CHUNK, sp)
    gdn_cp = min(GDN_CHUNK, sp)

    cos_p, sin_p = _rope_tables(jnp.arange(sp, dtype=jnp.int32))
    pos_s = past + jnp.arange(ls, dtype=jnp.int32)
    cos_s, sin_s = _rope_tables(jnp.tile(pos_s, bs))

    assert tq >= DIFF_CORNER
    bias_p = jnp.repeat(_t5_bias(t5_table, tq, tq, (0,), (0,))[0], 2, axis=0).reshape(2 * N_HEADS * tq, tq)
    corner = _t5_bias(t5_table, DIFF_CORNER, DIFF_CORNER, (DIFF_CORNER,), (0,))[0]
    bias_pc = jnp.repeat(jnp.pad(corner, ((0, 0), (0, tq - DIFF_CORNER), (0, 0))), 2, axis=0).reshape(
        2 * N_HEADS * tq, DIFF_CORNER)
    bias_sc = _t5_bias(t5_table, ls, past, (past,), (0,))
    bias_sn = _t5_bias(t5_table, ls, ls, (past,), (past,))

    xp = x_prompt.reshape(bp * sp, d)
    xs = x_sample.reshape(bs * ls, d)
    zeros_gconv = jnp.zeros((bp, GDN_CONV - 1, 3 * GROUP_W), F32)
    zeros_gs = jnp.zeros((bp, N_HEADS, HEAD_W, HEAD_W), F32)
    zeros_fconv = jnp.zeros((bp, FFN_CONV - 1, 2 * d_ff), F32)

    outs_p, outs_s = [], []
    for l in range(depth):
        lp = _layer_params(l, p)
        lam_init = 0.8 - 0.6 * math.exp(-0.3 * l)
        final = l == depth - 1

        (qm, km, vm, lat, kr, dq, dk, dv, kd, vd, osgu, _vn, gqkv, ggate, gab) = _run_inproj(
            xp, lp, cos_p, sin_p, tm=tm_p, sgu_c=sgu_cp)
        o_mla = _mla_prompt(qm, km, vm, lp['wuv'], batch=bp, seq=sp, tq=tq, tk=min(2 * tq, sp))
        o_diff = _diff_prompt(dq, kd, vd, bias_p, bias_pc, lp['lamv'], jnp.tile(lp['subg'], (1, N_HEADS)),
                              batch=bp, seq=sp, tq=tq, lam_init=lam_init)
        o_gdn, gconv, gs = _gdn(gqkv, ggate, gab, zeros_gconv, zeros_gs, lp['gcw'], lp['alog'], lp['dtb'],
                                lp['ng'], batch=bp, seq=sp, tg=tg_p, c=gdn_cp)
        xp, fconv = _ffn(xp, o_mla, o_diff, osgu, o_gdn, zeros_fconv, lp['wout'], lp['g2'], lp['wup'],
                         lp['fcw'], lp['wdn'], lp['fg'], batch=bp, seq=sp, tm=tm_p, ft=ft, final=final)
        outs_p.append(dict(
            lat=lat.reshape(bp, sp, MLA_R), kr=kr.reshape(bp, sp, MLA_DR),
            dk=dk.reshape(bp, sp, N_HEADS, 2, DIFF_DH), dv=dv.reshape(bp, sp, N_HEADS, 2 * DIFF_DH),
            gconv=gconv, gs=gs, fconv=fconv))

        (qm, km, vm, lat, kr, dq, dk, dv, kd, vd, osgu, vn, gqkv, ggate, gab) = _run_inproj(
            xs, lp, cos_s, sin_s, tm=bs * ls, sgu_c=min(SGU_CHUNK, ls))
        kc_m = jnp.concatenate([cache_mla_latent[l], jnp.tile(cache_mla_krope[l], (1, 1, N_HEADS))],
                               axis=-1).astype(BF16)
        o_mla = _mla_sample(qm, kc_m, km, vm, lp['wuv'], batch=bs, lq=ls)
        kc_d = cache_diff_k[l].reshape(bs, past, GROUP_W).astype(BF16)
        vc_d = cache_diff_v[l].reshape(bs, past, GROUP_W).astype(BF16)
        o_diff = _diff_sample(dq, kc_d, vc_d, dk, dv, bias_sc, bias_sn, lp['lamv'], lp['subg'],
                              batch=bs, lq=ls, lam_init=lam_init)
        o_gdn, gconv, gs = _gdn(gqkv, ggate, gab, state_gdn_conv[l], state_gdn_s[l], lp['gcw'], lp['alog'],
                                lp['dtb'], lp['ng'], batch=bs, seq=ls, tg=ls, c=min(GDN_CHUNK, ls))
        xs, fconv = _ffn(xs, o_mla, o_diff, osgu, o_gdn, state_ffn_conv[l], lp['wout'], lp['g2'], lp['wup'],
                         lp['fcw'], lp['wdn'], lp['fg'], batch=bs, seq=ls, tm=ls, ft=ft, final=final)
        outs_s.append(dict(
            lat=lat.reshape(bs, ls, MLA_R), kr=kr.reshape(bs, ls, MLA_DR),
            dk=dk.reshape(bs, ls, N_HEADS, 2, DIFF_DH), dv=dv.reshape(bs, ls, N_HEADS, 2 * DIFF_DH),
            gconv=gconv, gs=gs, fconv=fconv, sv=vn.reshape(bs, ls, GROUP_W)))

    st = lambda lst, key: jnp.stack([o[key] for o in lst])
    return (xp.reshape(bp, sp, d), xs.reshape(bs, ls, d),
            st(outs_p, 'lat'), st(outs_p, 'kr'), st(outs_p, 'dk'), st(outs_p, 'dv'),
            st(outs_p, 'gconv'), st(outs_p, 'gs'), st(outs_p, 'fconv'),
            st(outs_s, 'lat'), st(outs_s, 'kr'), st(outs_s, 'dk'), st(outs_s, 'dv'),
            st(outs_s, 'gconv'), st(outs_s, 'gs'), st(outs_s, 'fconv'), st(outs_s, 'sv'))
```

```python
import functools
import math

import numpy as np
import jax
import jax.numpy as jnp
from jax import lax
from jax.experimental import pallas as pl
from jax.experimental.pallas import tpu as pltpu

F32 = jnp.float32
BF16 = jnp.bfloat16

CHUNK = 64
EPS = 1e-6
N_HEADS = 4
HEAD_W = 64
GROUP_W = 256
MLA_DN, MLA_DR, MLA_R = 64, 32, 128
DIFF_DH = 32
ROPE_THETA = 10000.0
SGU_CHUNK = 128
GDN_CHUNK = 64
GDN_CONV = 4
FFN_CONV = 3
T5_BUCKETS = 32
T5_MAX_DIST = 128
NEG = -1e30
LOG2E = math.log2(math.e)
SOLVE_BLOCK = 16

VMEM_LIMIT_BYTES = 56 * 1024 * 1024

C_QN, C_QR, C_QRR, C_LAT, C_KR, C_KRR = 0, 256, 384, 512, 640, 768
C_DQ, C_DK, C_DV, C_SU, C_SV = 896, 1152, 1408, 1664, 1920
C_GQKV, C_GG, C_GA, C_GB = 2176, 2944, 3200, 3456
C_TOTAL = 3712


def _cparams(sem):
    return pltpu.CompilerParams(dimension_semantics=sem, vmem_limit_bytes=VMEM_LIMIT_BYTES)


def _const_spec(shape):
    nd = len(shape)
    return pl.BlockSpec(shape, lambda *_: (0,) * nd, pipeline_mode=pl.Buffered(1))


def _dot(a, b):
    return jnp.dot(a, b, preferred_element_type=F32)


def _dot_nt(a, b):
    return lax.dot_general(a, b, (((1,), (1,)), ((), ())), preferred_element_type=F32)


def _dot_tn(a, b):
    return lax.dot_general(a, b, (((0,), (0,)), ((), ())), preferred_element_type=F32)


def _lane_group(shape, group):
    return lax.broadcasted_iota(jnp.int32, shape, len(shape) - 1) // group


def _silu(x):
    return x * (1.0 / (1.0 + jnp.exp(-x)))


def _shift_rows(x, prev8, s):
    if s == 0:
        return x
    r = pltpu.roll(x, s, axis=0)
    rp = pltpu.roll(prev8, s, axis=0)
    row8 = lax.broadcasted_iota(jnp.int32, rp.shape, 0)
    top = jnp.where(row8 < s, rp, r[0:8])
    if x.shape[0] == 8:
        return top
    return jnp.concatenate([top, r[8:]], axis=0)


def _inproj_kernel(x_ref, g1_ref, w_ref, wdt_ref, cos_ref, sin_ref, wukt_ref, latg_ref,
                   sguw_ref, sgub_ref, lng_ref, lnb_ref,
                   qm_ref, km_ref, vm_ref, lat_ref, kr_ref,
                   dq_ref, dk_ref, dv_ref, kd_ref, dqt_ref, vdt_ref,
                   osgu_ref, vn_ref, gqkv_ref, ggate_ref, gab_ref, *, sgu_c):
    tm = x_ref.shape[0]
    x = x_ref[...]
    h = (x * lax.rsqrt(jnp.mean(x * x, axis=-1, keepdims=True) + EPS) * g1_ref[...]).astype(BF16)

    def sec(off, n):
        return _dot(h, w_ref[:, off:off + n])

    cos = cos_ref[...]
    sin = sin_ref[...]

    mla_scale = (MLA_DN + MLA_DR) ** -0.5 * LOG2E
    zq = sec(C_QN, 512)
    qr = (zq[:, 256:384] * cos + zq[:, 384:512] * sin) * mla_scale
    head_of_lane = _lane_group((1, 128), MLA_DR)
    for hh in range(N_HEADS):
        qn = (zq[:, MLA_DN * hh:MLA_DN * (hh + 1)] * mla_scale).astype(BF16)
        qm_ref[hh, :, 0:128] = _dot(qn, wukt_ref[hh]).astype(BF16)
        qm_ref[hh, :, 128:256] = jnp.where(head_of_lane == hh, qr, 0.0).astype(BF16)

    zl = sec(C_LAT, 384)
    zlat = zl[:, 0:128]
    lat = zlat * lax.rsqrt(jnp.mean(zlat * zlat, axis=-1, keepdims=True) + EPS) * latg_ref[...]
    kr4 = zl[:, 128:256] * cos + zl[:, 256:384] * sin
    lat_ref[...] = lat
    kr_ref[...] = kr4[:, 0:MLA_DR]
    lat_b = lat.astype(BF16)
    km_ref[:, 0:128] = lat_b
    km_ref[:, 128:256] = kr4.astype(BF16)
    vm_ref[:, 0:128] = lat_b
    vm_ref[:, 128:256] = jnp.ones((tm, 128), BF16)

    zd = sec(C_DQ, 768)
    dq_ref[...] = (zd[:, 0:256] * (DIFF_DH ** -0.5 * LOG2E)).astype(BF16)
    dk = zd[:, 256:512]
    dv = zd[:, 512:768]
    dk_ref[...] = dk
    dv_ref[...] = dv
    kd_ref[...] = dk.astype(BF16)
    zt = _dot_nt(wdt_ref[...], h)
    dqt_ref[...] = (zt[0:GROUP_W] * (DIFF_DH ** -0.5 * LOG2E)).astype(BF16)
    ones_rows = jnp.ones((VT_ROWS - HEAD_W, tm), F32)
    for hh in range(N_HEADS):
        vt = zt[GROUP_W + HEAD_W * hh:GROUP_W + HEAD_W * (hh + 1)]
        vdt_ref[hh, 0] = jnp.concatenate([vt, ones_rows], axis=0).astype(BF16)

    zs = sec(C_SU, 512)
    su = zs[:, 0:256]
    sv = zs[:, 256:512]
    mu = jnp.mean(sv, axis=-1, keepdims=True)
    svc = sv - mu
    var = jnp.mean(svc * svc, axis=-1, keepdims=True)
    vn = svc * lax.rsqrt(var + EPS) * lng_ref[...] + lnb_ref[...]
    vn_ref[...] = vn
    c = sgu_c
    wr = lax.broadcasted_iota(jnp.int32, (N_HEADS * c, c), 0) % c
    wc = lax.broadcasted_iota(jnp.int32, (N_HEADS * c, c), 1)
    w4 = jnp.where(wc <= wr, sguw_ref[...], 0.0).astype(BF16)
    hl = _lane_group((1, GROUP_W), HEAD_W)
    sgub = sgub_ref[...]
    for ci in range(tm // c):
        rows = slice(ci * c, (ci + 1) * c)
        m4 = _dot(w4, vn[rows].astype(BF16))
        mix = jnp.where(hl == 0, m4[0:c], 0.0)
        for hh in range(1, N_HEADS):
            mix = mix + jnp.where(hl == hh, m4[hh * c:(hh + 1) * c], 0.0)
        osgu_ref[rows, :] = (su[rows] * (mix + sgub)).astype(BF16)

    zg = sec(C_GQKV, 1536)
    gqkv_ref[...] = zg[:, 0:768]
    ggate_ref[...] = zg[:, 768:1024]
    gab_ref[...] = zg[:, 1024:1536]


def _inproj(x, g1, w, wdt, cos, sin, wukt, latg, sguw, sgub, lng, lnb, *, tm, sgu_c):
    t, d = x.shape
    n = t // tm
    npos = cos.shape[0] // tm
    row = lambda wd: pl.BlockSpec((tm, wd), lambda i: (i, 0))
    out_shape = [
        jax.ShapeDtypeStruct((N_HEADS, t, 256), BF16),
        jax.ShapeDtypeStruct((t, 256), BF16),
        jax.ShapeDtypeStruct((t, 256), BF16),
        jax.ShapeDtypeStruct((t, MLA_R), F32),
        jax.ShapeDtypeStruct((t, MLA_DR), F32),
        jax.ShapeDtypeStruct((t, 256), BF16),
        jax.ShapeDtypeStruct((t, 256), F32),
        jax.ShapeDtypeStruct((t, 256), F32),
        jax.ShapeDtypeStruct((t, 256), BF16),
        jax.ShapeDtypeStruct((GROUP_W, t), BF16),
        jax.ShapeDtypeStruct((N_HEADS, n, VT_ROWS, tm), BF16),
        jax.ShapeDtypeStruct((t, 256), BF16),
        jax.ShapeDtypeStruct((t, 256), F32),
        jax.ShapeDtypeStruct((t, 768), F32),
        jax.ShapeDtypeStruct((t, 256), F32),
        jax.ShapeDtypeStruct((t, 512), F32),
    ]
    out_specs = [
        pl.BlockSpec((N_HEADS, tm, 256), lambda i: (0, i, 0)),
        row(256), row(256), row(MLA_R), row(MLA_DR),
        row(256), row(256), row(256), row(256),
        pl.BlockSpec((GROUP_W, tm), lambda i: (0, i)),
        pl.BlockSpec((N_HEADS, 1, VT_ROWS, tm), lambda i: (0, i, 0, 0)),
        row(256), row(256), row(768), row(256), row(512),
    ]
    in_specs = [
        row(d), _const_spec(g1.shape), _const_spec(w.shape), _const_spec(wdt.shape),
        pl.BlockSpec((tm, 128), lambda i: (i % npos, 0)),
        pl.BlockSpec((tm, 128), lambda i: (i % npos, 0)),
        _const_spec(wukt.shape), _const_spec(latg.shape),
        _const_spec(sguw.shape), _const_spec(sgub.shape),
        _const_spec(lng.shape), _const_spec(lnb.shape),
    ]
    return pl.pallas_call(
        functools.partial(_inproj_kernel, sgu_c=sgu_c),
        grid=(n,), in_specs=in_specs, out_specs=out_specs, out_shape=out_shape,
        compiler_params=_cparams(("parallel",)), name="inproj",
    )(x, g1, w, wdt, cos, sin, wukt, latg, sguw, sgub, lng, lnb)


def _t5_thresholds():
    nb = T5_BUCKETS // 2
    max_exact = nb // 2
    ratio = T5_MAX_DIST // max_exact
    thr = []
    for j in range(1, nb - max_exact):
        n = max_exact
        while (n ** (nb - max_exact)) * 1 < (ratio ** j) * (max_exact ** (nb - max_exact)):
            n += 1
        thr.append(n)
    return nb, max_exact, thr


def _bias_kernel(t_ref, o_ref, *, q0s, k0s):
    nb, max_exact, thr = _t5_thresholds()
    tq, tk = o_ref.shape[2], o_ref.shape[3]
    row = lax.broadcasted_iota(jnp.int32, (tq, tk), 0)
    col = lax.broadcasted_iota(jnp.int32, (tq, tk), 1)
    for di, (q0, k0) in enumerate(zip(q0s, k0s)):
        qpos = row + q0
        kpos = col + k0
        rel = kpos - qpos
        n = jnp.abs(rel)
        visible = (kpos // CHUNK) <= (qpos // CHUNK)
        for hh in range(N_HEADS):
            def side(base):
                val = jnp.full((tq, tk), t_ref[base + nb - 1, hh], F32)
                for j in range(len(thr) - 1, -1, -1):
                    val = jnp.where(n < thr[j], t_ref[base + max_exact + j, hh], val)
                for e in range(max_exact - 1, -1, -1):
                    val = jnp.where(n == e, t_ref[base + e, hh], val)
                return val
            b = jnp.where(rel > 0, side(nb), side(0)) - t_ref[nb - 1, hh]
            o_ref[di, hh] = jnp.where(visible, b * LOG2E, NEG)


def _t5_bias(t5_table, tq, tk, q0s, k0s):
    nd = len(q0s)
    return pl.pallas_call(
        functools.partial(_bias_kernel, q0s=tuple(q0s), k0s=tuple(k0s)),
        in_specs=[pl.BlockSpec(memory_space=pltpu.SMEM)],
        out_shape=jax.ShapeDtypeStruct((nd, N_HEADS, tq, tk), F32),
        name="t5_bias",
    )(t5_table)


def _mla_prompt_kernel(q_ref, k_ref, v_ref, wuv_ref, o_ref, *, tq, tk):
    qi = pl.program_id(1)
    qpos = qi * tq + lax.broadcasted_iota(jnp.int32, (tq, 1), 0)
    qlim = (qpos // CHUNK + 1) * CHUNK
    col = lax.broadcasted_iota(jnp.int32, (1, tk), 1)

    def step(kb, carry, masked):
        k0 = pl.multiple_of(kb * tk, tk)
        kblk = k_ref[0, pl.ds(k0, tk), :]
        vblk = v_ref[0, pl.ds(k0, tk), :]
        scores = [_dot_nt(q_ref[hh], kblk) for hh in range(N_HEADS)]
        ps, mns, alphas = [], [], []
        for hh in range(N_HEADS):
            m = carry[2 * hh]
            s = scores[hh]
            if masked:
                s = jnp.where(col < qlim - k0, s, NEG)
            mn = jnp.maximum(m, jnp.max(s, axis=1, keepdims=True))
            alphas.append(jnp.exp2(m - mn))
            ps.append(jnp.exp2(s - mn).astype(BF16))
            mns.append(mn)
        out = []
        for hh in range(N_HEADS):
            out += [mns[hh], alphas[hh] * carry[2 * hh + 1] + _dot(ps[hh], vblk)]
        return tuple(out)

    n_full = (qi * tq + CHUNK) // tk
    n_tot = ((qi + 1) * tq + tk - 1) // tk
    carry = (jnp.full((tq, 1), NEG, F32), jnp.zeros((tq, 256), F32)) * N_HEADS
    carry = lax.fori_loop(0, n_full, functools.partial(step, masked=False), carry)
    carry = lax.fori_loop(n_full, n_tot, functools.partial(step, masked=True), carry)
    outs = []
    for hh in range(N_HEADS):
        acc = carry[2 * hh + 1]
        on = (acc[:, 0:128] / acc[:, 128:129]).astype(BF16)
        outs.append(_dot(on, wuv_ref[hh]))
    o_ref[...] = jnp.concatenate(outs, axis=1).astype(BF16)


def _mla_prompt(qm, km, vm, wuv, *, batch, seq, tq, tk):
    assert seq % tq == 0 and seq % tk == 0
    nq = seq // tq
    t = batch * seq
    return pl.pallas_call(
        functools.partial(_mla_prompt_kernel, tq=tq, tk=tk),
        grid=(batch, nq),
        in_specs=[
            pl.BlockSpec((N_HEADS, tq, 256), lambda b, i: (0, b * nq + i, 0)),
            pl.BlockSpec((1, seq, 256), lambda b, i: (b, 0, 0)),
            pl.BlockSpec((1, seq, 256), lambda b, i: (b, 0, 0)),
            _const_spec(wuv.shape),
        ],
        out_specs=pl.BlockSpec((tq, 256), lambda b, i: (b * nq + i, 0)),
        out_shape=jax.ShapeDtypeStruct((t, 256), BF16),
        compiler_params=_cparams(("parallel", "arbitrary")), name="mla_prompt",
    )(qm, km.reshape(batch, seq, 256), vm.reshape(batch, seq, 256), wuv)


def _diff_lambda(lam_ref, lam_init):
    l = lam_ref[...]
    a = jnp.sum(l[0:1] * l[1:2], axis=-1, keepdims=True)
    b = jnp.sum(l[2:3] * l[3:4], axis=-1, keepdims=True)
    return jnp.exp(a) - jnp.exp(b) + lam_init


def _stack_q8(q):
    grp = _lane_group((1, GROUP_W), DIFF_DH)
    return jnp.concatenate([jnp.where(grp == g, q, jnp.zeros_like(q)) for g in range(2 * N_HEADS)], axis=0)


def _diff_finish(on, lam, subg, lam_init, tq):
    outs = []
    for hh in range(N_HEADS):
        o = on[(2 * hh) * tq:(2 * hh + 1) * tq] - lam * on[(2 * hh + 1) * tq:(2 * hh + 2) * tq]
        o = o * lax.rsqrt(jnp.mean(o * o, axis=-1, keepdims=True) + 1e-5) * subg
        outs.append(o * (1.0 - lam_init))
    return jnp.concatenate(outs, axis=1)


DIFF_CORNER = 128
VT_ROWS = 80


def _diff_prompt_kernel(qt_ref, k_ref, vt_ref, bias_ref, cbias_ref, lam_ref, subg_ref, o_ref, *, tq, lam_init):
    qi = pl.program_id(1)
    rg = 2 * tq
    qt = qt_ref[...]
    fgrp = lax.broadcasted_iota(jnp.int32, (GROUP_W, 1), 0) // DIFF_DH
    qts = [jnp.concatenate([jnp.where(fgrp == 2 * hh + c, qt, jnp.zeros_like(qt)) for c in range(2)], axis=1)
           for hh in range(N_HEADS)]

    def step(kb, carry, bias):
        k0 = pl.multiple_of(kb * tq, tq)
        kblk = k_ref[0, pl.ds(k0, tq), :]
        scores = [_dot(kblk, qts[hh]) for hh in range(N_HEADS)]
        ps, mns, alphas = [], [], []
        for hh in range(N_HEADS):
            s = scores[hh]
            cols = slice(hh * rg, (hh + 1) * rg)
            if bias == "diag":
                s = s + bias_ref[:, cols]
            elif bias == "corner":
                s = jnp.concatenate([s[0:tq - DIFF_CORNER], s[tq - DIFF_CORNER:tq] + cbias_ref[:, cols]], axis=0)
            m = carry[2 * hh]
            mn = jnp.maximum(m, jnp.max(s, axis=0, keepdims=True))
            alphas.append(jnp.exp2(m - mn))
            ps.append(jnp.exp2(s - mn).astype(BF16))
            mns.append(mn)
        out = []
        for hh in range(N_HEADS):
            out += [mns[hh], alphas[hh] * carry[2 * hh + 1] + _dot(vt_ref[hh, kb], ps[hh])]
        return tuple(out)

    carry = (jnp.full((1, rg), NEG, F32), jnp.zeros((VT_ROWS, rg), F32)) * N_HEADS
    carry = lax.fori_loop(0, jnp.maximum(qi - 1, 0), functools.partial(step, bias=None), carry)
    carry = lax.fori_loop(jnp.maximum(qi - 1, 0), qi, functools.partial(step, bias="corner"), carry)
    carry = step(qi, carry, "diag")
    lam = _diff_lambda(lam_ref, lam_init)
    outs = []
    for hh in range(N_HEADS):
        acc = carry[2 * hh + 1]
        on = acc[0:HEAD_W] * (1.0 / acc[HEAD_W:HEAD_W + 1])
        o = on[:, 0:tq] - lam * on[:, tq:rg]
        outs.append(o * lax.rsqrt(jnp.mean(o * o, axis=0, keepdims=True) + 1e-5))
    ot = jnp.concatenate(outs, axis=0) * (subg_ref[...] * (1.0 - lam_init))
    o_ref[...] = jnp.transpose(ot).astype(BF16)


def _diff_prompt(dqt, kd, vdt, bias, cbias, lamv, subg, *, batch, seq, tq, lam_init):
    nq = seq // tq
    t = batch * seq
    assert vdt.shape[3] == tq
    return pl.pallas_call(
        functools.partial(_diff_prompt_kernel, tq=tq, lam_init=lam_init),
        grid=(batch, nq),
        in_specs=[
            pl.BlockSpec((GROUP_W, tq), lambda b, i: (0, b * nq + i)),
            pl.BlockSpec((1, seq, 256), lambda b, i: (b, 0, 0)),
            pl.BlockSpec((N_HEADS, nq, VT_ROWS, tq), lambda b, i: (0, b, 0, 0)),
            _const_spec(bias.shape), _const_spec(cbias.shape), _const_spec(lamv.shape), _const_spec(subg.shape),
        ],
        out_specs=pl.BlockSpec((tq, 256), lambda b, i: (b * nq + i, 0)),
        out_shape=jax.ShapeDtypeStruct((t, 256), BF16),
        compiler_params=_cparams(("parallel", "arbitrary")), name="diff_prompt",
    )(dqt, kd.reshape(batch, seq, 256), vdt, bias, cbias, lamv, subg)


def _mla_sample_kernel(q_ref, kc_ref, kn_ref, vn_ref, wuv_ref, o_ref, *, lq):
    r = N_HEADS * lq
    q = q_ref[...].reshape(r, 256)
    kc = kc_ref[0]
    sc = _dot_nt(q, kc)
    sn = _dot_nt(q, kn_ref[...])
    m = jnp.maximum(jnp.max(sc, axis=1, keepdims=True), jnp.max(sn, axis=1, keepdims=True))
    pc = jnp.exp2(sc - m)
    pn = jnp.exp2(sn - m)
    l = jnp.sum(pc, axis=1, keepdims=True) + jnp.sum(pn, axis=1, keepdims=True)
    pv = _dot(pc.astype(BF16), kc[:, 0:128]) + _dot(pn.astype(BF16), vn_ref[:, 0:128])
    on = (pv / l).astype(BF16)
    o_ref[...] = jnp.concatenate(
        [_dot(on[hh * lq:(hh + 1) * lq], wuv_ref[hh]) for hh in range(N_HEADS)], axis=1).astype(BF16)


def _mla_sample(qm, kc, km, vm, wuv, *, batch, lq):
    past = kc.shape[1]
    return pl.pallas_call(
        functools.partial(_mla_sample_kernel, lq=lq),
        grid=(batch,),
        in_specs=[
            pl.BlockSpec((N_HEADS, lq, 256), lambda b: (0, b, 0)),
            pl.BlockSpec((1, past, 256), lambda b: (b, 0, 0)),
            pl.BlockSpec((lq, 256), lambda b: (b, 0)),
            pl.BlockSpec((lq, 256), lambda b: (b, 0)),
            _const_spec(wuv.shape),
        ],
        out_specs=pl.BlockSpec((lq, 256), lambda b: (b, 0)),
        out_shape=jax.ShapeDtypeStruct((batch * lq, 256), BF16),
        compiler_params=_cparams(("parallel",)), name="mla_sample",
    )(qm, kc, km, vm, wuv)


def _diff_sample_kernel(q_ref, kc_ref, vc_ref, kn_ref, vn_ref, bc_ref, bn_ref, lam_ref, subg_ref, o_ref,
                        *, lq, lam_init):
    q8 = _stack_q8(q_ref[...])
    bc = jnp.concatenate([bc_ref[0, hh // 2] for hh in range(2 * N_HEADS)], axis=0)
    bn = jnp.concatenate([bn_ref[0, hh // 2] for hh in range(2 * N_HEADS)], axis=0)
    sc = _dot_nt(q8, kc_ref[0]) + bc
    sn = _dot_nt(q8, kn_ref[...].astype(BF16)) + bn
    m = jnp.maximum(jnp.max(sc, axis=1, keepdims=True), jnp.max(sn, axis=1, keepdims=True))
    pc = jnp.exp2(sc - m)
    pn = jnp.exp2(sn - m)
    l = jnp.sum(pc, axis=1, keepdims=True) + jnp.sum(pn, axis=1, keepdims=True)
    pv = (_dot(pc.astype(BF16), vc_ref[0]) + _dot(pn.astype(BF16), vn_ref[...].astype(BF16))) / l
    on = jnp.concatenate(
        [pv[g * lq:(g + 1) * lq, HEAD_W * (g // 2):HEAD_W * (g // 2 + 1)] for g in range(2 * N_HEADS)], axis=0)
    lam = _diff_lambda(lam_ref, lam_init)
    o_ref[...] = _diff_finish(on, lam, subg_ref[...], lam_init, lq).astype(BF16)


def _diff_sample(dq, kc, vc, dk, dv, bias_c, bias_n, lamv, subg, *, batch, lq, lam_init):
    past = kc.shape[1]
    return pl.pallas_call(
        functools.partial(_diff_sample_kernel, lq=lq, lam_init=lam_init),
        grid=(batch,),
        in_specs=[
            pl.BlockSpec((lq, 256), lambda b: (b, 0)),
            pl.BlockSpec((1, past, 256), lambda b: (b, 0, 0)),
            pl.BlockSpec((1, past, 256), lambda b: (b, 0, 0)),
            pl.BlockSpec((lq, 256), lambda b: (b, 0)),
            pl.BlockSpec((lq, 256), lambda b: (b, 0)),
            _const_spec(bias_c.shape), _const_spec(bias_n.shape),
            _const_spec(lamv.shape), _const_spec(subg.shape),
        ],
        out_specs=pl.BlockSpec((lq, 256), lambda b: (b, 0)),
        out_shape=jax.ShapeDtypeStruct((batch * lq, 256), BF16),
        compiler_params=_cparams(("parallel",)), name="diff_sample",
    )(dq, kc, vc, dk, dv, bias_c, bias_n, lamv, subg)


def _stack4(y, group):
    grp = _lane_group((1,) * (y.ndim - 1) + (y.shape[-1],), group)
    return jnp.concatenate([jnp.where(grp == g, y, 0.0) for g in range(N_HEADS)], axis=-2)


def _diag_sum(f, group):
    rr = f.shape[0] // N_HEADS
    grp = _lane_group((1, f.shape[1]), group)
    out = jnp.where(grp == 0, f[0:rr], 0.0)
    for g in range(1, N_HEADS):
        out = out + jnp.where(grp == g, f[g * rr:(g + 1) * rr], 0.0)
    return out


def _bmm(x, y, group):
    return _dot(x, _stack4(y, group))


def _bdot(x, w):
    return lax.dot_general(x, w, (((2,), (1,)), ((0,), (0,))), preferred_element_type=F32)


def _bdot_nt(x, w):
    return lax.dot_general(x, w, (((2,), (2,)), ((0,), (0,))), preferred_element_type=F32)


def _bbmm(x, y, group):
    return _bdot(x, _stack4(y, group))


def _split3(x):
    hi = x.astype(BF16)
    r1 = x - hi.astype(F32)
    mid = r1.astype(BF16)
    lo = (r1 - mid.astype(F32)).astype(BF16)
    return hi, mid, lo


def _unit_lower_inverse(a, c):
    shape = (1, c, N_HEADS * c)
    i = lax.broadcasted_iota(jnp.int32, shape, 1)
    j = lax.broadcasted_iota(jnp.int32, shape, 2) % c
    eye = jnp.where(i == j, 1.0, 0.0)
    blockdiag = (i // SOLVE_BLOCK) == (j // SOLVE_BLOCK)
    ad = jnp.where(blockdiag, a, 0.0)
    p = _bbmm(ad, ad, c)
    rinv = eye - ad
    rinv = rinv + _bbmm(rinv, p, c)
    for _ in range(2):
        p = _bbmm(p, p, c)
        rinv = rinv + _bbmm(rinv, p, c)
    if c <= SOLVE_BLOCK:
        return rinv
    mm = _bbmm(rinv, a - ad, c)
    t = eye - mm
    p = _bbmm(mm, mm, c)
    t = t + _bbmm(t, p, c)
    nblk = c // SOLVE_BLOCK
    span = 4
    while span < nblk:
        p = _bbmm(p, p, c)
        t = t + _bbmm(t, p, c)
        span *= 2
    return _bbmm(t, rinv, c)


def _to_square(x, c):
    if c == HEAD_W:
        return x
    return jnp.concatenate([x[..., HEAD_W * hh:HEAD_W * hh + c] for hh in range(N_HEADS)], axis=-1)


def _gdn_kernel(qkv_ref, gate_ref, gab_ref, buf_ref, s0_ref, cw_ref, alog_ref, dtb_ref, ng_ref,
                o_ref, conv_ref, s_ref, prev_scr, s_scr, *, tg, c):
    j = pl.program_id(1)
    nj = pl.num_programs(1)

    @pl.when(j == 0)
    def _():
        prev_scr[...] = jnp.zeros_like(prev_scr)
        prev_scr[8 - (GDN_CONV - 1):8, :] = buf_ref[0]
        for hh in range(N_HEADS):
            s_scr[:, HEAD_W * hh:HEAD_W * (hh + 1)] = s0_ref[0, hh]

    x = qkv_ref[...]
    prev8 = prev_scr[...]
    cw = cw_ref[...]
    y = x * cw[GDN_CONV - 1:GDN_CONV]
    for s in range(1, GDN_CONV):
        y = y + _shift_rows(x, prev8, s) * cw[GDN_CONV - 1 - s:GDN_CONV - s]
    prev_scr[...] = x[tg - 8:tg]

    @pl.when(j == nj - 1)
    def _():
        conv_ref[0] = x[tg - (GDN_CONV - 1):tg]

    y = _silu(y)
    q = y[:, 0:256]
    k = y[:, 256:512]
    v = y[:, 512:768]
    gones = jnp.where(_lane_group((GROUP_W, GROUP_W), HEAD_W)
                      == lax.broadcasted_iota(jnp.int32, (GROUP_W, GROUP_W), 0) // HEAD_W,
                      1.0, 0.0).astype(BF16)

    def head_sum(z):
        z_hi, z_mid, z_lo = _split3(z)
        return _dot(z_hi, gones) + _dot(z_mid, gones) + _dot(z_lo, gones)

    q = q * lax.rsqrt(head_sum(q * q) + 1e-6) * (HEAD_W ** -0.5)
    k = k * lax.rsqrt(head_sum(k * k) + 1e-6)
    gab = gab_ref[...]
    za = gab[:, 0:256] + dtb_ref[...]
    softplus = jnp.maximum(za, 0.0) + jnp.log(1.0 + jnp.exp(-jnp.abs(za)))
    g = -jnp.exp(alog_ref[...]) * softplus
    beta = 1.0 / (1.0 + jnp.exp(-gab[:, 256:512]))

    nc = tg // c
    sq = (1, c, N_HEADS * c)
    ri = lax.broadcasted_iota(jnp.int32, sq, 1)
    cj = lax.broadcasted_iota(jnp.int32, sq, 2) % c
    tr = lax.broadcasted_iota(jnp.int32, (tg, tg), 0)
    tc = lax.broadcasted_iota(jnp.int32, (tg, tg), 1)
    tri_b = jnp.where((tc <= tr) & (tc // c == tr // c), 1.0, 0.0).astype(BF16)
    g_hi, g_mid, g_lo = _split3(g)
    decay2 = _dot(tri_b, g_hi) + _dot(tri_b, g_mid) + _dot(tri_b, g_lo)
    chunked = lambda z: z.reshape(nc, c, z.shape[-1])
    qb, kb_, vb, bb, decay = chunked(q), chunked(k), chunked(v), chunked(beta), chunked(decay2)
    dsq = _to_square(decay, c)
    drow = jnp.sum(jnp.where(ri == cj, dsq, 0.0), axis=1, keepdims=True)
    lm = jnp.where(cj <= ri, jnp.exp(jnp.where(cj <= ri, dsq - drow, 0.0)), 0.0)
    kbeta = kb_ * bb
    kq = _bdot_nt(jnp.concatenate([kbeta, qb], axis=1), _stack4(kb_, HEAD_W))
    a_mat = jnp.where(cj < ri, kq[:, 0:c] * lm, 0.0)
    qk_all = kq[:, c:2 * c] * lm
    edec = jnp.exp(decay)
    tinv = _unit_lower_inverse(a_mat, c)
    uw = _bdot(tinv, jnp.concatenate([_stack4(vb * bb, HEAD_W), _stack4(kbeta * edec, HEAD_W)], axis=2))
    dlast = decay[:, c - 1:c]
    qd_all = qb * edec
    kt_all = kb_ * jnp.exp(dlast - decay)
    gl_all = jnp.exp(dlast)
    pre = [(uw[n, :, 0:GROUP_W], uw[n, :, GROUP_W:2 * GROUP_W], qk_all[n], qd_all[n], kt_all[n], gl_all[n])
           for n in range(nc)]

    s_mat = s_scr[...]
    outs = []
    for (u, w, qk, qd, kt, glast) in pre:
        ws = _dot(jnp.concatenate([w, qd], axis=0), _stack4(s_mat, HEAD_W))
        vnew = u - ws[0:c]
        outs.append(ws[c:2 * c] + _bmm(qk, vnew, HEAD_W))
        s_mat = s_mat * glast + _diag_sum(_dot_tn(kt, vnew), HEAD_W)
    s_scr[...] = s_mat

    o = outs[0] if len(outs) == 1 else jnp.concatenate(outs, axis=0)
    o = o * lax.rsqrt(head_sum(o * o) * (1.0 / HEAD_W) + EPS) * ng_ref[...]
    o_ref[...] = (o * _silu(gate_ref[...])).astype(BF16)

    @pl.when(j == nj - 1)
    def _():
        s_fin = s_scr[...]
        for hh in range(N_HEADS):
            s_ref[0, hh] = s_fin[:, HEAD_W * hh:HEAD_W * (hh + 1)]


def _gdn(gqkv, ggate, gab, buf, s0, cw, alog, dtb, ng, *, batch, seq, tg, c):
    nj = seq // tg
    t = batch * seq
    row = lambda wd: pl.BlockSpec((tg, wd), lambda b, j: (b * nj + j, 0))
    return pl.pallas_call(
        functools.partial(_gdn_kernel, tg=tg, c=c),
        grid=(batch, nj),
        in_specs=[
            row(768), row(256), row(512),
            pl.BlockSpec((1, GDN_CONV - 1, 768), lambda b, j: (b, 0, 0)),
            pl.BlockSpec((1, N_HEADS, HEAD_W, HEAD_W), lambda b, j: (b, 0, 0, 0)),
            _const_spec(cw.shape), _const_spec(alog.shape), _const_spec(dtb.shape), _const_spec(ng.shape),
        ],
        out_specs=[
            row(256),
            pl.BlockSpec((1, GDN_CONV - 1, 768), lambda b, j: (b, 0, 0)),
            pl.BlockSpec((1, N_HEADS, HEAD_W, HEAD_W), lambda b, j: (b, 0, 0, 0)),
        ],
        out_shape=[
            jax.ShapeDtypeStruct((t, 256), BF16),
            jax.ShapeDtypeStruct((batch, GDN_CONV - 1, 768), F32),
            jax.ShapeDtypeStruct((batch, N_HEADS, HEAD_W, HEAD_W), F32),
        ],
        scratch_shapes=[pltpu.VMEM((8, 768), F32), pltpu.VMEM((HEAD_W, GROUP_W), F32)],
        compiler_params=_cparams(("parallel", "arbitrary")), name="gdn",
    )(gqkv, ggate, gab, buf, s0, cw, alog, dtb, ng)


def _ffn_kernel(x_ref, om_ref, od_ref, os_ref, og_ref, buf_ref, wout_ref, g2_ref, wup_ref, cw_ref, wdn_ref,
                fg_ref, y_ref, conv_ref, prev_scr, act_scr, *, tm, d_ff, ft, final):
    j = pl.program_id(1)
    nj = pl.num_programs(1)

    @pl.when(j == 0)
    def _():
        prev_scr[...] = jnp.zeros_like(prev_scr)
        prev_scr[8 - (FFN_CONV - 1):8, :] = buf_ref[0]

    mixed = jnp.concatenate([om_ref[...], od_ref[...], os_ref[...], og_ref[...]], axis=1)
    x1 = x_ref[...] + _dot(mixed, wout_ref[...])
    h2 =(x1 * lax.rsqrt(jnp.mean(x1 * x1, axis=-1, keepdims=True) + EPS) * g2_ref[...]).astype(BF16)

    def conv_cols(off):
        a = _dot(h2, wup_ref[:, off:off + ft])
        prev8 = prev_scr[:, off:off + ft]
        cw = cw_ref[:, off:off + ft]
        y = a * cw[FFN_CONV - 1:FFN_CONV]
        for s in range(1, FFN_CONV):
            y = y + _shift_rows(a, prev8, s) * cw[FFN_CONV - 1 - s:FFN_CONV - s]
        prev_scr[:, off:off + ft] = a[tm - 8:tm]
        return y

    for fi in range(d_ff // ft):
        gate = conv_cols(fi * ft)
        up = conv_cols(d_ff + fi * ft)
        act_scr[:, fi * ft:(fi + 1) * ft] = (_silu(gate) * up).astype(BF16)
    acc = x1 + _dot(act_scr[...], wdn_ref[...])

    if final:
        acc = acc * lax.rsqrt(jnp.mean(acc * acc, axis=-1, keepdims=True) + EPS) * fg_ref[...]
    y_ref[...] = acc

    @pl.when(j == nj - 1)
    def _():
        conv_ref[0] = prev_scr[8 - (FFN_CONV - 1):8, :]


def _ffn(x, om, od, osg, og, buf, wout, g2, wup, cw, wdn, fg, *, batch, seq, tm, ft, final):
    nj = seq // tm
    t, d = x.shape
    d_ff = wdn.shape[0]
    row = lambda wd: pl.BlockSpec((tm, wd), lambda b, j: (b * nj + j, 0))
    return pl.pallas_call(
        functools.partial(_ffn_kernel, tm=tm, d_ff=d_ff, ft=ft, final=final),
        grid=(batch, nj),
        in_specs=[
            row(d), row(256), row(256), row(256), row(256),
            pl.BlockSpec((1, FFN_CONV - 1, 2 * d_ff), lambda b, j: (b, 0, 0)),
            _const_spec(wout.shape), _const_spec(g2.shape), _const_spec(wup.shape),
            _const_spec(cw.shape), _const_spec(wdn.shape), _const_spec(fg.shape),
        ],
        out_specs=[row(d), pl.BlockSpec((1, FFN_CONV - 1, 2 * d_ff), lambda b, j: (b, 0, 0))],
        out_shape=[jax.ShapeDtypeStruct((t, d), F32),
                   jax.ShapeDtypeStruct((batch, FFN_CONV - 1, 2 * d_ff), F32)],
        scratch_shapes=[pltpu.VMEM((8, 2 * d_ff), F32), pltpu.VMEM((tm, d_ff), BF16)],
        compiler_params=_cparams(("parallel", "arbitrary")), name="ffn",
    )(x, om, od, osg, og, buf, wout, g2, wup, cw, wdn, fg)


def _rot_cols(w):
    d = w.shape[0]
    wg = w.reshape(d, -1, 2, MLA_DR // 2)
    return jnp.concatenate([-wg[:, :, 1:2], wg[:, :, 0:1]], axis=2).reshape(d, -1)


def _permute_w_in(w_in):
    d = w_in.shape[0]
    pts = np.cumsum([384, 128, 32, 256, 256, 256, 256, 256, 768, 256, 4, 4])[:-1]
    mq, mlat, mkr, dq, dk, dv, su, sv, gqkv, ggate, ga, gb = jnp.split(w_in, pts, axis=1)
    mq = mq.reshape(d, N_HEADS, MLA_DN + MLA_DR)
    qn = mq[:, :, :MLA_DN].reshape(d, N_HEADS * MLA_DN)
    qr = mq[:, :, MLA_DN:].reshape(d, N_HEADS * MLA_DR)
    kr4 = jnp.tile(mkr, (1, N_HEADS))
    rep = lambda a: jnp.repeat(a, HEAD_W, axis=1)
    w = jnp.concatenate([qn, qr, _rot_cols(qr), mlat, kr4, _rot_cols(kr4),
                         dq, dk, dv, su, sv, gqkv, ggate, rep(ga), rep(gb)], axis=1)
    assert w.shape[1] == C_TOTAL
    wdt = jnp.concatenate([jnp.transpose(dq), jnp.transpose(dv)], axis=0)
    return w.astype(BF16), wdt.astype(BF16)


def _rope_tables(pos):
    inv = ROPE_THETA ** (-jnp.arange(0, MLA_DR, 2, dtype=F32) / MLA_DR)
    ang = pos.astype(F32)[:, None] * inv[None, :]
    cos = jnp.tile(jnp.cos(ang), (1, 2 * N_HEADS))
    sin = jnp.tile(jnp.sin(ang), (1, 2 * N_HEADS))
    return cos, sin


def _layer_params(l, p):
    row = lambda a: a.reshape(1, -1)
    rep = lambda a: jnp.repeat(a, HEAD_W).reshape(1, -1)
    sgu_w = p['sgu_w'][l]
    w_perm, w_dt = _permute_w_in(p['w_in'][l])
    return dict(
        w_in=w_perm, w_dt=w_dt,
        g1=row(p['norm1_g'][l]),
        wukt=jnp.transpose(p['mla_w_uk'][l], (1, 2, 0)).astype(BF16),
        wuv=jnp.transpose(p['mla_w_uv'][l], (1, 0, 2)).astype(BF16),
        latg=row(p['mla_lat_g'][l]),
        sgu_w=sgu_w,
        sgu_b=p['sgu_b'][l],
        lng=row(p['sgu_ln_g'][l]), lnb=row(p['sgu_ln_b'][l]),
        lamv=jnp.stack([p['diff_lam_q1'][l], p['diff_lam_k1'][l], p['diff_lam_q2'][l], p['diff_lam_k2'][l]]),
        subg=row(p['diff_sub_g'][l]),
        gcw=p['gdn_conv_w'][l],
        alog=rep(p['gdn_a_log'][l]), dtb=rep(p['gdn_dt_bias'][l]),
        ng=jnp.tile(p['gdn_norm_g'][l], N_HEADS).reshape(1, -1),
        wout=p['w_out'][l].astype(BF16),
        g2=row(p['norm2_g'][l]),
        wup=p['ffn_w_up'][l].astype(BF16),
        fcw=p['ffn_conv_w'][l],
        wdn=p['ffn_w_down'][l].astype(BF16),
        fg=row(p['final_g']),
    )


def _sgu_tables(lp, c):
    w = lp['sgu_w'][:, :c, :c].reshape(N_HEADS * c, c)
    b = jnp.repeat(jnp.transpose(lp['sgu_b'][:, :c]), HEAD_W, axis=1)
    return w, b


def _run_inproj(x, lp, cos, sin, *, tm, sgu_c):
    sw, sb = _sgu_tables(lp, sgu_c)
    return _inproj(x, lp['g1'], lp['w_in'], lp['w_dt'], cos, sin, lp['wukt'], lp['latg'], sw, sb, lp['lng'], lp['lnb'],
                   tm=tm, sgu_c=sgu_c)


def kernel(x_prompt, x_sample, cache_mla_latent, cache_mla_krope, cache_diff_k, cache_diff_v, state_gdn_conv, state_gdn_s, state_ffn_conv, t5_table, final_g, norm1_g, w_in, mla_lat_g, mla_w_uk, mla_w_uv, diff_lam_q1, diff_lam_k1, diff_lam_q2, diff_lam_k2, diff_sub_g, sgu_ln_g, sgu_ln_b, sgu_w, sgu_b, gdn_conv_w, gdn_a_log, gdn_dt_bias, gdn_norm_g, w_out, norm2_g, ffn_w_up, ffn_conv_w, ffn_w_down):
    p = dict(final_g=final_g, norm1_g=norm1_g, w_in=w_in, mla_lat_g=mla_lat_g, mla_w_uk=mla_w_uk,
             mla_w_uv=mla_w_uv, diff_lam_q1=diff_lam_q1, diff_lam_k1=diff_lam_k1, diff_lam_q2=diff_lam_q2,
             diff_lam_k2=diff_lam_k2, diff_sub_g=diff_sub_g, sgu_ln_g=sgu_ln_g, sgu_ln_b=sgu_ln_b,
             sgu_w=sgu_w, sgu_b=sgu_b, gdn_conv_w=gdn_conv_w, gdn_a_log=gdn_a_log, gdn_dt_bias=gdn_dt_bias,
             gdn_norm_g=gdn_norm_g, w_out=w_out, norm2_g=norm2_g, ffn_w_up=ffn_w_up, ffn_conv_w=ffn_conv_w,
             ffn_w_down=ffn_w_down)
    depth = w_in.shape[0]
    bp, sp, d = x_prompt.shape
    bs, ls, _ = x_sample.shape
    past = cache_mla_latent.shape[2]
    d_ff = ffn_w_down.shape[1]
    assert past % CHUNK == 0 and ls <= CHUNK

    tm_p = min(512, sp)
    tq = min(512, sp)
    tg_p = min(512, sp)
    ft = 256
    sgu_cp = min(SGU_CHUNK, sp)
    gdn_cp = min(GDN_CHUNK, sp)

    cos_p, sin_p = _rope_tables(jnp.arange(sp, dtype=jnp.int32))
    pos_s = past + jnp.arange(ls, dtype=jnp.int32)
    cos_s, sin_s = _rope_tables(jnp.tile(pos_s, bs))

    assert tq >= DIFF_CORNER
    stack_t = lambda b: jnp.transpose(jnp.repeat(b, 2, axis=0), (2, 0, 1)).reshape(b.shape[2], -1)
    bias_p = stack_t(_t5_bias(t5_table, tq, tq, (0,), (0,))[0])
    corner = _t5_bias(t5_table, DIFF_CORNER, DIFF_CORNER, (DIFF_CORNER,), (0,))[0]
    bias_pc = stack_t(jnp.pad(corner, ((0, 0), (0, tq - DIFF_CORNER), (0, 0))))
    bias_sc = _t5_bias(t5_table, ls, past, (past,), (0,))
    bias_sn = _t5_bias(t5_table, ls, ls, (past,), (past,))

    xp = x_prompt.reshape(bp * sp, d)
    xs = x_sample.reshape(bs * ls, d)
    zeros_gconv = jnp.zeros((bp, GDN_CONV - 1, 3 * GROUP_W), F32)
    zeros_gs = jnp.zeros((bp, N_HEADS, HEAD_W, HEAD_W), F32)
    zeros_fconv = jnp.zeros((bp, FFN_CONV - 1, 2 * d_ff), F32)

    outs_p, outs_s = [], []
    for l in range(depth):
        lp = _layer_params(l, p)
        lam_init = 0.8 - 0.6 * math.exp(-0.3 * l)
        final = l == depth - 1

        (qm, km, vm, lat, kr, dq, dk, dv, kd, dqt, vdt, osgu, _vn, gqkv, ggate, gab) = _run_inproj(
            xp, lp, cos_p, sin_p, tm=tm_p, sgu_c=sgu_cp)
        o_mla = _mla_prompt(qm, km, vm, lp['wuv'], batch=bp, seq=sp, tq=tq, tk=2 * tq if sp % (2 * tq) == 0 else tq)
        o_diff = _diff_prompt(dqt, kd, vdt, bias_p, bias_pc, lp['lamv'], jnp.tile(lp['subg'], (1, N_HEADS)).reshape(-1, 1),
                              batch=bp, seq=sp, tq=tq, lam_init=lam_init)
        o_gdn, gconv, gs = _gdn(gqkv, ggate, gab, zeros_gconv, zeros_gs, lp['gcw'], lp['alog'], lp['dtb'],
                                lp['ng'], batch=bp, seq=sp, tg=tg_p, c=gdn_cp)
        xp, fconv = _ffn(xp, o_mla, o_diff, osgu, o_gdn, zeros_fconv, lp['wout'], lp['g2'], lp['wup'],
                         lp['fcw'], lp['wdn'], lp['fg'], batch=bp, seq=sp, tm=tm_p, ft=ft, final=final)
        outs_p.append(dict(
            lat=lat.reshape(bp, sp, MLA_R), kr=kr.reshape(bp, sp, MLA_DR),
            dk=dk.reshape(bp, sp, N_HEADS, 2, DIFF_DH), dv=dv.reshape(bp, sp, N_HEADS, 2 * DIFF_DH),
            gconv=gconv, gs=gs, fconv=fconv))

        (qm, km, vm, lat, kr, dq, dk, dv, kd, _dqt, _vdt, osgu, vn, gqkv, ggate, gab) = _run_inproj(
            xs, lp, cos_s, sin_s, tm=bs * ls, sgu_c=min(SGU_CHUNK, ls))
        kc_m = jnp.concatenate([cache_mla_latent[l], jnp.tile(cache_mla_krope[l], (1, 1, N_HEADS))],
                               axis=-1).astype(BF16)
        o_mla = _mla_sample(qm, kc_m, km, vm, lp['wuv'], batch=bs, lq=ls)
        kc_d = cache_diff_k[l].reshape(bs, past, GROUP_W).astype(BF16)
        vc_d = cache_diff_v[l].reshape(bs, past, GROUP_W).astype(BF16)
        o_diff = _diff_sample(dq, kc_d, vc_d, dk, dv, bias_sc, bias_sn, lp['lamv'], lp['subg'],
                              batch=bs, lq=ls, lam_init=lam_init)
        o_gdn, gconv, gs = _gdn(gqkv, ggate, gab, state_gdn_conv[l], state_gdn_s[l], lp['gcw'], lp['alog'],
                                lp['dtb'], lp['ng'], batch=bs, seq=ls, tg=ls, c=min(GDN_CHUNK, ls))
        xs, fconv = _ffn(xs, o_mla, o_diff, osgu, o_gdn, state_ffn_conv[l], lp['wout'], lp['g2'], lp['wup'],
                         lp['fcw'], lp['wdn'], lp['fg'], batch=bs, seq=ls, tm=ls, ft=ft, final=final)
        outs_s.append(dict(
            lat=lat.reshape(bs, ls, MLA_R), kr=kr.reshape(bs, ls, MLA_DR),
            dk=dk.reshape(bs, ls, N_HEADS, 2, DIFF_DH), dv=dv.reshape(bs, ls, N_HEADS, 2 * DIFF_DH),
            gconv=gconv, gs=gs, fconv=fconv, sv=vn.reshape(bs, ls, GROUP_W)))

    st = lambda lst, key: jnp.stack([o[key] for o in lst])
    return (xp.reshape(bp, sp, d), xs.reshape(bs, ls, d),
            st(outs_p, 'lat'), st(outs_p, 'kr'), st(outs_p, 'dk'), st(outs_p, 'dv'),
            st(outs_p, 'gconv'), st(outs_p, 'gs'), st(outs_p, 'fconv'),
            st(outs_s, 'lat'), st(outs_s, 'kr'), st(outs_s, 'dk'), st(outs_s, 'dv'),
            st(outs_s, 'gconv'), st(outs_s, 'gs'), st(outs_s, 'fconv'), st(outs_s, 'sv'))
```

```python
import functools
import math

import numpy as np
import jax
import jax.numpy as jnp
from jax import lax
from jax.experimental import pallas as pl
from jax.experimental.pallas import tpu as pltpu

F32 = jnp.float32
BF16 = jnp.bfloat16

CHUNK = 64
EPS = 1e-6
N_HEADS = 4
HEAD_W = 64
GROUP_W = 256
MLA_DN, MLA_DR, MLA_R = 64, 32, 128
DIFF_DH = 32
ROPE_THETA = 10000.0
SGU_CHUNK = 128
GDN_CHUNK = 64
GDN_CONV = 4
FFN_CONV = 3
T5_BUCKETS = 32
T5_MAX_DIST = 128
NEG = -1e30
LOG2E = math.log2(math.e)
SOLVE_BLOCK = 16
DIFF_CORNER = 128
MLA_VT_ROWS = 144
VT_ROWS = 80

VMEM_LIMIT_BYTES = 56 * 1024 * 1024

C_QN, C_QR, C_QRR, C_LAT, C_KR, C_KRR = 0, 256, 384, 512, 640, 768
C_DQ, C_DK, C_DV, C_SU, C_SV = 896, 1152, 1408, 1664, 1920
C_GQKV, C_GG, C_GA, C_GB = 2176, 2944, 3200, 3456
C_TOTAL = 3712
WT_DQ, WT_ROWS = 640, 1152


def _cparams(sem):
    return pltpu.CompilerParams(dimension_semantics=sem, vmem_limit_bytes=VMEM_LIMIT_BYTES)


def _const_spec(shape):
    nd = len(shape)
    return pl.BlockSpec(shape, lambda *_: (0,) * nd, pipeline_mode=pl.Buffered(1))


def _dot(a, b):
    return jnp.dot(a, b, preferred_element_type=F32)


def _dot_nt(a, b):
    return lax.dot_general(a, b, (((1,), (1,)), ((), ())), preferred_element_type=F32)


def _dot_tn(a, b):
    return lax.dot_general(a, b, (((0,), (0,)), ((), ())), preferred_element_type=F32)


def _lane_group(shape, group):
    return lax.broadcasted_iota(jnp.int32, shape, len(shape) - 1) // group


def _silu(x):
    return x * (1.0 / (1.0 + jnp.exp(-x)))


def _shift_rows(x, prev8, s):
    if s == 0:
        return x
    r = pltpu.roll(x, s, axis=0)
    rp = pltpu.roll(prev8, s, axis=0)
    row8 = lax.broadcasted_iota(jnp.int32, rp.shape, 0)
    top = jnp.where(row8 < s, rp, r[0:8])
    if x.shape[0] == 8:
        return top
    return jnp.concatenate([top, r[8:]], axis=0)


_INPROJ_COMMON = ("km", "lat", "kr", "dk", "dv", "kd", "osgu", "vn", "gqkv", "ggate", "gab")
_INPROJ_PROMPT = ("qmt", "vmt", "dqt", "vdt")
_INPROJ_SAMPLE = ("qm", "vm", "dq")


def _inproj_kernel(x_ref, g1_ref, w_ref, wt_ref, cos_ref, sin_ref, cost_ref, sint_ref, wukt_ref, wuk_ref,
                   latg_ref, latgt_ref, sguw_ref, sgub_ref, lng_ref, lnb_ref, *out_refs, sgu_c, prompt):
    o = dict(zip(_INPROJ_COMMON + (_INPROJ_PROMPT if prompt else _INPROJ_SAMPLE), out_refs))
    tm = x_ref.shape[0]
    x = x_ref[...]
    h = (x * lax.rsqrt(jnp.mean(x * x, axis=-1, keepdims=True) + EPS) * g1_ref[...]).astype(BF16)

    def sec(off, n):
        return _dot(h, w_ref[:, off:off + n])

    cos = cos_ref[...]
    sin = sin_ref[...]
    mla_scale = (MLA_DN + MLA_DR) ** -0.5 * LOG2E
    diff_scale = DIFF_DH ** -0.5 * LOG2E

    zl = sec(C_LAT, 384)
    zlat = zl[:, 0:128]
    lat = zlat * lax.rsqrt(jnp.mean(zlat * zlat, axis=-1, keepdims=True) + EPS) * latg_ref[...]
    kr4 = zl[:, 128:256] * cos + zl[:, 256:384] * sin
    o["lat"][...] = lat
    o["kr"][...] = kr4[:, 0:MLA_DR]
    lat_b = lat.astype(BF16)
    o["km"][:, 0:128] = lat_b
    o["km"][:, 128:256] = kr4.astype(BF16)

    zd = sec(C_DK, 512)
    dk = zd[:, 0:256]
    dv = zd[:, 256:512]
    o["dk"][...] = dk
    o["dv"][...] = dv
    o["kd"][...] = dk.astype(BF16)

    if prompt:
        zt = _dot_nt(wt_ref[0:WT_DQ, :], h)
        qrt = (zt[256:384] * cost_ref[...] + zt[384:512] * sint_ref[...]) * mla_scale
        head_of_row = lax.broadcasted_iota(jnp.int32, (128, 1), 0) // MLA_DR
        for hh in range(N_HEADS):
            qn = (zt[MLA_DN * hh:MLA_DN * (hh + 1)] * mla_scale).astype(BF16)
            o["qmt"][hh, 0:128, :] = _dot(wuk_ref[hh], qn).astype(BF16)
            o["qmt"][hh, 128:256, :] = jnp.where(head_of_row == hh, qrt, 0.0).astype(BF16)
        zlt = zt[512:640]
        latt = zlt * lax.rsqrt(jnp.mean(zlt * zlt, axis=0, keepdims=True) + EPS) * latgt_ref[...]
        o["vmt"][0] = jnp.concatenate([latt, jnp.ones((MLA_VT_ROWS - MLA_R, tm), F32)], axis=0).astype(BF16)
        zt2 = _dot_nt(wt_ref[WT_DQ:WT_ROWS, :], h)
        o["dqt"][...] = (zt2[0:GROUP_W] * diff_scale).astype(BF16)
        ones_rows = jnp.ones((VT_ROWS - HEAD_W, tm), F32)
        for hh in range(N_HEADS):
            vt = zt2[GROUP_W + HEAD_W * hh:GROUP_W + HEAD_W * (hh + 1)]
            o["vdt"][hh, 0] = jnp.concatenate([vt, ones_rows], axis=0).astype(BF16)
    else:
        zq = sec(C_QN, 512)
        qr = (zq[:, 256:384] * cos + zq[:, 384:512] * sin) * mla_scale
        head_of_lane = _lane_group((1, 128), MLA_DR)
        for hh in range(N_HEADS):
            qn = (zq[:, MLA_DN * hh:MLA_DN * (hh + 1)] * mla_scale).astype(BF16)
            o["qm"][hh, :, 0:128] = _dot(qn, wukt_ref[hh]).astype(BF16)
            o["qm"][hh, :, 128:256] = jnp.where(head_of_lane == hh, qr, 0.0).astype(BF16)
        o["vm"][:, 0:128] = lat_b
        o["vm"][:, 128:256] = jnp.ones((tm, 128), BF16)
        o["dq"][...] = (sec(C_DQ, 256) * diff_scale).astype(BF16)

    zs = sec(C_SU, 512)
    su = zs[:, 0:256]
    sv = zs[:, 256:512]
    mu = jnp.mean(sv, axis=-1, keepdims=True)
    svc = sv - mu
    var = jnp.mean(svc * svc, axis=-1, keepdims=True)
    vn = svc * lax.rsqrt(var + EPS) * lng_ref[...] + lnb_ref[...]
    o["vn"][...] = vn
    c = sgu_c
    wr = lax.broadcasted_iota(jnp.int32, (N_HEADS * c, c), 0) % c
    wc = lax.broadcasted_iota(jnp.int32, (N_HEADS * c, c), 1)
    w4 = jnp.where(wc <= wr, sguw_ref[...], 0.0).astype(BF16)
    hl = _lane_group((1, GROUP_W), HEAD_W)
    sgub = sgub_ref[...]
    for ci in range(tm // c):
        rows = slice(ci * c, (ci + 1) * c)
        m4 = _dot(w4, vn[rows].astype(BF16))
        mix = jnp.where(hl == 0, m4[0:c], 0.0)
        for hh in range(1, N_HEADS):
            mix = mix + jnp.where(hl == hh, m4[hh * c:(hh + 1) * c], 0.0)
        o["osgu"][rows, :] = (su[rows] * (mix + sgub)).astype(BF16)

    zg = sec(C_GQKV, 1536)
    o["gqkv"][...] = zg[:, 0:768]
    o["ggate"][...] = zg[:, 768:1024]
    o["gab"][...] = zg[:, 1024:1536]


def _inproj(x, g1, w, wt, cos, sin, cost, sint, wukt, wuk, latg, latgt, sguw, sgub, lng, lnb, *, tm, sgu_c, prompt):
    t, d = x.shape
    n = t // tm
    npos = cos.shape[0] // tm
    row = lambda wd: pl.BlockSpec((tm, wd), lambda i: (i, 0))
    shapes = dict(
        km=((t, 256), BF16, row(256)), lat=((t, MLA_R), F32, row(MLA_R)), kr=((t, MLA_DR), F32, row(MLA_DR)),
        dk=((t, 256), F32, row(256)), dv=((t, 256), F32, row(256)), kd=((t, 256), BF16, row(256)),
        osgu=((t, 256), BF16, row(256)), vn=((t, 256), F32, row(256)), gqkv=((t, 768), F32, row(768)),
        ggate=((t, 256), F32, row(256)), gab=((t, 512), F32, row(512)),
        qmt=((N_HEADS, 256, t), BF16, pl.BlockSpec((N_HEADS, 256, tm), lambda i: (0, 0, i))),
        vmt=((n, MLA_VT_ROWS, tm), BF16, pl.BlockSpec((1, MLA_VT_ROWS, tm), lambda i: (i, 0, 0))),
        dqt=((GROUP_W, t), BF16, pl.BlockSpec((GROUP_W, tm), lambda i: (0, i))),
        vdt=((N_HEADS, n, VT_ROWS, tm), BF16, pl.BlockSpec((N_HEADS, 1, VT_ROWS, tm), lambda i: (0, i, 0, 0))),
        qm=((N_HEADS, t, 256), BF16, pl.BlockSpec((N_HEADS, tm, 256), lambda i: (0, i, 0))),
        vm=((t, 256), BF16, row(256)), dq=((t, 256), BF16, row(256)),
    )
    names = _INPROJ_COMMON + (_INPROJ_PROMPT if prompt else _INPROJ_SAMPLE)
    pos_row = pl.BlockSpec((tm, 128), lambda i: (i % npos, 0))
    pos_col = pl.BlockSpec((128, tm), lambda i: (0, i % npos))
    consts = (g1, w, wt)
    in_specs = ([row(d)] + [_const_spec(a.shape) for a in consts] + [pos_row, pos_row, pos_col, pos_col]
                + [_const_spec(a.shape) for a in (wukt, wuk, latg, latgt, sguw, sgub, lng, lnb)])
    outs = pl.pallas_call(
        functools.partial(_inproj_kernel, sgu_c=sgu_c, prompt=prompt),
        grid=(n,), in_specs=in_specs,
        out_specs=[shapes[k][2] for k in names],
        out_shape=[jax.ShapeDtypeStruct(shapes[k][0], shapes[k][1]) for k in names],
        compiler_params=_cparams(("parallel",)), name="inproj",
    )(x, g1, w, wt, cos, sin, cost, sint, wukt, wuk, latg, latgt, sguw, sgub, lng, lnb)
    return dict(zip(names, outs))


def _t5_thresholds():
    nb = T5_BUCKETS // 2
    max_exact = nb // 2
    ratio = T5_MAX_DIST // max_exact
    thr = []
    for j in range(1, nb - max_exact):
        n = max_exact
        while n ** (nb - max_exact) < (ratio ** j) * (max_exact ** (nb - max_exact)):
            n += 1
        thr.append(n)
    return nb, max_exact, thr


def _bias_kernel(t_ref, o_ref, *, q0s, k0s):
    nb, max_exact, thr = _t5_thresholds()
    tq, tk = o_ref.shape[2], o_ref.shape[3]
    row = lax.broadcasted_iota(jnp.int32, (tq, tk), 0)
    col = lax.broadcasted_iota(jnp.int32, (tq, tk), 1)
    for di, (q0, k0) in enumerate(zip(q0s, k0s)):
        qpos = row + q0
        kpos = col + k0
        rel = kpos - qpos
        n = jnp.abs(rel)
        visible = (kpos // CHUNK) <= (qpos // CHUNK)
        for hh in range(N_HEADS):
            def side(base):
                val = jnp.full((tq, tk), t_ref[base + nb - 1, hh], F32)
                for j in range(len(thr) - 1, -1, -1):
                    val = jnp.where(n < thr[j], t_ref[base + max_exact + j, hh], val)
                for e in range(max_exact - 1, -1, -1):
                    val = jnp.where(n == e, t_ref[base + e, hh], val)
                return val
            b = jnp.where(rel > 0, side(nb), side(0)) - t_ref[nb - 1, hh]
            o_ref[di, hh] = jnp.where(visible, b * LOG2E, NEG)


def _t5_bias(t5_table, tq, tk, q0s, k0s):
    nd = len(q0s)
    return pl.pallas_call(
        functools.partial(_bias_kernel, q0s=tuple(q0s), k0s=tuple(k0s)),
        in_specs=[pl.BlockSpec(memory_space=pltpu.SMEM)],
        out_shape=jax.ShapeDtypeStruct((nd, N_HEADS, tq, tk), F32),
        name="t5_bias",
    )(t5_table)


def _mla_prompt_kernel(qt_ref, k_ref, vt_ref, wuvt_ref, o_ref, *, tq):
    qi = pl.program_id(1)
    qlim = ((qi * tq + lax.broadcasted_iota(jnp.int32, (1, tq), 1)) // CHUNK + 1) * CHUNK
    krow = lax.broadcasted_iota(jnp.int32, (tq, 1), 0)

    def step(kb, carry, masked):
        k0 = pl.multiple_of(kb * tq, tq)
        kblk = k_ref[0, pl.ds(k0, tq), :]
        scores = [_dot(kblk, qt_ref[hh]) for hh in range(N_HEADS)]
        ps, mns, alphas = [], [], []
        for hh in range(N_HEADS):
            m = carry[2 * hh]
            s = scores[hh]
            if masked:
                s = jnp.where(krow + k0 < qlim, s, NEG)
            mn = jnp.maximum(m, jnp.max(s, axis=0, keepdims=True))
            alphas.append(jnp.exp2(m - mn))
            ps.append(jnp.exp2(s - mn).astype(BF16))
            mns.append(mn)
        out = []
        for hh in range(N_HEADS):
            out += [mns[hh], alphas[hh] * carry[2 * hh + 1] + _dot(vt_ref[kb], ps[hh])]
        return tuple(out)

    carry = (jnp.full((1, tq), NEG, F32), jnp.zeros((MLA_VT_ROWS, tq), F32)) * N_HEADS
    carry = lax.fori_loop(0, qi, functools.partial(step, masked=False), carry)
    carry = step(qi, carry, True)
    outs = []
    for hh in range(N_HEADS):
        acc = carry[2 * hh + 1]
        on = (acc[0:MLA_R] * (1.0 / acc[MLA_R:MLA_R + 1])).astype(BF16)
        outs.append(_dot(wuvt_ref[hh], on))
    o_ref[...] = jnp.transpose(jnp.concatenate(outs, axis=0)).astype(BF16)


def _mla_prompt(qmt, km, vmt, wuvt, *, batch, seq, tq):
    assert seq % tq == 0 and vmt.shape[2] == tq and tq % CHUNK == 0
    nq = seq // tq
    t = batch * seq
    return pl.pallas_call(
        functools.partial(_mla_prompt_kernel, tq=tq),
        grid=(batch, nq),
        in_specs=[
            pl.BlockSpec((N_HEADS, 256, tq), lambda b, i: (0, 0, b * nq + i)),
            pl.BlockSpec((1, seq, 256), lambda b, i: (b, 0, 0)),
            pl.BlockSpec((nq, MLA_VT_ROWS, tq), lambda b, i: (b, 0, 0)),
            _const_spec(wuvt.shape),
        ],
        out_specs=pl.BlockSpec((tq, 256), lambda b, i: (b * nq + i, 0)),
        out_shape=jax.ShapeDtypeStruct((t, 256), BF16),
        compiler_params=_cparams(("parallel", "arbitrary")), name="mla_prompt",
    )(qmt, km.reshape(batch, seq, 256), vmt, wuvt)


def _diff_lambda(lam_ref, lam_init):
    l = lam_ref[...]
    a = jnp.sum(l[0:1] * l[1:2], axis=-1, keepdims=True)
    b = jnp.sum(l[2:3] * l[3:4], axis=-1, keepdims=True)
    return jnp.exp(a) - jnp.exp(b) + lam_init


def _stack_q8(q):
    grp = _lane_group((1, GROUP_W), DIFF_DH)
    return jnp.concatenate([jnp.where(grp == g, q, jnp.zeros_like(q)) for g in range(2 * N_HEADS)], axis=0)


def _diff_finish(on, lam, subg, lam_init, tq):
    outs = []
    for hh in range(N_HEADS):
        o = on[(2 * hh) * tq:(2 * hh + 1) * tq] - lam * on[(2 * hh + 1) * tq:(2 * hh + 2) * tq]
        o = o * lax.rsqrt(jnp.mean(o * o, axis=-1, keepdims=True) + 1e-5) * subg
        outs.append(o * (1.0 - lam_init))
    return jnp.concatenate(outs, axis=1)


def _diff_prompt_kernel(qt_ref, k_ref, vt_ref, bias_ref, cbias_ref, lam_ref, subg_ref, o_ref, *, tq, lam_init):
    qi = pl.program_id(1)
    rg = 2 * tq
    qt = qt_ref[...]
    fgrp = lax.broadcasted_iota(jnp.int32, (GROUP_W, 1), 0) // DIFF_DH
    qts = [jnp.concatenate([jnp.where(fgrp == 2 * hh + c, qt, jnp.zeros_like(qt)) for c in range(2)], axis=1)
           for hh in range(N_HEADS)]

    def step(kb, carry, bias):
        k0 = pl.multiple_of(kb * tq, tq)
        kblk = k_ref[0, pl.ds(k0, tq), :]
        scores = [_dot(kblk, qts[hh]) for hh in range(N_HEADS)]
        ps, mns, alphas = [], [], []
        for hh in range(N_HEADS):
            s = scores[hh]
            cols = slice(hh * rg, (hh + 1) * rg)
            if bias == "diag":
                s = s + bias_ref[:, cols]
            elif bias == "corner":
                s = jnp.concatenate([s[0:tq - DIFF_CORNER], s[tq - DIFF_CORNER:tq] + cbias_ref[:, cols]], axis=0)
            m = carry[2 * hh]
            mn = jnp.maximum(m, jnp.max(s, axis=0, keepdims=True))
            alphas.append(jnp.exp2(m - mn))
            ps.append(jnp.exp2(s - mn).astype(BF16))
            mns.append(mn)
        out = []
        for hh in range(N_HEADS):
            out += [mns[hh], alphas[hh] * carry[2 * hh + 1] + _dot(vt_ref[hh, kb], ps[hh])]
        return tuple(out)

    carry = (jnp.full((1, rg), NEG, F32), jnp.zeros((VT_ROWS, rg), F32)) * N_HEADS
    carry = lax.fori_loop(0, jnp.maximum(qi - 1, 0), functools.partial(step, bias=None), carry)
    carry = lax.fori_loop(jnp.maximum(qi - 1, 0), qi, functools.partial(step, bias="corner"), carry)
    carry = step(qi, carry, "diag")
    lam = _diff_lambda(lam_ref, lam_init)
    outs = []
    for hh in range(N_HEADS):
        acc = carry[2 * hh + 1]
        on = acc[0:HEAD_W] * (1.0 / acc[HEAD_W:HEAD_W + 1])
        o = on[:, 0:tq] - lam * on[:, tq:rg]
        outs.append(o * lax.rsqrt(jnp.mean(o * o, axis=0, keepdims=True) + 1e-5))
    ot = jnp.concatenate(outs, axis=0) * (subg_ref[...] * (1.0 - lam_init))
    o_ref[...] = jnp.transpose(ot).astype(BF16)


def _diff_prompt(dqt, kd, vdt, bias, cbias, lamv, subg, *, batch, seq, tq, lam_init):
    nq = seq // tq
    t = batch * seq
    assert vdt.shape[3] == tq
    return pl.pallas_call(
        functools.partial(_diff_prompt_kernel, tq=tq, lam_init=lam_init),
        grid=(batch, nq),
        in_specs=[
            pl.BlockSpec((GROUP_W, tq), lambda b, i: (0, b * nq + i)),
            pl.BlockSpec((1, seq, 256), lambda b, i: (b, 0, 0)),
            pl.BlockSpec((N_HEADS, nq, VT_ROWS, tq), lambda b, i: (0, b, 0, 0)),
            _const_spec(bias.shape), _const_spec(cbias.shape), _const_spec(lamv.shape), _const_spec(subg.shape),
        ],
        out_specs=pl.BlockSpec((tq, 256), lambda b, i: (b * nq + i, 0)),
        out_shape=jax.ShapeDtypeStruct((t, 256), BF16),
        compiler_params=_cparams(("parallel", "arbitrary")), name="diff_prompt",
    )(dqt, kd.reshape(batch, seq, 256), vdt, bias, cbias, lamv, subg)


def _mla_sample_kernel(q_ref, kc_ref, kn_ref, vn_ref, wuv_ref, o_ref, *, lq):
    r = N_HEADS * lq
    q = q_ref[...].reshape(r, 256)
    kc = kc_ref[0]
    sc = _dot_nt(q, kc)
    sn = _dot_nt(q, kn_ref[...])
    m = jnp.maximum(jnp.max(sc, axis=1, keepdims=True), jnp.max(sn, axis=1, keepdims=True))
    pc = jnp.exp2(sc - m)
    pn = jnp.exp2(sn - m)
    l = jnp.sum(pc, axis=1, keepdims=True) + jnp.sum(pn, axis=1, keepdims=True)
    pv = _dot(pc.astype(BF16), kc[:, 0:128]) + _dot(pn.astype(BF16), vn_ref[:, 0:128])
    on = (pv / l).astype(BF16)
    o_ref[...] = jnp.concatenate(
        [_dot(on[hh * lq:(hh + 1) * lq], wuv_ref[hh]) for hh in range(N_HEADS)], axis=1).astype(BF16)


def _mla_sample(qm, kc, km, vm, wuv, *, batch, lq):
    past = kc.shape[1]
    return pl.pallas_call(
        functools.partial(_mla_sample_kernel, lq=lq),
        grid=(batch,),
        in_specs=[
            pl.BlockSpec((N_HEADS, lq, 256), lambda b: (0, b, 0)),
            pl.BlockSpec((1, past, 256), lambda b: (b, 0, 0)),
            pl.BlockSpec((lq, 256), lambda b: (b, 0)),
            pl.BlockSpec((lq, 256), lambda b: (b, 0)),
            _const_spec(wuv.shape),
        ],
        out_specs=pl.BlockSpec((lq, 256), lambda b: (b, 0)),
        out_shape=jax.ShapeDtypeStruct((batch * lq, 256), BF16),
        compiler_params=_cparams(("parallel",)), name="mla_sample",
    )(qm, kc, km, vm, wuv)


def _diff_sample_kernel(q_ref, kc_ref, vc_ref, kn_ref, vn_ref, bc_ref, bn_ref, lam_ref, subg_ref, o_ref,
                        *, lq, lam_init):
    q8 = _stack_q8(q_ref[...])
    bc = jnp.concatenate([bc_ref[0, hh // 2] for hh in range(2 * N_HEADS)], axis=0)
    bn = jnp.concatenate([bn_ref[0, hh // 2] for hh in range(2 * N_HEADS)], axis=0)
    sc = _dot_nt(q8, kc_ref[0]) + bc
    sn = _dot_nt(q8, kn_ref[...].astype(BF16)) + bn
    m = jnp.maximum(jnp.max(sc, axis=1, keepdims=True), jnp.max(sn, axis=1, keepdims=True))
    pc = jnp.exp2(sc - m)
    pn = jnp.exp2(sn - m)
    l = jnp.sum(pc, axis=1, keepdims=True) + jnp.sum(pn, axis=1, keepdims=True)
    pv = (_dot(pc.astype(BF16), vc_ref[0]) + _dot(pn.astype(BF16), vn_ref[...].astype(BF16))) / l
    on = jnp.concatenate(
        [pv[g * lq:(g + 1) * lq, HEAD_W * (g // 2):HEAD_W * (g // 2 + 1)] for g in range(2 * N_HEADS)], axis=0)
    lam = _diff_lambda(lam_ref, lam_init)
    o_ref[...] = _diff_finish(on, lam, subg_ref[...], lam_init, lq).astype(BF16)


def _diff_sample(dq, kc, vc, dk, dv, bias_c, bias_n, lamv, subg, *, batch, lq, lam_init):
    past = kc.shape[1]
    return pl.pallas_call(
        functools.partial(_diff_sample_kernel, lq=lq, lam_init=lam_init),
        grid=(batch,),
        in_specs=[
            pl.BlockSpec((lq, 256), lambda b: (b, 0)),
            pl.BlockSpec((1, past, 256), lambda b: (b, 0, 0)),
            pl.BlockSpec((1, past, 256), lambda b: (b, 0, 0)),
            pl.BlockSpec((lq, 256), lambda b: (b, 0)),
            pl.BlockSpec((lq, 256), lambda b: (b, 0)),
            _const_spec(bias_c.shape), _const_spec(bias_n.shape),
            _const_spec(lamv.shape), _const_spec(subg.shape),
        ],
        out_specs=pl.BlockSpec((lq, 256), lambda b: (b, 0)),
        out_shape=jax.ShapeDtypeStruct((batch * lq, 256), BF16),
        compiler_params=_cparams(("parallel",)), name="diff_sample",
    )(dq, kc, vc, dk, dv, bias_c, bias_n, lamv, subg)


def _stack4(y, group):
    grp = _lane_group((1,) * (y.ndim - 1) + (y.shape[-1],), group)
    return jnp.concatenate([jnp.where(grp == g, y, 0.0) for g in range(N_HEADS)], axis=-2)


def _diag_sum(f, group):
    rr = f.shape[0] // N_HEADS
    grp = _lane_group((1, f.shape[1]), group)
    out = jnp.where(grp == 0, f[0:rr], 0.0)
    for g in range(1, N_HEADS):
        out = out + jnp.where(grp == g, f[g * rr:(g + 1) * rr], 0.0)
    return out


def _bmm(x, y, group):
    return _dot(x, _stack4(y, group))


def _bdot(x, w):
    return lax.dot_general(x, w, (((2,), (1,)), ((0,), (0,))), preferred_element_type=F32)


def _bdot_nt(x, w):
    return lax.dot_general(x, w, (((2,), (2,)), ((0,), (0,))), preferred_element_type=F32)


def _bbmm(x, y, group):
    return _bdot(x, _stack4(y, group))


def _split3(x):
    hi = x.astype(BF16)
    r1 = x - hi.astype(F32)
    mid = r1.astype(BF16)
    lo = (r1 - mid.astype(F32)).astype(BF16)
    return hi, mid, lo


def _unit_lower_inverse(a, c):
    shape = (1, c, N_HEADS * c)
    i = lax.broadcasted_iota(jnp.int32, shape, 1)
    j = lax.broadcasted_iota(jnp.int32, shape, 2) % c
    eye = jnp.where(i == j, 1.0, 0.0)
    blockdiag = (i // SOLVE_BLOCK) == (j // SOLVE_BLOCK)
    ad = jnp.where(blockdiag, a, 0.0)
    p = _bbmm(ad, ad, c)
    rinv = eye - ad
    rinv = rinv + _bbmm(rinv, p, c)
    for _ in range(2):
        p = _bbmm(p, p, c)
        rinv = rinv + _bbmm(rinv, p, c)
    if c <= SOLVE_BLOCK:
        return rinv
    mm = _bbmm(rinv, a - ad, c)
    t = eye - mm
    p = _bbmm(mm, mm, c)
    t = t + _bbmm(t, p, c)
    nblk = c // SOLVE_BLOCK
    span = 4
    while span < nblk:
        p = _bbmm(p, p, c)
        t = t + _bbmm(t, p, c)
        span *= 2
    return _bbmm(t, rinv, c)


def _to_square(x, c):
    if c == HEAD_W:
        return x
    return jnp.concatenate([x[..., HEAD_W * hh:HEAD_W * hh + c] for hh in range(N_HEADS)], axis=-1)


def _gdn_kernel(qkv_ref, gate_ref, gab_ref, buf_ref, s0_ref, cw_ref, alog_ref, dtb_ref, ng_ref,
                o_ref, conv_ref, s_ref, prev_scr, s_scr, *, tg, c):
    j = pl.program_id(1)
    nj = pl.num_programs(1)

    @pl.when(j == 0)
    def _():
        prev_scr[...] = jnp.zeros_like(prev_scr)
        prev_scr[8 - (GDN_CONV - 1):8, :] = buf_ref[0]
        for hh in range(N_HEADS):
            s_scr[:, HEAD_W * hh:HEAD_W * (hh + 1)] = s0_ref[0, hh]

    x = qkv_ref[...]
    prev8 = prev_scr[...]
    cw = cw_ref[...]
    y = x * cw[GDN_CONV - 1:GDN_CONV]
    for s in range(1, GDN_CONV):
        y = y + _shift_rows(x, prev8, s) * cw[GDN_CONV - 1 - s:GDN_CONV - s]
    prev_scr[...] = x[tg - 8:tg]

    @pl.when(j == nj - 1)
    def _():
        conv_ref[0] = x[tg - (GDN_CONV - 1):tg]

    y = _silu(y)
    q = y[:, 0:256]
    k = y[:, 256:512]
    v = y[:, 512:768]
    gones = jnp.where(_lane_group((GROUP_W, GROUP_W), HEAD_W)
                      == lax.broadcasted_iota(jnp.int32, (GROUP_W, GROUP_W), 0) // HEAD_W,
                      1.0, 0.0).astype(BF16)

    def head_sum(z):
        z_hi, z_mid, z_lo = _split3(z)
        return _dot(z_hi, gones) + _dot(z_mid, gones) + _dot(z_lo, gones)

    q = q * lax.rsqrt(head_sum(q * q) + 1e-6) * (HEAD_W ** -0.5)
    k = k * lax.rsqrt(head_sum(k * k) + 1e-6)
    gab = gab_ref[...]
    za = gab[:, 0:256] + dtb_ref[...]
    softplus = jnp.maximum(za, 0.0) + jnp.log(1.0 + jnp.exp(-jnp.abs(za)))
    g = -jnp.exp(alog_ref[...]) * softplus
    beta = 1.0 / (1.0 + jnp.exp(-gab[:, 256:512]))

    nc = tg // c
    sq = (1, c, N_HEADS * c)
    ri = lax.broadcasted_iota(jnp.int32, sq, 1)
    cj = lax.broadcasted_iota(jnp.int32, sq, 2) % c
    tr = lax.broadcasted_iota(jnp.int32, (tg, tg), 0)
    tc = lax.broadcasted_iota(jnp.int32, (tg, tg), 1)
    tri_b = jnp.where((tc <= tr) & (tc // c == tr // c), 1.0, 0.0).astype(BF16)
    g_hi, g_mid, g_lo = _split3(g)
    decay2 = _dot(tri_b, g_hi) + _dot(tri_b, g_mid) + _dot(tri_b, g_lo)
    chunked = lambda z: z.reshape(nc, c, z.shape[-1])
    qb, kb_, vb, bb, decay = chunked(q), chunked(k), chunked(v), chunked(beta), chunked(decay2)
    dsq = _to_square(decay, c)
    drow = jnp.sum(jnp.where(ri == cj, dsq, 0.0), axis=1, keepdims=True)
    lm = jnp.where(cj <= ri, jnp.exp(jnp.where(cj <= ri, dsq - drow, 0.0)), 0.0)
    kbeta = kb_ * bb
    kq = _bdot_nt(jnp.concatenate([kbeta, qb], axis=1), _stack4(kb_, HEAD_W))
    a_mat = jnp.where(cj < ri, kq[:, 0:c] * lm, 0.0)
    qk_all = kq[:, c:2 * c] * lm
    edec = jnp.exp(decay)
    tinv = _unit_lower_inverse(a_mat, c)
    uw = _bdot(tinv, jnp.concatenate([_stack4(vb * bb, HEAD_W), _stack4(kbeta * edec, HEAD_W)], axis=2))
    dlast = decay[:, c - 1:c]
    qd_all = qb * edec
    kt_all = kb_ * jnp.exp(dlast - decay)
    gl_all = jnp.exp(dlast)
    pre = [(uw[n, :, 0:GROUP_W], uw[n, :, GROUP_W:2 * GROUP_W], qk_all[n], qd_all[n], kt_all[n], gl_all[n])
           for n in range(nc)]

    s_mat = s_scr[...]
    outs = []
    for (u, w, qk, qd, kt, glast) in pre:
        ws = _dot(jnp.concatenate([w, qd], axis=0), _stack4(s_mat, HEAD_W))
        vnew = u - ws[0:c]
        outs.append(ws[c:2 * c] + _bmm(qk, vnew, HEAD_W))
        s_mat = s_mat * glast + _diag_sum(_dot_tn(kt, vnew), HEAD_W)
    s_scr[...] = s_mat

    o = outs[0] if len(outs) == 1 else jnp.concatenate(outs, axis=0)
    o = o * lax.rsqrt(head_sum(o * o) * (1.0 / HEAD_W) + EPS) * ng_ref[...]
    o_ref[...] = (o * _silu(gate_ref[...])).astype(BF16)

    @pl.when(j == nj - 1)
    def _():
        s_fin = s_scr[...]
        for hh in range(N_HEADS):
            s_ref[0, hh] = s_fin[:, HEAD_W * hh:HEAD_W * (hh + 1)]


def _gdn(gqkv, ggate, gab, buf, s0, cw, alog, dtb, ng, *, batch, seq, tg, c):
    nj = seq // tg
    t = batch * seq
    row = lambda wd: pl.BlockSpec((tg, wd), lambda b, j: (b * nj + j, 0))
    return pl.pallas_call(
        functools.partial(_gdn_kernel, tg=tg, c=c),
        grid=(batch, nj),
        in_specs=[
            row(768), row(256), row(512),
            pl.BlockSpec((1, GDN_CONV - 1, 768), lambda b, j: (b, 0, 0)),
            pl.BlockSpec((1, N_HEADS, HEAD_W, HEAD_W), lambda b, j: (b, 0, 0, 0)),
            _const_spec(cw.shape), _const_spec(alog.shape), _const_spec(dtb.shape), _const_spec(ng.shape),
        ],
        out_specs=[
            row(256),
            pl.BlockSpec((1, GDN_CONV - 1, 768), lambda b, j: (b, 0, 0)),
            pl.BlockSpec((1, N_HEADS, HEAD_W, HEAD_W), lambda b, j: (b, 0, 0, 0)),
        ],
        out_shape=[
            jax.ShapeDtypeStruct((t, 256), BF16),
            jax.ShapeDtypeStruct((batch, GDN_CONV - 1, 768), F32),
            jax.ShapeDtypeStruct((batch, N_HEADS, HEAD_W, HEAD_W), F32),
        ],
        scratch_shapes=[pltpu.VMEM((8, 768), F32), pltpu.VMEM((HEAD_W, GROUP_W), F32)],
        compiler_params=_cparams(("parallel", "arbitrary")), name="gdn",
    )(gqkv, ggate, gab, buf, s0, cw, alog, dtb, ng)


def _ffn_kernel(x_ref, om_ref, od_ref, os_ref, og_ref, buf_ref, wout_ref, g2_ref, wup_ref, cw_ref, wdn_ref,
                fg_ref, y_ref, conv_ref, prev_scr, act_scr, *, tm, d_ff, ft, final):
    j = pl.program_id(1)
    nj = pl.num_programs(1)

    @pl.when(j == 0)
    def _():
        prev_scr[...] = jnp.zeros_like(prev_scr)
        prev_scr[8 - (FFN_CONV - 1):8, :] = buf_ref[0]

    mixed = jnp.concatenate([om_ref[...], od_ref[...], os_ref[...], og_ref[...]], axis=1)
    x1 = x_ref[...] + _dot(mixed, wout_ref[...])
    h2 = (x1 * lax.rsqrt(jnp.mean(x1 * x1, axis=-1, keepdims=True) + EPS) * g2_ref[...]).astype(BF16)

    def conv_cols(off):
        a = _dot(h2, wup_ref[:, off:off + ft])
        prev8 = prev_scr[:, off:off + ft]
        cw = cw_ref[:, off:off + ft]
        y = a * cw[FFN_CONV - 1:FFN_CONV]
        for s in range(1, FFN_CONV):
            y = y + _shift_rows(a, prev8, s) * cw[FFN_CONV - 1 - s:FFN_CONV - s]
        prev_scr[:, off:off + ft] = a[tm - 8:tm]
        return y

    for fi in range(d_ff // ft):
        gate = conv_cols(fi * ft)
        up = conv_cols(d_ff + fi * ft)
        act_scr[:, fi * ft:(fi + 1) * ft] = (_silu(gate) * up).astype(BF16)
    acc = x1 + _dot(act_scr[...], wdn_ref[...])

    if final:
        acc = acc * lax.rsqrt(jnp.mean(acc * acc, axis=-1, keepdims=True) + EPS) * fg_ref[...]
    y_ref[...] = acc

    @pl.when(j == nj - 1)
    def _():
        conv_ref[0] = prev_scr[8 - (FFN_CONV - 1):8, :]


def _ffn(x, om, od, osg, og, buf, wout, g2, wup, cw, wdn, fg, *, batch, seq, tm, ft, final):
    nj = seq // tm
    t, d = x.shape
    d_ff = wdn.shape[0]
    row = lambda wd: pl.BlockSpec((tm, wd), lambda b, j: (b * nj + j, 0))
    return pl.pallas_call(
        functools.partial(_ffn_kernel, tm=tm, d_ff=d_ff, ft=ft, final=final),
        grid=(batch, nj),
        in_specs=[
            row(d), row(256), row(256), row(256), row(256),
            pl.BlockSpec((1, FFN_CONV - 1, 2 * d_ff), lambda b, j: (b, 0, 0)),
            _const_spec(wout.shape), _const_spec(g2.shape), _const_spec(wup.shape),
            _const_spec(cw.shape), _const_spec(wdn.shape), _const_spec(fg.shape),
        ],
        out_specs=[row(d), pl.BlockSpec((1, FFN_CONV - 1, 2 * d_ff), lambda b, j: (b, 0, 0))],
        out_shape=[jax.ShapeDtypeStruct((t, d), F32),
                   jax.ShapeDtypeStruct((batch, FFN_CONV - 1, 2 * d_ff), F32)],
        scratch_shapes=[pltpu.VMEM((8, 2 * d_ff), F32), pltpu.VMEM((tm, d_ff), BF16)],
        compiler_params=_cparams(("parallel", "arbitrary")), name="ffn",
    )(x, om, od, osg, og, buf, wout, g2, wup, cw, wdn, fg)


def _rot_cols(w):
    d = w.shape[0]
    wg = w.reshape(d, -1, 2, MLA_DR // 2)
    return jnp.concatenate([-wg[:, :, 1:2], wg[:, :, 0:1]], axis=2).reshape(d, -1)


def _permute_w_in(w_in):
    d = w_in.shape[0]
    pts = np.cumsum([384, 128, 32, 256, 256, 256, 256, 256, 768, 256, 4, 4])[:-1]
    mq, mlat, mkr, dq, dk, dv, su, sv, gqkv, ggate, ga, gb = jnp.split(w_in, pts, axis=1)
    mq = mq.reshape(d, N_HEADS, MLA_DN + MLA_DR)
    qn = mq[:, :, :MLA_DN].reshape(d, N_HEADS * MLA_DN)
    qr = mq[:, :, MLA_DN:].reshape(d, N_HEADS * MLA_DR)
    kr4 = jnp.tile(mkr, (1, N_HEADS))
    rep = lambda a: jnp.repeat(a, HEAD_W, axis=1)
    w = jnp.concatenate([qn, qr, _rot_cols(qr), mlat, kr4, _rot_cols(kr4),
                         dq, dk, dv, su, sv, gqkv, ggate, rep(ga), rep(gb)], axis=1)
    assert w.shape[1] == C_TOTAL
    wt = jnp.transpose(jnp.concatenate([qn, qr, _rot_cols(qr), mlat, dq, dv], axis=1))
    assert wt.shape[0] == WT_ROWS
    return w.astype(BF16), wt.astype(BF16)


def _rope_tables(pos):
    inv = ROPE_THETA ** (-jnp.arange(0, MLA_DR, 2, dtype=F32) / MLA_DR)
    ang = pos.astype(F32)[:, None] * inv[None, :]
    cos = jnp.tile(jnp.cos(ang), (1, 2 * N_HEADS))
    sin = jnp.tile(jnp.sin(ang), (1, 2 * N_HEADS))
    return cos, sin


def _layer_params(l, p):
    row = lambda a: a.reshape(1, -1)
    rep = lambda a: jnp.repeat(a, HEAD_W).reshape(1, -1)
    sgu_w = p['sgu_w'][l]
    w_perm, w_t = _permute_w_in(p['w_in'][l])
    return dict(
        w_in=w_perm, w_t=w_t,
        g1=row(p['norm1_g'][l]),
        wukt=jnp.transpose(p['mla_w_uk'][l], (1, 2, 0)).astype(BF16),
        wuk=jnp.transpose(p['mla_w_uk'][l], (1, 0, 2)).astype(BF16),
        wuv=jnp.transpose(p['mla_w_uv'][l], (1, 0, 2)).astype(BF16),
        wuvt=jnp.transpose(p['mla_w_uv'][l], (1, 2, 0)).astype(BF16),
        latg=row(p['mla_lat_g'][l]), latgt=p['mla_lat_g'][l].reshape(-1, 1),
        sgu_w=sgu_w,
        sgu_b=p['sgu_b'][l],
        lng=row(p['sgu_ln_g'][l]), lnb=row(p['sgu_ln_b'][l]),
        lamv=jnp.stack([p['diff_lam_q1'][l], p['diff_lam_k1'][l], p['diff_lam_q2'][l], p['diff_lam_k2'][l]]),
        subg=row(p['diff_sub_g'][l]),
        gcw=p['gdn_conv_w'][l],
        alog=rep(p['gdn_a_log'][l]), dtb=rep(p['gdn_dt_bias'][l]),
        ng=jnp.tile(p['gdn_norm_g'][l], N_HEADS).reshape(1, -1),
        wout=p['w_out'][l].astype(BF16),
        g2=row(p['norm2_g'][l]),
        wup=p['ffn_w_up'][l].astype(BF16),
        fcw=p['ffn_conv_w'][l],
        wdn=p['ffn_w_down'][l].astype(BF16),
        fg=row(p['final_g']),
    )


def _sgu_tables(lp, c):
    w = lp['sgu_w'][:, :c, :c].reshape(N_HEADS * c, c)
    b = jnp.repeat(jnp.transpose(lp['sgu_b'][:, :c]), HEAD_W, axis=1)
    return w, b


def _run_inproj(x, lp, cos, sin, *, tm, sgu_c, prompt):
    sw, sb = _sgu_tables(lp, sgu_c)
    return _inproj(x, lp['g1'], lp['w_in'], lp['w_t'], cos, sin, jnp.transpose(cos), jnp.transpose(sin),
                   lp['wukt'], lp['wuk'], lp['latg'], lp['latgt'], sw, sb, lp['lng'], lp['lnb'],
                   tm=tm, sgu_c=sgu_c, prompt=prompt)


def kernel(x_prompt, x_sample, cache_mla_latent, cache_mla_krope, cache_diff_k, cache_diff_v, state_gdn_conv, state_gdn_s, state_ffn_conv, t5_table, final_g, norm1_g, w_in, mla_lat_g, mla_w_uk, mla_w_uv, diff_lam_q1, diff_lam_k1, diff_lam_q2, diff_lam_k2, diff_sub_g, sgu_ln_g, sgu_ln_b, sgu_w, sgu_b, gdn_conv_w, gdn_a_log, gdn_dt_bias, gdn_norm_g, w_out, norm2_g, ffn_w_up, ffn_conv_w, ffn_w_down):
    p = dict(final_g=final_g, norm1_g=norm1_g, w_in=w_in, mla_lat_g=mla_lat_g, mla_w_uk=mla_w_uk,
             mla_w_uv=mla_w_uv, diff_lam_q1=diff_lam_q1, diff_lam_k1=diff_lam_k1, diff_lam_q2=diff_lam_q2,
             diff_lam_k2=diff_lam_k2, diff_sub_g=diff_sub_g, sgu_ln_g=sgu_ln_g, sgu_ln_b=sgu_ln_b,
             sgu_w=sgu_w, sgu_b=sgu_b, gdn_conv_w=gdn_conv_w, gdn_a_log=gdn_a_log, gdn_dt_bias=gdn_dt_bias,
             gdn_norm_g=gdn_norm_g, w_out=w_out, norm2_g=norm2_g, ffn_w_up=ffn_w_up, ffn_conv_w=ffn_conv_w,
             ffn_w_down=ffn_w_down)
    depth = w_in.shape[0]
    bp, sp, d = x_prompt.shape
    bs, ls, _ = x_sample.shape
    past = cache_mla_latent.shape[2]
    d_ff = ffn_w_down.shape[1]
    assert past % CHUNK == 0 and ls <= CHUNK

    tm_p = min(512, sp)
    tq = min(512, sp)
    tg_p = min(512, sp)
    ft = 256
    sgu_cp = min(SGU_CHUNK, sp)
    gdn_cp = min(GDN_CHUNK, sp)

    cos_p, sin_p = _rope_tables(jnp.arange(sp, dtype=jnp.int32))
    pos_s = past + jnp.arange(ls, dtype=jnp.int32)
    cos_s, sin_s = _rope_tables(jnp.tile(pos_s, bs))

    assert tq >= DIFF_CORNER and tm_p == tq
    stack_t = lambda b: jnp.transpose(jnp.repeat(b, 2, axis=0), (2, 0, 1)).reshape(b.shape[2], -1)
    bias_p = stack_t(_t5_bias(t5_table, tq, tq, (0,), (0,))[0])
    corner = _t5_bias(t5_table, DIFF_CORNER, DIFF_CORNER, (DIFF_CORNER,), (0,))[0]
    bias_pc = stack_t(jnp.pad(corner, ((0, 0), (0, tq - DIFF_CORNER), (0, 0))))
    bias_sc = _t5_bias(t5_table, ls, past, (past,), (0,))
    bias_sn = _t5_bias(t5_table, ls, ls, (past,), (past,))

    xp = x_prompt.reshape(bp * sp, d)
    xs = x_sample.reshape(bs * ls, d)
    zeros_gconv = jnp.zeros((bp, GDN_CONV - 1, 3 * GROUP_W), F32)
    zeros_gs = jnp.zeros((bp, N_HEADS, HEAD_W, HEAD_W), F32)
    zeros_fconv = jnp.zeros((bp, FFN_CONV - 1, 2 * d_ff), F32)

    outs_p, outs_s = [], []
    for l in range(depth):
        lp = _layer_params(l, p)
        lam_init = 0.8 - 0.6 * math.exp(-0.3 * l)
        final = l == depth - 1

        ip = _run_inproj(xp, lp, cos_p, sin_p, tm=tm_p, sgu_c=sgu_cp, prompt=True)
        lat, kr, dk, dv, osgu = ip['lat'], ip['kr'], ip['dk'], ip['dv'], ip['osgu']
        gqkv, ggate, gab = ip['gqkv'], ip['ggate'], ip['gab']
        o_mla = _mla_prompt(ip['qmt'], ip['km'], ip['vmt'], lp['wuvt'], batch=bp, seq=sp, tq=tq)
        o_diff = _diff_prompt(ip['dqt'], ip['kd'], ip['vdt'], bias_p, bias_pc, lp['lamv'],
                              jnp.tile(lp['subg'], (1, N_HEADS)).reshape(-1, 1),
                              batch=bp, seq=sp, tq=tq, lam_init=lam_init)
        o_gdn, gconv, gs = _gdn(gqkv, ggate, gab, zeros_gconv, zeros_gs, lp['gcw'], lp['alog'], lp['dtb'],
                                lp['ng'], batch=bp, seq=sp, tg=tg_p, c=gdn_cp)
        xp, fconv = _ffn(xp, o_mla, o_diff, osgu, o_gdn, zeros_fconv, lp['wout'], lp['g2'], lp['wup'],
                         lp['fcw'], lp['wdn'], lp['fg'], batch=bp, seq=sp, tm=tm_p, ft=ft, final=final)
        outs_p.append(dict(
            lat=lat.reshape(bp, sp, MLA_R), kr=kr.reshape(bp, sp, MLA_DR),
            dk=dk.reshape(bp, sp, N_HEADS, 2, DIFF_DH), dv=dv.reshape(bp, sp, N_HEADS, 2 * DIFF_DH),
            gconv=gconv, gs=gs, fconv=fconv))

        ip = _run_inproj(xs, lp, cos_s, sin_s, tm=bs * ls, sgu_c=min(SGU_CHUNK, ls), prompt=False)
        qm, km, vm, lat, kr, dq, dk, dv = (ip[k] for k in ('qm', 'km', 'vm', 'lat', 'kr', 'dq', 'dk', 'dv'))
        osgu, vn, gqkv, ggate, gab = (ip[k] for k in ('osgu', 'vn', 'gqkv', 'ggate', 'gab'))
        kc_m = jnp.concatenate([cache_mla_latent[l], jnp.tile(cache_mla_krope[l], (1, 1, N_HEADS))],
                               axis=-1).astype(BF16)
        o_mla = _mla_sample(qm, kc_m, km, vm, lp['wuv'], batch=bs, lq=ls)
        kc_d = cache_diff_k[l].reshape(bs, past, GROUP_W).astype(BF16)
        vc_d = cache_diff_v[l].reshape(bs, past, GROUP_W).astype(BF16)
        o_diff = _diff_sample(dq, kc_d, vc_d, dk, dv, bias_sc, bias_sn, lp['lamv'], lp['subg'],
                              batch=bs, lq=ls, lam_init=lam_init)
        o_gdn, gconv, gs = _gdn(gqkv, ggate, gab, state_gdn_conv[l], state_gdn_s[l], lp['gcw'], lp['alog'],
                                lp['dtb'], lp['ng'], batch=bs, seq=ls, tg=ls, c=min(GDN_CHUNK, ls))
        xs, fconv = _ffn(xs, o_mla, o_diff, osgu, o_gdn, state_ffn_conv[l], lp['wout'], lp['g2'], lp['wup'],
                         lp['fcw'], lp['wdn'], lp['fg'], batch=bs, seq=ls, tm=ls, ft=ft, final=final)
        outs_s.append(dict(
            lat=lat.reshape(bs, ls, MLA_R), kr=kr.reshape(bs, ls, MLA_DR),
            dk=dk.reshape(bs, ls, N_HEADS, 2, DIFF_DH), dv=dv.reshape(bs, ls, N_HEADS, 2 * DIFF_DH),
            gconv=gconv, gs=gs, fconv=fconv, sv=vn.reshape(bs, ls, GROUP_W)))

    st = lambda lst, key: jnp.stack([o[key] for o in lst])
    return (xp.reshape(bp, sp, d), xs.reshape(bs, ls, d),
            st(outs_p, 'lat'), st(outs_p, 'kr'), st(outs_p, 'dk'), st(outs_p, 'dv'),
            st(outs_p, 'gconv'), st(outs_p, 'gs'), st(outs_p, 'fconv'),
            st(outs_s, 'lat'), st(outs_s, 'kr'), st(outs_s, 'dk'), st(outs_s, 'dv'),
            st(outs_s, 'gconv'), st(outs_s, 'gs'), st(outs_s, 'fconv'), st(outs_s, 'sv'))
```

```python
import functools
import math

import numpy as np
import jax
import jax.numpy as jnp
from jax import lax
from jax.experimental import pallas as pl
from jax.experimental.pallas import tpu as pltpu

F32 = jnp.float32
BF16 = jnp.bfloat16

CHUNK = 64
EPS = 1e-6
N_HEADS = 4
HEAD_W = 64
GROUP_W = 256
MLA_DN, MLA_DR, MLA_R = 64, 32, 128
DIFF_DH = 32
ROPE_THETA = 10000.0
SGU_CHUNK = 128
GDN_CHUNK = 64
GDN_CONV = 4
FFN_CONV = 3
T5_BUCKETS = 32
T5_MAX_DIST = 128
NEG = -1e30
LOG2E = math.log2(math.e)
SOLVE_BLOCK = 16
DIFF_CORNER = 128
MLA_VT_ROWS = 144
VT_ROWS = 80

VMEM_LIMIT_BYTES = 56 * 1024 * 1024

C_QN, C_QR, C_QRR, C_LAT, C_KR, C_KRR = 0, 256, 384, 512, 640, 768
C_DQ, C_DK, C_DV, C_SU, C_SV = 896, 1152, 1408, 1664, 1920
C_GQKV, C_GG, C_GA, C_GB = 2176, 2944, 3200, 3456
C_TOTAL = 3712
WT_DQ, WT_ROWS = 640, 1152


def _cparams(sem):
    return pltpu.CompilerParams(dimension_semantics=sem, vmem_limit_bytes=VMEM_LIMIT_BYTES)


def _const_spec(shape):
    nd = len(shape)
    return pl.BlockSpec(shape, lambda *_: (0,) * nd, pipeline_mode=pl.Buffered(1))


def _dot(a, b):
    return jnp.dot(a, b, preferred_element_type=F32)


def _dot_nt(a, b):
    return lax.dot_general(a, b, (((1,), (1,)), ((), ())), preferred_element_type=F32)


def _dot_tn(a, b):
    return lax.dot_general(a, b, (((0,), (0,)), ((), ())), preferred_element_type=F32)


def _lane_group(shape, group):
    return lax.broadcasted_iota(jnp.int32, shape, len(shape) - 1) // group


def _silu(x):
    return x * (1.0 / (1.0 + jnp.exp(-x)))


def _shift_rows(x, prev8, s):
    if s == 0:
        return x
    r = pltpu.roll(x, s, axis=0)
    rp = pltpu.roll(prev8, s, axis=0)
    row8 = lax.broadcasted_iota(jnp.int32, rp.shape, 0)
    top = jnp.where(row8 < s, rp, r[0:8])
    if x.shape[0] == 8:
        return top
    return jnp.concatenate([top, r[8:]], axis=0)


_INPROJ_COMMON = ("km", "lat", "kr", "dk", "dv", "kd", "osgu", "vn", "gqkv", "ggate", "gab")
_INPROJ_PROMPT = ("qmt", "vmt", "dqt", "vdt")
_INPROJ_SAMPLE = ("qm", "vm", "dq")


def _inproj_kernel(x_ref, g1_ref, w_ref, wt_ref, cos_ref, sin_ref, cost_ref, sint_ref, wukt_ref, wuk_ref,
                   latg_ref, latgt_ref, sguw_ref, sgub_ref, lng_ref, lnb_ref, *out_refs, sgu_c, prompt):
    o = dict(zip(_INPROJ_COMMON + (_INPROJ_PROMPT if prompt else _INPROJ_SAMPLE), out_refs))
    tm = x_ref.shape[0]
    x = x_ref[...]
    h = (x * lax.rsqrt(jnp.mean(x * x, axis=-1, keepdims=True) + EPS) * g1_ref[...]).astype(BF16)

    def sec(off, n):
        return _dot(h, w_ref[:, off:off + n])

    cos = cos_ref[...]
    sin = sin_ref[...]
    mla_scale = (MLA_DN + MLA_DR) ** -0.5 * LOG2E
    diff_scale = DIFF_DH ** -0.5 * LOG2E

    zl = sec(C_LAT, 384)
    zlat = zl[:, 0:128]
    lat = zlat * lax.rsqrt(jnp.mean(zlat * zlat, axis=-1, keepdims=True) + EPS) * latg_ref[...]
    kr4 = zl[:, 128:256] * cos + zl[:, 256:384] * sin
    o["lat"][...] = lat
    o["kr"][...] = kr4[:, 0:MLA_DR]
    lat_b = lat.astype(BF16)
    o["km"][:, 0:128] = lat_b
    o["km"][:, 128:256] = kr4.astype(BF16)

    zd = sec(C_DK, 512)
    dk = zd[:, 0:256]
    dv = zd[:, 256:512]
    o["dk"][...] = dk
    o["dv"][...] = dv
    o["kd"][...] = dk.astype(BF16)

    if prompt:
        zt = _dot_nt(wt_ref[0:WT_DQ, :], h)
        qrt = (zt[256:384] * cost_ref[...] + zt[384:512] * sint_ref[...]) * mla_scale
        head_of_row = lax.broadcasted_iota(jnp.int32, (128, 1), 0) // MLA_DR
        for hh in range(N_HEADS):
            qn = (zt[MLA_DN * hh:MLA_DN * (hh + 1)] * mla_scale).astype(BF16)
            o["qmt"][hh, 0:128, :] = _dot(wuk_ref[hh], qn).astype(BF16)
            o["qmt"][hh, 128:256, :] = jnp.where(head_of_row == hh, qrt, 0.0).astype(BF16)
        zlt = zt[512:640]
        latt = zlt * lax.rsqrt(jnp.mean(zlt * zlt, axis=0, keepdims=True) + EPS) * latgt_ref[...]
        o["vmt"][0] = jnp.concatenate([latt, jnp.ones((MLA_VT_ROWS - MLA_R, tm), F32)], axis=0).astype(BF16)
        zt2 = _dot_nt(wt_ref[WT_DQ:WT_ROWS, :], h)
        o["dqt"][...] = (zt2[0:GROUP_W] * diff_scale).astype(BF16)
        ones_rows = jnp.ones((VT_ROWS - HEAD_W, tm), F32)
        for hh in range(N_HEADS):
            vt = zt2[GROUP_W + HEAD_W * hh:GROUP_W + HEAD_W * (hh + 1)]
            o["vdt"][hh, 0] = jnp.concatenate([vt, ones_rows], axis=0).astype(BF16)
    else:
        zq = sec(C_QN, 512)
        qr = (zq[:, 256:384] * cos + zq[:, 384:512] * sin) * mla_scale
        head_of_lane = _lane_group((1, 128), MLA_DR)
        for hh in range(N_HEADS):
            qn = (zq[:, MLA_DN * hh:MLA_DN * (hh + 1)] * mla_scale).astype(BF16)
            o["qm"][hh, :, 0:128] = _dot(qn, wukt_ref[hh]).astype(BF16)
            o["qm"][hh, :, 128:256] = jnp.where(head_of_lane == hh, qr, 0.0).astype(BF16)
        o["vm"][:, 0:128] = lat_b
        o["vm"][:, 128:256] = jnp.ones((tm, 128), BF16)
        o["dq"][...] = (sec(C_DQ, 256) * diff_scale).astype(BF16)

    zs = sec(C_SU, 512)
    su = zs[:, 0:256]
    sv = zs[:, 256:512]
    mu = jnp.mean(sv, axis=-1, keepdims=True)
    svc = sv - mu
    var = jnp.mean(svc * svc, axis=-1, keepdims=True)
    vn = svc * lax.rsqrt(var + EPS) * lng_ref[...] + lnb_ref[...]
    o["vn"][...] = vn
    c = sgu_c
    wr = lax.broadcasted_iota(jnp.int32, (N_HEADS * c, c), 0) % c
    wc = lax.broadcasted_iota(jnp.int32, (N_HEADS * c, c), 1)
    w4 = jnp.where(wc <= wr, sguw_ref[...], 0.0).astype(BF16)
    hl = _lane_group((1, GROUP_W), HEAD_W)
    sgub = sgub_ref[...]
    for ci in range(tm // c):
        rows = slice(ci * c, (ci + 1) * c)
        m4 = _dot(w4, vn[rows].astype(BF16))
        mix = jnp.where(hl == 0, m4[0:c], 0.0)
        for hh in range(1, N_HEADS):
            mix = mix + jnp.where(hl == hh, m4[hh * c:(hh + 1) * c], 0.0)
        o["osgu"][rows, :] = (su[rows] * (mix + sgub)).astype(BF16)

    zg = sec(C_GQKV, 1536)
    o["gqkv"][...] = zg[:, 0:768]
    o["ggate"][...] = zg[:, 768:1024]
    o["gab"][...] = zg[:, 1024:1536]


def _inproj(x, g1, w, wt, cos, sin, cost, sint, wukt, wuk, latg, latgt, sguw, sgub, lng, lnb, *, tm, sgu_c, prompt):
    t, d = x.shape
    n = t // tm
    npos = cos.shape[0] // tm
    row = lambda wd: pl.BlockSpec((tm, wd), lambda i: (i, 0))
    shapes = dict(
        km=((t, 256), BF16, row(256)), lat=((t, MLA_R), F32, row(MLA_R)), kr=((t, MLA_DR), F32, row(MLA_DR)),
        dk=((t, 256), F32, row(256)), dv=((t, 256), F32, row(256)), kd=((t, 256), BF16, row(256)),
        osgu=((t, 256), BF16, row(256)), vn=((t, 256), F32, row(256)), gqkv=((t, 768), F32, row(768)),
        ggate=((t, 256), F32, row(256)), gab=((t, 512), F32, row(512)),
        qmt=((N_HEADS, 256, t), BF16, pl.BlockSpec((N_HEADS, 256, tm), lambda i: (0, 0, i))),
        vmt=((n, MLA_VT_ROWS, tm), BF16, pl.BlockSpec((1, MLA_VT_ROWS, tm), lambda i: (i, 0, 0))),
        dqt=((GROUP_W, t), BF16, pl.BlockSpec((GROUP_W, tm), lambda i: (0, i))),
        vdt=((N_HEADS, n, VT_ROWS, tm), BF16, pl.BlockSpec((N_HEADS, 1, VT_ROWS, tm), lambda i: (0, i, 0, 0))),
        qm=((N_HEADS, t, 256), BF16, pl.BlockSpec((N_HEADS, tm, 256), lambda i: (0, i, 0))),
        vm=((t, 256), BF16, row(256)), dq=((t, 256), BF16, row(256)),
    )
    names = _INPROJ_COMMON + (_INPROJ_PROMPT if prompt else _INPROJ_SAMPLE)
    pos_row = pl.BlockSpec((tm, 128), lambda i: (i % npos, 0))
    pos_col = pl.BlockSpec((128, tm), lambda i: (0, i % npos))
    consts = (g1, w, wt)
    in_specs = ([row(d)] + [_const_spec(a.shape) for a in consts] + [pos_row, pos_row, pos_col, pos_col]
                + [_const_spec(a.shape) for a in (wukt, wuk, latg, latgt, sguw, sgub, lng, lnb)])
    outs = pl.pallas_call(
        functools.partial(_inproj_kernel, sgu_c=sgu_c, prompt=prompt),
        grid=(n,), in_specs=in_specs,
        out_specs=[shapes[k][2] for k in names],
        out_shape=[jax.ShapeDtypeStruct(shapes[k][0], shapes[k][1]) for k in names],
        compiler_params=_cparams(("parallel",)), name="inproj",
    )(x, g1, w, wt, cos, sin, cost, sint, wukt, wuk, latg, latgt, sguw, sgub, lng, lnb)
    return dict(zip(names, outs))


def _t5_thresholds():
    nb = T5_BUCKETS // 2
    max_exact = nb // 2
    ratio = T5_MAX_DIST // max_exact
    thr = []
    for j in range(1, nb - max_exact):
        n = max_exact
        while n ** (nb - max_exact) < (ratio ** j) * (max_exact ** (nb - max_exact)):
            n += 1
        thr.append(n)
    return nb, max_exact, thr


def _bias_kernel(t_ref, o_ref, *, q0s, k0s):
    nb, max_exact, thr = _t5_thresholds()
    tq, tk = o_ref.shape[2], o_ref.shape[3]
    row = lax.broadcasted_iota(jnp.int32, (tq, tk), 0)
    col = lax.broadcasted_iota(jnp.int32, (tq, tk), 1)
    for di, (q0, k0) in enumerate(zip(q0s, k0s)):
        qpos = row + q0
        kpos = col + k0
        rel = kpos - qpos
        n = jnp.abs(rel)
        visible = (kpos // CHUNK) <= (qpos // CHUNK)
        for hh in range(N_HEADS):
            def side(base):
                val = jnp.full((tq, tk), t_ref[base + nb - 1, hh], F32)
                for j in range(len(thr) - 1, -1, -1):
                    val = jnp.where(n < thr[j], t_ref[base + max_exact + j, hh], val)
                for e in range(max_exact - 1, -1, -1):
                    val = jnp.where(n == e, t_ref[base + e, hh], val)
                return val
            b = jnp.where(rel > 0, side(nb), side(0)) - t_ref[nb - 1, hh]
            o_ref[di, hh] = jnp.where(visible, b * LOG2E, NEG)


def _t5_bias(t5_table, tq, tk, q0s, k0s):
    nd = len(q0s)
    return pl.pallas_call(
        functools.partial(_bias_kernel, q0s=tuple(q0s), k0s=tuple(k0s)),
        in_specs=[pl.BlockSpec(memory_space=pltpu.SMEM)],
        out_shape=jax.ShapeDtypeStruct((nd, N_HEADS, tq, tk), F32),
        name="t5_bias",
    )(t5_table)


def _mla_prompt_kernel(qt_ref, k_ref, vt_ref, wuvt_ref, o_ref, *, tq):
    qi = pl.program_id(1)
    qlim = ((qi * tq + lax.broadcasted_iota(jnp.int32, (1, tq), 1)) // CHUNK + 1) * CHUNK
    krow = lax.broadcasted_iota(jnp.int32, (tq, 1), 0)

    def step(kb, carry, masked):
        k0 = pl.multiple_of(kb * tq, tq)
        kblk = k_ref[0, pl.ds(k0, tq), :]
        scores = [_dot(kblk, qt_ref[hh]) for hh in range(N_HEADS)]
        ps, mns, alphas = [], [], []
        for hh in range(N_HEADS):
            m = carry[2 * hh]
            s = scores[hh]
            if masked:
                s = jnp.where(krow + k0 < qlim, s, NEG)
            mn = jnp.maximum(m, jnp.max(s, axis=0, keepdims=True))
            alphas.append(jnp.exp2(m - mn))
            ps.append(jnp.exp2(s - mn).astype(BF16))
            mns.append(mn)
        out = []
        for hh in range(N_HEADS):
            out += [mns[hh], alphas[hh] * carry[2 * hh + 1] + _dot(vt_ref[kb], ps[hh])]
        return tuple(out)

    carry = (jnp.full((1, tq), NEG, F32), jnp.zeros((MLA_VT_ROWS, tq), F32)) * N_HEADS
    carry = lax.fori_loop(0, qi, functools.partial(step, masked=False), carry)
    carry = step(qi, carry, True)
    outs = []
    for hh in range(N_HEADS):
        acc = carry[2 * hh + 1]
        on = (acc[0:MLA_R] * (1.0 / acc[MLA_R:MLA_R + 1])).astype(BF16)
        outs.append(_dot(wuvt_ref[hh], on))
    o_ref[...] = jnp.transpose(jnp.concatenate(outs, axis=0)).astype(BF16)


def _mla_prompt(qmt, km, vmt, wuvt, *, batch, seq, tq):
    assert seq % tq == 0 and vmt.shape[2] == tq and tq % CHUNK == 0
    nq = seq // tq
    t = batch * seq
    return pl.pallas_call(
        functools.partial(_mla_prompt_kernel, tq=tq),
        grid=(batch, nq),
        in_specs=[
            pl.BlockSpec((N_HEADS, 256, tq), lambda b, i: (0, 0, b * nq + i)),
            pl.BlockSpec((1, seq, 256), lambda b, i: (b, 0, 0)),
            pl.BlockSpec((nq, MLA_VT_ROWS, tq), lambda b, i: (b, 0, 0)),
            _const_spec(wuvt.shape),
        ],
        out_specs=pl.BlockSpec((tq, 256), lambda b, i: (b * nq + i, 0)),
        out_shape=jax.ShapeDtypeStruct((t, 256), BF16),
        compiler_params=_cparams(("parallel", "arbitrary")), name="mla_prompt",
    )(qmt, km.reshape(batch, seq, 256), vmt, wuvt)


def _diff_lambda(lam_ref, lam_init):
    l = lam_ref[...]
    a = jnp.sum(l[0:1] * l[1:2], axis=-1, keepdims=True)
    b = jnp.sum(l[2:3] * l[3:4], axis=-1, keepdims=True)
    return jnp.exp(a) - jnp.exp(b) + lam_init


def _stack_q8(q):
    grp = _lane_group((1, GROUP_W), DIFF_DH)
    return jnp.concatenate([jnp.where(grp == g, q, jnp.zeros_like(q)) for g in range(2 * N_HEADS)], axis=0)


def _diff_finish(on, lam, subg, lam_init, tq):
    outs = []
    for hh in range(N_HEADS):
        o = on[(2 * hh) * tq:(2 * hh + 1) * tq] - lam * on[(2 * hh + 1) * tq:(2 * hh + 2) * tq]
        o = o * lax.rsqrt(jnp.mean(o * o, axis=-1, keepdims=True) + 1e-5) * subg
        outs.append(o * (1.0 - lam_init))
    return jnp.concatenate(outs, axis=1)


def _diff_prompt_kernel(qt_ref, k_ref, vt_ref, bias_ref, cbias_ref, lam_ref, subg_ref, o_ref, *, tq, lam_init):
    qi = pl.program_id(1)
    rg = 2 * tq
    qt = qt_ref[...]
    fgrp = lax.broadcasted_iota(jnp.int32, (GROUP_W, 1), 0) // DIFF_DH
    qts = [jnp.concatenate([jnp.where(fgrp == 2 * hh + c, qt, jnp.zeros_like(qt)) for c in range(2)], axis=1)
           for hh in range(N_HEADS)]

    def step(kb, carry, bias):
        k0 = pl.multiple_of(kb * tq, tq)
        kblk = k_ref[0, pl.ds(k0, tq), :]
        scores = [_dot(kblk, qts[hh]) for hh in range(N_HEADS)]
        ps, mns, alphas = [], [], []
        for hh in range(N_HEADS):
            s = scores[hh]
            cols = slice(hh * rg, (hh + 1) * rg)
            if bias == "diag":
                s = s + bias_ref[:, cols]
            elif bias == "corner":
                s = jnp.concatenate([s[0:tq - DIFF_CORNER], s[tq - DIFF_CORNER:tq] + cbias_ref[:, cols]], axis=0)
            m = carry[2 * hh]
            mn = jnp.maximum(m, jnp.max(s, axis=0, keepdims=True))
            alphas.append(jnp.exp2(m - mn))
            ps.append(jnp.exp2(s - mn).astype(BF16))
            mns.append(mn)
        out = []
        for hh in range(N_HEADS):
            out += [mns[hh], alphas[hh] * carry[2 * hh + 1] + _dot(vt_ref[hh, kb], ps[hh])]
        return tuple(out)

    carry = (jnp.full((1, rg), NEG, F32), jnp.zeros((VT_ROWS, rg), F32)) * N_HEADS
    carry = lax.fori_loop(0, jnp.maximum(qi - 1, 0), functools.partial(step, bias=None), carry)
    carry = lax.fori_loop(jnp.maximum(qi - 1, 0), qi, functools.partial(step, bias="corner"), carry)
    carry = step(qi, carry, "diag")
    lam = _diff_lambda(lam_ref, lam_init)
    outs = []
    for hh in range(N_HEADS):
        acc = carry[2 * hh + 1]
        on = acc[0:HEAD_W] * (1.0 / acc[HEAD_W:HEAD_W + 1])
        o = on[:, 0:tq] - lam * on[:, tq:rg]
        outs.append(o * lax.rsqrt(jnp.mean(o * o, axis=0, keepdims=True) + 1e-5))
    ot = jnp.concatenate(outs, axis=0) * (subg_ref[...] * (1.0 - lam_init))
    o_ref[...] = jnp.transpose(ot).astype(BF16)


def _diff_prompt(dqt, kd, vdt, bias, cbias, lamv, subg, *, batch, seq, tq, lam_init):
    nq = seq // tq
    t = batch * seq
    assert vdt.shape[3] == tq
    return pl.pallas_call(
        functools.partial(_diff_prompt_kernel, tq=tq, lam_init=lam_init),
        grid=(batch, nq),
        in_specs=[
            pl.BlockSpec((GROUP_W, tq), lambda b, i: (0, b * nq + i)),
            pl.BlockSpec((1, seq, 256), lambda b, i: (b, 0, 0)),
            pl.BlockSpec((N_HEADS, nq, VT_ROWS, tq), lambda b, i: (0, b, 0, 0)),
            _const_spec(bias.shape), _const_spec(cbias.shape), _const_spec(lamv.shape), _const_spec(subg.shape),
        ],
        out_specs=pl.BlockSpec((tq, 256), lambda b, i: (b * nq + i, 0)),
        out_shape=jax.ShapeDtypeStruct((t, 256), BF16),
        compiler_params=_cparams(("parallel", "arbitrary")), name="diff_prompt",
    )(dqt, kd.reshape(batch, seq, 256), vdt, bias, cbias, lamv, subg)


def _mla_sample_kernel(q_ref, kc_ref, kn_ref, vn_ref, wuv_ref, o_ref, *, lq):
    r = N_HEADS * lq
    q = q_ref[...].reshape(r, 256)
    kc = kc_ref[0]
    sc = _dot_nt(q, kc)
    sn = _dot_nt(q, kn_ref[...])
    m = jnp.maximum(jnp.max(sc, axis=1, keepdims=True), jnp.max(sn, axis=1, keepdims=True))
    pc = jnp.exp2(sc - m)
    pn = jnp.exp2(sn - m)
    l = jnp.sum(pc, axis=1, keepdims=True) + jnp.sum(pn, axis=1, keepdims=True)
    pv = _dot(pc.astype(BF16), kc[:, 0:128]) + _dot(pn.astype(BF16), vn_ref[:, 0:128])
    on = (pv / l).astype(BF16)
    o_ref[...] = jnp.concatenate(
        [_dot(on[hh * lq:(hh + 1) * lq], wuv_ref[hh]) for hh in range(N_HEADS)], axis=1).astype(BF16)


def _mla_sample(qm, kc, km, vm, wuv, *, batch, lq):
    past = kc.shape[1]
    return pl.pallas_call(
        functools.partial(_mla_sample_kernel, lq=lq),
        grid=(batch,),
        in_specs=[
            pl.BlockSpec((N_HEADS, lq, 256), lambda b: (0, b, 0)),
            pl.BlockSpec((1, past, 256), lambda b: (b, 0, 0)),
            pl.BlockSpec((lq, 256), lambda b: (b, 0)),
            pl.BlockSpec((lq, 256), lambda b: (b, 0)),
            _const_spec(wuv.shape),
        ],
        out_specs=pl.BlockSpec((lq, 256), lambda b: (b, 0)),
        out_shape=jax.ShapeDtypeStruct((batch * lq, 256), BF16),
        compiler_params=_cparams(("parallel",)), name="mla_sample",
    )(qm, kc, km, vm, wuv)


def _diff_sample_kernel(q_ref, kc_ref, vc_ref, kn_ref, vn_ref, bc_ref, bn_ref, lam_ref, subg_ref, o_ref,
                        *, lq, lam_init):
    q8 = _stack_q8(q_ref[...])
    bc = jnp.concatenate([bc_ref[0, hh // 2] for hh in range(2 * N_HEADS)], axis=0)
    bn = jnp.concatenate([bn_ref[0, hh // 2] for hh in range(2 * N_HEADS)], axis=0)
    sc = _dot_nt(q8, kc_ref[0]) + bc
    sn = _dot_nt(q8, kn_ref[...].astype(BF16)) + bn
    m = jnp.maximum(jnp.max(sc, axis=1, keepdims=True), jnp.max(sn, axis=1, keepdims=True))
    pc = jnp.exp2(sc - m)
    pn = jnp.exp2(sn - m)
    l = jnp.sum(pc, axis=1, keepdims=True) + jnp.sum(pn, axis=1, keepdims=True)
    pv = (_dot(pc.astype(BF16), vc_ref[0]) + _dot(pn.astype(BF16), vn_ref[...].astype(BF16))) / l
    on = jnp.concatenate(
        [pv[g * lq:(g + 1) * lq, HEAD_W * (g // 2):HEAD_W * (g // 2 + 1)] for g in range(2 * N_HEADS)], axis=0)
    lam = _diff_lambda(lam_ref, lam_init)
    o_ref[...] = _diff_finish(on, lam, subg_ref[...], lam_init, lq).astype(BF16)


def _diff_sample(dq, kc, vc, dk, dv, bias_c, bias_n, lamv, subg, *, batch, lq, lam_init):
    past = kc.shape[1]
    return pl.pallas_call(
        functools.partial(_diff_sample_kernel, lq=lq, lam_init=lam_init),
        grid=(batch,),
        in_specs=[
            pl.BlockSpec((lq, 256), lambda b: (b, 0)),
            pl.BlockSpec((1, past, 256), lambda b: (b, 0, 0)),
            pl.BlockSpec((1, past, 256), lambda b: (b, 0, 0)),
            pl.BlockSpec((lq, 256), lambda b: (b, 0)),
            pl.BlockSpec((lq, 256), lambda b: (b, 0)),
            _const_spec(bias_c.shape), _const_spec(bias_n.shape),
            _const_spec(lamv.shape), _const_spec(subg.shape),
        ],
        out_specs=pl.BlockSpec((lq, 256), lambda b: (b, 0)),
        out_shape=jax.ShapeDtypeStruct((batch * lq, 256), BF16),
        compiler_params=_cparams(("parallel",)), name="diff_sample",
    )(dq, kc, vc, dk, dv, bias_c, bias_n, lamv, subg)


def _stack4(y, group):
    grp = _lane_group((1,) * (y.ndim - 1) + (y.shape[-1],), group)
    return jnp.concatenate([jnp.where(grp == g, y, 0.0) for g in range(N_HEADS)], axis=-2)


def _diag_sum(f, group):
    rr = f.shape[0] // N_HEADS
    grp = _lane_group((1, f.shape[1]), group)
    out = jnp.where(grp == 0, f[0:rr], 0.0)
    for g in range(1, N_HEADS):
        out = out + jnp.where(grp == g, f[g * rr:(g + 1) * rr], 0.0)
    return out


def _bmm(x, y, group):
    return _dot(x, _stack4(y, group))


def _bdot(x, w):
    return lax.dot_general(x, w, (((2,), (1,)), ((0,), (0,))), preferred_element_type=F32)


def _bdot_nt(x, w):
    return lax.dot_general(x, w, (((2,), (2,)), ((0,), (0,))), preferred_element_type=F32)


def _bbmm(x, y, group):
    return _bdot(x, _stack4(y, group))


def _split3(x):
    hi = x.astype(BF16)
    r1 = x - hi.astype(F32)
    mid = r1.astype(BF16)
    lo = (r1 - mid.astype(F32)).astype(BF16)
    return hi, mid, lo


def _unit_lower_inverse(a, c, tick):
    def bb(x, y, group):
        out = _bbmm(x, y, group)
        tick()
        return out

    shape = (1, c, N_HEADS * c)
    i = lax.broadcasted_iota(jnp.int32, shape, 1)
    j = lax.broadcasted_iota(jnp.int32, shape, 2) % c
    eye = jnp.where(i == j, 1.0, 0.0)
    blockdiag = (i // SOLVE_BLOCK) == (j // SOLVE_BLOCK)
    ad = jnp.where(blockdiag, a, 0.0)
    p = bb(ad, ad, c)
    rinv = eye - ad
    rinv = rinv + bb(rinv, p, c)
    for _ in range(2):
        p = bb(p, p, c)
        rinv = rinv + bb(rinv, p, c)
    if c <= SOLVE_BLOCK:
        return rinv
    mm = bb(rinv, a - ad, c)
    t = eye - mm
    p = bb(mm, mm, c)
    t = t + bb(t, p, c)
    nblk = c // SOLVE_BLOCK
    span = 4
    while span < nblk:
        p = bb(p, p, c)
        t = t + bb(t, p, c)
        span *= 2
    return bb(t, rinv, c)


def _to_square(x, c):
    if c == HEAD_W:
        return x
    return jnp.concatenate([x[..., HEAD_W * hh:HEAD_W * hh + c] for hh in range(N_HEADS)], axis=-1)


def _gdn_kernel(qkv_ref, gate_ref, gab_ref, buf_ref, s0_ref, cw_ref, alog_ref, dtb_ref, ng_ref,
                o_ref, conv_ref, s_ref, prev_scr, s_scr, u_scr, w_scr, qk_scr, qd_scr, kt_scr, gl_scr, *, tg, c, nj):
    t = pl.program_id(0)
    nc = tg // c
    slot = t % 2
    pslot = 1 - slot
    pre_scrs = (u_scr, w_scr, qk_scr, qd_scr, kt_scr, gl_scr)

    @pl.when(t == 0)
    def _():
        s_scr[...] = jnp.zeros_like(s_scr)
        for scr in pre_scrs:
            scr[1] = jnp.zeros(scr.shape[1:], F32)

    @pl.when(t % nj == 0)
    def _():
        prev_scr[...] = jnp.zeros_like(prev_scr)
        prev_scr[8 - (GDN_CONV - 1):8, :] = buf_ref[0]

    @pl.when((t > 0) & ((t - 1) % nj == 0))
    def _():
        for hh in range(N_HEADS):
            s_scr[:, HEAD_W * hh:HEAD_W * (hh + 1)] = s0_ref[0, hh]

    pre = [tuple(scr[pslot, n] for scr in pre_scrs) for n in range(nc)]
    s_old = s_scr[...]
    st = dict(s=s_old, n=0, ws=None, outs=[])

    def tick():
        if st["n"] >= nc:
            return
        u, w, qk, qd, kt, glast = pre[st["n"]]
        if st["ws"] is None:
            st["ws"] = _dot(jnp.concatenate([w, qd], axis=0), _stack4(st["s"], HEAD_W))
        else:
            ws = st["ws"]
            vnew = u - ws[0:c]
            st["outs"].append(ws[c:2 * c] + _bmm(qk, vnew, HEAD_W))
            st["s"] = st["s"] * glast + _diag_sum(_dot_tn(kt, vnew), HEAD_W)
            st["ws"] = None
            st["n"] += 1

    x = qkv_ref[...]
    prev8 = prev_scr[...]
    cw = cw_ref[...]
    y = x * cw[GDN_CONV - 1:GDN_CONV]
    for s in range(1, GDN_CONV):
        y = y + _shift_rows(x, prev8, s) * cw[GDN_CONV - 1 - s:GDN_CONV - s]
    prev_scr[...] = x[tg - 8:tg]
    conv_ref[0] = x[tg - (GDN_CONV - 1):tg]

    y = _silu(y)
    q = y[:, 0:256]
    k = y[:, 256:512]
    v = y[:, 512:768]
    gones = jnp.where(_lane_group((GROUP_W, GROUP_W), HEAD_W)
                      == lax.broadcasted_iota(jnp.int32, (GROUP_W, GROUP_W), 0) // HEAD_W,
                      1.0, 0.0).astype(BF16)

    def head_sum(z):
        z_hi, z_mid, z_lo = _split3(z)
        return _dot(z_hi, gones) + _dot(z_mid, gones) + _dot(z_lo, gones)

    q = q * lax.rsqrt(head_sum(q * q) + 1e-6) * (HEAD_W ** -0.5)
    tick()
    k = k * lax.rsqrt(head_sum(k * k) + 1e-6)
    tick()
    gab = gab_ref[...]
    za = gab[:, 0:256] + dtb_ref[...]
    softplus = jnp.maximum(za, 0.0) + jnp.log(1.0 + jnp.exp(-jnp.abs(za)))
    g = -jnp.exp(alog_ref[...]) * softplus
    beta = 1.0 / (1.0 + jnp.exp(-gab[:, 256:512]))

    sq = (1, c, N_HEADS * c)
    ri = lax.broadcasted_iota(jnp.int32, sq, 1)
    cj = lax.broadcasted_iota(jnp.int32, sq, 2) % c
    tr = lax.broadcasted_iota(jnp.int32, (tg, tg), 0)
    tc = lax.broadcasted_iota(jnp.int32, (tg, tg), 1)
    tri_b = jnp.where((tc <= tr) & (tc // c == tr // c), 1.0, 0.0).astype(BF16)
    g_hi, g_mid, g_lo = _split3(g)
    decay2 = _dot(tri_b, g_hi) + _dot(tri_b, g_mid) + _dot(tri_b, g_lo)
    tick()
    chunked = lambda z: z.reshape(nc, c, z.shape[-1])
    qb, kb_, vb, bb, decay = chunked(q), chunked(k), chunked(v), chunked(beta), chunked(decay2)
    dsq = _to_square(decay, c)
    drow = jnp.sum(jnp.where(ri == cj, dsq, 0.0), axis=1, keepdims=True)
    lm = jnp.where(cj <= ri, jnp.exp(jnp.where(cj <= ri, dsq - drow, 0.0)), 0.0)
    kbeta = kb_ * bb
    kq = _bdot_nt(jnp.concatenate([kbeta, qb], axis=1), _stack4(kb_, HEAD_W))
    tick()
    a_mat = jnp.where(cj < ri, kq[:, 0:c] * lm, 0.0)
    edec = jnp.exp(decay)
    tinv = _unit_lower_inverse(a_mat, c, tick)
    uw = _bdot(tinv, jnp.concatenate([_stack4(vb * bb, HEAD_W), _stack4(kbeta * edec, HEAD_W)], axis=2))
    tick()
    dlast = decay[:, c - 1:c]
    u_scr[slot] = uw[:, :, 0:GROUP_W]
    w_scr[slot] = uw[:, :, GROUP_W:2 * GROUP_W]
    qk_scr[slot] = kq[:, c:2 * c] * lm
    qd_scr[slot] = qb * edec
    kt_scr[slot] = kb_ * jnp.exp(dlast - decay)
    gl_scr[slot] = jnp.exp(dlast)

    while st["n"] < nc:
        tick()
    s_fin = jnp.where(t > 0, st["s"], s_old)
    s_scr[...] = s_fin
    for hh in range(N_HEADS):
        s_ref[0, hh] = s_fin[:, HEAD_W * hh:HEAD_W * (hh + 1)]
    outs = st["outs"]
    o = outs[0] if len(outs) == 1 else jnp.concatenate(outs, axis=0)
    o = o * lax.rsqrt(head_sum(o * o) * (1.0 / HEAD_W) + EPS) * ng_ref[...]
    o_ref[...] = (o * _silu(gate_ref[...])).astype(BF16)


def _gdn(gqkv, ggate, gab, buf, s0, cw, alog, dtb, ng, *, batch, seq, tg, c):
    nj = seq // tg
    nt = batch * nj
    nc = tg // c
    t = batch * seq
    cur_t = lambda i: jnp.minimum(i, nt - 1)
    prv_t = lambda i: jnp.maximum(i - 1, 0)
    cur = lambda wd: pl.BlockSpec((tg, wd), lambda i: (cur_t(i), 0))
    prv = lambda wd: pl.BlockSpec((tg, wd), lambda i: (prv_t(i), 0))
    pre_shape = lambda wd: pltpu.VMEM((2, nc, c, wd), F32)
    return pl.pallas_call(
        functools.partial(_gdn_kernel, tg=tg, c=c, nj=nj),
        grid=(nt + 1,),
        in_specs=[
            cur(768), prv(256), cur(512),
            pl.BlockSpec((1, GDN_CONV - 1, 768), lambda i: (cur_t(i) // nj, 0, 0)),
            pl.BlockSpec((1, N_HEADS, HEAD_W, HEAD_W), lambda i: (prv_t(i) // nj, 0, 0, 0)),
            _const_spec(cw.shape), _const_spec(alog.shape), _const_spec(dtb.shape), _const_spec(ng.shape),
        ],
        out_specs=[
            prv(256),
            pl.BlockSpec((1, GDN_CONV - 1, 768), lambda i: (cur_t(i) // nj, 0, 0)),
            pl.BlockSpec((1, N_HEADS, HEAD_W, HEAD_W), lambda i: (prv_t(i) // nj, 0, 0, 0)),
        ],
        out_shape=[
            jax.ShapeDtypeStruct((t, 256), BF16),
            jax.ShapeDtypeStruct((batch, GDN_CONV - 1, 768), F32),
            jax.ShapeDtypeStruct((batch, N_HEADS, HEAD_W, HEAD_W), F32),
        ],
        scratch_shapes=[pltpu.VMEM((8, 768), F32), pltpu.VMEM((HEAD_W, GROUP_W), F32),
                        pre_shape(GROUP_W), pre_shape(GROUP_W), pre_shape(N_HEADS * c), pre_shape(GROUP_W),
                        pre_shape(GROUP_W), pltpu.VMEM((2, nc, 1, GROUP_W), F32)],
        compiler_params=_cparams(("arbitrary",)), name="gdn",
    )(gqkv, ggate, gab, buf, s0, cw, alog, dtb, ng)


def _ffn_kernel(x_ref, om_ref, od_ref, os_ref, og_ref, buf_ref, wout_ref, g2_ref, wup_ref, cw_ref, wdn_ref,
                fg_ref, y_ref, conv_ref, prev_scr, act_scr, *, tm, d_ff, ft, final):
    j = pl.program_id(1)
    nj = pl.num_programs(1)

    @pl.when(j == 0)
    def _():
        prev_scr[...] = jnp.zeros_like(prev_scr)
        prev_scr[8 - (FFN_CONV - 1):8, :] = buf_ref[0]

    mixed = jnp.concatenate([om_ref[...], od_ref[...], os_ref[...], og_ref[...]], axis=1)
    x1 = x_ref[...] + _dot(mixed, wout_ref[...])
    h2 = (x1 * lax.rsqrt(jnp.mean(x1 * x1, axis=-1, keepdims=True) + EPS) * g2_ref[...]).astype(BF16)

    def conv_cols(off):
        a = _dot(h2, wup_ref[:, off:off + ft])
        prev8 = prev_scr[:, off:off + ft]
        cw = cw_ref[:, off:off + ft]
        y = a * cw[FFN_CONV - 1:FFN_CONV]
        for s in range(1, FFN_CONV):
            y = y + _shift_rows(a, prev8, s) * cw[FFN_CONV - 1 - s:FFN_CONV - s]
        prev_scr[:, off:off + ft] = a[tm - 8:tm]
        return y

    for fi in range(d_ff // ft):
        gate = conv_cols(fi * ft)
        up = conv_cols(d_ff + fi * ft)
        act_scr[:, fi * ft:(fi + 1) * ft] = (_silu(gate) * up).astype(BF16)
    acc = x1 + _dot(act_scr[...], wdn_ref[...])

    if final:
        acc = acc * lax.rsqrt(jnp.mean(acc * acc, axis=-1, keepdims=True) + EPS) * fg_ref[...]
    y_ref[...] = acc

    @pl.when(j == nj - 1)
    def _():
        conv_ref[0] = prev_scr[8 - (FFN_CONV - 1):8, :]


def _ffn(x, om, od, osg, og, buf, wout, g2, wup, cw, wdn, fg, *, batch, seq, tm, ft, final):
    nj = seq // tm
    t, d = x.shape
    d_ff = wdn.shape[0]
    row = lambda wd: pl.BlockSpec((tm, wd), lambda b, j: (b * nj + j, 0))
    return pl.pallas_call(
        functools.partial(_ffn_kernel, tm=tm, d_ff=d_ff, ft=ft, final=final),
        grid=(batch, nj),
        in_specs=[
            row(d), row(256), row(256), row(256), row(256),
            pl.BlockSpec((1, FFN_CONV - 1, 2 * d_ff), lambda b, j: (b, 0, 0)),
            _const_spec(wout.shape), _const_spec(g2.shape), _const_spec(wup.shape),
            _const_spec(cw.shape), _const_spec(wdn.shape), _const_spec(fg.shape),
        ],
        out_specs=[row(d), pl.BlockSpec((1, FFN_CONV - 1, 2 * d_ff), lambda b, j: (b, 0, 0))],
        out_shape=[jax.ShapeDtypeStruct((t, d), F32),
                   jax.ShapeDtypeStruct((batch, FFN_CONV - 1, 2 * d_ff), F32)],
        scratch_shapes=[pltpu.VMEM((8, 2 * d_ff), F32), pltpu.VMEM((tm, d_ff), BF16)],
        compiler_params=_cparams(("parallel", "arbitrary")), name="ffn",
    )(x, om, od, osg, og, buf, wout, g2, wup, cw, wdn, fg)


def _rot_cols(w):
    d = w.shape[0]
    wg = w.reshape(d, -1, 2, MLA_DR // 2)
    return jnp.concatenate([-wg[:, :, 1:2], wg[:, :, 0:1]], axis=2).reshape(d, -1)


def _permute_w_in(w_in):
    d = w_in.shape[0]
    pts = np.cumsum([384, 128, 32, 256, 256, 256, 256, 256, 768, 256, 4, 4])[:-1]
    mq, mlat, mkr, dq, dk, dv, su, sv, gqkv, ggate, ga, gb = jnp.split(w_in, pts, axis=1)
    mq = mq.reshape(d, N_HEADS, MLA_DN + MLA_DR)
    qn = mq[:, :, :MLA_DN].reshape(d, N_HEADS * MLA_DN)
    qr = mq[:, :, MLA_DN:].reshape(d, N_HEADS * MLA_DR)
    kr4 = jnp.tile(mkr, (1, N_HEADS))
    rep = lambda a: jnp.repeat(a, HEAD_W, axis=1)
    w = jnp.concatenate([qn, qr, _rot_cols(qr), mlat, kr4, _rot_cols(kr4),
                         dq, dk, dv, su, sv, gqkv, ggate, rep(ga), rep(gb)], axis=1)
    assert w.shape[1] == C_TOTAL
    wt = jnp.transpose(jnp.concatenate([qn, qr, _rot_cols(qr), mlat, dq, dv], axis=1))
    assert wt.shape[0] == WT_ROWS
    return w.astype(BF16), wt.astype(BF16)


def _rope_tables(pos):
    inv = ROPE_THETA ** (-jnp.arange(0, MLA_DR, 2, dtype=F32) / MLA_DR)
    ang = pos.astype(F32)[:, None] * inv[None, :]
    cos = jnp.tile(jnp.cos(ang), (1, 2 * N_HEADS))
    sin = jnp.tile(jnp.sin(ang), (1, 2 * N_HEADS))
    return cos, sin


def _layer_params(l, p):
    row = lambda a: a.reshape(1, -1)
    rep = lambda a: jnp.repeat(a, HEAD_W).reshape(1, -1)
    sgu_w = p['sgu_w'][l]
    w_perm, w_t = _permute_w_in(p['w_in'][l])
    return dict(
        w_in=w_perm, w_t=w_t,
        g1=row(p['norm1_g'][l]),
        wukt=jnp.transpose(p['mla_w_uk'][l], (1, 2, 0)).astype(BF16),
        wuk=jnp.transpose(p['mla_w_uk'][l], (1, 0, 2)).astype(BF16),
        wuv=jnp.transpose(p['mla_w_uv'][l], (1, 0, 2)).astype(BF16),
        wuvt=jnp.transpose(p['mla_w_uv'][l], (1, 2, 0)).astype(BF16),
        latg=row(p['mla_lat_g'][l]), latgt=p['mla_lat_g'][l].reshape(-1, 1),
        sgu_w=sgu_w,
        sgu_b=p['sgu_b'][l],
        lng=row(p['sgu_ln_g'][l]), lnb=row(p['sgu_ln_b'][l]),
        lamv=jnp.stack([p['diff_lam_q1'][l], p['diff_lam_k1'][l], p['diff_lam_q2'][l], p['diff_lam_k2'][l]]),
        subg=row(p['diff_sub_g'][l]),
        gcw=p['gdn_conv_w'][l],
        alog=rep(p['gdn_a_log'][l]), dtb=rep(p['gdn_dt_bias'][l]),
        ng=jnp.tile(p['gdn_norm_g'][l], N_HEADS).reshape(1, -1),
        wout=p['w_out'][l].astype(BF16),
        g2=row(p['norm2_g'][l]),
        wup=p['ffn_w_up'][l].astype(BF16),
        fcw=p['ffn_conv_w'][l],
        wdn=p['ffn_w_down'][l].astype(BF16),
        fg=row(p['final_g']),
    )


def _sgu_tables(lp, c):
    w = lp['sgu_w'][:, :c, :c].reshape(N_HEADS * c, c)
    b = jnp.repeat(jnp.transpose(lp['sgu_b'][:, :c]), HEAD_W, axis=1)
    return w, b


def _run_inproj(x, lp, cos, sin, *, tm, sgu_c, prompt):
    sw, sb = _sgu_tables(lp, sgu_c)
    return _inproj(x, lp['g1'], lp['w_in'], lp['w_t'], cos, sin, jnp.transpose(cos), jnp.transpose(sin),
                   lp['wukt'], lp['wuk'], lp['latg'], lp['latgt'], sw, sb, lp['lng'], lp['lnb'],
                   tm=tm, sgu_c=sgu_c, prompt=prompt)


def kernel(x_prompt, x_sample, cache_mla_latent, cache_mla_krope, cache_diff_k, cache_diff_v, state_gdn_conv, state_gdn_s, state_ffn_conv, t5_table, final_g, norm1_g, w_in, mla_lat_g, mla_w_uk, mla_w_uv, diff_lam_q1, diff_lam_k1, diff_lam_q2, diff_lam_k2, diff_sub_g, sgu_ln_g, sgu_ln_b, sgu_w, sgu_b, gdn_conv_w, gdn_a_log, gdn_dt_bias, gdn_norm_g, w_out, norm2_g, ffn_w_up, ffn_conv_w, ffn_w_down):
    p = dict(final_g=final_g, norm1_g=norm1_g, w_in=w_in, mla_lat_g=mla_lat_g, mla_w_uk=mla_w_uk,
             mla_w_uv=mla_w_uv, diff_lam_q1=diff_lam_q1, diff_lam_k1=diff_lam_k1, diff_lam_q2=diff_lam_q2,
             diff_lam_k2=diff_lam_k2, diff_sub_g=diff_sub_g, sgu_ln_g=sgu_ln_g, sgu_ln_b=sgu_ln_b,
             sgu_w=sgu_w, sgu_b=sgu_b, gdn_conv_w=gdn_conv_w, gdn_a_log=gdn_a_log, gdn_dt_bias=gdn_dt_bias,
             gdn_norm_g=gdn_norm_g, w_out=w_out, norm2_g=norm2_g, ffn_w_up=ffn_w_up, ffn_conv_w=ffn_conv_w,
             ffn_w_down=ffn_w_down)
    depth = w_in.shape[0]
    bp, sp, d = x_prompt.shape
    bs, ls, _ = x_sample.shape
    past = cache_mla_latent.shape[2]
    d_ff = ffn_w_down.shape[1]
    assert past % CHUNK == 0 and ls <= CHUNK

    tm_p = min(512, sp)
    tq = min(512, sp)
    tg_p = min(512, sp)
    ft = 256
    sgu_cp = min(SGU_CHUNK, sp)
    gdn_cp = min(GDN_CHUNK, sp)

    cos_p, sin_p = _rope_tables(jnp.arange(sp, dtype=jnp.int32))
    pos_s = past + jnp.arange(ls, dtype=jnp.int32)
    cos_s, sin_s = _rope_tables(jnp.tile(pos_s, bs))

    assert tq >= DIFF_CORNER and tm_p == tq
    stack_t = lambda b: jnp.transpose(jnp.repeat(b, 2, axis=0), (2, 0, 1)).reshape(b.shape[2], -1)
    bias_p = stack_t(_t5_bias(t5_table, tq, tq, (0,), (0,))[0])
    corner = _t5_bias(t5_table, DIFF_CORNER, DIFF_CORNER, (DIFF_CORNER,), (0,))[0]
    bias_pc = stack_t(jnp.pad(corner, ((0, 0), (0, tq - DIFF_CORNER), (0, 0))))
    bias_sc = _t5_bias(t5_table, ls, past, (past,), (0,))
    bias_sn = _t5_bias(t5_table, ls, ls, (past,), (past,))

    xp = x_prompt.reshape(bp * sp, d)
    xs = x_sample.reshape(bs * ls, d)
    zeros_gconv = jnp.zeros((bp, GDN_CONV - 1, 3 * GROUP_W), F32)
    zeros_gs = jnp.zeros((bp, N_HEADS, HEAD_W, HEAD_W), F32)
    zeros_fconv = jnp.zeros((bp, FFN_CONV - 1, 2 * d_ff), F32)

    outs_p, outs_s = [], []
    for l in range(depth):
        lp = _layer_params(l, p)
        lam_init = 0.8 - 0.6 * math.exp(-0.3 * l)
        final = l == depth - 1

        ip = _run_inproj(xp, lp, cos_p, sin_p, tm=tm_p, sgu_c=sgu_cp, prompt=True)
        lat, kr, dk, dv, osgu = ip['lat'], ip['kr'], ip['dk'], ip['dv'], ip['osgu']
        gqkv, ggate, gab = ip['gqkv'], ip['ggate'], ip['gab']
        o_mla = _mla_prompt(ip['qmt'], ip['km'], ip['vmt'], lp['wuvt'], batch=bp, seq=sp, tq=tq)
        o_diff = _diff_prompt(ip['dqt'], ip['kd'], ip['vdt'], bias_p, bias_pc, lp['lamv'],
                              jnp.tile(lp['subg'], (1, N_HEADS)).reshape(-1, 1),
                              batch=bp, seq=sp, tq=tq, lam_init=lam_init)
        o_gdn, gconv, gs = _gdn(gqkv, ggate, gab, zeros_gconv, zeros_gs, lp['gcw'], lp['alog'], lp['dtb'],
                                lp['ng'], batch=bp, seq=sp, tg=tg_p, c=gdn_cp)
        xp, fconv = _ffn(xp, o_mla, o_diff, osgu, o_gdn, zeros_fconv, lp['wout'], lp['g2'], lp['wup'],
                         lp['fcw'], lp['wdn'], lp['fg'], batch=bp, seq=sp, tm=tm_p, ft=ft, final=final)
        outs_p.append(dict(
            lat=lat.reshape(bp, sp, MLA_R), kr=kr.reshape(bp, sp, MLA_DR),
            dk=dk.reshape(bp, sp, N_HEADS, 2, DIFF_DH), dv=dv.reshape(bp, sp, N_HEADS, 2 * DIFF_DH),
            gconv=gconv, gs=gs, fconv=fconv))

        ip = _run_inproj(xs, lp, cos_s, sin_s, tm=bs * ls, sgu_c=min(SGU_CHUNK, ls), prompt=False)
        qm, km, vm, lat, kr, dq, dk, dv = (ip[k] for k in ('qm', 'km', 'vm', 'lat', 'kr', 'dq', 'dk', 'dv'))
        osgu, vn, gqkv, ggate, gab = (ip[k] for k in ('osgu', 'vn', 'gqkv', 'ggate', 'gab'))
        kc_m = jnp.concatenate([cache_mla_latent[l], jnp.tile(cache_mla_krope[l], (1, 1, N_HEADS))],
                               axis=-1).astype(BF16)
        o_mla = _mla_sample(qm, kc_m, km, vm, lp['wuv'], batch=bs, lq=ls)
        kc_d = cache_diff_k[l].reshape(bs, past, GROUP_W).astype(BF16)
        vc_d = cache_diff_v[l].reshape(bs, past, GROUP_W).astype(BF16)
        o_diff = _diff_sample(dq, kc_d, vc_d, dk, dv, bias_sc, bias_sn, lp['lamv'], lp['subg'],
                              batch=bs, lq=ls, lam_init=lam_init)
        o_gdn, gconv, gs = _gdn(gqkv, ggate, gab, state_gdn_conv[l], state_gdn_s[l], lp['gcw'], lp['alog'],
                                lp['dtb'], lp['ng'], batch=bs, seq=ls, tg=ls, c=min(GDN_CHUNK, ls))
        xs, fconv = _ffn(xs, o_mla, o_diff, osgu, o_gdn, state_ffn_conv[l], lp['wout'], lp['g2'], lp['wup'],
                         lp['fcw'], lp['wdn'], lp['fg'], batch=bs, seq=ls, tm=ls, ft=ft, final=final)
        outs_s.append(dict(
            lat=lat.reshape(bs, ls, MLA_R), kr=kr.reshape(bs, ls, MLA_DR),
            dk=dk.reshape(bs, ls, N_HEADS, 2, DIFF_DH), dv=dv.reshape(bs, ls, N_HEADS, 2 * DIFF_DH),
            gconv=gconv, gs=gs, fconv=fconv, sv=vn.reshape(bs, ls, GROUP_W)))

    st = lambda lst, key: jnp.stack([o[key] for o in lst])
    return (xp.reshape(bp, sp, d), xs.reshape(bs, ls, d),
            st(outs_p, 'lat'), st(outs_p, 'kr'), st(outs_p, 'dk'), st(outs_p, 'dv'),
            st(outs_p, 'gconv'), st(outs_p, 'gs'), st(outs_p, 'fconv'),
            st(outs_s, 'lat'), st(outs_s, 'kr'), st(outs_s, 'dk'), st(outs_s, 'dv'),
            st(outs_s, 'gconv'), st(outs_s, 'gs'), st(outs_s, 'fconv'), st(outs_s, 'sv'))
```

```python
import functools
import math

import numpy as np
import jax
import jax.numpy as jnp
from jax import lax
from jax.experimental import pallas as pl
from jax.experimental.pallas import tpu as pltpu

F32 = jnp.float32
BF16 = jnp.bfloat16

CHUNK = 64
EPS = 1e-6
N_HEADS = 4
HEAD_W = 64
GROUP_W = 256
MLA_DN, MLA_DR, MLA_R = 64, 32, 128
DIFF_DH = 32
ROPE_THETA = 10000.0
SGU_CHUNK = 128
GDN_CHUNK = 64
GDN_CONV = 4
FFN_CONV = 3
T5_BUCKETS = 32
T5_MAX_DIST = 128
NEG = -1e30
LOG2E = math.log2(math.e)
SOLVE_BLOCK = 16
DIFF_CORNER = 128
MLA_VT_ROWS = 144
VT_ROWS = 80

VMEM_LIMIT_BYTES = 56 * 1024 * 1024

C_QN, C_QR, C_QRR, C_LAT, C_KR, C_KRR = 0, 256, 384, 512, 640, 768
C_DQ, C_DK, C_DV, C_SU, C_SV = 896, 1152, 1408, 1664, 1920
C_GQKV, C_GG, C_GA, C_GB = 2176, 2944, 3200, 3456
C_TOTAL = 3712
WT_DQ, WT_ROWS = 640, 1152
N_INPROJ_INPUTS = 16


def _cparams(sem):
    return pltpu.CompilerParams(dimension_semantics=sem, vmem_limit_bytes=VMEM_LIMIT_BYTES)


def _const_spec(shape):
    nd = len(shape)
    return pl.BlockSpec(shape, lambda *_: (0,) * nd, pipeline_mode=pl.Buffered(1))


def _dot(a, b):
    return jnp.dot(a, b, preferred_element_type=F32)


def _dot_nt(a, b):
    return lax.dot_general(a, b, (((1,), (1,)), ((), ())), preferred_element_type=F32)


def _dot_tn(a, b):
    return lax.dot_general(a, b, (((0,), (0,)), ((), ())), preferred_element_type=F32)


def _lane_group(shape, group):
    return lax.broadcasted_iota(jnp.int32, shape, len(shape) - 1) // group


def _silu(x):
    return x * (1.0 / (1.0 + jnp.exp(-x)))


def _shift_rows(x, prev8, s):
    if s == 0:
        return x
    r = pltpu.roll(x, s, axis=0)
    rp = pltpu.roll(prev8, s, axis=0)
    row8 = lax.broadcasted_iota(jnp.int32, rp.shape, 0)
    top = jnp.where(row8 < s, rp, r[0:8])
    if x.shape[0] == 8:
        return top
    return jnp.concatenate([top, r[8:]], axis=0)


_INPROJ_COMMON = ("km", "lat", "kr", "dk", "dv", "kd", "osgu", "vn", "gqkv", "ggate", "gab")
_INPROJ_PROMPT = ("qmt", "vmt", "dqt", "vdt")
_INPROJ_SAMPLE = ("qm", "vm", "dq")
_INPROJ_STACKED = ("lat", "kr", "dk", "dv")


def _inproj_kernel(x_ref, g1_ref, w_ref, wt_ref, cos_ref, sin_ref, cost_ref, sint_ref, wukt_ref, wuk_ref,
                   latg_ref, latgt_ref, sguw_ref, sgub_ref, lng_ref, lnb_ref, *rest, sgu_c, prompt):
    out_refs = rest[len(_INPROJ_STACKED):] if prompt else rest
    o = dict(zip(_INPROJ_COMMON + (_INPROJ_PROMPT if prompt else _INPROJ_SAMPLE), out_refs))
    leaf = (lambda k: o[k].at[0]) if prompt else (lambda k: o[k])
    tm = x_ref.shape[0]
    x = x_ref[...]
    h = (x * lax.rsqrt(jnp.mean(x * x, axis=-1, keepdims=True) + EPS) * g1_ref[...]).astype(BF16)

    def sec(off, n):
        return _dot(h, w_ref[:, off:off + n])

    cos = cos_ref[...]
    sin = sin_ref[...]
    mla_scale = (MLA_DN + MLA_DR) ** -0.5 * LOG2E
    diff_scale = DIFF_DH ** -0.5 * LOG2E

    zl = sec(C_LAT, 384)
    zlat = zl[:, 0:128]
    lat = zlat * lax.rsqrt(jnp.mean(zlat * zlat, axis=-1, keepdims=True) + EPS) * latg_ref[...]
    kr4 = zl[:, 128:256] * cos + zl[:, 256:384] * sin
    leaf("lat")[...] = lat
    leaf("kr")[...] = kr4[:, 0:MLA_DR]
    lat_b = lat.astype(BF16)
    o["km"][:, 0:128] = lat_b
    o["km"][:, 128:256] = kr4.astype(BF16)

    zd = sec(C_DK, 512)
    dk = zd[:, 0:256]
    dv = zd[:, 256:512]
    leaf("dk")[...] = dk
    leaf("dv")[...] = dv
    o["kd"][...] = dk.astype(BF16)

    if prompt:
        zt = _dot_nt(wt_ref[0:WT_DQ, :], h)
        qrt = (zt[256:384] * cost_ref[...] + zt[384:512] * sint_ref[...]) * mla_scale
        head_of_row = lax.broadcasted_iota(jnp.int32, (128, 1), 0) // MLA_DR
        for hh in range(N_HEADS):
            qn = (zt[MLA_DN * hh:MLA_DN * (hh + 1)] * mla_scale).astype(BF16)
            o["qmt"][hh, 0:128, :] = _dot(wuk_ref[hh], qn).astype(BF16)
            o["qmt"][hh, 128:256, :] = jnp.where(head_of_row == hh, qrt, 0.0).astype(BF16)
        zlt = zt[512:640]
        latt = zlt * lax.rsqrt(jnp.mean(zlt * zlt, axis=0, keepdims=True) + EPS) * latgt_ref[...]
        o["vmt"][0] = jnp.concatenate([latt, jnp.ones((MLA_VT_ROWS - MLA_R, tm), F32)], axis=0).astype(BF16)
        zt2 = _dot_nt(wt_ref[WT_DQ:WT_ROWS, :], h)
        o["dqt"][...] = (zt2[0:GROUP_W] * diff_scale).astype(BF16)
        ones_rows = jnp.ones((VT_ROWS - HEAD_W, tm), F32)
        for hh in range(N_HEADS):
            vt = zt2[GROUP_W + HEAD_W * hh:GROUP_W + HEAD_W * (hh + 1)]
            o["vdt"][hh, 0] = jnp.concatenate([vt, ones_rows], axis=0).astype(BF16)
    else:
        zq = sec(C_QN, 512)
        qr = (zq[:, 256:384] * cos + zq[:, 384:512] * sin) * mla_scale
        head_of_lane = _lane_group((1, 128), MLA_DR)
        for hh in range(N_HEADS):
            qn = (zq[:, MLA_DN * hh:MLA_DN * (hh + 1)] * mla_scale).astype(BF16)
            o["qm"][hh, :, 0:128] = _dot(qn, wukt_ref[hh]).astype(BF16)
            o["qm"][hh, :, 128:256] = jnp.where(head_of_lane == hh, qr, 0.0).astype(BF16)
        o["vm"][:, 0:128] = lat_b
        o["vm"][:, 128:256] = jnp.ones((tm, 128), BF16)
        o["dq"][...] = (sec(C_DQ, 256) * diff_scale).astype(BF16)

    zs = sec(C_SU, 512)
    su = zs[:, 0:256]
    sv = zs[:, 256:512]
    mu = jnp.mean(sv, axis=-1, keepdims=True)
    svc = sv - mu
    var = jnp.mean(svc * svc, axis=-1, keepdims=True)
    vn = svc * lax.rsqrt(var + EPS) * lng_ref[...] + lnb_ref[...]
    o["vn"][...] = vn
    c = sgu_c
    wr = lax.broadcasted_iota(jnp.int32, (N_HEADS * c, c), 0) % c
    wc = lax.broadcasted_iota(jnp.int32, (N_HEADS * c, c), 1)
    w4 = jnp.where(wc <= wr, sguw_ref[...], 0.0).astype(BF16)
    hl = _lane_group((1, GROUP_W), HEAD_W)
    sgub = sgub_ref[...]
    for ci in range(tm // c):
        rows = slice(ci * c, (ci + 1) * c)
        m4 = _dot(w4, vn[rows].astype(BF16))
        mix = jnp.where(hl == 0, m4[0:c], 0.0)
        for hh in range(1, N_HEADS):
            mix = mix + jnp.where(hl == hh, m4[hh * c:(hh + 1) * c], 0.0)
        o["osgu"][rows, :] = (su[rows] * (mix + sgub)).astype(BF16)

    zg = sec(C_GQKV, 1536)
    o["gqkv"][...] = zg[:, 0:768]
    o["ggate"][...] = zg[:, 768:1024]
    o["gab"][...] = zg[:, 1024:1536]


def _inproj(x, g1, w, wt, cos, sin, cost, sint, wukt, wuk, latg, latgt, sguw, sgub, lng, lnb, stacked, *,
            tm, sgu_c, prompt, layer):
    t, d = x.shape
    n = t // tm
    npos = cos.shape[0] // tm
    row = lambda wd: pl.BlockSpec((tm, wd), lambda i: (i, 0))
    shapes = dict(
        km=((t, 256), BF16, row(256)), lat=((t, MLA_R), F32, row(MLA_R)), kr=((t, MLA_DR), F32, row(MLA_DR)),
        dk=((t, 256), F32, row(256)), dv=((t, 256), F32, row(256)), kd=((t, 256), BF16, row(256)),
        osgu=((t, 256), BF16, row(256)), vn=((t, 256), F32, row(256)), gqkv=((t, 768), F32, row(768)),
        ggate=((t, 256), F32, row(256)), gab=((t, 512), F32, row(512)),
        qmt=((N_HEADS, 256, t), BF16, pl.BlockSpec((N_HEADS, 256, tm), lambda i: (0, 0, i))),
        vmt=((n, MLA_VT_ROWS, tm), BF16, pl.BlockSpec((1, MLA_VT_ROWS, tm), lambda i: (i, 0, 0))),
        dqt=((GROUP_W, t), BF16, pl.BlockSpec((GROUP_W, tm), lambda i: (0, i))),
        vdt=((N_HEADS, n, VT_ROWS, tm), BF16, pl.BlockSpec((N_HEADS, 1, VT_ROWS, tm), lambda i: (0, i, 0, 0))),
        qm=((N_HEADS, t, 256), BF16, pl.BlockSpec((N_HEADS, tm, 256), lambda i: (0, i, 0))),
        vm=((t, 256), BF16, row(256)), dq=((t, 256), BF16, row(256)),
    )
    names = _INPROJ_COMMON + (_INPROJ_PROMPT if prompt else _INPROJ_SAMPLE)
    aliases = {}
    if prompt:
        for idx, k in enumerate(_INPROJ_STACKED):
            depth, wd = stacked[idx].shape[0], shapes[k][0][1]
            shapes[k] = ((depth, t, wd), F32, pl.BlockSpec((1, tm, wd), lambda i: (layer, i, 0)))
            aliases[N_INPROJ_INPUTS + idx] = names.index(k)
    pos_row = pl.BlockSpec((tm, 128), lambda i: (i % npos, 0))
    pos_col = pl.BlockSpec((128, tm), lambda i: (0, i % npos))
    consts = (g1, w, wt)
    in_specs = ([row(d)] + [_const_spec(a.shape) for a in consts] + [pos_row, pos_row, pos_col, pos_col]
                + [_const_spec(a.shape) for a in (wukt, wuk, latg, latgt, sguw, sgub, lng, lnb)]
                + [pl.BlockSpec(memory_space=pl.ANY) for _ in stacked])
    assert len(in_specs) == N_INPROJ_INPUTS + len(stacked)
    outs = pl.pallas_call(
        functools.partial(_inproj_kernel, sgu_c=sgu_c, prompt=prompt),
        grid=(n,), in_specs=in_specs,
        out_specs=[shapes[k][2] for k in names],
        out_shape=[jax.ShapeDtypeStruct(shapes[k][0], shapes[k][1]) for k in names],
        input_output_aliases=aliases,
        compiler_params=_cparams(("parallel",)), name="inproj",
    )(x, g1, w, wt, cos, sin, cost, sint, wukt, wuk, latg, latgt, sguw, sgub, lng, lnb, *stacked)
    return dict(zip(names, outs))


def _t5_thresholds():
    nb = T5_BUCKETS // 2
    max_exact = nb // 2
    ratio = T5_MAX_DIST // max_exact
    thr = []
    for j in range(1, nb - max_exact):
        n = max_exact
        while n ** (nb - max_exact) < (ratio ** j) * (max_exact ** (nb - max_exact)):
            n += 1
        thr.append(n)
    return nb, max_exact, thr


def _bias_kernel(t_ref, o_ref, *, q0s, k0s):
    nb, max_exact, thr = _t5_thresholds()
    tq, tk = o_ref.shape[2], o_ref.shape[3]
    row = lax.broadcasted_iota(jnp.int32, (tq, tk), 0)
    col = lax.broadcasted_iota(jnp.int32, (tq, tk), 1)
    for di, (q0, k0) in enumerate(zip(q0s, k0s)):
        qpos = row + q0
        kpos = col + k0
        rel = kpos - qpos
        n = jnp.abs(rel)
        visible = (kpos // CHUNK) <= (qpos // CHUNK)
        for hh in range(N_HEADS):
            def side(base):
                val = jnp.full((tq, tk), t_ref[base + nb - 1, hh], F32)
                for j in range(len(thr) - 1, -1, -1):
                    val = jnp.where(n < thr[j], t_ref[base + max_exact + j, hh], val)
                for e in range(max_exact - 1, -1, -1):
                    val = jnp.where(n == e, t_ref[base + e, hh], val)
                return val
            b = jnp.where(rel > 0, side(nb), side(0)) - t_ref[nb - 1, hh]
            o_ref[di, hh] = jnp.where(visible, b * LOG2E, NEG)


def _t5_bias(t5_table, tq, tk, q0s, k0s):
    nd = len(q0s)
    return pl.pallas_call(
        functools.partial(_bias_kernel, q0s=tuple(q0s), k0s=tuple(k0s)),
        in_specs=[pl.BlockSpec(memory_space=pltpu.SMEM)],
        out_shape=jax.ShapeDtypeStruct((nd, N_HEADS, tq, tk), F32),
        name="t5_bias",
    )(t5_table)


def _mla_prompt_kernel(qt_ref, k_ref, vt_ref, wuvt_ref, o_ref, *, tq):
    qi = pl.program_id(1)
    qlim = ((qi * tq + lax.broadcasted_iota(jnp.int32, (1, tq), 1)) // CHUNK + 1) * CHUNK
    krow = lax.broadcasted_iota(jnp.int32, (tq, 1), 0)

    def step(kb, carry, masked):
        k0 = pl.multiple_of(kb * tq, tq)
        kblk = k_ref[0, pl.ds(k0, tq), :]
        scores = [_dot(kblk, qt_ref[hh]) for hh in range(N_HEADS)]
        ps, mns, alphas = [], [], []
        for hh in range(N_HEADS):
            m = carry[2 * hh]
            s = scores[hh]
            if masked:
                s = jnp.where(krow + k0 < qlim, s, NEG)
            mn = jnp.maximum(m, jnp.max(s, axis=0, keepdims=True))
            alphas.append(jnp.exp2(m - mn))
            ps.append(jnp.exp2(s - mn).astype(BF16))
            mns.append(mn)
        out = []
        for hh in range(N_HEADS):
            out += [mns[hh], alphas[hh] * carry[2 * hh + 1] + _dot(vt_ref[kb], ps[hh])]
        return tuple(out)

    carry = (jnp.full((1, tq), NEG, F32), jnp.zeros((MLA_VT_ROWS, tq), F32)) * N_HEADS
    carry = lax.fori_loop(0, qi, functools.partial(step, masked=False), carry)
    carry = step(qi, carry, True)
    outs = []
    for hh in range(N_HEADS):
        acc = carry[2 * hh + 1]
        on = (acc[0:MLA_R] * (1.0 / acc[MLA_R:MLA_R + 1])).astype(BF16)
        outs.append(_dot(wuvt_ref[hh], on))
    o_ref[...] = jnp.transpose(jnp.concatenate(outs, axis=0)).astype(BF16)


def _mla_prompt(qmt, km, vmt, wuvt, *, batch, seq, tq):
    assert seq % tq == 0 and vmt.shape[2] == tq and tq % CHUNK == 0
    nq = seq // tq
    t = batch * seq
    return pl.pallas_call(
        functools.partial(_mla_prompt_kernel, tq=tq),
        grid=(batch, nq),
        in_specs=[
            pl.BlockSpec((N_HEADS, 256, tq), lambda b, i: (0, 0, b * nq + i)),
            pl.BlockSpec((1, seq, 256), lambda b, i: (b, 0, 0)),
            pl.BlockSpec((nq, MLA_VT_ROWS, tq), lambda b, i: (b, 0, 0)),
            _const_spec(wuvt.shape),
        ],
        out_specs=pl.BlockSpec((tq, 256), lambda b, i: (b * nq + i, 0)),
        out_shape=jax.ShapeDtypeStruct((t, 256), BF16),
        compiler_params=_cparams(("parallel", "arbitrary")), name="mla_prompt",
    )(qmt, km.reshape(batch, seq, 256), vmt, wuvt)


def _diff_lambda(lam_ref, lam_init):
    l = lam_ref[...]
    a = jnp.sum(l[0:1] * l[1:2], axis=-1, keepdims=True)
    b = jnp.sum(l[2:3] * l[3:4], axis=-1, keepdims=True)
    return jnp.exp(a) - jnp.exp(b) + lam_init


def _stack_q8(q):
    grp = _lane_group((1, GROUP_W), DIFF_DH)
    return jnp.concatenate([jnp.where(grp == g, q, jnp.zeros_like(q)) for g in range(2 * N_HEADS)], axis=0)


def _diff_finish(on, lam, subg, lam_init, tq):
    outs = []
    for hh in range(N_HEADS):
        o = on[(2 * hh) * tq:(2 * hh + 1) * tq] - lam * on[(2 * hh + 1) * tq:(2 * hh + 2) * tq]
        o = o * lax.rsqrt(jnp.mean(o * o, axis=-1, keepdims=True) + 1e-5) * subg
        outs.append(o * (1.0 - lam_init))
    return jnp.concatenate(outs, axis=1)


def _diff_prompt_kernel(qt_ref, k_ref, vt_ref, bias_ref, cbias_ref, lam_ref, subg_ref, o_ref, *, tq, lam_init):
    qi = pl.program_id(1)
    rg = 2 * tq
    qt = qt_ref[...]
    fgrp = lax.broadcasted_iota(jnp.int32, (GROUP_W, 1), 0) // DIFF_DH
    qts = [jnp.concatenate([jnp.where(fgrp == 2 * hh + c, qt, jnp.zeros_like(qt)) for c in range(2)], axis=1)
           for hh in range(N_HEADS)]

    def step(kb, carry, bias):
        k0 = pl.multiple_of(kb * tq, tq)
        kblk = k_ref[0, pl.ds(k0, tq), :]
        scores = [_dot(kblk, qts[hh]) for hh in range(N_HEADS)]
        ps, mns, alphas = [], [], []
        for hh in range(N_HEADS):
            s = scores[hh]
            cols = slice(hh * rg, (hh + 1) * rg)
            if bias == "diag":
                s = s + bias_ref[:, cols]
            elif bias == "corner":
                s = jnp.concatenate([s[0:tq - DIFF_CORNER], s[tq - DIFF_CORNER:tq] + cbias_ref[:, cols]], axis=0)
            m = carry[2 * hh]
            mn = jnp.maximum(m, jnp.max(s, axis=0, keepdims=True))
            alphas.append(jnp.exp2(m - mn))
            ps.append(jnp.exp2(s - mn).astype(BF16))
            mns.append(mn)
        out = []
        for hh in range(N_HEADS):
            out += [mns[hh], alphas[hh] * carry[2 * hh + 1] + _dot(vt_ref[hh, kb], ps[hh])]
        return tuple(out)

    carry = (jnp.full((1, rg), NEG, F32), jnp.zeros((VT_ROWS, rg), F32)) * N_HEADS
    carry = lax.fori_loop(0, jnp.maximum(qi - 1, 0), functools.partial(step, bias=None), carry)
    carry = lax.fori_loop(jnp.maximum(qi - 1, 0), qi, functools.partial(step, bias="corner"), carry)
    carry = step(qi, carry, "diag")
    lam = _diff_lambda(lam_ref, lam_init)
    outs = []
    for hh in range(N_HEADS):
        acc = carry[2 * hh + 1]
        on = acc[0:HEAD_W] * (1.0 / acc[HEAD_W:HEAD_W + 1])
        o = on[:, 0:tq] - lam * on[:, tq:rg]
        outs.append(o * lax.rsqrt(jnp.mean(o * o, axis=0, keepdims=True) + 1e-5))
    ot = jnp.concatenate(outs, axis=0) * (subg_ref[...] * (1.0 - lam_init))
    o_ref[...] = jnp.transpose(ot).astype(BF16)


def _diff_prompt(dqt, kd, vdt, bias, cbias, lamv, subg, *, batch, seq, tq, lam_init):
    nq = seq // tq
    t = batch * seq
    assert vdt.shape[3] == tq
    return pl.pallas_call(
        functools.partial(_diff_prompt_kernel, tq=tq, lam_init=lam_init),
        grid=(batch, nq),
        in_specs=[
            pl.BlockSpec((GROUP_W, tq), lambda b, i: (0, b * nq + i)),
            pl.BlockSpec((1, seq, 256), lambda b, i: (b, 0, 0)),
            pl.BlockSpec((N_HEADS, nq, VT_ROWS, tq), lambda b, i: (0, b, 0, 0)),
            _const_spec(bias.shape), _const_spec(cbias.shape), _const_spec(lamv.shape), _const_spec(subg.shape),
        ],
        out_specs=pl.BlockSpec((tq, 256), lambda b, i: (b * nq + i, 0)),
        out_shape=jax.ShapeDtypeStruct((t, 256), BF16),
        compiler_params=_cparams(("parallel", "arbitrary")), name="diff_prompt",
    )(dqt, kd.reshape(batch, seq, 256), vdt, bias, cbias, lamv, subg)


def _mla_sample_kernel(q_ref, kc_ref, kn_ref, vn_ref, wuv_ref, o_ref, *, lq):
    r = N_HEADS * lq
    q = q_ref[...].reshape(r, 256)
    kc = kc_ref[0]
    sc = _dot_nt(q, kc)
    sn = _dot_nt(q, kn_ref[...])
    m = jnp.maximum(jnp.max(sc, axis=1, keepdims=True), jnp.max(sn, axis=1, keepdims=True))
    pc = jnp.exp2(sc - m)
    pn = jnp.exp2(sn - m)
    l = jnp.sum(pc, axis=1, keepdims=True) + jnp.sum(pn, axis=1, keepdims=True)
    pv = _dot(pc.astype(BF16), kc[:, 0:128]) + _dot(pn.astype(BF16), vn_ref[:, 0:128])
    on = (pv / l).astype(BF16)
    o_ref[...] = jnp.concatenate(
        [_dot(on[hh * lq:(hh + 1) * lq], wuv_ref[hh]) for hh in range(N_HEADS)], axis=1).astype(BF16)


def _mla_sample(qm, kc, km, vm, wuv, *, batch, lq):
    past = kc.shape[1]
    return pl.pallas_call(
        functools.partial(_mla_sample_kernel, lq=lq),
        grid=(batch,),
        in_specs=[
            pl.BlockSpec((N_HEADS, lq, 256), lambda b: (0, b, 0)),
            pl.BlockSpec((1, past, 256), lambda b: (b, 0, 0)),
            pl.BlockSpec((lq, 256), lambda b: (b, 0)),
            pl.BlockSpec((lq, 256), lambda b: (b, 0)),
            _const_spec(wuv.shape),
        ],
        out_specs=pl.BlockSpec((lq, 256), lambda b: (b, 0)),
        out_shape=jax.ShapeDtypeStruct((batch * lq, 256), BF16),
        compiler_params=_cparams(("parallel",)), name="mla_sample",
    )(qm, kc, km, vm, wuv)


def _diff_sample_kernel(q_ref, kc_ref, vc_ref, kn_ref, vn_ref, bc_ref, bn_ref, lam_ref, subg_ref, o_ref,
                        *, lq, lam_init):
    q8 = _stack_q8(q_ref[...])
    bc = jnp.concatenate([bc_ref[0, hh // 2] for hh in range(2 * N_HEADS)], axis=0)
    bn = jnp.concatenate([bn_ref[0, hh // 2] for hh in range(2 * N_HEADS)], axis=0)
    sc = _dot_nt(q8, kc_ref[0]) + bc
    sn = _dot_nt(q8, kn_ref[...].astype(BF16)) + bn
    m = jnp.maximum(jnp.max(sc, axis=1, keepdims=True), jnp.max(sn, axis=1, keepdims=True))
    pc = jnp.exp2(sc - m)
    pn = jnp.exp2(sn - m)
    l = jnp.sum(pc, axis=1, keepdims=True) + jnp.sum(pn, axis=1, keepdims=True)
    pv = (_dot(pc.astype(BF16), vc_ref[0]) + _dot(pn.astype(BF16), vn_ref[...].astype(BF16))) / l
    on = jnp.concatenate(
        [pv[g * lq:(g + 1) * lq, HEAD_W * (g // 2):HEAD_W * (g // 2 + 1)] for g in range(2 * N_HEADS)], axis=0)
    lam = _diff_lambda(lam_ref, lam_init)
    o_ref[...] = _diff_finish(on, lam, subg_ref[...], lam_init, lq).astype(BF16)


def _diff_sample(dq, kc, vc, dk, dv, bias_c, bias_n, lamv, subg, *, batch, lq, lam_init):
    past = kc.shape[1]
    return pl.pallas_call(
        functools.partial(_diff_sample_kernel, lq=lq, lam_init=lam_init),
        grid=(batch,),
        in_specs=[
            pl.BlockSpec((lq, 256), lambda b: (b, 0)),
            pl.BlockSpec((1, past, 256), lambda b: (b, 0, 0)),
            pl.BlockSpec((1, past, 256), lambda b: (b, 0, 0)),
            pl.BlockSpec((lq, 256), lambda b: (b, 0)),
            pl.BlockSpec((lq, 256), lambda b: (b, 0)),
            _const_spec(bias_c.shape), _const_spec(bias_n.shape),
            _const_spec(lamv.shape), _const_spec(subg.shape),
        ],
        out_specs=pl.BlockSpec((lq, 256), lambda b: (b, 0)),
        out_shape=jax.ShapeDtypeStruct((batch * lq, 256), BF16),
        compiler_params=_cparams(("parallel",)), name="diff_sample",
    )(dq, kc, vc, dk, dv, bias_c, bias_n, lamv, subg)


def _stack4(y, group):
    grp = _lane_group((1,) * (y.ndim - 1) + (y.shape[-1],), group)
    return jnp.concatenate([jnp.where(grp == g, y, 0.0) for g in range(N_HEADS)], axis=-2)


def _diag_sum(f, group):
    rr = f.shape[0] // N_HEADS
    grp = _lane_group((1, f.shape[1]), group)
    out = jnp.where(grp == 0, f[0:rr], 0.0)
    for g in range(1, N_HEADS):
        out = out + jnp.where(grp == g, f[g * rr:(g + 1) * rr], 0.0)
    return out


def _bmm(x, y, group):
    return _dot(x, _stack4(y, group))


def _bdot(x, w):
    return lax.dot_general(x, w, (((2,), (1,)), ((0,), (0,))), preferred_element_type=F32)


def _bdot_nt(x, w):
    return lax.dot_general(x, w, (((2,), (2,)), ((0,), (0,))), preferred_element_type=F32)


def _bbmm(x, y, group):
    return _bdot(x, _stack4(y, group))


def _split3(x):
    hi = x.astype(BF16)
    r1 = x - hi.astype(F32)
    mid = r1.astype(BF16)
    lo = (r1 - mid.astype(F32)).astype(BF16)
    return hi, mid, lo


def _unit_lower_inverse(a, c, tick):
    def bb(x, y, group):
        out = _bbmm(x, y, group)
        tick()
        return out

    shape = (1, c, N_HEADS * c)
    i = lax.broadcasted_iota(jnp.int32, shape, 1)
    j = lax.broadcasted_iota(jnp.int32, shape, 2) % c
    eye = jnp.where(i == j, 1.0, 0.0)
    blockdiag = (i // SOLVE_BLOCK) == (j // SOLVE_BLOCK)
    ad = jnp.where(blockdiag, a, 0.0)
    p = bb(ad, ad, c)
    rinv = eye - ad
    rinv = rinv + bb(rinv, p, c)
    for _ in range(2):
        p = bb(p, p, c)
        rinv = rinv + bb(rinv, p, c)
    if c <= SOLVE_BLOCK:
        return rinv
    mm = bb(rinv, a - ad, c)
    t = eye - mm
    p = bb(mm, mm, c)
    t = t + bb(t, p, c)
    nblk = c // SOLVE_BLOCK
    span = 4
    while span < nblk:
        p = bb(p, p, c)
        t = t + bb(t, p, c)
        span *= 2
    return bb(t, rinv, c)


def _to_square(x, c):
    if c == HEAD_W:
        return x
    return jnp.concatenate([x[..., HEAD_W * hh:HEAD_W * hh + c] for hh in range(N_HEADS)], axis=-1)


def _gdn_kernel(qkv_ref, gate_ref, gab_ref, buf_ref, s0_ref, cw_ref, alog_ref, dtb_ref, ng_ref,
                o_ref, conv_ref, s_ref, prev_scr, s_scr, u_scr, w_scr, qk_scr, qd_scr, kt_scr, gl_scr, *, tg, c, nj):
    t = pl.program_id(0)
    nc = tg // c
    slot = t % 2
    pslot = 1 - slot
    pre_scrs = (u_scr, w_scr, qk_scr, qd_scr, kt_scr, gl_scr)

    @pl.when(t == 0)
    def _():
        s_scr[...] = jnp.zeros_like(s_scr)
        for scr in pre_scrs:
            scr[1] = jnp.zeros(scr.shape[1:], F32)

    @pl.when(t % nj == 0)
    def _():
        prev_scr[...] = jnp.zeros_like(prev_scr)
        prev_scr[8 - (GDN_CONV - 1):8, :] = buf_ref[0]

    @pl.when((t > 0) & ((t - 1) % nj == 0))
    def _():
        for hh in range(N_HEADS):
            s_scr[:, HEAD_W * hh:HEAD_W * (hh + 1)] = s0_ref[0, hh]

    pre = [tuple(scr[pslot, n] for scr in pre_scrs) for n in range(nc)]
    s_old = s_scr[...]
    st = dict(s=s_old, n=0, ws=None, outs=[])

    def tick():
        if st["n"] >= nc:
            return
        u, w, qk, qd, kt, glast = pre[st["n"]]
        if st["ws"] is None:
            st["ws"] = _dot(jnp.concatenate([w, qd], axis=0), _stack4(st["s"], HEAD_W))
        else:
            ws = st["ws"]
            vnew = u - ws[0:c]
            st["outs"].append(ws[c:2 * c] + _bmm(qk, vnew, HEAD_W))
            st["s"] = st["s"] * glast + _diag_sum(_dot_tn(kt, vnew), HEAD_W)
            st["ws"] = None
            st["n"] += 1

    x = qkv_ref[...]
    prev8 = prev_scr[...]
    cw = cw_ref[...]
    y = x * cw[GDN_CONV - 1:GDN_CONV]
    for s in range(1, GDN_CONV):
        y = y + _shift_rows(x, prev8, s) * cw[GDN_CONV - 1 - s:GDN_CONV - s]
    prev_scr[...] = x[tg - 8:tg]
    conv_ref[0] = x[tg - (GDN_CONV - 1):tg]

    y = _silu(y)
    q = y[:, 0:256]
    k = y[:, 256:512]
    v = y[:, 512:768]
    gones = jnp.where(_lane_group((GROUP_W, GROUP_W), HEAD_W)
                      == lax.broadcasted_iota(jnp.int32, (GROUP_W, GROUP_W), 0) // HEAD_W,
                      1.0, 0.0).astype(BF16)

    def head_sum(z):
        z_hi, z_mid, z_lo = _split3(z)
        return _dot(z_hi, gones) + _dot(z_mid, gones) + _dot(z_lo, gones)

    q = q * lax.rsqrt(head_sum(q * q) + 1e-6) * (HEAD_W ** -0.5)
    tick()
    k = k * lax.rsqrt(head_sum(k * k) + 1e-6)
    tick()
    gab = gab_ref[...]
    za = gab[:, 0:256] + dtb_ref[...]
    softplus = jnp.maximum(za, 0.0) + jnp.log(1.0 + jnp.exp(-jnp.abs(za)))
    g = -jnp.exp(alog_ref[...]) * softplus
    beta = 1.0 / (1.0 + jnp.exp(-gab[:, 256:512]))

    sq = (1, c, N_HEADS * c)
    ri = lax.broadcasted_iota(jnp.int32, sq, 1)
    cj = lax.broadcasted_iota(jnp.int32, sq, 2) % c
    tr = lax.broadcasted_iota(jnp.int32, (tg, tg), 0)
    tc = lax.broadcasted_iota(jnp.int32, (tg, tg), 1)
    tri_b = jnp.where((tc <= tr) & (tc // c == tr // c), 1.0, 0.0).astype(BF16)
    g_hi, g_mid, g_lo = _split3(g)
    decay2 = _dot(tri_b, g_hi) + _dot(tri_b, g_mid) + _dot(tri_b, g_lo)
    tick()
    chunked = lambda z: z.reshape(nc, c, z.shape[-1])
    qb, kb_, vb, bb, decay = chunked(q), chunked(k), chunked(v), chunked(beta), chunked(decay2)
    dsq = _to_square(decay, c)
    drow = jnp.sum(jnp.where(ri == cj, dsq, 0.0), axis=1, keepdims=True)
    lm = jnp.where(cj <= ri, jnp.exp(jnp.where(cj <= ri, dsq - drow, 0.0)), 0.0)
    kbeta = kb_ * bb
    kq = _bdot_nt(jnp.concatenate([kbeta, qb], axis=1), _stack4(kb_, HEAD_W))
    tick()
    a_mat = jnp.where(cj < ri, kq[:, 0:c] * lm, 0.0)
    edec = jnp.exp(decay)
    tinv = _unit_lower_inverse(a_mat, c, tick)
    uw = _bdot(tinv, jnp.concatenate([_stack4(vb * bb, HEAD_W), _stack4(kbeta * edec, HEAD_W)], axis=2))
    tick()
    dlast = decay[:, c - 1:c]
    u_scr[slot] = uw[:, :, 0:GROUP_W]
    w_scr[slot] = uw[:, :, GROUP_W:2 * GROUP_W]
    qk_scr[slot] = kq[:, c:2 * c] * lm
    qd_scr[slot] = qb * edec
    kt_scr[slot] = kb_ * jnp.exp(dlast - decay)
    gl_scr[slot] = jnp.exp(dlast)

    while st["n"] < nc:
        tick()
    s_fin = jnp.where(t > 0, st["s"], s_old)
    s_scr[...] = s_fin
    for hh in range(N_HEADS):
        s_ref[0, hh] = s_fin[:, HEAD_W * hh:HEAD_W * (hh + 1)]
    outs = st["outs"]
    o = outs[0] if len(outs) == 1 else jnp.concatenate(outs, axis=0)
    o = o * lax.rsqrt(head_sum(o * o) * (1.0 / HEAD_W) + EPS) * ng_ref[...]
    o_ref[...] = (o * _silu(gate_ref[...])).astype(BF16)


def _gdn(gqkv, ggate, gab, buf, s0, cw, alog, dtb, ng, *, batch, seq, tg, c):
    nj = seq // tg
    nt = batch * nj
    nc = tg // c
    t = batch * seq
    cur_t = lambda i: jnp.minimum(i, nt - 1)
    prv_t = lambda i: jnp.maximum(i - 1, 0)
    cur = lambda wd: pl.BlockSpec((tg, wd), lambda i: (cur_t(i), 0))
    prv = lambda wd: pl.BlockSpec((tg, wd), lambda i: (prv_t(i), 0))
    pre_shape = lambda wd: pltpu.VMEM((2, nc, c, wd), F32)
    return pl.pallas_call(
        functools.partial(_gdn_kernel, tg=tg, c=c, nj=nj),
        grid=(nt + 1,),
        in_specs=[
            cur(768), prv(256), cur(512),
            pl.BlockSpec((1, GDN_CONV - 1, 768), lambda i: (cur_t(i) // nj, 0, 0)),
            pl.BlockSpec((1, N_HEADS, HEAD_W, HEAD_W), lambda i: (prv_t(i) // nj, 0, 0, 0)),
            _const_spec(cw.shape), _const_spec(alog.shape), _const_spec(dtb.shape), _const_spec(ng.shape),
        ],
        out_specs=[
            prv(256),
            pl.BlockSpec((1, GDN_CONV - 1, 768), lambda i: (cur_t(i) // nj, 0, 0)),
            pl.BlockSpec((1, N_HEADS, HEAD_W, HEAD_W), lambda i: (prv_t(i) // nj, 0, 0, 0)),
        ],
        out_shape=[
            jax.ShapeDtypeStruct((t, 256), BF16),
            jax.ShapeDtypeStruct((batch, GDN_CONV - 1, 768), F32),
            jax.ShapeDtypeStruct((batch, N_HEADS, HEAD_W, HEAD_W), F32),
        ],
        scratch_shapes=[pltpu.VMEM((8, 768), F32), pltpu.VMEM((HEAD_W, GROUP_W), F32),
                        pre_shape(GROUP_W), pre_shape(GROUP_W), pre_shape(N_HEADS * c), pre_shape(GROUP_W),
                        pre_shape(GROUP_W), pltpu.VMEM((2, nc, 1, GROUP_W), F32)],
        compiler_params=_cparams(("arbitrary",)), name="gdn",
    )(gqkv, ggate, gab, buf, s0, cw, alog, dtb, ng)


def _ffn_kernel(x_ref, om_ref, od_ref, os_ref, og_ref, buf_ref, wout_ref, g2_ref, wup_ref, cw_ref, wdn_ref,
                fg_ref, y_ref, conv_ref, prev_scr, act_scr, *, tm, d_ff, ft, final):
    j = pl.program_id(1)
    nj = pl.num_programs(1)

    @pl.when(j == 0)
    def _():
        prev_scr[...] = jnp.zeros_like(prev_scr)
        prev_scr[8 - (FFN_CONV - 1):8, :] = buf_ref[0]

    mixed = jnp.concatenate([om_ref[...], od_ref[...], os_ref[...], og_ref[...]], axis=1)
    x1 = x_ref[...] + _dot(mixed, wout_ref[...])
    h2 = (x1 * lax.rsqrt(jnp.mean(x1 * x1, axis=-1, keepdims=True) + EPS) * g2_ref[...]).astype(BF16)

    def conv_cols(off):
        a = _dot(h2, wup_ref[:, off:off + ft])
        prev8 = prev_scr[:, off:off + ft]
        cw = cw_ref[:, off:off + ft]
        y = a * cw[FFN_CONV - 1:FFN_CONV]
        for s in range(1, FFN_CONV):
            y = y + _shift_rows(a, prev8, s) * cw[FFN_CONV - 1 - s:FFN_CONV - s]
        prev_scr[:, off:off + ft] = a[tm - 8:tm]
        return y

    for fi in range(d_ff // ft):
        gate = conv_cols(fi * ft)
        up = conv_cols(d_ff + fi * ft)
        act_scr[:, fi * ft:(fi + 1) * ft] = (_silu(gate) * up).astype(BF16)
    acc = x1 + _dot(act_scr[...], wdn_ref[...])

    if final:
        acc = acc * lax.rsqrt(jnp.mean(acc * acc, axis=-1, keepdims=True) + EPS) * fg_ref[...]
    y_ref[...] = acc

    @pl.when(j == nj - 1)
    def _():
        conv_ref[0] = prev_scr[8 - (FFN_CONV - 1):8, :]


def _ffn(x, om, od, osg, og, buf, wout, g2, wup, cw, wdn, fg, *, batch, seq, tm, ft, final):
    nj = seq // tm
    t, d = x.shape
    d_ff = wdn.shape[0]
    row = lambda wd: pl.BlockSpec((tm, wd), lambda b, j: (b * nj + j, 0))
    return pl.pallas_call(
        functools.partial(_ffn_kernel, tm=tm, d_ff=d_ff, ft=ft, final=final),
        grid=(batch, nj),
        in_specs=[
            row(d), row(256), row(256), row(256), row(256),
            pl.BlockSpec((1, FFN_CONV - 1, 2 * d_ff), lambda b, j: (b, 0, 0)),
            _const_spec(wout.shape), _const_spec(g2.shape), _const_spec(wup.shape),
            _const_spec(cw.shape), _const_spec(wdn.shape), _const_spec(fg.shape),
        ],
        out_specs=[row(d), pl.BlockSpec((1, FFN_CONV - 1, 2 * d_ff), lambda b, j: (b, 0, 0))],
        out_shape=[jax.ShapeDtypeStruct((t, d), F32),
                   jax.ShapeDtypeStruct((batch, FFN_CONV - 1, 2 * d_ff), F32)],
        scratch_shapes=[pltpu.VMEM((8, 2 * d_ff), F32), pltpu.VMEM((tm, d_ff), BF16)],
        compiler_params=_cparams(("parallel", "arbitrary")), name="ffn",
    )(x, om, od, osg, og, buf, wout, g2, wup, cw, wdn, fg)


def _rot_cols(w):
    d = w.shape[0]
    wg = w.reshape(d, -1, 2, MLA_DR // 2)
    return jnp.concatenate([-wg[:, :, 1:2], wg[:, :, 0:1]], axis=2).reshape(d, -1)


def _permute_w_in(w_in):
    d = w_in.shape[0]
    pts = np.cumsum([384, 128, 32, 256, 256, 256, 256, 256, 768, 256, 4, 4])[:-1]
    mq, mlat, mkr, dq, dk, dv, su, sv, gqkv, ggate, ga, gb = jnp.split(w_in, pts, axis=1)
    mq = mq.reshape(d, N_HEADS, MLA_DN + MLA_DR)
    qn = mq[:, :, :MLA_DN].reshape(d, N_HEADS * MLA_DN)
    qr = mq[:, :, MLA_DN:].reshape(d, N_HEADS * MLA_DR)
    kr4 = jnp.tile(mkr, (1, N_HEADS))
    rep = lambda a: jnp.repeat(a, HEAD_W, axis=1)
    w = jnp.concatenate([qn, qr, _rot_cols(qr), mlat, kr4, _rot_cols(kr4),
                         dq, dk, dv, su, sv, gqkv, ggate, rep(ga), rep(gb)], axis=1)
    assert w.shape[1] == C_TOTAL
    wt = jnp.transpose(jnp.concatenate([qn, qr, _rot_cols(qr), mlat, dq, dv], axis=1))
    assert wt.shape[0] == WT_ROWS
    return w.astype(BF16), wt.astype(BF16)


def _rope_tables(pos):
    inv = ROPE_THETA ** (-jnp.arange(0, MLA_DR, 2, dtype=F32) / MLA_DR)
    ang = pos.astype(F32)[:, None] * inv[None, :]
    cos = jnp.tile(jnp.cos(ang), (1, 2 * N_HEADS))
    sin = jnp.tile(jnp.sin(ang), (1, 2 * N_HEADS))
    return cos, sin


def _layer_params(l, p):
    row = lambda a: a.reshape(1, -1)
    rep = lambda a: jnp.repeat(a, HEAD_W).reshape(1, -1)
    sgu_w = p['sgu_w'][l]
    w_perm, w_t = _permute_w_in(p['w_in'][l])
    return dict(
        w_in=w_perm, w_t=w_t,
        g1=row(p['norm1_g'][l]),
        wukt=jnp.transpose(p['mla_w_uk'][l], (1, 2, 0)).astype(BF16),
        wuk=jnp.transpose(p['mla_w_uk'][l], (1, 0, 2)).astype(BF16),
        wuv=jnp.transpose(p['mla_w_uv'][l], (1, 0, 2)).astype(BF16),
        wuvt=jnp.transpose(p['mla_w_uv'][l], (1, 2, 0)).astype(BF16),
        latg=row(p['mla_lat_g'][l]), latgt=p['mla_lat_g'][l].reshape(-1, 1),
        sgu_w=sgu_w,
        sgu_b=p['sgu_b'][l],
        lng=row(p['sgu_ln_g'][l]), lnb=row(p['sgu_ln_b'][l]),
        lamv=jnp.stack([p['diff_lam_q1'][l], p['diff_lam_k1'][l], p['diff_lam_q2'][l], p['diff_lam_k2'][l]]),
        subg=row(p['diff_sub_g'][l]),
        gcw=p['gdn_conv_w'][l],
        alog=rep(p['gdn_a_log'][l]), dtb=rep(p['gdn_dt_bias'][l]),
        ng=jnp.tile(p['gdn_norm_g'][l], N_HEADS).reshape(1, -1),
        wout=p['w_out'][l].astype(BF16),
        g2=row(p['norm2_g'][l]),
        wup=p['ffn_w_up'][l].astype(BF16),
        fcw=p['ffn_conv_w'][l],
        wdn=p['ffn_w_down'][l].astype(BF16),
        fg=row(p['final_g']),
    )


def _sgu_tables(lp, c):
    w = lp['sgu_w'][:, :c, :c].reshape(N_HEADS * c, c)
    b = jnp.repeat(jnp.transpose(lp['sgu_b'][:, :c]), HEAD_W, axis=1)
    return w, b


def _run_inproj(x, lp, cos, sin, stacked=(), *, tm, sgu_c, prompt, layer=0):
    sw, sb = _sgu_tables(lp, sgu_c)
    return _inproj(x, lp['g1'], lp['w_in'], lp['w_t'], cos, sin, jnp.transpose(cos), jnp.transpose(sin),
                   lp['wukt'], lp['wuk'], lp['latg'], lp['latgt'], sw, sb, lp['lng'], lp['lnb'], tuple(stacked),
                   tm=tm, sgu_c=sgu_c, prompt=prompt, layer=layer)


def kernel(x_prompt, x_sample, cache_mla_latent, cache_mla_krope, cache_diff_k, cache_diff_v, state_gdn_conv, state_gdn_s, state_ffn_conv, t5_table, final_g, norm1_g, w_in, mla_lat_g, mla_w_uk, mla_w_uv, diff_lam_q1, diff_lam_k1, diff_lam_q2, diff_lam_k2, diff_sub_g, sgu_ln_g, sgu_ln_b, sgu_w, sgu_b, gdn_conv_w, gdn_a_log, gdn_dt_bias, gdn_norm_g, w_out, norm2_g, ffn_w_up, ffn_conv_w, ffn_w_down):
    p = dict(final_g=final_g, norm1_g=norm1_g, w_in=w_in, mla_lat_g=mla_lat_g, mla_w_uk=mla_w_uk,
             mla_w_uv=mla_w_uv, diff_lam_q1=diff_lam_q1, diff_lam_k1=diff_lam_k1, diff_lam_q2=diff_lam_q2,
             diff_lam_k2=diff_lam_k2, diff_sub_g=diff_sub_g, sgu_ln_g=sgu_ln_g, sgu_ln_b=sgu_ln_b,
             sgu_w=sgu_w, sgu_b=sgu_b, gdn_conv_w=gdn_conv_w, gdn_a_log=gdn_a_log, gdn_dt_bias=gdn_dt_bias,
             gdn_norm_g=gdn_norm_g, w_out=w_out, norm2_g=norm2_g, ffn_w_up=ffn_w_up, ffn_conv_w=ffn_conv_w,
             ffn_w_down=ffn_w_down)
    depth = w_in.shape[0]
    bp, sp, d = x_prompt.shape
    bs, ls, _ = x_sample.shape
    past = cache_mla_latent.shape[2]
    d_ff = ffn_w_down.shape[1]
    assert past % CHUNK == 0 and ls <= CHUNK

    tm_p = min(512, sp)
    tq = min(512, sp)
    tg_p = min(512, sp)
    ft = 256
    sgu_cp = min(SGU_CHUNK, sp)
    gdn_cp = min(GDN_CHUNK, sp)

    cos_p, sin_p = _rope_tables(jnp.arange(sp, dtype=jnp.int32))
    pos_s = past + jnp.arange(ls, dtype=jnp.int32)
    cos_s, sin_s = _rope_tables(jnp.tile(pos_s, bs))

    assert tq >= DIFF_CORNER and tm_p == tq
    stack_t = lambda b: jnp.transpose(jnp.repeat(b, 2, axis=0), (2, 0, 1)).reshape(b.shape[2], -1)
    bias_p = stack_t(_t5_bias(t5_table, tq, tq, (0,), (0,))[0])
    corner = _t5_bias(t5_table, DIFF_CORNER, DIFF_CORNER, (DIFF_CORNER,), (0,))[0]
    bias_pc = stack_t(jnp.pad(corner, ((0, 0), (0, tq - DIFF_CORNER), (0, 0))))
    bias_sc = _t5_bias(t5_table, ls, past, (past,), (0,))
    bias_sn = _t5_bias(t5_table, ls, ls, (past,), (past,))

    xp = x_prompt.reshape(bp * sp, d)
    xs = x_sample.reshape(bs * ls, d)
    zeros_gconv = jnp.zeros((bp, GDN_CONV - 1, 3 * GROUP_W), F32)
    zeros_gs = jnp.zeros((bp, N_HEADS, HEAD_W, HEAD_W), F32)
    zeros_fconv = jnp.zeros((bp, FFN_CONV - 1, 2 * d_ff), F32)

    stacked = tuple(jnp.zeros((depth, bp * sp, wd), F32) for wd in (MLA_R, MLA_DR, GROUP_W, GROUP_W))

    outs_p, outs_s = [], []
    for l in range(depth):
        lp = _layer_params(l, p)
        lam_init = 0.8 - 0.6 * math.exp(-0.3 * l)
        final = l == depth - 1

        ip = _run_inproj(xp, lp, cos_p, sin_p, stacked, tm=tm_p, sgu_c=sgu_cp, prompt=True, layer=l)
        stacked = tuple(ip[k] for k in _INPROJ_STACKED)
        osgu, gqkv, ggate, gab = ip['osgu'], ip['gqkv'], ip['ggate'], ip['gab']
        o_mla = _mla_prompt(ip['qmt'], ip['km'], ip['vmt'], lp['wuvt'], batch=bp, seq=sp, tq=tq)
        o_diff = _diff_prompt(ip['dqt'], ip['kd'], ip['vdt'], bias_p, bias_pc, lp['lamv'],
                              jnp.tile(lp['subg'], (1, N_HEADS)).reshape(-1, 1),
                              batch=bp, seq=sp, tq=tq, lam_init=lam_init)
        o_gdn, gconv, gs = _gdn(gqkv, ggate, gab, zeros_gconv, zeros_gs, lp['gcw'], lp['alog'], lp['dtb'],
                                lp['ng'], batch=bp, seq=sp, tg=tg_p, c=gdn_cp)
        xp, fconv = _ffn(xp, o_mla, o_diff, osgu, o_gdn, zeros_fconv, lp['wout'], lp['g2'], lp['wup'],
                         lp['fcw'], lp['wdn'], lp['fg'], batch=bp, seq=sp, tm=tm_p, ft=ft, final=final)
        outs_p.append(dict(gconv=gconv, gs=gs, fconv=fconv))

        ip = _run_inproj(xs, lp, cos_s, sin_s, tm=bs * ls, sgu_c=min(SGU_CHUNK, ls), prompt=False)
        qm, km, vm, lat, kr, dq, dk, dv = (ip[k] for k in ('qm', 'km', 'vm', 'lat', 'kr', 'dq', 'dk', 'dv'))
        osgu, vn, gqkv, ggate, gab = (ip[k] for k in ('osgu', 'vn', 'gqkv', 'ggate', 'gab'))
        kc_m = jnp.concatenate([cache_mla_latent[l], jnp.tile(cache_mla_krope[l], (1, 1, N_HEADS))],
                               axis=-1).astype(BF16)
        o_mla = _mla_sample(qm, kc_m, km, vm, lp['wuv'], batch=bs, lq=ls)
        kc_d = cache_diff_k[l].reshape(bs, past, GROUP_W).astype(BF16)
        vc_d = cache_diff_v[l].reshape(bs, past, GROUP_W).astype(BF16)
        o_diff = _diff_sample(dq, kc_d, vc_d, dk, dv, bias_sc, bias_sn, lp['lamv'], lp['subg'],
                              batch=bs, lq=ls, lam_init=lam_init)
        o_gdn, gconv, gs = _gdn(gqkv, ggate, gab, state_gdn_conv[l], state_gdn_s[l], lp['gcw'], lp['alog'],
                                lp['dtb'], lp['ng'], batch=bs, seq=ls, tg=ls, c=min(GDN_CHUNK, ls))
        xs, fconv = _ffn(xs, o_mla, o_diff, osgu, o_gdn, state_ffn_conv[l], lp['wout'], lp['g2'], lp['wup'],
                         lp['fcw'], lp['wdn'], lp['fg'], batch=bs, seq=ls, tm=ls, ft=ft, final=final)
        outs_s.append(dict(
            lat=lat.reshape(bs, ls, MLA_R), kr=kr.reshape(bs, ls, MLA_DR),
            dk=dk.reshape(bs, ls, N_HEADS, 2, DIFF_DH), dv=dv.reshape(bs, ls, N_HEADS, 2 * DIFF_DH),
            gconv=gconv, gs=gs, fconv=fconv, sv=vn.reshape(bs, ls, GROUP_W)))

    st = lambda lst, key: jnp.stack([o[key] for o in lst])
    lat_p, kr_p, dk_p, dv_p = stacked
    return (xp.reshape(bp, sp, d), xs.reshape(bs, ls, d),
            lat_p.reshape(depth, bp, sp, MLA_R), kr_p.reshape(depth, bp, sp, MLA_DR),
            dk_p.reshape(depth, bp, sp, N_HEADS, 2, DIFF_DH), dv_p.reshape(depth, bp, sp, N_HEADS, 2 * DIFF_DH),
            st(outs_p, 'gconv'), st(outs_p, 'gs'), st(outs_p, 'fconv'),
            st(outs_s, 'lat'), st(outs_s, 'kr'), st(outs_s, 'dk'), st(outs_s, 'dv'),
            st(outs_s, 'gconv'), st(outs_s, 'gs'), st(outs_s, 'fconv'), st(outs_s, 'sv'))
```

```python
import functools
import math

import numpy as np
import jax
import jax.numpy as jnp
from jax import lax
from jax.experimental import pallas as pl
from jax.experimental.pallas import tpu as pltpu

F32 = jnp.float32
BF16 = jnp.bfloat16

CHUNK = 64
EPS = 1e-6
N_HEADS = 4
HEAD_W = 64
GROUP_W = 256
MLA_DN, MLA_DR, MLA_R = 64, 32, 128
DIFF_DH = 32
ROPE_THETA = 10000.0
SGU_CHUNK = 128
GDN_CHUNK = 64
GDN_CONV = 4
FFN_CONV = 3
T5_BUCKETS = 32
T5_MAX_DIST = 128
NEG = -1e30
LOG2E = math.log2(math.e)
SOLVE_BLOCK = 16
DIFF_CORNER = 128
MLA_VT_ROWS = 144
VT_ROWS = 80

VMEM_LIMIT_BYTES = 56 * 1024 * 1024

C_QN, C_QR, C_QRR, C_LAT, C_KR, C_KRR = 0, 256, 384, 512, 640, 768
C_DQ, C_DK, C_DV, C_SU, C_SV = 896, 1152, 1408, 1664, 1920
C_GQKV, C_GG, C_GA, C_GB = 2176, 2944, 3200, 3456
C_TOTAL = 3712
WT_DQ, WT_ROWS = 640, 1152
N_INPROJ_INPUTS = 16


def _cparams(sem):
    return pltpu.CompilerParams(dimension_semantics=sem, vmem_limit_bytes=VMEM_LIMIT_BYTES)


def _const_spec(shape):
    nd = len(shape)
    return pl.BlockSpec(shape, lambda *_: (0,) * nd, pipeline_mode=pl.Buffered(1))


def _dot(a, b):
    return jnp.dot(a, b, preferred_element_type=F32)


def _dot_nt(a, b):
    return lax.dot_general(a, b, (((1,), (1,)), ((), ())), preferred_element_type=F32)


def _dot_tn(a, b):
    return lax.dot_general(a, b, (((0,), (0,)), ((), ())), preferred_element_type=F32)


def _lane_group(shape, group):
    return lax.broadcasted_iota(jnp.int32, shape, len(shape) - 1) // group


def _silu(x):
    return x * (1.0 / (1.0 + jnp.exp(-x)))


def _shift_rows(x, prev8, s):
    if s == 0:
        return x
    r = pltpu.roll(x, s, axis=0)
    rp = pltpu.roll(prev8, s, axis=0)
    row8 = lax.broadcasted_iota(jnp.int32, rp.shape, 0)
    top = jnp.where(row8 < s, rp, r[0:8])
    if x.shape[0] == 8:
        return top
    return jnp.concatenate([top, r[8:]], axis=0)


_INPROJ_COMMON = ("km", "lat", "kr", "dk", "dv", "kd", "osgu", "vn", "gqkv", "ggate", "gab")
_INPROJ_PROMPT = ("qmt", "vmt", "dqt", "vdt")
_INPROJ_SAMPLE = ("qm", "vm", "dq")
_INPROJ_STACKED = ("lat", "kr", "dk", "dv")


def _inproj_kernel(x_ref, g1_ref, w_ref, wt_ref, cos_ref, sin_ref, cost_ref, sint_ref, wukt_ref, wuk_ref,
                   latg_ref, latgt_ref, sguw_ref, sgub_ref, lng_ref, lnb_ref, *rest, sgu_c, prompt):
    out_refs = rest[len(_INPROJ_STACKED):] if prompt else rest
    o = dict(zip(_INPROJ_COMMON + (_INPROJ_PROMPT if prompt else _INPROJ_SAMPLE), out_refs))
    leaf = (lambda k: o[k].at[0]) if prompt else (lambda k: o[k])
    tm = x_ref.shape[0]
    x = x_ref[...]
    h = (x * lax.rsqrt(jnp.mean(x * x, axis=-1, keepdims=True) + EPS) * g1_ref[...]).astype(BF16)

    def sec(off, n):
        return _dot(h, w_ref[:, off:off + n])

    cos = cos_ref[...]
    sin = sin_ref[...]
    mla_scale = (MLA_DN + MLA_DR) ** -0.5 * LOG2E
    diff_scale = DIFF_DH ** -0.5 * LOG2E

    zl = sec(C_LAT, 384)
    zlat = zl[:, 0:128]
    lat = zlat * lax.rsqrt(jnp.mean(zlat * zlat, axis=-1, keepdims=True) + EPS) * latg_ref[...]
    kr4 = zl[:, 128:256] * cos + zl[:, 256:384] * sin
    leaf("lat")[...] = lat
    leaf("kr")[...] = kr4[:, 0:MLA_DR]
    lat_b = lat.astype(BF16)
    o["km"][:, 0:128] = lat_b
    o["km"][:, 128:256] = kr4.astype(BF16)

    zd = sec(C_DK, 512)
    dk = zd[:, 0:256]
    dv = zd[:, 256:512]
    leaf("dk")[...] = dk
    leaf("dv")[...] = dv
    o["kd"][...] = dk.astype(BF16)

    if prompt:
        zt = _dot_nt(wt_ref[0:WT_DQ, :], h)
        qrt = (zt[256:384] * cost_ref[...] + zt[384:512] * sint_ref[...]) * mla_scale
        head_of_row = lax.broadcasted_iota(jnp.int32, (128, 1), 0) // MLA_DR
        for hh in range(N_HEADS):
            qn = (zt[MLA_DN * hh:MLA_DN * (hh + 1)] * mla_scale).astype(BF16)
            o["qmt"][hh, 0:128, :] = _dot(wuk_ref[hh], qn).astype(BF16)
            o["qmt"][hh, 128:256, :] = jnp.where(head_of_row == hh, qrt, 0.0).astype(BF16)
        zlt = zt[512:640]
        latt = zlt * lax.rsqrt(jnp.mean(zlt * zlt, axis=0, keepdims=True) + EPS) * latgt_ref[...]
        o["vmt"][0] = jnp.concatenate([latt, jnp.ones((MLA_VT_ROWS - MLA_R, tm), F32)], axis=0).astype(BF16)
        zt2 = _dot_nt(wt_ref[WT_DQ:WT_ROWS, :], h)
        o["dqt"][...] = (zt2[0:GROUP_W] * diff_scale).astype(BF16)
        ones_rows = jnp.ones((VT_ROWS - HEAD_W, tm), F32)
        for hh in range(N_HEADS):
            vt = zt2[GROUP_W + HEAD_W * hh:GROUP_W + HEAD_W * (hh + 1)]
            o["vdt"][hh, 0] = jnp.concatenate([vt, ones_rows], axis=0).astype(BF16)
    else:
        zq = sec(C_QN, 512)
        qr = (zq[:, 256:384] * cos + zq[:, 384:512] * sin) * mla_scale
        head_of_lane = _lane_group((1, 128), MLA_DR)
        for hh in range(N_HEADS):
            qn = (zq[:, MLA_DN * hh:MLA_DN * (hh + 1)] * mla_scale).astype(BF16)
            o["qm"][hh, :, 0:128] = _dot(qn, wukt_ref[hh]).astype(BF16)
            o["qm"][hh, :, 128:256] = jnp.where(head_of_lane == hh, qr, 0.0).astype(BF16)
        o["vm"][:, 0:128] = lat_b
        o["vm"][:, 128:256] = jnp.ones((tm, 128), BF16)
        o["dq"][...] = (sec(C_DQ, 256) * diff_scale).astype(BF16)

    zs = sec(C_SU, 512)
    su = zs[:, 0:256]
    sv = zs[:, 256:512]
    mu = jnp.mean(sv, axis=-1, keepdims=True)
    svc = sv - mu
    var = jnp.mean(svc * svc, axis=-1, keepdims=True)
    vn = svc * lax.rsqrt(var + EPS) * lng_ref[...] + lnb_ref[...]
    o["vn"][...] = vn
    c = sgu_c
    wr = lax.broadcasted_iota(jnp.int32, (N_HEADS * c, c), 0) % c
    wc = lax.broadcasted_iota(jnp.int32, (N_HEADS * c, c), 1)
    w4 = jnp.where(wc <= wr, sguw_ref[...], 0.0).astype(BF16)
    hl = _lane_group((1, GROUP_W), HEAD_W)
    sgub = sgub_ref[...]
    for ci in range(tm // c):
        rows = slice(ci * c, (ci + 1) * c)
        m4 = _dot(w4, vn[rows].astype(BF16))
        mix = jnp.where(hl == 0, m4[0:c], 0.0)
        for hh in range(1, N_HEADS):
            mix = mix + jnp.where(hl == hh, m4[hh * c:(hh + 1) * c], 0.0)
        o["osgu"][rows, :] = (su[rows] * (mix + sgub)).astype(BF16)

    zg = sec(C_GQKV, 1536)
    o["gqkv"][...] = zg[:, 0:768]
    o["ggate"][...] = zg[:, 768:1024]
    o["gab"][...] = zg[:, 1024:1536]


def _inproj(x, g1, w, wt, cos, sin, cost, sint, wukt, wuk, latg, latgt, sguw, sgub, lng, lnb, stacked, *,
            tm, sgu_c, prompt, layer):
    t, d = x.shape
    n = t // tm
    npos = cos.shape[0] // tm
    row = lambda wd: pl.BlockSpec((tm, wd), lambda i: (i, 0))
    shapes = dict(
        km=((t, 256), BF16, row(256)), lat=((t, MLA_R), F32, row(MLA_R)), kr=((t, MLA_DR), F32, row(MLA_DR)),
        dk=((t, 256), F32, row(256)), dv=((t, 256), F32, row(256)), kd=((t, 256), BF16, row(256)),
        osgu=((t, 256), BF16, row(256)), vn=((t, 256), F32, row(256)), gqkv=((t, 768), F32, row(768)),
        ggate=((t, 256), F32, row(256)), gab=((t, 512), F32, row(512)),
        qmt=((N_HEADS, 256, t), BF16, pl.BlockSpec((N_HEADS, 256, tm), lambda i: (0, 0, i))),
        vmt=((n, MLA_VT_ROWS, tm), BF16, pl.BlockSpec((1, MLA_VT_ROWS, tm), lambda i: (i, 0, 0))),
        dqt=((GROUP_W, t), BF16, pl.BlockSpec((GROUP_W, tm), lambda i: (0, i))),
        vdt=((N_HEADS, n, VT_ROWS, tm), BF16, pl.BlockSpec((N_HEADS, 1, VT_ROWS, tm), lambda i: (0, i, 0, 0))),
        qm=((N_HEADS, t, 256), BF16, pl.BlockSpec((N_HEADS, tm, 256), lambda i: (0, i, 0))),
        vm=((t, 256), BF16, row(256)), dq=((t, 256), BF16, row(256)),
    )
    names = _INPROJ_COMMON + (_INPROJ_PROMPT if prompt else _INPROJ_SAMPLE)
    aliases = {}
    if prompt:
        for idx, k in enumerate(_INPROJ_STACKED):
            depth, wd = stacked[idx].shape[0], shapes[k][0][1]
            shapes[k] = ((depth, t, wd), F32, pl.BlockSpec((1, tm, wd), lambda i: (layer, i, 0)))
            aliases[N_INPROJ_INPUTS + idx] = names.index(k)
    pos_row = pl.BlockSpec((tm, 128), lambda i: (i % npos, 0))
    pos_col = pl.BlockSpec((128, tm), lambda i: (0, i % npos))
    consts = (g1, w, wt)
    in_specs = ([row(d)] + [_const_spec(a.shape) for a in consts] + [pos_row, pos_row, pos_col, pos_col]
                + [_const_spec(a.shape) for a in (wukt, wuk, latg, latgt, sguw, sgub, lng, lnb)]
                + [pl.BlockSpec(memory_space=pl.ANY) for _ in stacked])
    assert len(in_specs) == N_INPROJ_INPUTS + len(stacked)
    outs = pl.pallas_call(
        functools.partial(_inproj_kernel, sgu_c=sgu_c, prompt=prompt),
        grid=(n,), in_specs=in_specs,
        out_specs=[shapes[k][2] for k in names],
        out_shape=[jax.ShapeDtypeStruct(shapes[k][0], shapes[k][1]) for k in names],
        input_output_aliases=aliases,
        compiler_params=_cparams(("parallel",)), name="inproj",
    )(x, g1, w, wt, cos, sin, cost, sint, wukt, wuk, latg, latgt, sguw, sgub, lng, lnb, *stacked)
    return dict(zip(names, outs))


def _t5_thresholds():
    nb = T5_BUCKETS // 2
    max_exact = nb // 2
    ratio = T5_MAX_DIST // max_exact
    thr = []
    for j in range(1, nb - max_exact):
        n = max_exact
        while n ** (nb - max_exact) < (ratio ** j) * (max_exact ** (nb - max_exact)):
            n += 1
        thr.append(n)
    return nb, max_exact, thr


def _bias_kernel(t_ref, o_ref, *, q0s, k0s):
    nb, max_exact, thr = _t5_thresholds()
    tq, tk = o_ref.shape[2], o_ref.shape[3]
    row = lax.broadcasted_iota(jnp.int32, (tq, tk), 0)
    col = lax.broadcasted_iota(jnp.int32, (tq, tk), 1)
    for di, (q0, k0) in enumerate(zip(q0s, k0s)):
        qpos = row + q0
        kpos = col + k0
        rel = kpos - qpos
        n = jnp.abs(rel)
        visible = (kpos // CHUNK) <= (qpos // CHUNK)
        for hh in range(N_HEADS):
            def side(base):
                val = jnp.full((tq, tk), t_ref[base + nb - 1, hh], F32)
                for j in range(len(thr) - 1, -1, -1):
                    val = jnp.where(n < thr[j], t_ref[base + max_exact + j, hh], val)
                for e in range(max_exact - 1, -1, -1):
                    val = jnp.where(n == e, t_ref[base + e, hh], val)
                return val
            b = jnp.where(rel > 0, side(nb), side(0)) - t_ref[nb - 1, hh]
            o_ref[di, hh] = jnp.where(visible, b * LOG2E, NEG)


def _t5_bias(t5_table, tq, tk, q0s, k0s):
    nd = len(q0s)
    return pl.pallas_call(
        functools.partial(_bias_kernel, q0s=tuple(q0s), k0s=tuple(k0s)),
        in_specs=[pl.BlockSpec(memory_space=pltpu.SMEM)],
        out_shape=jax.ShapeDtypeStruct((nd, N_HEADS, tq, tk), F32),
        name="t5_bias",
    )(t5_table)


def _mla_prompt_kernel(qt_ref, k_ref, vt_ref, wuvt_ref, o_ref, *, tq):
    qi = pl.program_id(1)
    qlim = ((qi * tq + lax.broadcasted_iota(jnp.int32, (1, tq), 1)) // CHUNK + 1) * CHUNK
    krow = lax.broadcasted_iota(jnp.int32, (tq, 1), 0)

    def scores_of(kb):
        kblk = k_ref[0, pl.ds(pl.multiple_of(kb * tq, tq), tq), :]
        return [_dot(kblk, qt_ref[hh]) for hh in range(N_HEADS)]

    def step(kb, carry, masked, scores=None):
        k0 = kb * tq
        scores = scores_of(kb) if scores is None else scores
        ps, mns, alphas = [], [], []
        for hh in range(N_HEADS):
            m = carry[2 * hh]
            s = scores[hh]
            if masked:
                s = jnp.where(krow + k0 < qlim, s, NEG)
            mn = jnp.maximum(m, jnp.max(s, axis=0, keepdims=True))
            alphas.append(jnp.exp2(m - mn))
            ps.append(jnp.exp2(s - mn).astype(BF16))
            mns.append(mn)
        out = []
        for hh in range(N_HEADS):
            out += [mns[hh], alphas[hh] * carry[2 * hh + 1] + _dot(vt_ref[kb], ps[hh])]
        return tuple(out)

    carry = (jnp.full((1, tq), NEG, F32), jnp.zeros((MLA_VT_ROWS, tq), F32)) * N_HEADS
    def two_steps(i, carry):
        sa, sb = scores_of(2 * i), scores_of(2 * i + 1)
        return step(2 * i + 1, step(2 * i, carry, False, sa), False, sb)

    carry = lax.fori_loop(0, qi // 2, two_steps, carry)
    carry = lax.fori_loop(2 * (qi // 2), qi, functools.partial(step, masked=False), carry)
    carry = step(qi, carry, True)
    outs = []
    for hh in range(N_HEADS):
        acc = carry[2 * hh + 1]
        on = (acc[0:MLA_R] * (1.0 / acc[MLA_R:MLA_R + 1])).astype(BF16)
        outs.append(_dot(wuvt_ref[hh], on))
    o_ref[...] = jnp.transpose(jnp.concatenate(outs, axis=0)).astype(BF16)


def _mla_prompt(qmt, km, vmt, wuvt, *, batch, seq, tq):
    assert seq % tq == 0 and vmt.shape[2] == tq and tq % CHUNK == 0
    nq = seq // tq
    t = batch * seq
    return pl.pallas_call(
        functools.partial(_mla_prompt_kernel, tq=tq),
        grid=(batch, nq),
        in_specs=[
            pl.BlockSpec((N_HEADS, 256, tq), lambda b, i: (0, 0, b * nq + i)),
            pl.BlockSpec((1, seq, 256), lambda b, i: (b, 0, 0)),
            pl.BlockSpec((nq, MLA_VT_ROWS, tq), lambda b, i: (b, 0, 0)),
            _const_spec(wuvt.shape),
        ],
        out_specs=pl.BlockSpec((tq, 256), lambda b, i: (b * nq + i, 0)),
        out_shape=jax.ShapeDtypeStruct((t, 256), BF16),
        compiler_params=_cparams(("parallel", "arbitrary")), name="mla_prompt",
    )(qmt, km.reshape(batch, seq, 256), vmt, wuvt)


def _diff_lambda(lam_ref, lam_init):
    l = lam_ref[...]
    a = jnp.sum(l[0:1] * l[1:2], axis=-1, keepdims=True)
    b = jnp.sum(l[2:3] * l[3:4], axis=-1, keepdims=True)
    return jnp.exp(a) - jnp.exp(b) + lam_init


def _stack_q8(q):
    grp = _lane_group((1, GROUP_W), DIFF_DH)
    return jnp.concatenate([jnp.where(grp == g, q, jnp.zeros_like(q)) for g in range(2 * N_HEADS)], axis=0)


def _diff_finish(on, lam, subg, lam_init, tq):
    outs = []
    for hh in range(N_HEADS):
        o = on[(2 * hh) * tq:(2 * hh + 1) * tq] - lam * on[(2 * hh + 1) * tq:(2 * hh + 2) * tq]
        o = o * lax.rsqrt(jnp.mean(o * o, axis=-1, keepdims=True) + 1e-5) * subg
        outs.append(o * (1.0 - lam_init))
    return jnp.concatenate(outs, axis=1)


def _diff_prompt_kernel(qt_ref, k_ref, vt_ref, bias_ref, cbias_ref, lam_ref, subg_ref, o_ref, *, tq, lam_init):
    qi = pl.program_id(1)
    rg = 2 * tq
    qt = qt_ref[...]
    fgrp = lax.broadcasted_iota(jnp.int32, (GROUP_W, 1), 0) // DIFF_DH
    qts = [jnp.concatenate([jnp.where(fgrp == 2 * hh + c, qt, jnp.zeros_like(qt)) for c in range(2)], axis=1)
           for hh in range(N_HEADS)]

    def scores_of(kb):
        kblk = k_ref[0, pl.ds(pl.multiple_of(kb * tq, tq), tq), :]
        return [_dot(kblk, qts[hh]) for hh in range(N_HEADS)]

    def step(kb, carry, bias, scores=None):
        scores = scores_of(kb) if scores is None else scores
        ps, mns, alphas = [], [], []
        for hh in range(N_HEADS):
            s = scores[hh]
            cols = slice(hh * rg, (hh + 1) * rg)
            if bias == "diag":
                s = s + bias_ref[:, cols]
            elif bias == "corner":
                s = jnp.concatenate([s[0:tq - DIFF_CORNER], s[tq - DIFF_CORNER:tq] + cbias_ref[:, cols]], axis=0)
            m = carry[2 * hh]
            mn = jnp.maximum(m, jnp.max(s, axis=0, keepdims=True))
            alphas.append(jnp.exp2(m - mn))
            ps.append(jnp.exp2(s - mn).astype(BF16))
            mns.append(mn)
        out = []
        for hh in range(N_HEADS):
            out += [mns[hh], alphas[hh] * carry[2 * hh + 1] + _dot(vt_ref[hh, kb], ps[hh])]
        return tuple(out)

    carry = (jnp.full((1, rg), NEG, F32), jnp.zeros((VT_ROWS, rg), F32)) * N_HEADS
    nfar = jnp.maximum(qi - 1, 0)

    def two_steps(i, carry):
        sa, sb = scores_of(2 * i), scores_of(2 * i + 1)
        return step(2 * i + 1, step(2 * i, carry, None, sa), None, sb)

    carry = lax.fori_loop(0, nfar // 2, two_steps, carry)
    carry = lax.fori_loop(2 * (nfar // 2), nfar, functools.partial(step, bias=None), carry)
    carry = lax.fori_loop(nfar, qi, functools.partial(step, bias="corner"), carry)
    carry = step(qi, carry, "diag")
    lam = _diff_lambda(lam_ref, lam_init)
    outs = []
    for hh in range(N_HEADS):
        acc = carry[2 * hh + 1]
        on = acc[0:HEAD_W] * (1.0 / acc[HEAD_W:HEAD_W + 1])
        o = on[:, 0:tq] - lam * on[:, tq:rg]
        outs.append(o * lax.rsqrt(jnp.mean(o * o, axis=0, keepdims=True) + 1e-5))
    ot = jnp.concatenate(outs, axis=0) * (subg_ref[...] * (1.0 - lam_init))
    o_ref[...] = jnp.transpose(ot).astype(BF16)


def _diff_prompt(dqt, kd, vdt, bias, cbias, lamv, subg, *, batch, seq, tq, lam_init):
    nq = seq // tq
    t = batch * seq
    assert vdt.shape[3] == tq
    return pl.pallas_call(
        functools.partial(_diff_prompt_kernel, tq=tq, lam_init=lam_init),
        grid=(batch, nq),
        in_specs=[
            pl.BlockSpec((GROUP_W, tq), lambda b, i: (0, b * nq + i)),
            pl.BlockSpec((1, seq, 256), lambda b, i: (b, 0, 0)),
            pl.BlockSpec((N_HEADS, nq, VT_ROWS, tq), lambda b, i: (0, b, 0, 0)),
            _const_spec(bias.shape), _const_spec(cbias.shape), _const_spec(lamv.shape), _const_spec(subg.shape),
        ],
        out_specs=pl.BlockSpec((tq, 256), lambda b, i: (b * nq + i, 0)),
        out_shape=jax.ShapeDtypeStruct((t, 256), BF16),
        compiler_params=_cparams(("parallel", "arbitrary")), name="diff_prompt",
    )(dqt, kd.reshape(batch, seq, 256), vdt, bias, cbias, lamv, subg)


def _mla_sample_kernel(q_ref, kc_ref, kn_ref, vn_ref, wuv_ref, o_ref, *, lq):
    r = N_HEADS * lq
    q = q_ref[...].reshape(r, 256)
    kc = kc_ref[0]
    sc = _dot_nt(q, kc)
    sn = _dot_nt(q, kn_ref[...])
    m = jnp.maximum(jnp.max(sc, axis=1, keepdims=True), jnp.max(sn, axis=1, keepdims=True))
    pc = jnp.exp2(sc - m)
    pn = jnp.exp2(sn - m)
    l = jnp.sum(pc, axis=1, keepdims=True) + jnp.sum(pn, axis=1, keepdims=True)
    pv = _dot(pc.astype(BF16), kc[:, 0:128]) + _dot(pn.astype(BF16), vn_ref[:, 0:128])
    on = (pv / l).astype(BF16)
    o_ref[...] = jnp.concatenate(
        [_dot(on[hh * lq:(hh + 1) * lq], wuv_ref[hh]) for hh in range(N_HEADS)], axis=1).astype(BF16)


def _mla_sample(qm, kc, km, vm, wuv, *, batch, lq):
    past = kc.shape[1]
    return pl.pallas_call(
        functools.partial(_mla_sample_kernel, lq=lq),
        grid=(batch,),
        in_specs=[
            pl.BlockSpec((N_HEADS, lq, 256), lambda b: (0, b, 0)),
            pl.BlockSpec((1, past, 256), lambda b: (b, 0, 0)),
            pl.BlockSpec((lq, 256), lambda b: (b, 0)),
            pl.BlockSpec((lq, 256), lambda b: (b, 0)),
            _const_spec(wuv.shape),
        ],
        out_specs=pl.BlockSpec((lq, 256), lambda b: (b, 0)),
        out_shape=jax.ShapeDtypeStruct((batch * lq, 256), BF16),
        compiler_params=_cparams(("parallel",)), name="mla_sample",
    )(qm, kc, km, vm, wuv)


def _diff_sample_kernel(q_ref, kc_ref, vc_ref, kn_ref, vn_ref, bc_ref, bn_ref, lam_ref, subg_ref, o_ref,
                        *, lq, lam_init):
    q8 = _stack_q8(q_ref[...])
    bc = jnp.concatenate([bc_ref[0, hh // 2] for hh in range(2 * N_HEADS)], axis=0)
    bn = jnp.concatenate([bn_ref[0, hh // 2] for hh in range(2 * N_HEADS)], axis=0)
    sc = _dot_nt(q8, kc_ref[0]) + bc
    sn = _dot_nt(q8, kn_ref[...].astype(BF16)) + bn
    m = jnp.maximum(jnp.max(sc, axis=1, keepdims=True), jnp.max(sn, axis=1, keepdims=True))
    pc = jnp.exp2(sc - m)
    pn = jnp.exp2(sn - m)
    l = jnp.sum(pc, axis=1, keepdims=True) + jnp.sum(pn, axis=1, keepdims=True)
    pv = (_dot(pc.astype(BF16), vc_ref[0]) + _dot(pn.astype(BF16), vn_ref[...].astype(BF16))) / l
    on = jnp.concatenate(
        [pv[g * lq:(g + 1) * lq, HEAD_W * (g // 2):HEAD_W * (g // 2 + 1)] for g in range(2 * N_HEADS)], axis=0)
    lam = _diff_lambda(lam_ref, lam_init)
    o_ref[...] = _diff_finish(on, lam, subg_ref[...], lam_init, lq).astype(BF16)


def _diff_sample(dq, kc, vc, dk, dv, bias_c, bias_n, lamv, subg, *, batch, lq, lam_init):
    past = kc.shape[1]
    return pl.pallas_call(
        functools.partial(_diff_sample_kernel, lq=lq, lam_init=lam_init),
        grid=(batch,),
        in_specs=[
            pl.BlockSpec((lq, 256), lambda b: (b, 0)),
            pl.BlockSpec((1, past, 256), lambda b: (b, 0, 0)),
            pl.BlockSpec((1, past, 256), lambda b: (b, 0, 0)),
            pl.BlockSpec((lq, 256), lambda b: (b, 0)),
            pl.BlockSpec((lq, 256), lambda b: (b, 0)),
            _const_spec(bias_c.shape), _const_spec(bias_n.shape),
            _const_spec(lamv.shape), _const_spec(subg.shape),
        ],
        out_specs=pl.BlockSpec((lq, 256), lambda b: (b, 0)),
        out_shape=jax.ShapeDtypeStruct((batch * lq, 256), BF16),
        compiler_params=_cparams(("parallel",)), name="diff_sample",
    )(dq, kc, vc, dk, dv, bias_c, bias_n, lamv, subg)


def _stack4(y, group):
    grp = _lane_group((1,) * (y.ndim - 1) + (y.shape[-1],), group)
    return jnp.concatenate([jnp.where(grp == g, y, 0.0) for g in range(N_HEADS)], axis=-2)


def _diag_sum(f, group):
    rr = f.shape[0] // N_HEADS
    grp = _lane_group((1, f.shape[1]), group)
    out = jnp.where(grp == 0, f[0:rr], 0.0)
    for g in range(1, N_HEADS):
        out = out + jnp.where(grp == g, f[g * rr:(g + 1) * rr], 0.0)
    return out


def _bmm(x, y, group):
    return _dot(x, _stack4(y, group))


def _bdot(x, w):
    return lax.dot_general(x, w, (((2,), (1,)), ((0,), (0,))), preferred_element_type=F32)


def _bdot_nt(x, w):
    return lax.dot_general(x, w, (((2,), (2,)), ((0,), (0,))), preferred_element_type=F32)


def _bbmm(x, y, group):
    return _bdot(x, _stack4(y, group))


def _split3(x):
    hi = x.astype(BF16)
    r1 = x - hi.astype(F32)
    mid = r1.astype(BF16)
    lo = (r1 - mid.astype(F32)).astype(BF16)
    return hi, mid, lo


def _unit_lower_inverse(a, c, tick):
    def bb(x, y, group):
        out = _bbmm(x, y, group)
        tick()
        return out

    shape = (1, c, N_HEADS * c)
    i = lax.broadcasted_iota(jnp.int32, shape, 1)
    j = lax.broadcasted_iota(jnp.int32, shape, 2) % c
    eye = jnp.where(i == j, 1.0, 0.0)
    blockdiag = (i // SOLVE_BLOCK) == (j // SOLVE_BLOCK)
    ad = jnp.where(blockdiag, a, 0.0)
    p = bb(ad, ad, c)
    rinv = eye - ad
    rinv = rinv + bb(rinv, p, c)
    for _ in range(2):
        p = bb(p, p, c)
        rinv = rinv + bb(rinv, p, c)
    if c <= SOLVE_BLOCK:
        return rinv
    mm = bb(rinv, a - ad, c)
    t = eye - mm
    p = bb(mm, mm, c)
    t = t + bb(t, p, c)
    nblk = c // SOLVE_BLOCK
    span = 4
    while span < nblk:
        p = bb(p, p, c)
        t = t + bb(t, p, c)
        span *= 2
    return bb(t, rinv, c)


def _to_square(x, c):
    if c == HEAD_W:
        return x
    return jnp.concatenate([x[..., HEAD_W * hh:HEAD_W * hh + c] for hh in range(N_HEADS)], axis=-1)


def _gdn_kernel(qkv_ref, gate_ref, gab_ref, buf_ref, s0_ref, cw_ref, alog_ref, dtb_ref, ng_ref,
                o_ref, conv_ref, s_ref, prev_scr, s_scr, u_scr, w_scr, qk_scr, qd_scr, kt_scr, gl_scr, *, tg, c, nj):
    t = pl.program_id(0)
    nc = tg // c
    slot = t % 2
    pslot = 1 - slot
    pre_scrs = (u_scr, w_scr, qk_scr, qd_scr, kt_scr, gl_scr)

    @pl.when(t == 0)
    def _():
        s_scr[...] = jnp.zeros_like(s_scr)
        for scr in pre_scrs:
            scr[1] = jnp.zeros(scr.shape[1:], F32)

    @pl.when(t % nj == 0)
    def _():
        prev_scr[...] = jnp.zeros_like(prev_scr)
        prev_scr[8 - (GDN_CONV - 1):8, :] = buf_ref[0]

    @pl.when((t > 0) & ((t - 1) % nj == 0))
    def _():
        for hh in range(N_HEADS):
            s_scr[:, HEAD_W * hh:HEAD_W * (hh + 1)] = s0_ref[0, hh]

    pre = [tuple(scr[pslot, n] for scr in pre_scrs) for n in range(nc)]
    s_old = s_scr[...]
    st = dict(s=s_old, n=0, ws=None, outs=[])

    def tick():
        if st["n"] >= nc:
            return
        u, w, qk, qd, kt, glast = pre[st["n"]]
        if st["ws"] is None:
            st["ws"] = _dot(jnp.concatenate([w, qd], axis=0), _stack4(st["s"], HEAD_W))
        else:
            ws = st["ws"]
            vnew = u - ws[0:c]
            st["outs"].append(ws[c:2 * c] + _bmm(qk, vnew, HEAD_W))
            st["s"] = st["s"] * glast + _diag_sum(_dot_tn(kt, vnew), HEAD_W)
            st["ws"] = None
            st["n"] += 1

    x = qkv_ref[...]
    prev8 = prev_scr[...]
    cw = cw_ref[...]
    y = x * cw[GDN_CONV - 1:GDN_CONV]
    for s in range(1, GDN_CONV):
        y = y + _shift_rows(x, prev8, s) * cw[GDN_CONV - 1 - s:GDN_CONV - s]
    prev_scr[...] = x[tg - 8:tg]
    conv_ref[0] = x[tg - (GDN_CONV - 1):tg]

    y = _silu(y)
    q = y[:, 0:256]
    k = y[:, 256:512]
    v = y[:, 512:768]
    gones = jnp.where(_lane_group((GROUP_W, GROUP_W), HEAD_W)
                      == lax.broadcasted_iota(jnp.int32, (GROUP_W, GROUP_W), 0) // HEAD_W,
                      1.0, 0.0).astype(BF16)

    def head_sum(z):
        z_hi, z_mid, z_lo = _split3(z)
        return _dot(z_hi, gones) + _dot(z_mid, gones) + _dot(z_lo, gones)

    q = q * lax.rsqrt(head_sum(q * q) + 1e-6) * (HEAD_W ** -0.5)
    tick()
    k = k * lax.rsqrt(head_sum(k * k) + 1e-6)
    tick()
    gab = gab_ref[...]
    za = gab[:, 0:256] + dtb_ref[...]
    softplus = jnp.maximum(za, 0.0) + jnp.log(1.0 + jnp.exp(-jnp.abs(za)))
    g = -jnp.exp(alog_ref[...]) * softplus
    beta = 1.0 / (1.0 + jnp.exp(-gab[:, 256:512]))

    sq = (1, c, N_HEADS * c)
    ri = lax.broadcasted_iota(jnp.int32, sq, 1)
    cj = lax.broadcasted_iota(jnp.int32, sq, 2) % c
    tr = lax.broadcasted_iota(jnp.int32, (tg, tg), 0)
    tc = lax.broadcasted_iota(jnp.int32, (tg, tg), 1)
    tri_b = jnp.where((tc <= tr) & (tc // c == tr // c), 1.0, 0.0).astype(BF16)
    g_hi, g_mid, g_lo = _split3(g)
    decay2 = _dot(tri_b, g_hi) + _dot(tri_b, g_mid) + _dot(tri_b, g_lo)
    tick()
    chunked = lambda z: z.reshape(nc, c, z.shape[-1])
    qb, kb_, vb, bb, decay = chunked(q), chunked(k), chunked(v), chunked(beta), chunked(decay2)
    dsq = _to_square(decay, c)
    drow = jnp.sum(jnp.where(ri == cj, dsq, 0.0), axis=1, keepdims=True)
    lm = jnp.where(cj <= ri, jnp.exp(jnp.where(cj <= ri, dsq - drow, 0.0)), 0.0)
    kbeta = kb_ * bb
    kq = _bdot_nt(jnp.concatenate([kbeta, qb], axis=1), _stack4(kb_, HEAD_W))
    tick()
    a_mat = jnp.where(cj < ri, kq[:, 0:c] * lm, 0.0)
    edec = jnp.exp(decay)
    tinv = _unit_lower_inverse(a_mat, c, tick)
    uw = _bdot(tinv, jnp.concatenate([_stack4(vb * bb, HEAD_W), _stack4(kbeta * edec, HEAD_W)], axis=2))
    tick()
    dlast = decay[:, c - 1:c]
    u_scr[slot] = uw[:, :, 0:GROUP_W]
    w_scr[slot] = uw[:, :, GROUP_W:2 * GROUP_W]
    qk_scr[slot] = kq[:, c:2 * c] * lm
    qd_scr[slot] = qb * edec
    kt_scr[slot] = kb_ * jnp.exp(dlast - decay)
    gl_scr[slot] = jnp.exp(dlast)

    while st["n"] < nc:
        tick()
    s_fin = jnp.where(t > 0, st["s"], s_old)
    s_scr[...] = s_fin
    for hh in range(N_HEADS):
        s_ref[0, hh] = s_fin[:, HEAD_W * hh:HEAD_W * (hh + 1)]
    outs = st["outs"]
    o = outs[0] if len(outs) == 1 else jnp.concatenate(outs, axis=0)
    o = o * lax.rsqrt(head_sum(o * o) * (1.0 / HEAD_W) + EPS) * ng_ref[...]
    o_ref[...] = (o * _silu(gate_ref[...])).astype(BF16)


def _gdn(gqkv, ggate, gab, buf, s0, cw, alog, dtb, ng, *, batch, seq, tg, c):
    nj = seq // tg
    nt = batch * nj
    nc = tg // c
    t = batch * seq
    cur_t = lambda i: jnp.minimum(i, nt - 1)
    prv_t = lambda i: jnp.maximum(i - 1, 0)
    cur = lambda wd: pl.BlockSpec((tg, wd), lambda i: (cur_t(i), 0))
    prv = lambda wd: pl.BlockSpec((tg, wd), lambda i: (prv_t(i), 0))
    pre_shape = lambda wd: pltpu.VMEM((2, nc, c, wd), F32)
    return pl.pallas_call(
        functools.partial(_gdn_kernel, tg=tg, c=c, nj=nj),
        grid=(nt + 1,),
        in_specs=[
            cur(768), prv(256), cur(512),
            pl.BlockSpec((1, GDN_CONV - 1, 768), lambda i: (cur_t(i) // nj, 0, 0)),
            pl.BlockSpec((1, N_HEADS, HEAD_W, HEAD_W), lambda i: (prv_t(i) // nj, 0, 0, 0)),
            _const_spec(cw.shape), _const_spec(alog.shape), _const_spec(dtb.shape), _const_spec(ng.shape),
        ],
        out_specs=[
            prv(256),
            pl.BlockSpec((1, GDN_CONV - 1, 768), lambda i: (cur_t(i) // nj, 0, 0)),
            pl.BlockSpec((1, N_HEADS, HEAD_W, HEAD_W), lambda i: (prv_t(i) // nj, 0, 0, 0)),
        ],
        out_shape=[
            jax.ShapeDtypeStruct((t, 256), BF16),
            jax.ShapeDtypeStruct((batch, GDN_CONV - 1, 768), F32),
            jax.ShapeDtypeStruct((batch, N_HEADS, HEAD_W, HEAD_W), F32),
        ],
        scratch_shapes=[pltpu.VMEM((8, 768), F32), pltpu.VMEM((HEAD_W, GROUP_W), F32),
                        pre_shape(GROUP_W), pre_shape(GROUP_W), pre_shape(N_HEADS * c), pre_shape(GROUP_W),
                        pre_shape(GROUP_W), pltpu.VMEM((2, nc, 1, GROUP_W), F32)],
        compiler_params=_cparams(("arbitrary",)), name="gdn",
    )(gqkv, ggate, gab, buf, s0, cw, alog, dtb, ng)


def _ffn_kernel(x_ref, om_ref, od_ref, os_ref, og_ref, buf_ref, wout_ref, g2_ref, wup_ref, cw_ref, wdn_ref,
                fg_ref, y_ref, conv_ref, prev_scr, act_scr, *, tm, d_ff, ft, final):
    j = pl.program_id(1)
    nj = pl.num_programs(1)

    @pl.when(j == 0)
    def _():
        prev_scr[...] = jnp.zeros_like(prev_scr)
        prev_scr[8 - (FFN_CONV - 1):8, :] = buf_ref[0]

    mixed = jnp.concatenate([om_ref[...], od_ref[...], os_ref[...], og_ref[...]], axis=1)
    x1 = x_ref[...] + _dot(mixed, wout_ref[...])
    h2 = (x1 * lax.rsqrt(jnp.mean(x1 * x1, axis=-1, keepdims=True) + EPS) * g2_ref[...]).astype(BF16)

    def conv_cols(off):
        a = _dot(h2, wup_ref[:, off:off + ft])
        prev8 = prev_scr[:, off:off + ft]
        cw = cw_ref[:, off:off + ft]
        y = a * cw[FFN_CONV - 1:FFN_CONV]
        for s in range(1, FFN_CONV):
            y = y + _shift_rows(a, prev8, s) * cw[FFN_CONV - 1 - s:FFN_CONV - s]
        prev_scr[:, off:off + ft] = a[tm - 8:tm]
        return y

    for fi in range(d_ff // ft):
        gate = conv_cols(fi * ft)
        up = conv_cols(d_ff + fi * ft)
        act_scr[:, fi * ft:(fi + 1) * ft] = (_silu(gate) * up).astype(BF16)
    acc = x1 + _dot(act_scr[...], wdn_ref[...])

    if final:
        acc = acc * lax.rsqrt(jnp.mean(acc * acc, axis=-1, keepdims=True) + EPS) * fg_ref[...]
    y_ref[...] = acc

    @pl.when(j == nj - 1)
    def _():
        conv_ref[0] = prev_scr[8 - (FFN_CONV - 1):8, :]


def _ffn(x, om, od, osg, og, buf, wout, g2, wup, cw, wdn, fg, *, batch, seq, tm, ft, final):
    nj = seq // tm
    t, d = x.shape
    d_ff = wdn.shape[0]
    row = lambda wd: pl.BlockSpec((tm, wd), lambda b, j: (b * nj + j, 0))
    return pl.pallas_call(
        functools.partial(_ffn_kernel, tm=tm, d_ff=d_ff, ft=ft, final=final),
        grid=(batch, nj),
        in_specs=[
            row(d), row(256), row(256), row(256), row(256),
            pl.BlockSpec((1, FFN_CONV - 1, 2 * d_ff), lambda b, j: (b, 0, 0)),
            _const_spec(wout.shape), _const_spec(g2.shape), _const_spec(wup.shape),
            _const_spec(cw.shape), _const_spec(wdn.shape), _const_spec(fg.shape),
        ],
        out_specs=[row(d), pl.BlockSpec((1, FFN_CONV - 1, 2 * d_ff), lambda b, j: (b, 0, 0))],
        out_shape=[jax.ShapeDtypeStruct((t, d), F32),
                   jax.ShapeDtypeStruct((batch, FFN_CONV - 1, 2 * d_ff), F32)],
        scratch_shapes=[pltpu.VMEM((8, 2 * d_ff), F32), pltpu.VMEM((tm, d_ff), BF16)],
        compiler_params=_cparams(("parallel", "arbitrary")), name="ffn",
    )(x, om, od, osg, og, buf, wout, g2, wup, cw, wdn, fg)


def _rot_cols(w):
    d = w.shape[0]
    wg = w.reshape(d, -1, 2, MLA_DR // 2)
    return jnp.concatenate([-wg[:, :, 1:2], wg[:, :, 0:1]], axis=2).reshape(d, -1)


def _permute_w_in(w_in):
    d = w_in.shape[0]
    pts = np.cumsum([384, 128, 32, 256, 256, 256, 256, 256, 768, 256, 4, 4])[:-1]
    mq, mlat, mkr, dq, dk, dv, su, sv, gqkv, ggate, ga, gb = jnp.split(w_in, pts, axis=1)
    mq = mq.reshape(d, N_HEADS, MLA_DN + MLA_DR)
    qn = mq[:, :, :MLA_DN].reshape(d, N_HEADS * MLA_DN)
    qr = mq[:, :, MLA_DN:].reshape(d, N_HEADS * MLA_DR)
    kr4 = jnp.tile(mkr, (1, N_HEADS))
    rep = lambda a: jnp.repeat(a, HEAD_W, axis=1)
    w = jnp.concatenate([qn, qr, _rot_cols(qr), mlat, kr4, _rot_cols(kr4),
                         dq, dk, dv, su, sv, gqkv, ggate, rep(ga), rep(gb)], axis=1)
    assert w.shape[1] == C_TOTAL
    wt = jnp.transpose(jnp.concatenate([qn, qr, _rot_cols(qr), mlat, dq, dv], axis=1))
    assert wt.shape[0] == WT_ROWS
    return w.astype(BF16), wt.astype(BF16)


def _rope_tables(pos):
    inv = ROPE_THETA ** (-jnp.arange(0, MLA_DR, 2, dtype=F32) / MLA_DR)
    ang = pos.astype(F32)[:, None] * inv[None, :]
    cos = jnp.tile(jnp.cos(ang), (1, 2 * N_HEADS))
    sin = jnp.tile(jnp.sin(ang), (1, 2 * N_HEADS))
    return cos, sin


def _layer_params(l, p):
    row = lambda a: a.reshape(1, -1)
    rep = lambda a: jnp.repeat(a, HEAD_W).reshape(1, -1)
    sgu_w = p['sgu_w'][l]
    w_perm, w_t = _permute_w_in(p['w_in'][l])
    return dict(
        w_in=w_perm, w_t=w_t,
        g1=row(p['norm1_g'][l]),
        wukt=jnp.transpose(p['mla_w_uk'][l], (1, 2, 0)).astype(BF16),
        wuk=jnp.transpose(p['mla_w_uk'][l], (1, 0, 2)).astype(BF16),
        wuv=jnp.transpose(p['mla_w_uv'][l], (1, 0, 2)).astype(BF16),
        wuvt=jnp.transpose(p['mla_w_uv'][l], (1, 2, 0)).astype(BF16),
        latg=row(p['mla_lat_g'][l]), latgt=p['mla_lat_g'][l].reshape(-1, 1),
        sgu_w=sgu_w,
        sgu_b=p['sgu_b'][l],
        lng=row(p['sgu_ln_g'][l]), lnb=row(p['sgu_ln_b'][l]),
        lamv=jnp.stack([p['diff_lam_q1'][l], p['diff_lam_k1'][l], p['diff_lam_q2'][l], p['diff_lam_k2'][l]]),
        subg=row(p['diff_sub_g'][l]),
        gcw=p['gdn_conv_w'][l],
        alog=rep(p['gdn_a_log'][l]), dtb=rep(p['gdn_dt_bias'][l]),
        ng=jnp.tile(p['gdn_norm_g'][l], N_HEADS).reshape(1, -1),
        wout=p['w_out'][l].astype(BF16),
        g2=row(p['norm2_g'][l]),
        wup=p['ffn_w_up'][l].astype(BF16),
        fcw=p['ffn_conv_w'][l],
        wdn=p['ffn_w_down'][l].astype(BF16),
        fg=row(p['final_g']),
    )


def _sgu_tables(lp, c):
    w = lp['sgu_w'][:, :c, :c].reshape(N_HEADS * c, c)
    b = jnp.repeat(jnp.transpose(lp['sgu_b'][:, :c]), HEAD_W, axis=1)
    return w, b


def _run_inproj(x, lp, cos, sin, stacked=(), *, tm, sgu_c, prompt, layer=0):
    sw, sb = _sgu_tables(lp, sgu_c)
    return _inproj(x, lp['g1'], lp['w_in'], lp['w_t'], cos, sin, jnp.transpose(cos), jnp.transpose(sin),
                   lp['wukt'], lp['wuk'], lp['latg'], lp['latgt'], sw, sb, lp['lng'], lp['lnb'], tuple(stacked),
                   tm=tm, sgu_c=sgu_c, prompt=prompt, layer=layer)


def kernel(x_prompt, x_sample, cache_mla_latent, cache_mla_krope, cache_diff_k, cache_diff_v, state_gdn_conv, state_gdn_s, state_ffn_conv, t5_table, final_g, norm1_g, w_in, mla_lat_g, mla_w_uk, mla_w_uv, diff_lam_q1, diff_lam_k1, diff_lam_q2, diff_lam_k2, diff_sub_g, sgu_ln_g, sgu_ln_b, sgu_w, sgu_b, gdn_conv_w, gdn_a_log, gdn_dt_bias, gdn_norm_g, w_out, norm2_g, ffn_w_up, ffn_conv_w, ffn_w_down):
    p = dict(final_g=final_g, norm1_g=norm1_g, w_in=w_in, mla_lat_g=mla_lat_g, mla_w_uk=mla_w_uk,
             mla_w_uv=mla_w_uv, diff_lam_q1=diff_lam_q1, diff_lam_k1=diff_lam_k1, diff_lam_q2=diff_lam_q2,
             diff_lam_k2=diff_lam_k2, diff_sub_g=diff_sub_g, sgu_ln_g=sgu_ln_g, sgu_ln_b=sgu_ln_b,
             sgu_w=sgu_w, sgu_b=sgu_b, gdn_conv_w=gdn_conv_w, gdn_a_log=gdn_a_log, gdn_dt_bias=gdn_dt_bias,
             gdn_norm_g=gdn_norm_g, w_out=w_out, norm2_g=norm2_g, ffn_w_up=ffn_w_up, ffn_conv_w=ffn_conv_w,
             ffn_w_down=ffn_w_down)
    depth = w_in.shape[0]
    bp, sp, d = x_prompt.shape
    bs, ls, _ = x_sample.shape
    past = cache_mla_latent.shape[2]
    d_ff = ffn_w_down.shape[1]
    assert past % CHUNK == 0 and ls <= CHUNK

    tm_p = min(512, sp)
    tq = min(512, sp)
    tg_p = min(512, sp)
    ft = 256
    sgu_cp = min(SGU_CHUNK, sp)
    gdn_cp = min(GDN_CHUNK, sp)

    cos_p, sin_p = _rope_tables(jnp.arange(sp, dtype=jnp.int32))
    pos_s = past + jnp.arange(ls, dtype=jnp.int32)
    cos_s, sin_s = _rope_tables(jnp.tile(pos_s, bs))

    assert tq >= DIFF_CORNER and tm_p == tq
    stack_t = lambda b: jnp.transpose(jnp.repeat(b, 2, axis=0), (2, 0, 1)).reshape(b.shape[2], -1)
    bias_p = stack_t(_t5_bias(t5_table, tq, tq, (0,), (0,))[0])
    corner = _t5_bias(t5_table, DIFF_CORNER, DIFF_CORNER, (DIFF_CORNER,), (0,))[0]
    bias_pc = stack_t(jnp.pad(corner, ((0, 0), (0, tq - DIFF_CORNER), (0, 0))))
    bias_sc = _t5_bias(t5_table, ls, past, (past,), (0,))
    bias_sn = _t5_bias(t5_table, ls, ls, (past,), (past,))

    xp = x_prompt.reshape(bp * sp, d)
    xs = x_sample.reshape(bs * ls, d)
    zeros_gconv = jnp.zeros((bp, GDN_CONV - 1, 3 * GROUP_W), F32)
    zeros_gs = jnp.zeros((bp, N_HEADS, HEAD_W, HEAD_W), F32)
    zeros_fconv = jnp.zeros((bp, FFN_CONV - 1, 2 * d_ff), F32)

    stacked = tuple(jnp.zeros((depth, bp * sp, wd), F32) for wd in (MLA_R, MLA_DR, GROUP_W, GROUP_W))

    outs_p, outs_s = [], []
    for l in range(depth):
        lp = _layer_params(l, p)
        lam_init = 0.8 - 0.6 * math.exp(-0.3 * l)
        final = l == depth - 1

        ip = _run_inproj(xp, lp, cos_p, sin_p, stacked, tm=tm_p, sgu_c=sgu_cp, prompt=True, layer=l)
        stacked = tuple(ip[k] for k in _INPROJ_STACKED)
        osgu, gqkv, ggate, gab = ip['osgu'], ip['gqkv'], ip['ggate'], ip['gab']
        o_mla = _mla_prompt(ip['qmt'], ip['km'], ip['vmt'], lp['wuvt'], batch=bp, seq=sp, tq=tq)
        o_diff = _diff_prompt(ip['dqt'], ip['kd'], ip['vdt'], bias_p, bias_pc, lp['lamv'],
                              jnp.tile(lp['subg'], (1, N_HEADS)).reshape(-1, 1),
                              batch=bp, seq=sp, tq=tq, lam_init=lam_init)
        o_gdn, gconv, gs = _gdn(gqkv, ggate, gab, zeros_gconv, zeros_gs, lp['gcw'], lp['alog'], lp['dtb'],
                                lp['ng'], batch=bp, seq=sp, tg=tg_p, c=gdn_cp)
        xp, fconv = _ffn(xp, o_mla, o_diff, osgu, o_gdn, zeros_fconv, lp['wout'], lp['g2'], lp['wup'],
                         lp['fcw'], lp['wdn'], lp['fg'], batch=bp, seq=sp, tm=tm_p, ft=ft, final=final)
        outs_p.append(dict(gconv=gconv, gs=gs, fconv=fconv))

        ip = _run_inproj(xs, lp, cos_s, sin_s, tm=bs * ls, sgu_c=min(SGU_CHUNK, ls), prompt=False)
        qm, km, vm, lat, kr, dq, dk, dv = (ip[k] for k in ('qm', 'km', 'vm', 'lat', 'kr', 'dq', 'dk', 'dv'))
        osgu, vn, gqkv, ggate, gab = (ip[k] for k in ('osgu', 'vn', 'gqkv', 'ggate', 'gab'))
        kc_m = jnp.concatenate([cache_mla_latent[l], jnp.tile(cache_mla_krope[l], (1, 1, N_HEADS))],
                               axis=-1).astype(BF16)
        o_mla = _mla_sample(qm, kc_m, km, vm, lp['wuv'], batch=bs, lq=ls)
        kc_d = cache_diff_k[l].reshape(bs, past, GROUP_W).astype(BF16)
        vc_d = cache_diff_v[l].reshape(bs, past, GROUP_W).astype(BF16)
        o_diff = _diff_sample(dq, kc_d, vc_d, dk, dv, bias_sc, bias_sn, lp['lamv'], lp['subg'],
                              batch=bs, lq=ls, lam_init=lam_init)
        o_gdn, gconv, gs = _gdn(gqkv, ggate, gab, state_gdn_conv[l], state_gdn_s[l], lp['gcw'], lp['alog'],
                                lp['dtb'], lp['ng'], batch=bs, seq=ls, tg=ls, c=min(GDN_CHUNK, ls))
        xs, fconv = _ffn(xs, o_mla, o_diff, osgu, o_gdn, state_ffn_conv[l], lp['wout'], lp['g2'], lp['wup'],
                         lp['fcw'], lp['wdn'], lp['fg'], batch=bs, seq=ls, tm=ls, ft=ft, final=final)
        outs_s.append(dict(
            lat=lat.reshape(bs, ls, MLA_R), kr=kr.reshape(bs, ls, MLA_DR),
            dk=dk.reshape(bs, ls, N_HEADS, 2, DIFF_DH), dv=dv.reshape(bs, ls, N_HEADS, 2 * DIFF_DH),
            gconv=gconv, gs=gs, fconv=fconv, sv=vn.reshape(bs, ls, GROUP_W)))

    st = lambda lst, key: jnp.stack([o[key] for o in lst])
    lat_p, kr_p, dk_p, dv_p = stacked
    return (xp.reshape(bp, sp, d), xs.reshape(bs, ls, d),
            lat_p.reshape(depth, bp, sp, MLA_R), kr_p.reshape(depth, bp, sp, MLA_DR),
            dk_p.reshape(depth, bp, sp, N_HEADS, 2, DIFF_DH), dv_p.reshape(depth, bp, sp, N_HEADS, 2 * DIFF_DH),
            st(outs_p, 'gconv'), st(outs_p, 'gs'), st(outs_p, 'fconv'),
            st(outs_s, 'lat'), st(outs_s, 'kr'), st(outs_s, 'dk'), st(outs_s, 'dv'),
            st(outs_s, 'gconv'), st(outs_s, 'gs'), st(outs_s, 'fconv'), st(outs_s, 'sv'))
```

```python
import functools
import math

import numpy as np
import jax
import jax.numpy as jnp
from jax import lax
from jax.experimental import pallas as pl
from jax.experimental.pallas import tpu as pltpu

F32 = jnp.float32
BF16 = jnp.bfloat16

CHUNK = 64
EPS = 1e-6
N_HEADS = 4
HEAD_W = 64
GROUP_W = 256
MLA_DN, MLA_DR, MLA_R = 64, 32, 128
DIFF_DH = 32
ROPE_THETA = 10000.0
SGU_CHUNK = 128
GDN_CHUNK = 64
GDN_CONV = 4
FFN_CONV = 3
T5_BUCKETS = 32
T5_MAX_DIST = 128
NEG = -1e30
LOG2E = math.log2(math.e)
SOLVE_BLOCK = 16
DIFF_CORNER = 128
MLA_VT_ROWS = 144
VT_ROWS = 80

VMEM_LIMIT_BYTES = 56 * 1024 * 1024

C_QN, C_QR, C_QRR, C_LAT, C_KR, C_KRR = 0, 256, 384, 512, 640, 768
C_DQ, C_DK, C_DV, C_SU, C_SV = 896, 1152, 1408, 1664, 1920
C_GQKV, C_GG, C_GA, C_GB = 2176, 2944, 3200, 3456
C_TOTAL = 3712
WT_DQ, WT_ROWS = 640, 1472
N_INPROJ_INPUTS = 16


def _cparams(sem):
    return pltpu.CompilerParams(dimension_semantics=sem, vmem_limit_bytes=VMEM_LIMIT_BYTES)


def _const_spec(shape):
    nd = len(shape)
    return pl.BlockSpec(shape, lambda *_: (0,) * nd, pipeline_mode=pl.Buffered(1))


def _dot(a, b):
    return jnp.dot(a, b, preferred_element_type=F32)


def _dot_nt(a, b):
    return lax.dot_general(a, b, (((1,), (1,)), ((), ())), preferred_element_type=F32)


def _dot_tn(a, b):
    return lax.dot_general(a, b, (((0,), (0,)), ((), ())), preferred_element_type=F32)


def _lane_group(shape, group):
    return lax.broadcasted_iota(jnp.int32, shape, len(shape) - 1) // group


def _silu(x):
    return x * (1.0 / (1.0 + jnp.exp(-x)))


def _shift_rows(x, prev8, s):
    if s == 0:
        return x
    r = pltpu.roll(x, s, axis=0)
    rp = pltpu.roll(prev8, s, axis=0)
    row8 = lax.broadcasted_iota(jnp.int32, rp.shape, 0)
    top = jnp.where(row8 < s, rp, r[0:8])
    if x.shape[0] == 8:
        return top
    return jnp.concatenate([top, r[8:]], axis=0)


_INPROJ_COMMON = ("km", "kd", "osgu", "vn", "gqkv", "ggate", "gab")
_INPROJ_STACKED = ("lat", "krt", "dkt", "dvt")
_INPROJ_PROMPT = _INPROJ_STACKED + ("qmt", "vmt", "dqt", "vdt")
_INPROJ_SAMPLE = ("lat", "kr", "dk", "dv", "qm", "vm", "dq")


def _inproj_kernel(x_ref, g1_ref, w_ref, wt_ref, cos_ref, sin_ref, cost_ref, sint_ref, wukt_ref, wuk_ref,
                   latg_ref, latgt_ref, sguw_ref, sgub_ref, lng_ref, lnb_ref, *rest, sgu_c, prompt):
    out_refs = rest[len(_INPROJ_STACKED):] if prompt else rest
    o = dict(zip(_INPROJ_COMMON + (_INPROJ_PROMPT if prompt else _INPROJ_SAMPLE), out_refs))
    tm = x_ref.shape[0]
    x = x_ref[...]
    h = (x * lax.rsqrt(jnp.mean(x * x, axis=-1, keepdims=True) + EPS) * g1_ref[...]).astype(BF16)

    def sec(off, n):
        return _dot(h, w_ref[:, off:off + n])

    cos = cos_ref[...]
    sin = sin_ref[...]
    mla_scale = (MLA_DN + MLA_DR) ** -0.5 * LOG2E
    diff_scale = DIFF_DH ** -0.5 * LOG2E

    zl = sec(C_LAT, 384)
    zlat = zl[:, 0:128]
    lat = zlat * lax.rsqrt(jnp.mean(zlat * zlat, axis=-1, keepdims=True) + EPS) * latg_ref[...]
    kr4 = zl[:, 128:256] * cos + zl[:, 256:384] * sin
    lat_b = lat.astype(BF16)
    o["km"][:, 0:128] = lat_b
    o["km"][:, 128:256] = kr4.astype(BF16)
    dk = sec(C_DK, 256)
    o["kd"][...] = dk.astype(BF16)

    if prompt:
        zt = _dot_nt(wt_ref[0:WT_DQ, :], h)
        qrt = (zt[256:384] * cost_ref[...] + zt[384:512] * sint_ref[...]) * mla_scale
        head_of_row = lax.broadcasted_iota(jnp.int32, (128, 1), 0) // MLA_DR
        for hh in range(N_HEADS):
            qn = (zt[MLA_DN * hh:MLA_DN * (hh + 1)] * mla_scale).astype(BF16)
            o["qmt"][hh, 0:128, :] = _dot(wuk_ref[hh], qn).astype(BF16)
            o["qmt"][hh, 128:256, :] = jnp.where(head_of_row == hh, qrt, 0.0).astype(BF16)
        zlt = zt[512:640]
        latt = zlt * lax.rsqrt(jnp.mean(zlt * zlt, axis=0, keepdims=True) + EPS) * latgt_ref[...]
        o["vmt"][0] = jnp.concatenate([latt, jnp.ones((MLA_VT_ROWS - MLA_R, tm), F32)], axis=0).astype(BF16)
        zt2 = _dot_nt(wt_ref[WT_DQ:WT_ROWS, :], h)
        o["dqt"][...] = (zt2[0:GROUP_W] * diff_scale).astype(BF16)
        o["lat"][0] = lat
        o["dvt"][0, 0] = zt2[GROUP_W:2 * GROUP_W]
        o["dkt"][0, 0] = zt2[2 * GROUP_W:3 * GROUP_W]
        krt = zt2[3 * GROUP_W:3 * GROUP_W + MLA_DR]
        krrt = zt2[3 * GROUP_W + MLA_DR:3 * GROUP_W + 2 * MLA_DR]
        o["krt"][0, 0] = krt * cost_ref[0:MLA_DR, :] + krrt * sint_ref[0:MLA_DR, :]
        ones_rows = jnp.ones((VT_ROWS - HEAD_W, tm), F32)
        for hh in range(N_HEADS):
            vt = zt2[GROUP_W + HEAD_W * hh:GROUP_W + HEAD_W * (hh + 1)]
            o["vdt"][hh, 0] = jnp.concatenate([vt, ones_rows], axis=0).astype(BF16)
    else:
        zq = sec(C_QN, 512)
        qr = (zq[:, 256:384] * cos + zq[:, 384:512] * sin) * mla_scale
        head_of_lane = _lane_group((1, 128), MLA_DR)
        for hh in range(N_HEADS):
            qn = (zq[:, MLA_DN * hh:MLA_DN * (hh + 1)] * mla_scale).astype(BF16)
            o["qm"][hh, :, 0:128] = _dot(qn, wukt_ref[hh]).astype(BF16)
            o["qm"][hh, :, 128:256] = jnp.where(head_of_lane == hh, qr, 0.0).astype(BF16)
        o["vm"][:, 0:128] = lat_b
        o["vm"][:, 128:256] = jnp.ones((tm, 128), BF16)
        o["dq"][...] = (sec(C_DQ, 256) * diff_scale).astype(BF16)
        o["lat"][...] = lat
        o["kr"][...] = kr4[:, 0:MLA_DR]
        o["dk"][...] = dk
        o["dv"][...] = sec(C_DV, 256)

    zs = sec(C_SU, 512)
    su = zs[:, 0:256]
    sv = zs[:, 256:512]
    mu = jnp.mean(sv, axis=-1, keepdims=True)
    svc = sv - mu
    var = jnp.mean(svc * svc, axis=-1, keepdims=True)
    vn = svc * lax.rsqrt(var + EPS) * lng_ref[...] + lnb_ref[...]
    o["vn"][...] = vn
    c = sgu_c
    wr = lax.broadcasted_iota(jnp.int32, (N_HEADS * c, c), 0) % c
    wc = lax.broadcasted_iota(jnp.int32, (N_HEADS * c, c), 1)
    w4 = jnp.where(wc <= wr, sguw_ref[...], 0.0).astype(BF16)
    hl = _lane_group((1, GROUP_W), HEAD_W)
    sgub = sgub_ref[...]
    for ci in range(tm // c):
        rows = slice(ci * c, (ci + 1) * c)
        m4 = _dot(w4, vn[rows].astype(BF16))
        mix = jnp.where(hl == 0, m4[0:c], 0.0)
        for hh in range(1, N_HEADS):
            mix = mix + jnp.where(hl == hh, m4[hh * c:(hh + 1) * c], 0.0)
        o["osgu"][rows, :] = (su[rows] * (mix + sgub)).astype(BF16)

    zg = sec(C_GQKV, 1536)
    o["gqkv"][...] = zg[:, 0:768]
    o["ggate"][...] = zg[:, 768:1024]
    o["gab"][...] = zg[:, 1024:1536]


def _inproj(x, g1, w, wt, cos, sin, cost, sint, wukt, wuk, latg, latgt, sguw, sgub, lng, lnb, stacked, *,
            tm, sgu_c, prompt, layer):
    t, d = x.shape
    n = t // tm
    npos = cos.shape[0] // tm
    row = lambda wd: pl.BlockSpec((tm, wd), lambda i: (i, 0))
    shapes = dict(
        km=((t, 256), BF16, row(256)), lat=((t, MLA_R), F32, row(MLA_R)), kr=((t, MLA_DR), F32, row(MLA_DR)),
        dk=((t, 256), F32, row(256)), dv=((t, 256), F32, row(256)), kd=((t, 256), BF16, row(256)),
        osgu=((t, 256), BF16, row(256)), vn=((t, 256), F32, row(256)), gqkv=((t, 768), F32, row(768)),
        ggate=((t, 256), F32, row(256)), gab=((t, 512), F32, row(512)),
        qmt=((N_HEADS, 256, t), BF16, pl.BlockSpec((N_HEADS, 256, tm), lambda i: (0, 0, i))),
        vmt=((n, MLA_VT_ROWS, tm), BF16, pl.BlockSpec((1, MLA_VT_ROWS, tm), lambda i: (i, 0, 0))),
        dqt=((GROUP_W, t), BF16, pl.BlockSpec((GROUP_W, tm), lambda i: (0, i))),
        vdt=((N_HEADS, n, VT_ROWS, tm), BF16, pl.BlockSpec((N_HEADS, 1, VT_ROWS, tm), lambda i: (0, i, 0, 0))),
        qm=((N_HEADS, t, 256), BF16, pl.BlockSpec((N_HEADS, tm, 256), lambda i: (0, i, 0))),
        vm=((t, 256), BF16, row(256)), dq=((t, 256), BF16, row(256)),
    )
    names = _INPROJ_COMMON + (_INPROJ_PROMPT if prompt else _INPROJ_SAMPLE)
    aliases = {}
    if prompt:
        depth = stacked[0].shape[0]
        seq = cos.shape[0]
        shapes["lat"] = ((depth, t, MLA_R), F32, pl.BlockSpec((1, tm, MLA_R), lambda i: (layer, i, 0)))
        for k, rows in (("krt", MLA_DR), ("dkt", GROUP_W), ("dvt", GROUP_W)):
            shapes[k] = ((depth, t // seq, rows, seq), F32,
                         pl.BlockSpec((1, 1, rows, tm), lambda i: (layer, i // npos, 0, i % npos)))
        for idx, k in enumerate(_INPROJ_STACKED):
            assert stacked[idx].shape == shapes[k][0]
            aliases[N_INPROJ_INPUTS + idx] = names.index(k)
    pos_row = pl.BlockSpec((tm, 128), lambda i: (i % npos, 0))
    pos_col = pl.BlockSpec((128, tm), lambda i: (0, i % npos))
    consts = (g1, w, wt)
    in_specs = ([row(d)] + [_const_spec(a.shape) for a in consts] + [pos_row, pos_row, pos_col, pos_col]
                + [_const_spec(a.shape) for a in (wukt, wuk, latg, latgt, sguw, sgub, lng, lnb)]
                + [pl.BlockSpec(memory_space=pl.ANY) for _ in stacked])
    assert len(in_specs) == N_INPROJ_INPUTS + len(stacked)
    outs = pl.pallas_call(
        functools.partial(_inproj_kernel, sgu_c=sgu_c, prompt=prompt),
        grid=(n,), in_specs=in_specs,
        out_specs=[shapes[k][2] for k in names],
        out_shape=[jax.ShapeDtypeStruct(shapes[k][0], shapes[k][1]) for k in names],
        input_output_aliases=aliases,
        compiler_params=_cparams(("parallel",)), name="inproj",
    )(x, g1, w, wt, cos, sin, cost, sint, wukt, wuk, latg, latgt, sguw, sgub, lng, lnb, *stacked)
    return dict(zip(names, outs))


def _t5_thresholds():
    nb = T5_BUCKETS // 2
    max_exact = nb // 2
    ratio = T5_MAX_DIST // max_exact
    thr = []
    for j in range(1, nb - max_exact):
        n = max_exact
        while n ** (nb - max_exact) < (ratio ** j) * (max_exact ** (nb - max_exact)):
            n += 1
        thr.append(n)
    return nb, max_exact, thr


def _bias_kernel(t_ref, o_ref, *, q0s, k0s):
    nb, max_exact, thr = _t5_thresholds()
    tq, tk = o_ref.shape[2], o_ref.shape[3]
    row = lax.broadcasted_iota(jnp.int32, (tq, tk), 0)
    col = lax.broadcasted_iota(jnp.int32, (tq, tk), 1)
    for di, (q0, k0) in enumerate(zip(q0s, k0s)):
        qpos = row + q0
        kpos = col + k0
        rel = kpos - qpos
        n = jnp.abs(rel)
        visible = (kpos // CHUNK) <= (qpos // CHUNK)
        for hh in range(N_HEADS):
            def side(base):
                val = jnp.full((tq, tk), t_ref[base + nb - 1, hh], F32)
                for j in range(len(thr) - 1, -1, -1):
                    val = jnp.where(n < thr[j], t_ref[base + max_exact + j, hh], val)
                for e in range(max_exact - 1, -1, -1):
                    val = jnp.where(n == e, t_ref[base + e, hh], val)
                return val
            b = jnp.where(rel > 0, side(nb), side(0)) - t_ref[nb - 1, hh]
            o_ref[di, hh] = jnp.where(visible, b * LOG2E, NEG)


def _t5_bias(t5_table, tq, tk, q0s, k0s):
    nd = len(q0s)
    return pl.pallas_call(
        functools.partial(_bias_kernel, q0s=tuple(q0s), k0s=tuple(k0s)),
        in_specs=[pl.BlockSpec(memory_space=pltpu.SMEM)],
        out_shape=jax.ShapeDtypeStruct((nd, N_HEADS, tq, tk), F32),
        name="t5_bias",
    )(t5_table)


def _mla_prompt_kernel(qt_ref, k_ref, vt_ref, wuvt_ref, o_ref, *, tq):
    qi = pl.program_id(1)
    qlim = ((qi * tq + lax.broadcasted_iota(jnp.int32, (1, tq), 1)) // CHUNK + 1) * CHUNK
    krow = lax.broadcasted_iota(jnp.int32, (tq, 1), 0)

    def scores_of(kb):
        kblk = k_ref[0, pl.ds(pl.multiple_of(kb * tq, tq), tq), :]
        return [_dot(kblk, qt_ref[hh]) for hh in range(N_HEADS)]

    def step(kb, carry, masked, scores=None):
        k0 = kb * tq
        scores = scores_of(kb) if scores is None else scores
        ps, mns, alphas = [], [], []
        for hh in range(N_HEADS):
            m = carry[2 * hh]
            s = scores[hh]
            if masked:
                s = jnp.where(krow + k0 < qlim, s, NEG)
            mn = jnp.maximum(m, jnp.max(s, axis=0, keepdims=True))
            alphas.append(jnp.exp2(m - mn))
            ps.append(jnp.exp2(s - mn).astype(BF16))
            mns.append(mn)
        out = []
        for hh in range(N_HEADS):
            out += [mns[hh], alphas[hh] * carry[2 * hh + 1] + _dot(vt_ref[kb], ps[hh])]
        return tuple(out)

    carry = (jnp.full((1, tq), NEG, F32), jnp.zeros((MLA_VT_ROWS, tq), F32)) * N_HEADS
    def two_steps(i, carry):
        sa, sb = scores_of(2 * i), scores_of(2 * i + 1)
        return step(2 * i + 1, step(2 * i, carry, False, sa), False, sb)

    carry = lax.fori_loop(0, qi // 2, two_steps, carry)
    carry = lax.fori_loop(2 * (qi // 2), qi, functools.partial(step, masked=False), carry)
    carry = step(qi, carry, True)
    outs = []
    for hh in range(N_HEADS):
        acc = carry[2 * hh + 1]
        on = (acc[0:MLA_R] * (1.0 / acc[MLA_R:MLA_R + 1])).astype(BF16)
        outs.append(_dot(wuvt_ref[hh], on))
    o_ref[...] = jnp.transpose(jnp.concatenate(outs, axis=0)).astype(BF16)


def _mla_prompt(qmt, km, vmt, wuvt, *, batch, seq, tq):
    assert seq % tq == 0 and vmt.shape[2] == tq and tq % CHUNK == 0
    nq = seq // tq
    t = batch * seq
    return pl.pallas_call(
        functools.partial(_mla_prompt_kernel, tq=tq),
        grid=(batch, nq),
        in_specs=[
            pl.BlockSpec((N_HEADS, 256, tq), lambda b, i: (0, 0, b * nq + i)),
            pl.BlockSpec((1, seq, 256), lambda b, i: (b, 0, 0)),
            pl.BlockSpec((nq, MLA_VT_ROWS, tq), lambda b, i: (b, 0, 0)),
            _const_spec(wuvt.shape),
        ],
        out_specs=pl.BlockSpec((tq, 256), lambda b, i: (b * nq + i, 0)),
        out_shape=jax.ShapeDtypeStruct((t, 256), BF16),
        compiler_params=_cparams(("parallel", "arbitrary")), name="mla_prompt",
    )(qmt, km.reshape(batch, seq, 256), vmt, wuvt)


def _diff_lambda(lam_ref, lam_init):
    l = lam_ref[...]
    a = jnp.sum(l[0:1] * l[1:2], axis=-1, keepdims=True)
    b = jnp.sum(l[2:3] * l[3:4], axis=-1, keepdims=True)
    return jnp.exp(a) - jnp.exp(b) + lam_init


def _stack_q8(q):
    grp = _lane_group((1, GROUP_W), DIFF_DH)
    return jnp.concatenate([jnp.where(grp == g, q, jnp.zeros_like(q)) for g in range(2 * N_HEADS)], axis=0)


def _diff_finish(on, lam, subg, lam_init, tq):
    outs = []
    for hh in range(N_HEADS):
        o = on[(2 * hh) * tq:(2 * hh + 1) * tq] - lam * on[(2 * hh + 1) * tq:(2 * hh + 2) * tq]
        o = o * lax.rsqrt(jnp.mean(o * o, axis=-1, keepdims=True) + 1e-5) * subg
        outs.append(o * (1.0 - lam_init))
    return jnp.concatenate(outs, axis=1)


def _diff_prompt_kernel(qt_ref, k_ref, vt_ref, bias_ref, cbias_ref, lam_ref, subg_ref, o_ref, *, tq, lam_init):
    qi = pl.program_id(1)
    rg = 2 * tq
    qt = qt_ref[...]
    fgrp = lax.broadcasted_iota(jnp.int32, (GROUP_W, 1), 0) // DIFF_DH
    qts = [jnp.concatenate([jnp.where(fgrp == 2 * hh + c, qt, jnp.zeros_like(qt)) for c in range(2)], axis=1)
           for hh in range(N_HEADS)]

    def scores_of(kb):
        kblk = k_ref[0, pl.ds(pl.multiple_of(kb * tq, tq), tq), :]
        return [_dot(kblk, qts[hh]) for hh in range(N_HEADS)]

    def step(kb, carry, bias, scores=None):
        scores = scores_of(kb) if scores is None else scores
        ps, mns, alphas = [], [], []
        for hh in range(N_HEADS):
            s = scores[hh]
            cols = slice(hh * rg, (hh + 1) * rg)
            if bias == "diag":
                s = s + bias_ref[:, cols]
            elif bias == "corner":
                s = jnp.concatenate([s[0:tq - DIFF_CORNER], s[tq - DIFF_CORNER:tq] + cbias_ref[:, cols]], axis=0)
            m = carry[2 * hh]
            mn = jnp.maximum(m, jnp.max(s, axis=0, keepdims=True))
            alphas.append(jnp.exp2(m - mn))
            ps.append(jnp.exp2(s - mn).astype(BF16))
            mns.append(mn)
        out = []
        for hh in range(N_HEADS):
            out += [mns[hh], alphas[hh] * carry[2 * hh + 1] + _dot(vt_ref[hh, kb], ps[hh])]
        return tuple(out)

    carry = (jnp.full((1, rg), NEG, F32), jnp.zeros((VT_ROWS, rg), F32)) * N_HEADS
    nfar = jnp.maximum(qi - 1, 0)

    def two_steps(i, carry):
        sa, sb = scores_of(2 * i), scores_of(2 * i + 1)
        return step(2 * i + 1, step(2 * i, carry, None, sa), None, sb)

    carry = lax.fori_loop(0, nfar // 2, two_steps, carry)
    carry = lax.fori_loop(2 * (nfar // 2), nfar, functools.partial(step, bias=None), carry)
    carry = lax.fori_loop(nfar, qi, functools.partial(step, bias="corner"), carry)
    carry = step(qi, carry, "diag")
    lam = _diff_lambda(lam_ref, lam_init)
    outs = []
    for hh in range(N_HEADS):
        acc = carry[2 * hh + 1]
        on = acc[0:HEAD_W] * (1.0 / acc[HEAD_W:HEAD_W + 1])
        o = on[:, 0:tq] - lam * on[:, tq:rg]
        outs.append(o * lax.rsqrt(jnp.mean(o * o, axis=0, keepdims=True) + 1e-5))
    ot = jnp.concatenate(outs, axis=0) * (subg_ref[...] * (1.0 - lam_init))
    o_ref[...] = jnp.transpose(ot).astype(BF16)


def _diff_prompt(dqt, kd, vdt, bias, cbias, lamv, subg, *, batch, seq, tq, lam_init):
    nq = seq // tq
    t = batch * seq
    assert vdt.shape[3] == tq
    return pl.pallas_call(
        functools.partial(_diff_prompt_kernel, tq=tq, lam_init=lam_init),
        grid=(batch, nq),
        in_specs=[
            pl.BlockSpec((GROUP_W, tq), lambda b, i: (0, b * nq + i)),
            pl.BlockSpec((1, seq, 256), lambda b, i: (b, 0, 0)),
            pl.BlockSpec((N_HEADS, nq, VT_ROWS, tq), lambda b, i: (0, b, 0, 0)),
            _const_spec(bias.shape), _const_spec(cbias.shape), _const_spec(lamv.shape), _const_spec(subg.shape),
        ],
        out_specs=pl.BlockSpec((tq, 256), lambda b, i: (b * nq + i, 0)),
        out_shape=jax.ShapeDtypeStruct((t, 256), BF16),
        compiler_params=_cparams(("parallel", "arbitrary")), name="diff_prompt",
    )(dqt, kd.reshape(batch, seq, 256), vdt, bias, cbias, lamv, subg)


def _mla_sample_kernel(q_ref, kc_ref, kn_ref, vn_ref, wuv_ref, o_ref, *, lq):
    r = N_HEADS * lq
    q = q_ref[...].reshape(r, 256)
    kc = kc_ref[0]
    sc = _dot_nt(q, kc)
    sn = _dot_nt(q, kn_ref[...])
    m = jnp.maximum(jnp.max(sc, axis=1, keepdims=True), jnp.max(sn, axis=1, keepdims=True))
    pc = jnp.exp2(sc - m)
    pn = jnp.exp2(sn - m)
    l = jnp.sum(pc, axis=1, keepdims=True) + jnp.sum(pn, axis=1, keepdims=True)
    pv = _dot(pc.astype(BF16), kc[:, 0:128]) + _dot(pn.astype(BF16), vn_ref[:, 0:128])
    on = (pv / l).astype(BF16)
    o_ref[...] = jnp.concatenate(
        [_dot(on[hh * lq:(hh + 1) * lq], wuv_ref[hh]) for hh in range(N_HEADS)], axis=1).astype(BF16)


def _mla_sample(qm, kc, km, vm, wuv, *, batch, lq):
    past = kc.shape[1]
    return pl.pallas_call(
        functools.partial(_mla_sample_kernel, lq=lq),
        grid=(batch,),
        in_specs=[
            pl.BlockSpec((N_HEADS, lq, 256), lambda b: (0, b, 0)),
            pl.BlockSpec((1, past, 256), lambda b: (b, 0, 0)),
            pl.BlockSpec((lq, 256), lambda b: (b, 0)),
            pl.BlockSpec((lq, 256), lambda b: (b, 0)),
            _const_spec(wuv.shape),
        ],
        out_specs=pl.BlockSpec((lq, 256), lambda b: (b, 0)),
        out_shape=jax.ShapeDtypeStruct((batch * lq, 256), BF16),
        compiler_params=_cparams(("parallel",)), name="mla_sample",
    )(qm, kc, km, vm, wuv)


def _diff_sample_kernel(q_ref, kc_ref, vc_ref, kn_ref, vn_ref, bc_ref, bn_ref, lam_ref, subg_ref, o_ref,
                        *, lq, lam_init):
    q8 = _stack_q8(q_ref[...])
    bc = jnp.concatenate([bc_ref[0, hh // 2] for hh in range(2 * N_HEADS)], axis=0)
    bn = jnp.concatenate([bn_ref[0, hh // 2] for hh in range(2 * N_HEADS)], axis=0)
    sc = _dot_nt(q8, kc_ref[0]) + bc
    sn = _dot_nt(q8, kn_ref[...].astype(BF16)) + bn
    m = jnp.maximum(jnp.max(sc, axis=1, keepdims=True), jnp.max(sn, axis=1, keepdims=True))
    pc = jnp.exp2(sc - m)
    pn = jnp.exp2(sn - m)
    l = jnp.sum(pc, axis=1, keepdims=True) + jnp.sum(pn, axis=1, keepdims=True)
    pv = (_dot(pc.astype(BF16), vc_ref[0]) + _dot(pn.astype(BF16), vn_ref[...].astype(BF16))) / l
    on = jnp.concatenate(
        [pv[g * lq:(g + 1) * lq, HEAD_W * (g // 2):HEAD_W * (g // 2 + 1)] for g in range(2 * N_HEADS)], axis=0)
    lam = _diff_lambda(lam_ref, lam_init)
    o_ref[...] = _diff_finish(on, lam, subg_ref[...], lam_init, lq).astype(BF16)


def _diff_sample(dq, kc, vc, dk, dv, bias_c, bias_n, lamv, subg, *, batch, lq, lam_init):
    past = kc.shape[1]
    return pl.pallas_call(
        functools.partial(_diff_sample_kernel, lq=lq, lam_init=lam_init),
        grid=(batch,),
        in_specs=[
            pl.BlockSpec((lq, 256), lambda b: (b, 0)),
            pl.BlockSpec((1, past, 256), lambda b: (b, 0, 0)),
            pl.BlockSpec((1, past, 256), lambda b: (b, 0, 0)),
            pl.BlockSpec((lq, 256), lambda b: (b, 0)),
            pl.BlockSpec((lq, 256), lambda b: (b, 0)),
            _const_spec(bias_c.shape), _const_spec(bias_n.shape),
            _const_spec(lamv.shape), _const_spec(subg.shape),
        ],
        out_specs=pl.BlockSpec((lq, 256), lambda b: (b, 0)),
        out_shape=jax.ShapeDtypeStruct((batch * lq, 256), BF16),
        compiler_params=_cparams(("parallel",)), name="diff_sample",
    )(dq, kc, vc, dk, dv, bias_c, bias_n, lamv, subg)


def _stack4(y, group):
    grp = _lane_group((1,) * (y.ndim - 1) + (y.shape[-1],), group)
    return jnp.concatenate([jnp.where(grp == g, y, 0.0) for g in range(N_HEADS)], axis=-2)


def _diag_sum(f, group):
    rr = f.shape[0] // N_HEADS
    grp = _lane_group((1, f.shape[1]), group)
    out = jnp.where(grp == 0, f[0:rr], 0.0)
    for g in range(1, N_HEADS):
        out = out + jnp.where(grp == g, f[g * rr:(g + 1) * rr], 0.0)
    return out


def _bmm(x, y, group):
    return _dot(x, _stack4(y, group))


def _bdot(x, w):
    return lax.dot_general(x, w, (((2,), (1,)), ((0,), (0,))), preferred_element_type=F32)


def _bdot_nt(x, w):
    return lax.dot_general(x, w, (((2,), (2,)), ((0,), (0,))), preferred_element_type=F32)


def _bbmm(x, y, group):
    return _bdot(x, _stack4(y, group))


def _split3(x):
    hi = x.astype(BF16)
    r1 = x - hi.astype(F32)
    mid = r1.astype(BF16)
    lo = (r1 - mid.astype(F32)).astype(BF16)
    return hi, mid, lo


def _unit_lower_inverse(a, c, tick):
    def bb(x, y, group):
        out = _bbmm(x, y, group)
        tick()
        return out

    shape = (1, c, N_HEADS * c)
    i = lax.broadcasted_iota(jnp.int32, shape, 1)
    j = lax.broadcasted_iota(jnp.int32, shape, 2) % c
    eye = jnp.where(i == j, 1.0, 0.0)
    blockdiag = (i // SOLVE_BLOCK) == (j // SOLVE_BLOCK)
    ad = jnp.where(blockdiag, a, 0.0)
    p = bb(ad, ad, c)
    rinv = eye - ad
    rinv = rinv + bb(rinv, p, c)
    for _ in range(2):
        p = bb(p, p, c)
        rinv = rinv + bb(rinv, p, c)
    if c <= SOLVE_BLOCK:
        return rinv
    mm = bb(rinv, a - ad, c)
    t = eye - mm
    p = bb(mm, mm, c)
    t = t + bb(t, p, c)
    nblk = c // SOLVE_BLOCK
    span = 4
    while span < nblk:
        p = bb(p, p, c)
        t = t + bb(t, p, c)
        span *= 2
    return bb(t, rinv, c)


def _to_square(x, c):
    if c == HEAD_W:
        return x
    return jnp.concatenate([x[..., HEAD_W * hh:HEAD_W * hh + c] for hh in range(N_HEADS)], axis=-1)


def _gdn_kernel(qkv_ref, gate_ref, gab_ref, buf_ref, s0_ref, cw_ref, alog_ref, dtb_ref, ng_ref,
                o_ref, conv_ref, s_ref, prev_scr, s_scr, u_scr, w_scr, qk_scr, qd_scr, kt_scr, gl_scr, *, tg, c, nj):
    t = pl.program_id(0)
    nc = tg // c
    slot = t % 2
    pslot = 1 - slot
    pre_scrs = (u_scr, w_scr, qk_scr, qd_scr, kt_scr, gl_scr)

    @pl.when(t == 0)
    def _():
        s_scr[...] = jnp.zeros_like(s_scr)
        for scr in pre_scrs:
            scr[1] = jnp.zeros(scr.shape[1:], F32)

    @pl.when(t % nj == 0)
    def _():
        prev_scr[...] = jnp.zeros_like(prev_scr)
        prev_scr[8 - (GDN_CONV - 1):8, :] = buf_ref[0]

    @pl.when((t > 0) & ((t - 1) % nj == 0))
    def _():
        for hh in range(N_HEADS):
            s_scr[:, HEAD_W * hh:HEAD_W * (hh + 1)] = s0_ref[0, hh]

    pre = [tuple(scr[pslot, n] for scr in pre_scrs) for n in range(nc)]
    s_old = s_scr[...]
    st = dict(s=s_old, n=0, ws=None, outs=[])

    def tick():
        if st["n"] >= nc:
            return
        u, w, qk, qd, kt, glast = pre[st["n"]]
        if st["ws"] is None:
            st["ws"] = _dot(jnp.concatenate([w, qd], axis=0), _stack4(st["s"], HEAD_W))
        else:
            ws = st["ws"]
            vnew = u - ws[0:c]
            st["outs"].append(ws[c:2 * c] + _bmm(qk, vnew, HEAD_W))
            st["s"] = st["s"] * glast + _diag_sum(_dot_tn(kt, vnew), HEAD_W)
            st["ws"] = None
            st["n"] += 1

    x = qkv_ref[...]
    prev8 = prev_scr[...]
    cw = cw_ref[...]
    y = x * cw[GDN_CONV - 1:GDN_CONV]
    for s in range(1, GDN_CONV):
        y = y + _shift_rows(x, prev8, s) * cw[GDN_CONV - 1 - s:GDN_CONV - s]
    prev_scr[...] = x[tg - 8:tg]
    conv_ref[0] = x[tg - (GDN_CONV - 1):tg]

    y = _silu(y)
    q = y[:, 0:256]
    k = y[:, 256:512]
    v = y[:, 512:768]
    gones = jnp.where(_lane_group((GROUP_W, GROUP_W), HEAD_W)
                      == lax.broadcasted_iota(jnp.int32, (GROUP_W, GROUP_W), 0) // HEAD_W,
                      1.0, 0.0).astype(BF16)

    def head_sum(z):
        z_hi, z_mid, z_lo = _split3(z)
        return _dot(z_hi, gones) + _dot(z_mid, gones) + _dot(z_lo, gones)

    q = q * lax.rsqrt(head_sum(q * q) + 1e-6) * (HEAD_W ** -0.5)
    tick()
    k = k * lax.rsqrt(head_sum(k * k) + 1e-6)
    tick()
    gab = gab_ref[...]
    za = gab[:, 0:256] + dtb_ref[...]
    softplus = jnp.maximum(za, 0.0) + jnp.log(1.0 + jnp.exp(-jnp.abs(za)))
    g = -jnp.exp(alog_ref[...]) * softplus
    beta = 1.0 / (1.0 + jnp.exp(-gab[:, 256:512]))

    sq = (1, c, N_HEADS * c)
    ri = lax.broadcasted_iota(jnp.int32, sq, 1)
    cj = lax.broadcasted_iota(jnp.int32, sq, 2) % c
    tr = lax.broadcasted_iota(jnp.int32, (tg, tg), 0)
    tc = lax.broadcasted_iota(jnp.int32, (tg, tg), 1)
    tri_b = jnp.where((tc <= tr) & (tc // c == tr // c), 1.0, 0.0).astype(BF16)
    g_hi, g_mid, g_lo = _split3(g)
    decay2 = _dot(tri_b, g_hi) + _dot(tri_b, g_mid) + _dot(tri_b, g_lo)
    tick()
    chunked = lambda z: z.reshape(nc, c, z.shape[-1])
    qb, kb_, vb, bb, decay = chunked(q), chunked(k), chunked(v), chunked(beta), chunked(decay2)
    dsq = _to_square(decay, c)
    drow = jnp.sum(jnp.where(ri == cj, dsq, 0.0), axis=1, keepdims=True)
    lm = jnp.where(cj <= ri, jnp.exp(jnp.where(cj <= ri, dsq - drow, 0.0)), 0.0)
    kbeta = kb_ * bb
    kq = _bdot_nt(jnp.concatenate([kbeta, qb], axis=1), _stack4(kb_, HEAD_W))
    tick()
    a_mat = jnp.where(cj < ri, kq[:, 0:c] * lm, 0.0)
    edec = jnp.exp(decay)
    tinv = _unit_lower_inverse(a_mat, c, tick)
    uw = _bdot(tinv, jnp.concatenate([_stack4(vb * bb, HEAD_W), _stack4(kbeta * edec, HEAD_W)], axis=2))
    tick()
    dlast = decay[:, c - 1:c]
    u_scr[slot] = uw[:, :, 0:GROUP_W]
    w_scr[slot] = uw[:, :, GROUP_W:2 * GROUP_W]
    qk_scr[slot] = kq[:, c:2 * c] * lm
    qd_scr[slot] = qb * edec
    kt_scr[slot] = kb_ * jnp.exp(dlast - decay)
    gl_scr[slot] = jnp.exp(dlast)

    while st["n"] < nc:
        tick()
    s_fin = jnp.where(t > 0, st["s"], s_old)
    s_scr[...] = s_fin
    for hh in range(N_HEADS):
        s_ref[0, hh] = s_fin[:, HEAD_W * hh:HEAD_W * (hh + 1)]
    outs = st["outs"]
    o = outs[0] if len(outs) == 1 else jnp.concatenate(outs, axis=0)
    o = o * lax.rsqrt(head_sum(o * o) * (1.0 / HEAD_W) + EPS) * ng_ref[...]
    o_ref[...] = (o * _silu(gate_ref[...])).astype(BF16)


def _gdn(gqkv, ggate, gab, buf, s0, cw, alog, dtb, ng, *, batch, seq, tg, c):
    nj = seq // tg
    nt = batch * nj
    nc = tg // c
    t = batch * seq
    cur_t = lambda i: jnp.minimum(i, nt - 1)
    prv_t = lambda i: jnp.maximum(i - 1, 0)
    cur = lambda wd: pl.BlockSpec((tg, wd), lambda i: (cur_t(i), 0))
    prv = lambda wd: pl.BlockSpec((tg, wd), lambda i: (prv_t(i), 0))
    pre_shape = lambda wd: pltpu.VMEM((2, nc, c, wd), F32)
    return pl.pallas_call(
        functools.partial(_gdn_kernel, tg=tg, c=c, nj=nj),
        grid=(nt + 1,),
        in_specs=[
            cur(768), prv(256), cur(512),
            pl.BlockSpec((1, GDN_CONV - 1, 768), lambda i: (cur_t(i) // nj, 0, 0)),
            pl.BlockSpec((1, N_HEADS, HEAD_W, HEAD_W), lambda i: (prv_t(i) // nj, 0, 0, 0)),
            _const_spec(cw.shape), _const_spec(alog.shape), _const_spec(dtb.shape), _const_spec(ng.shape),
        ],
        out_specs=[
            prv(256),
            pl.BlockSpec((1, GDN_CONV - 1, 768), lambda i: (cur_t(i) // nj, 0, 0)),
            pl.BlockSpec((1, N_HEADS, HEAD_W, HEAD_W), lambda i: (prv_t(i) // nj, 0, 0, 0)),
        ],
        out_shape=[
            jax.ShapeDtypeStruct((t, 256), BF16),
            jax.ShapeDtypeStruct((batch, GDN_CONV - 1, 768), F32),
            jax.ShapeDtypeStruct((batch, N_HEADS, HEAD_W, HEAD_W), F32),
        ],
        scratch_shapes=[pltpu.VMEM((8, 768), F32), pltpu.VMEM((HEAD_W, GROUP_W), F32),
                        pre_shape(GROUP_W), pre_shape(GROUP_W), pre_shape(N_HEADS * c), pre_shape(GROUP_W),
                        pre_shape(GROUP_W), pltpu.VMEM((2, nc, 1, GROUP_W), F32)],
        compiler_params=_cparams(("arbitrary",)), name="gdn",
    )(gqkv, ggate, gab, buf, s0, cw, alog, dtb, ng)


def _ffn_kernel(x_ref, om_ref, od_ref, os_ref, og_ref, buf_ref, wout_ref, g2_ref, wup_ref, cw_ref, wdn_ref,
                fg_ref, y_ref, conv_ref, prev_scr, act_scr, *, tm, d_ff, ft, final):
    j = pl.program_id(1)
    nj = pl.num_programs(1)

    @pl.when(j == 0)
    def _():
        prev_scr[...] = jnp.zeros_like(prev_scr)
        prev_scr[8 - (FFN_CONV - 1):8, :] = buf_ref[0]

    mixed = jnp.concatenate([om_ref[...], od_ref[...], os_ref[...], og_ref[...]], axis=1)
    x1 = x_ref[...] + _dot(mixed, wout_ref[...])
    h2 = (x1 * lax.rsqrt(jnp.mean(x1 * x1, axis=-1, keepdims=True) + EPS) * g2_ref[...]).astype(BF16)

    def conv_cols(off):
        a = _dot(h2, wup_ref[:, off:off + ft])
        prev8 = prev_scr[:, off:off + ft]
        cw = cw_ref[:, off:off + ft]
        y = a * cw[FFN_CONV - 1:FFN_CONV]
        for s in range(1, FFN_CONV):
            y = y + _shift_rows(a, prev8, s) * cw[FFN_CONV - 1 - s:FFN_CONV - s]
        prev_scr[:, off:off + ft] = a[tm - 8:tm]
        return y

    for fi in range(d_ff // ft):
        gate = conv_cols(fi * ft)
        up = conv_cols(d_ff + fi * ft)
        act_scr[:, fi * ft:(fi + 1) * ft] = (_silu(gate) * up).astype(BF16)
    acc = x1 + _dot(act_scr[...], wdn_ref[...])

    if final:
        acc = acc * lax.rsqrt(jnp.mean(acc * acc, axis=-1, keepdims=True) + EPS) * fg_ref[...]
    y_ref[...] = acc

    @pl.when(j == nj - 1)
    def _():
        conv_ref[0] = prev_scr[8 - (FFN_CONV - 1):8, :]


def _ffn(x, om, od, osg, og, buf, wout, g2, wup, cw, wdn, fg, *, batch, seq, tm, ft, final):
    nj = seq // tm
    t, d = x.shape
    d_ff = wdn.shape[0]
    row = lambda wd: pl.BlockSpec((tm, wd), lambda b, j: (b * nj + j, 0))
    return pl.pallas_call(
        functools.partial(_ffn_kernel, tm=tm, d_ff=d_ff, ft=ft, final=final),
        grid=(batch, nj),
        in_specs=[
            row(d), row(256), row(256), row(256), row(256),
            pl.BlockSpec((1, FFN_CONV - 1, 2 * d_ff), lambda b, j: (b, 0, 0)),
            _const_spec(wout.shape), _const_spec(g2.shape), _const_spec(wup.shape),
            _const_spec(cw.shape), _const_spec(wdn.shape), _const_spec(fg.shape),
        ],
        out_specs=[row(d), pl.BlockSpec((1, FFN_CONV - 1, 2 * d_ff), lambda b, j: (b, 0, 0))],
        out_shape=[jax.ShapeDtypeStruct((t, d), F32),
                   jax.ShapeDtypeStruct((batch, FFN_CONV - 1, 2 * d_ff), F32)],
        scratch_shapes=[pltpu.VMEM((8, 2 * d_ff), F32), pltpu.VMEM((tm, d_ff), BF16)],
        compiler_params=_cparams(("parallel", "arbitrary")), name="ffn",
    )(x, om, od, osg, og, buf, wout, g2, wup, cw, wdn, fg)


def _rot_cols(w):
    d = w.shape[0]
    wg = w.reshape(d, -1, 2, MLA_DR // 2)
    return jnp.concatenate([-wg[:, :, 1:2], wg[:, :, 0:1]], axis=2).reshape(d, -1)


def _permute_w_in(w_in):
    d = w_in.shape[0]
    pts = np.cumsum([384, 128, 32, 256, 256, 256, 256, 256, 768, 256, 4, 4])[:-1]
    mq, mlat, mkr, dq, dk, dv, su, sv, gqkv, ggate, ga, gb = jnp.split(w_in, pts, axis=1)
    mq = mq.reshape(d, N_HEADS, MLA_DN + MLA_DR)
    qn = mq[:, :, :MLA_DN].reshape(d, N_HEADS * MLA_DN)
    qr = mq[:, :, MLA_DN:].reshape(d, N_HEADS * MLA_DR)
    kr4 = jnp.tile(mkr, (1, N_HEADS))
    rep = lambda a: jnp.repeat(a, HEAD_W, axis=1)
    w = jnp.concatenate([qn, qr, _rot_cols(qr), mlat, kr4, _rot_cols(kr4),
                         dq, dk, dv, su, sv, gqkv, ggate, rep(ga), rep(gb)], axis=1)
    assert w.shape[1] == C_TOTAL
    wt = jnp.transpose(jnp.concatenate([qn, qr, _rot_cols(qr), mlat, dq, dv, dk, mkr, _rot_cols(mkr)], axis=1))
    assert wt.shape[0] == WT_ROWS
    return w.astype(BF16), wt.astype(BF16)


def _rope_tables(pos):
    inv = ROPE_THETA ** (-jnp.arange(0, MLA_DR, 2, dtype=F32) / MLA_DR)
    ang = pos.astype(F32)[:, None] * inv[None, :]
    cos = jnp.tile(jnp.cos(ang), (1, 2 * N_HEADS))
    sin = jnp.tile(jnp.sin(ang), (1, 2 * N_HEADS))
    return cos, sin


def _layer_params(l, p):
    row = lambda a: a.reshape(1, -1)
    rep = lambda a: jnp.repeat(a, HEAD_W).reshape(1, -1)
    sgu_w = p['sgu_w'][l]
    w_perm, w_t = _permute_w_in(p['w_in'][l])
    return dict(
        w_in=w_perm, w_t=w_t,
        g1=row(p['norm1_g'][l]),
        wukt=jnp.transpose(p['mla_w_uk'][l], (1, 2, 0)).astype(BF16),
        wuk=jnp.transpose(p['mla_w_uk'][l], (1, 0, 2)).astype(BF16),
        wuv=jnp.transpose(p['mla_w_uv'][l], (1, 0, 2)).astype(BF16),
        wuvt=jnp.transpose(p['mla_w_uv'][l], (1, 2, 0)).astype(BF16),
        latg=row(p['mla_lat_g'][l]), latgt=p['mla_lat_g'][l].reshape(-1, 1),
        sgu_w=sgu_w,
        sgu_b=p['sgu_b'][l],
        lng=row(p['sgu_ln_g'][l]), lnb=row(p['sgu_ln_b'][l]),
        lamv=jnp.stack([p['diff_lam_q1'][l], p['diff_lam_k1'][l], p['diff_lam_q2'][l], p['diff_lam_k2'][l]]),
        subg=row(p['diff_sub_g'][l]),
        gcw=p['gdn_conv_w'][l],
        alog=rep(p['gdn_a_log'][l]), dtb=rep(p['gdn_dt_bias'][l]),
        ng=jnp.tile(p['gdn_norm_g'][l], N_HEADS).reshape(1, -1),
        wout=p['w_out'][l].astype(BF16),
        g2=row(p['norm2_g'][l]),
        wup=p['ffn_w_up'][l].astype(BF16),
        fcw=p['ffn_conv_w'][l],
        wdn=p['ffn_w_down'][l].astype(BF16),
        fg=row(p['final_g']),
    )


def _sgu_tables(lp, c):
    w = lp['sgu_w'][:, :c, :c].reshape(N_HEADS * c, c)
    b = jnp.repeat(jnp.transpose(lp['sgu_b'][:, :c]), HEAD_W, axis=1)
    return w, b


def _run_inproj(x, lp, cos, sin, stacked=(), *, tm, sgu_c, prompt, layer=0):
    sw, sb = _sgu_tables(lp, sgu_c)
    return _inproj(x, lp['g1'], lp['w_in'], lp['w_t'], cos, sin, jnp.transpose(cos), jnp.transpose(sin),
                   lp['wukt'], lp['wuk'], lp['latg'], lp['latgt'], sw, sb, lp['lng'], lp['lnb'], tuple(stacked),
                   tm=tm, sgu_c=sgu_c, prompt=prompt, layer=layer)


def kernel(x_prompt, x_sample, cache_mla_latent, cache_mla_krope, cache_diff_k, cache_diff_v, state_gdn_conv, state_gdn_s, state_ffn_conv, t5_table, final_g, norm1_g, w_in, mla_lat_g, mla_w_uk, mla_w_uv, diff_lam_q1, diff_lam_k1, diff_lam_q2, diff_lam_k2, diff_sub_g, sgu_ln_g, sgu_ln_b, sgu_w, sgu_b, gdn_conv_w, gdn_a_log, gdn_dt_bias, gdn_norm_g, w_out, norm2_g, ffn_w_up, ffn_conv_w, ffn_w_down):
    p = dict(final_g=final_g, norm1_g=norm1_g, w_in=w_in, mla_lat_g=mla_lat_g, mla_w_uk=mla_w_uk,
             mla_w_uv=mla_w_uv, diff_lam_q1=diff_lam_q1, diff_lam_k1=diff_lam_k1, diff_lam_q2=diff_lam_q2,
             diff_lam_k2=diff_lam_k2, diff_sub_g=diff_sub_g, sgu_ln_g=sgu_ln_g, sgu_ln_b=sgu_ln_b,
             sgu_w=sgu_w, sgu_b=sgu_b, gdn_conv_w=gdn_conv_w, gdn_a_log=gdn_a_log, gdn_dt_bias=gdn_dt_bias,
             gdn_norm_g=gdn_norm_g, w_out=w_out, norm2_g=norm2_g, ffn_w_up=ffn_w_up, ffn_conv_w=ffn_conv_w,
             ffn_w_down=ffn_w_down)
    depth = w_in.shape[0]
    bp, sp, d = x_prompt.shape
    bs, ls, _ = x_sample.shape
    past = cache_mla_latent.shape[2]
    d_ff = ffn_w_down.shape[1]
    assert past % CHUNK == 0 and ls <= CHUNK

    tm_p = min(512, sp)
    tq = min(512, sp)
    tg_p = min(512, sp)
    ft = 256
    sgu_cp = min(SGU_CHUNK, sp)
    gdn_cp = min(GDN_CHUNK, sp)

    cos_p, sin_p = _rope_tables(jnp.arange(sp, dtype=jnp.int32))
    pos_s = past + jnp.arange(ls, dtype=jnp.int32)
    cos_s, sin_s = _rope_tables(jnp.tile(pos_s, bs))

    assert tq >= DIFF_CORNER and tm_p == tq
    stack_t = lambda b: jnp.transpose(jnp.repeat(b, 2, axis=0), (2, 0, 1)).reshape(b.shape[2], -1)
    bias_p = stack_t(_t5_bias(t5_table, tq, tq, (0,), (0,))[0])
    corner = _t5_bias(t5_table, DIFF_CORNER, DIFF_CORNER, (DIFF_CORNER,), (0,))[0]
    bias_pc = stack_t(jnp.pad(corner, ((0, 0), (0, tq - DIFF_CORNER), (0, 0))))
    bias_sc = _t5_bias(t5_table, ls, past, (past,), (0,))
    bias_sn = _t5_bias(t5_table, ls, ls, (past,), (past,))

    xp = x_prompt.reshape(bp * sp, d)
    xs = x_sample.reshape(bs * ls, d)
    zeros_gconv = jnp.zeros((bp, GDN_CONV - 1, 3 * GROUP_W), F32)
    zeros_gs = jnp.zeros((bp, N_HEADS, HEAD_W, HEAD_W), F32)
    zeros_fconv = jnp.zeros((bp, FFN_CONV - 1, 2 * d_ff), F32)

    stacked = (jnp.zeros((depth, bp * sp, MLA_R), F32),) + tuple(
        jnp.zeros((depth, bp, rows, sp), F32) for rows in (MLA_DR, GROUP_W, GROUP_W))

    outs_p, outs_s = [], []
    for l in range(depth):
        lp = _layer_params(l, p)
        lam_init = 0.8 - 0.6 * math.exp(-0.3 * l)
        final = l == depth - 1

        ip = _run_inproj(xp, lp, cos_p, sin_p, stacked, tm=tm_p, sgu_c=sgu_cp, prompt=True, layer=l)
        stacked = tuple(ip[k] for k in _INPROJ_STACKED)
        osgu, gqkv, ggate, gab = ip['osgu'], ip['gqkv'], ip['ggate'], ip['gab']
        o_mla = _mla_prompt(ip['qmt'], ip['km'], ip['vmt'], lp['wuvt'], batch=bp, seq=sp, tq=tq)
        o_diff = _diff_prompt(ip['dqt'], ip['kd'], ip['vdt'], bias_p, bias_pc, lp['lamv'],
                              jnp.tile(lp['subg'], (1, N_HEADS)).reshape(-1, 1),
                              batch=bp, seq=sp, tq=tq, lam_init=lam_init)
        o_gdn, gconv, gs = _gdn(gqkv, ggate, gab, zeros_gconv, zeros_gs, lp['gcw'], lp['alog'], lp['dtb'],
                                lp['ng'], batch=bp, seq=sp, tg=tg_p, c=gdn_cp)
        xp, fconv = _ffn(xp, o_mla, o_diff, osgu, o_gdn, zeros_fconv, lp['wout'], lp['g2'], lp['wup'],
                         lp['fcw'], lp['wdn'], lp['fg'], batch=bp, seq=sp, tm=tm_p, ft=ft, final=final)
        outs_p.append(dict(gconv=gconv, gs=gs, fconv=fconv))

        ip = _run_inproj(xs, lp, cos_s, sin_s, tm=bs * ls, sgu_c=min(SGU_CHUNK, ls), prompt=False)
        qm, km, vm, lat, kr, dq, dk, dv = (ip[k] for k in ('qm', 'km', 'vm', 'lat', 'kr', 'dq', 'dk', 'dv'))
        osgu, vn, gqkv, ggate, gab = (ip[k] for k in ('osgu', 'vn', 'gqkv', 'ggate', 'gab'))
        kc_m = jnp.concatenate([cache_mla_latent[l], jnp.tile(cache_mla_krope[l], (1, 1, N_HEADS))],
                               axis=-1).astype(BF16)
        o_mla = _mla_sample(qm, kc_m, km, vm, lp['wuv'], batch=bs, lq=ls)
        kc_d = cache_diff_k[l].reshape(bs, past, GROUP_W).astype(BF16)
        vc_d = cache_diff_v[l].reshape(bs, past, GROUP_W).astype(BF16)
        o_diff = _diff_sample(dq, kc_d, vc_d, dk, dv, bias_sc, bias_sn, lp['lamv'], lp['subg'],
                              batch=bs, lq=ls, lam_init=lam_init)
        o_gdn, gconv, gs = _gdn(gqkv, ggate, gab, state_gdn_conv[l], state_gdn_s[l], lp['gcw'], lp['alog'],
                                lp['dtb'], lp['ng'], batch=bs, seq=ls, tg=ls, c=min(GDN_CHUNK, ls))
        xs, fconv = _ffn(xs, o_mla, o_diff, osgu, o_gdn, state_ffn_conv[l], lp['wout'], lp['g2'], lp['wup'],
                         lp['fcw'], lp['wdn'], lp['fg'], batch=bs, seq=ls, tm=ls, ft=ft, final=final)
        outs_s.append(dict(
            lat=lat.reshape(bs, ls, MLA_R), kr=kr.reshape(bs, ls, MLA_DR),
            dk=dk.reshape(bs, ls, N_HEADS, 2, DIFF_DH), dv=dv.reshape(bs, ls, N_HEADS, 2 * DIFF_DH),
            gconv=gconv, gs=gs, fconv=fconv, sv=vn.reshape(bs, ls, GROUP_W)))

    st = lambda lst, key: jnp.stack([o[key] for o in lst])
    lat_p, krt_p, dkt_p, dvt_p = stacked
    kr_p = jnp.transpose(krt_p, (0, 1, 3, 2))
    dk_p = jnp.transpose(dkt_p.reshape(depth, bp, N_HEADS, 2, DIFF_DH, sp), (0, 1, 5, 2, 3, 4))
    dv_p = jnp.transpose(dvt_p.reshape(depth, bp, N_HEADS, 2 * DIFF_DH, sp), (0, 1, 4, 2, 3))
    return (xp.reshape(bp, sp, d), xs.reshape(bs, ls, d),
            lat_p.reshape(depth, bp, sp, MLA_R), kr_p, dk_p, dv_p,
            st(outs_p, 'gconv'), st(outs_p, 'gs'), st(outs_p, 'fconv'),
            st(outs_s, 'lat'), st(outs_s, 'kr'), st(outs_s, 'dk'), st(outs_s, 'dv'),
            st(outs_s, 'gconv'), st(outs_s, 'gs'), st(outs_s, 'fconv'), st(outs_s, 'sv'))
```

```python
import functools
import math

import numpy as np
import jax
import jax.numpy as jnp
from jax import lax
from jax.experimental import pallas as pl
from jax.experimental.pallas import tpu as pltpu

F32 = jnp.float32
BF16 = jnp.bfloat16

CHUNK = 64
EPS = 1e-6
N_HEADS = 4
HEAD_W = 64
GROUP_W = 256
MLA_DN, MLA_DR, MLA_R = 64, 32, 128
DIFF_DH = 32
ROPE_THETA = 10000.0
SGU_CHUNK = 128
GDN_CHUNK = 64
GDN_CONV = 4
FFN_CONV = 3
T5_BUCKETS = 32
T5_MAX_DIST = 128
NEG = -1e30
LOG2E = math.log2(math.e)
SOLVE_BLOCK = 16
DIFF_CORNER = 128
MLA_VT_ROWS = 144
VT_ROWS = 80

VMEM_LIMIT_BYTES = 56 * 1024 * 1024

C_QN, C_QR, C_QRR, C_LAT, C_KR, C_KRR = 0, 256, 384, 512, 640, 768
C_DQ, C_DK, C_DV, C_SU, C_SV = 896, 1152, 1408, 1664, 1920
C_GQKV, C_GG, C_GA, C_GB = 2176, 2944, 3200, 3456
C_TOTAL = 3712
WT_DQ, WT_ROWS = 640, 1472
N_INPROJ_INPUTS = 16


def _cparams(sem):
    return pltpu.CompilerParams(dimension_semantics=sem, vmem_limit_bytes=VMEM_LIMIT_BYTES)


def _const_spec(shape):
    nd = len(shape)
    return pl.BlockSpec(shape, lambda *_: (0,) * nd, pipeline_mode=pl.Buffered(1))


def _dot(a, b):
    return jnp.dot(a, b, preferred_element_type=F32)


def _dot_nt(a, b):
    return lax.dot_general(a, b, (((1,), (1,)), ((), ())), preferred_element_type=F32)


def _dot_tn(a, b):
    return lax.dot_general(a, b, (((0,), (0,)), ((), ())), preferred_element_type=F32)


def _lane_group(shape, group):
    return lax.broadcasted_iota(jnp.int32, shape, len(shape) - 1) // group


def _silu(x):
    return x * (1.0 / (1.0 + jnp.exp(-x)))


def _shift_rows(x, prev8, s):
    if s == 0:
        return x
    r = pltpu.roll(x, s, axis=0)
    rp = pltpu.roll(prev8, s, axis=0)
    row8 = lax.broadcasted_iota(jnp.int32, rp.shape, 0)
    top = jnp.where(row8 < s, rp, r[0:8])
    if x.shape[0] == 8:
        return top
    return jnp.concatenate([top, r[8:]], axis=0)


_INPROJ_COMMON = ("km", "kd", "osgu", "gqkv", "ggate", "gab")
_INPROJ_STACKED = ("lat", "krt", "dkt", "dvt")
_INPROJ_PROMPT = _INPROJ_STACKED + ("qmt", "vmt", "dqt", "vdt")
_INPROJ_SAMPLE = ("lat", "kr", "dk", "dv", "vn", "qm", "vm", "dq")


def _inproj_kernel(x_ref, g1_ref, w_ref, wt_ref, cos_ref, sin_ref, cost_ref, sint_ref, wukt_ref, wuk_ref,
                   latg_ref, latgt_ref, sguw_ref, sgub_ref, lng_ref, lnb_ref, *rest, sgu_c, prompt):
    out_refs = rest[len(_INPROJ_STACKED):] if prompt else rest
    o = dict(zip(_INPROJ_COMMON + (_INPROJ_PROMPT if prompt else _INPROJ_SAMPLE), out_refs))
    tm = x_ref.shape[0]
    x = x_ref[...]
    h = (x * lax.rsqrt(jnp.mean(x * x, axis=-1, keepdims=True) + EPS) * g1_ref[...]).astype(BF16)

    def sec(off, n):
        return _dot(h, w_ref[:, off:off + n])

    cos = cos_ref[...]
    sin = sin_ref[...]
    mla_scale = (MLA_DN + MLA_DR) ** -0.5 * LOG2E
    diff_scale = DIFF_DH ** -0.5 * LOG2E

    zl = sec(C_LAT, 384)
    zlat = zl[:, 0:128]
    lat = zlat * lax.rsqrt(jnp.mean(zlat * zlat, axis=-1, keepdims=True) + EPS) * latg_ref[...]
    kr4 = zl[:, 128:256] * cos + zl[:, 256:384] * sin
    lat_b = lat.astype(BF16)
    o["km"][:, 0:128] = lat_b
    o["km"][:, 128:256] = kr4.astype(BF16)
    dk = sec(C_DK, 256)
    o["kd"][...] = dk.astype(BF16)

    if prompt:
        zt = _dot_nt(wt_ref[0:WT_DQ, :], h)
        qrt = (zt[256:384] * cost_ref[...] + zt[384:512] * sint_ref[...]) * mla_scale
        head_of_row = lax.broadcasted_iota(jnp.int32, (128, 1), 0) // MLA_DR
        for hh in range(N_HEADS):
            qn = (zt[MLA_DN * hh:MLA_DN * (hh + 1)] * mla_scale).astype(BF16)
            o["qmt"][hh, 0:128, :] = _dot(wuk_ref[hh], qn).astype(BF16)
            o["qmt"][hh, 128:256, :] = jnp.where(head_of_row == hh, qrt, 0.0).astype(BF16)
        zlt = zt[512:640]
        latt = zlt * lax.rsqrt(jnp.mean(zlt * zlt, axis=0, keepdims=True) + EPS) * latgt_ref[...]
        o["vmt"][0] = jnp.concatenate([latt, jnp.ones((MLA_VT_ROWS - MLA_R, tm), F32)], axis=0).astype(BF16)
        zt2 = _dot_nt(wt_ref[WT_DQ:WT_ROWS, :], h)
        o["dqt"][...] = (zt2[0:GROUP_W] * diff_scale).astype(BF16)
        o["lat"][0] = lat
        o["dvt"][0, 0] = zt2[GROUP_W:2 * GROUP_W]
        o["dkt"][0, 0] = zt2[2 * GROUP_W:3 * GROUP_W]
        krt = zt2[3 * GROUP_W:3 * GROUP_W + MLA_DR]
        krrt = zt2[3 * GROUP_W + MLA_DR:3 * GROUP_W + 2 * MLA_DR]
        o["krt"][0, 0] = krt * cost_ref[0:MLA_DR, :] + krrt * sint_ref[0:MLA_DR, :]
        ones_rows = jnp.ones((VT_ROWS - HEAD_W, tm), F32)
        for hh in range(N_HEADS):
            vt = zt2[GROUP_W + HEAD_W * hh:GROUP_W + HEAD_W * (hh + 1)]
            o["vdt"][hh, 0] = jnp.concatenate([vt, ones_rows], axis=0).astype(BF16)
    else:
        zq = sec(C_QN, 512)
        qr = (zq[:, 256:384] * cos + zq[:, 384:512] * sin) * mla_scale
        head_of_lane = _lane_group((1, 128), MLA_DR)
        for hh in range(N_HEADS):
            qn = (zq[:, MLA_DN * hh:MLA_DN * (hh + 1)] * mla_scale).astype(BF16)
            o["qm"][hh, :, 0:128] = _dot(qn, wukt_ref[hh]).astype(BF16)
            o["qm"][hh, :, 128:256] = jnp.where(head_of_lane == hh, qr, 0.0).astype(BF16)
        o["vm"][:, 0:128] = lat_b
        o["vm"][:, 128:256] = jnp.ones((tm, 128), BF16)
        o["dq"][...] = (sec(C_DQ, 256) * diff_scale).astype(BF16)
        o["lat"][...] = lat
        o["kr"][...] = kr4[:, 0:MLA_DR]
        o["dk"][...] = dk
        o["dv"][...] = sec(C_DV, 256)

    zs = sec(C_SU, 512)
    su = zs[:, 0:256]
    sv = zs[:, 256:512]
    mu = jnp.mean(sv, axis=-1, keepdims=True)
    svc = sv - mu
    var = jnp.mean(svc * svc, axis=-1, keepdims=True)
    vn = svc * lax.rsqrt(var + EPS) * lng_ref[...] + lnb_ref[...]
    if not prompt:
        o["vn"][...] = vn
    c = sgu_c
    wr = lax.broadcasted_iota(jnp.int32, (N_HEADS * c, c), 0) % c
    wc = lax.broadcasted_iota(jnp.int32, (N_HEADS * c, c), 1)
    w4 = jnp.where(wc <= wr, sguw_ref[...], 0.0).astype(BF16)
    hl = _lane_group((1, GROUP_W), HEAD_W)
    sgub = sgub_ref[...]
    for ci in range(tm // c):
        rows = slice(ci * c, (ci + 1) * c)
        m4 = _dot(w4, vn[rows].astype(BF16))
        mix = jnp.where(hl == 0, m4[0:c], 0.0)
        for hh in range(1, N_HEADS):
            mix = mix + jnp.where(hl == hh, m4[hh * c:(hh + 1) * c], 0.0)
        o["osgu"][rows, :] = (su[rows] * (mix + sgub)).astype(BF16)

    zg = sec(C_GQKV, 1536)
    o["gqkv"][...] = zg[:, 0:768]
    o["ggate"][...] = zg[:, 768:1024]
    o["gab"][...] = zg[:, 1024:1536]


def _inproj(x, g1, w, wt, cos, sin, cost, sint, wukt, wuk, latg, latgt, sguw, sgub, lng, lnb, stacked, *,
            tm, sgu_c, prompt, layer):
    t, d = x.shape
    n = t // tm
    npos = cos.shape[0] // tm
    row = lambda wd: pl.BlockSpec((tm, wd), lambda i: (i, 0))
    shapes = dict(
        km=((t, 256), BF16, row(256)), lat=((t, MLA_R), F32, row(MLA_R)), kr=((t, MLA_DR), F32, row(MLA_DR)),
        dk=((t, 256), F32, row(256)), dv=((t, 256), F32, row(256)), kd=((t, 256), BF16, row(256)),
        osgu=((t, 256), BF16, row(256)), vn=((t, 256), F32, row(256)), gqkv=((t, 768), F32, row(768)),
        ggate=((t, 256), F32, row(256)), gab=((t, 512), F32, row(512)),
        qmt=((N_HEADS, 256, t), BF16, pl.BlockSpec((N_HEADS, 256, tm), lambda i: (0, 0, i))),
        vmt=((n, MLA_VT_ROWS, tm), BF16, pl.BlockSpec((1, MLA_VT_ROWS, tm), lambda i: (i, 0, 0))),
        dqt=((GROUP_W, t), BF16, pl.BlockSpec((GROUP_W, tm), lambda i: (0, i))),
        vdt=((N_HEADS, n, VT_ROWS, tm), BF16, pl.BlockSpec((N_HEADS, 1, VT_ROWS, tm), lambda i: (0, i, 0, 0))),
        qm=((N_HEADS, t, 256), BF16, pl.BlockSpec((N_HEADS, tm, 256), lambda i: (0, i, 0))),
        vm=((t, 256), BF16, row(256)), dq=((t, 256), BF16, row(256)),
    )
    names = _INPROJ_COMMON + (_INPROJ_PROMPT if prompt else _INPROJ_SAMPLE)
    aliases = {}
    if prompt:
        depth = stacked[0].shape[0]
        seq = cos.shape[0]
        shapes["lat"] = ((depth, t, MLA_R), F32, pl.BlockSpec((1, tm, MLA_R), lambda i: (layer, i, 0)))
        for k, rows in (("krt", MLA_DR), ("dkt", GROUP_W), ("dvt", GROUP_W)):
            shapes[k] = ((depth, t // seq, rows, seq), F32,
                         pl.BlockSpec((1, 1, rows, tm), lambda i: (layer, i // npos, 0, i % npos)))
        for idx, k in enumerate(_INPROJ_STACKED):
            assert stacked[idx].shape == shapes[k][0]
            aliases[N_INPROJ_INPUTS + idx] = names.index(k)
    pos_row = pl.BlockSpec((tm, 128), lambda i: (i % npos, 0))
    pos_col = pl.BlockSpec((128, tm), lambda i: (0, i % npos))
    consts = (g1, w, wt)
    in_specs = ([row(d)] + [_const_spec(a.shape) for a in consts] + [pos_row, pos_row, pos_col, pos_col]
                + [_const_spec(a.shape) for a in (wukt, wuk, latg, latgt, sguw, sgub, lng, lnb)]
                + [pl.BlockSpec(memory_space=pl.ANY) for _ in stacked])
    assert len(in_specs) == N_INPROJ_INPUTS + len(stacked)
    outs = pl.pallas_call(
        functools.partial(_inproj_kernel, sgu_c=sgu_c, prompt=prompt),
        grid=(n,), in_specs=in_specs,
        out_specs=[shapes[k][2] for k in names],
        out_shape=[jax.ShapeDtypeStruct(shapes[k][0], shapes[k][1]) for k in names],
        input_output_aliases=aliases,
        compiler_params=_cparams(("parallel",)), name="inproj",
    )(x, g1, w, wt, cos, sin, cost, sint, wukt, wuk, latg, latgt, sguw, sgub, lng, lnb, *stacked)
    return dict(zip(names, outs))


def _t5_thresholds():
    nb = T5_BUCKETS // 2
    max_exact = nb // 2
    ratio = T5_MAX_DIST // max_exact
    thr = []
    for j in range(1, nb - max_exact):
        n = max_exact
        while n ** (nb - max_exact) < (ratio ** j) * (max_exact ** (nb - max_exact)):
            n += 1
        thr.append(n)
    return nb, max_exact, thr


def _bias_kernel(t_ref, o_ref, *, q0s, k0s):
    nb, max_exact, thr = _t5_thresholds()
    tq, tk = o_ref.shape[2], o_ref.shape[3]
    row = lax.broadcasted_iota(jnp.int32, (tq, tk), 0)
    col = lax.broadcasted_iota(jnp.int32, (tq, tk), 1)
    for di, (q0, k0) in enumerate(zip(q0s, k0s)):
        qpos = row + q0
        kpos = col + k0
        rel = kpos - qpos
        n = jnp.abs(rel)
        visible = (kpos // CHUNK) <= (qpos // CHUNK)
        for hh in range(N_HEADS):
            def side(base):
                val = jnp.full((tq, tk), t_ref[base + nb - 1, hh], F32)
                for j in range(len(thr) - 1, -1, -1):
                    val = jnp.where(n < thr[j], t_ref[base + max_exact + j, hh], val)
                for e in range(max_exact - 1, -1, -1):
                    val = jnp.where(n == e, t_ref[base + e, hh], val)
                return val
            b = jnp.where(rel > 0, side(nb), side(0)) - t_ref[nb - 1, hh]
            o_ref[di, hh] = jnp.where(visible, b * LOG2E, NEG)


def _t5_bias(t5_table, tq, tk, q0s, k0s):
    nd = len(q0s)
    return pl.pallas_call(
        functools.partial(_bias_kernel, q0s=tuple(q0s), k0s=tuple(k0s)),
        in_specs=[pl.BlockSpec(memory_space=pltpu.SMEM)],
        out_shape=jax.ShapeDtypeStruct((nd, N_HEADS, tq, tk), F32),
        name="t5_bias",
    )(t5_table)


def _mla_prompt_kernel(qt_ref, k_ref, vt_ref, wuvt_ref, o_ref, *, tq):
    qi = pl.program_id(1)
    qlim = ((qi * tq + lax.broadcasted_iota(jnp.int32, (1, tq), 1)) // CHUNK + 1) * CHUNK
    krow = lax.broadcasted_iota(jnp.int32, (tq, 1), 0)

    def scores_of(kb):
        kblk = k_ref[0, pl.ds(pl.multiple_of(kb * tq, tq), tq), :]
        return [_dot(kblk, qt_ref[hh]) for hh in range(N_HEADS)]

    def step(kb, carry, masked, scores=None):
        k0 = kb * tq
        scores = scores_of(kb) if scores is None else scores
        ps, mns, alphas = [], [], []
        for hh in range(N_HEADS):
            m = carry[2 * hh]
            s = scores[hh]
            if masked:
                s = jnp.where(krow + k0 < qlim, s, NEG)
            mn = jnp.maximum(m, jnp.max(s, axis=0, keepdims=True))
            alphas.append(jnp.exp2(m - mn))
            ps.append(jnp.exp2(s - mn).astype(BF16))
            mns.append(mn)
        out = []
        for hh in range(N_HEADS):
            out += [mns[hh], alphas[hh] * carry[2 * hh + 1] + _dot(vt_ref[kb], ps[hh])]
        return tuple(out)

    carry = (jnp.full((1, tq), NEG, F32), jnp.zeros((MLA_VT_ROWS, tq), F32)) * N_HEADS
    def two_steps(i, carry):
        sa, sb = scores_of(2 * i), scores_of(2 * i + 1)
        return step(2 * i + 1, step(2 * i, carry, False, sa), False, sb)

    carry = lax.fori_loop(0, qi // 2, two_steps, carry)
    carry = lax.fori_loop(2 * (qi // 2), qi, functools.partial(step, masked=False), carry)
    carry = step(qi, carry, True)
    outs = []
    for hh in range(N_HEADS):
        acc = carry[2 * hh + 1]
        on = (acc[0:MLA_R] * (1.0 / acc[MLA_R:MLA_R + 1])).astype(BF16)
        outs.append(_dot(wuvt_ref[hh], on))
    o_ref[...] = jnp.transpose(jnp.concatenate(outs, axis=0)).astype(BF16)


def _mla_prompt(qmt, km, vmt, wuvt, *, batch, seq, tq):
    assert seq % tq == 0 and vmt.shape[2] == tq and tq % CHUNK == 0
    nq = seq // tq
    t = batch * seq
    return pl.pallas_call(
        functools.partial(_mla_prompt_kernel, tq=tq),
        grid=(batch, nq),
        in_specs=[
            pl.BlockSpec((N_HEADS, 256, tq), lambda b, i: (0, 0, b * nq + i)),
            pl.BlockSpec((1, seq, 256), lambda b, i: (b, 0, 0)),
            pl.BlockSpec((nq, MLA_VT_ROWS, tq), lambda b, i: (b, 0, 0)),
            _const_spec(wuvt.shape),
        ],
        out_specs=pl.BlockSpec((tq, 256), lambda b, i: (b * nq + i, 0)),
        out_shape=jax.ShapeDtypeStruct((t, 256), BF16),
        compiler_params=_cparams(("parallel", "arbitrary")), name="mla_prompt",
    )(qmt, km.reshape(batch, seq, 256), vmt, wuvt)


def _diff_lambda(lam_ref, lam_init):
    l = lam_ref[...]
    a = jnp.sum(l[0:1] * l[1:2], axis=-1, keepdims=True)
    b = jnp.sum(l[2:3] * l[3:4], axis=-1, keepdims=True)
    return jnp.exp(a) - jnp.exp(b) + lam_init


def _stack_q8(q):
    grp = _lane_group((1, GROUP_W), DIFF_DH)
    return jnp.concatenate([jnp.where(grp == g, q, jnp.zeros_like(q)) for g in range(2 * N_HEADS)], axis=0)


def _diff_finish(on, lam, subg, lam_init, tq):
    outs = []
    for hh in range(N_HEADS):
        o = on[(2 * hh) * tq:(2 * hh + 1) * tq] - lam * on[(2 * hh + 1) * tq:(2 * hh + 2) * tq]
        o = o * lax.rsqrt(jnp.mean(o * o, axis=-1, keepdims=True) + 1e-5) * subg
        outs.append(o * (1.0 - lam_init))
    return jnp.concatenate(outs, axis=1)


def _diff_prompt_kernel(qt_ref, k_ref, vt_ref, bias_ref, cbias_ref, lam_ref, subg_ref, o_ref, *, tq, lam_init):
    qi = pl.program_id(1)
    rg = 2 * tq
    qt = qt_ref[...]
    fgrp = lax.broadcasted_iota(jnp.int32, (GROUP_W, 1), 0) // DIFF_DH
    qts = [jnp.concatenate([jnp.where(fgrp == 2 * hh + c, qt, jnp.zeros_like(qt)) for c in range(2)], axis=1)
           for hh in range(N_HEADS)]

    def scores_of(kb):
        kblk = k_ref[0, pl.ds(pl.multiple_of(kb * tq, tq), tq), :]
        return [_dot(kblk, qts[hh]) for hh in range(N_HEADS)]

    def step(kb, carry, bias, scores=None):
        scores = scores_of(kb) if scores is None else scores
        ps, mns, alphas = [], [], []
        for hh in range(N_HEADS):
            s = scores[hh]
            cols = slice(hh * rg, (hh + 1) * rg)
            if bias == "diag":
                s = s + bias_ref[:, cols]
            elif bias == "corner":
                s = jnp.concatenate([s[0:tq - DIFF_CORNER], s[tq - DIFF_CORNER:tq] + cbias_ref[:, cols]], axis=0)
            m = carry[2 * hh]
            mn = jnp.maximum(m, jnp.max(s, axis=0, keepdims=True))
            alphas.append(jnp.exp2(m - mn))
            ps.append(jnp.exp2(s - mn).astype(BF16))
            mns.append(mn)
        out = []
        for hh in range(N_HEADS):
            out += [mns[hh], alphas[hh] * carry[2 * hh + 1] + _dot(vt_ref[hh, kb], ps[hh])]
        return tuple(out)

    carry = (jnp.full((1, rg), NEG, F32), jnp.zeros((VT_ROWS, rg), F32)) * N_HEADS
    nfar = jnp.maximum(qi - 1, 0)

    def two_steps(i, carry):
        sa, sb = scores_of(2 * i), scores_of(2 * i + 1)
        return step(2 * i + 1, step(2 * i, carry, None, sa), None, sb)

    carry = lax.fori_loop(0, nfar // 2, two_steps, carry)
    carry = lax.fori_loop(2 * (nfar // 2), nfar, functools.partial(step, bias=None), carry)
    carry = lax.fori_loop(nfar, qi, functools.partial(step, bias="corner"), carry)
    carry = step(qi, carry, "diag")
    lam = _diff_lambda(lam_ref, lam_init)
    outs = []
    for hh in range(N_HEADS):
        acc = carry[2 * hh + 1]
        on = acc[0:HEAD_W] * (1.0 / acc[HEAD_W:HEAD_W + 1])
        o = on[:, 0:tq] - lam * on[:, tq:rg]
        outs.append(o * lax.rsqrt(jnp.mean(o * o, axis=0, keepdims=True) + 1e-5))
    ot = jnp.concatenate(outs, axis=0) * (subg_ref[...] * (1.0 - lam_init))
    o_ref[...] = jnp.transpose(ot).astype(BF16)


def _diff_prompt(dqt, kd, vdt, bias, cbias, lamv, subg, *, batch, seq, tq, lam_init):
    nq = seq // tq
    t = batch * seq
    assert vdt.shape[3] == tq
    return pl.pallas_call(
        functools.partial(_diff_prompt_kernel, tq=tq, lam_init=lam_init),
        grid=(batch, nq),
        in_specs=[
            pl.BlockSpec((GROUP_W, tq), lambda b, i: (0, b * nq + i)),
            pl.BlockSpec((1, seq, 256), lambda b, i: (b, 0, 0)),
            pl.BlockSpec((N_HEADS, nq, VT_ROWS, tq), lambda b, i: (0, b, 0, 0)),
            _const_spec(bias.shape), _const_spec(cbias.shape), _const_spec(lamv.shape), _const_spec(subg.shape),
        ],
        out_specs=pl.BlockSpec((tq, 256), lambda b, i: (b * nq + i, 0)),
        out_shape=jax.ShapeDtypeStruct((t, 256), BF16),
        compiler_params=_cparams(("parallel", "arbitrary")), name="diff_prompt",
    )(dqt, kd.reshape(batch, seq, 256), vdt, bias, cbias, lamv, subg)


def _mla_sample_kernel(q_ref, kc_ref, kn_ref, vn_ref, wuv_ref, o_ref, *, lq):
    r = N_HEADS * lq
    q = q_ref[...].reshape(r, 256)
    kc = kc_ref[0]
    sc = _dot_nt(q, kc)
    sn = _dot_nt(q, kn_ref[...])
    m = jnp.maximum(jnp.max(sc, axis=1, keepdims=True), jnp.max(sn, axis=1, keepdims=True))
    pc = jnp.exp2(sc - m)
    pn = jnp.exp2(sn - m)
    l = jnp.sum(pc, axis=1, keepdims=True) + jnp.sum(pn, axis=1, keepdims=True)
    pv = _dot(pc.astype(BF16), kc[:, 0:128]) + _dot(pn.astype(BF16), vn_ref[:, 0:128])
    on = (pv / l).astype(BF16)
    o_ref[...] = jnp.concatenate(
        [_dot(on[hh * lq:(hh + 1) * lq], wuv_ref[hh]) for hh in range(N_HEADS)], axis=1).astype(BF16)


def _mla_sample(qm, kc, km, vm, wuv, *, batch, lq):
    past = kc.shape[1]
    return pl.pallas_call(
        functools.partial(_mla_sample_kernel, lq=lq),
        grid=(batch,),
        in_specs=[
            pl.BlockSpec((N_HEADS, lq, 256), lambda b: (0, b, 0)),
            pl.BlockSpec((1, past, 256), lambda b: (b, 0, 0)),
            pl.BlockSpec((lq, 256), lambda b: (b, 0)),
            pl.BlockSpec((lq, 256), lambda b: (b, 0)),
            _const_spec(wuv.shape),
        ],
        out_specs=pl.BlockSpec((lq, 256), lambda b: (b, 0)),
        out_shape=jax.ShapeDtypeStruct((batch * lq, 256), BF16),
        compiler_params=_cparams(("parallel",)), name="mla_sample",
    )(qm, kc, km, vm, wuv)


def _diff_sample_kernel(q_ref, kc_ref, vc_ref, kn_ref, vn_ref, bc_ref, bn_ref, lam_ref, subg_ref, o_ref,
                        *, lq, lam_init):
    q8 = _stack_q8(q_ref[...])
    bc = jnp.concatenate([bc_ref[0, hh // 2] for hh in range(2 * N_HEADS)], axis=0)
    bn = jnp.concatenate([bn_ref[0, hh // 2] for hh in range(2 * N_HEADS)], axis=0)
    sc = _dot_nt(q8, kc_ref[0]) + bc
    sn = _dot_nt(q8, kn_ref[...].astype(BF16)) + bn
    m = jnp.maximum(jnp.max(sc, axis=1, keepdims=True), jnp.max(sn, axis=1, keepdims=True))
    pc = jnp.exp2(sc - m)
    pn = jnp.exp2(sn - m)
    l = jnp.sum(pc, axis=1, keepdims=True) + jnp.sum(pn, axis=1, keepdims=True)
    pv = (_dot(pc.astype(BF16), vc_ref[0]) + _dot(pn.astype(BF16), vn_ref[...].astype(BF16))) / l
    on = jnp.concatenate(
        [pv[g * lq:(g + 1) * lq, HEAD_W * (g // 2):HEAD_W * (g // 2 + 1)] for g in range(2 * N_HEADS)], axis=0)
    lam = _diff_lambda(lam_ref, lam_init)
    o_ref[...] = _diff_finish(on, lam, subg_ref[...], lam_init, lq).astype(BF16)


def _diff_sample(dq, kc, vc, dk, dv, bias_c, bias_n, lamv, subg, *, batch, lq, lam_init):
    past = kc.shape[1]
    return pl.pallas_call(
        functools.partial(_diff_sample_kernel, lq=lq, lam_init=lam_init),
        grid=(batch,),
        in_specs=[
            pl.BlockSpec((lq, 256), lambda b: (b, 0)),
            pl.BlockSpec((1, past, 256), lambda b: (b, 0, 0)),
            pl.BlockSpec((1, past, 256), lambda b: (b, 0, 0)),
            pl.BlockSpec((lq, 256), lambda b: (b, 0)),
            pl.BlockSpec((lq, 256), lambda b: (b, 0)),
            _const_spec(bias_c.shape), _const_spec(bias_n.shape),
            _const_spec(lamv.shape), _const_spec(subg.shape),
        ],
        out_specs=pl.BlockSpec((lq, 256), lambda b: (b, 0)),
        out_shape=jax.ShapeDtypeStruct((batch * lq, 256), BF16),
        compiler_params=_cparams(("parallel",)), name="diff_sample",
    )(dq, kc, vc, dk, dv, bias_c, bias_n, lamv, subg)


def _stack4(y, group):
    grp = _lane_group((1,) * (y.ndim - 1) + (y.shape[-1],), group)
    return jnp.concatenate([jnp.where(grp == g, y, 0.0) for g in range(N_HEADS)], axis=-2)


def _diag_sum(f, group):
    rr = f.shape[0] // N_HEADS
    grp = _lane_group((1, f.shape[1]), group)
    out = jnp.where(grp == 0, f[0:rr], 0.0)
    for g in range(1, N_HEADS):
        out = out + jnp.where(grp == g, f[g * rr:(g + 1) * rr], 0.0)
    return out


def _bmm(x, y, group):
    return _dot(x, _stack4(y, group))


def _bdot(x, w):
    return lax.dot_general(x, w, (((2,), (1,)), ((0,), (0,))), preferred_element_type=F32)


def _bdot_nt(x, w):
    return lax.dot_general(x, w, (((2,), (2,)), ((0,), (0,))), preferred_element_type=F32)


def _bbmm(x, y, group):
    return _bdot(x, _stack4(y, group))


def _split3(x):
    hi = x.astype(BF16)
    r1 = x - hi.astype(F32)
    mid = r1.astype(BF16)
    lo = (r1 - mid.astype(F32)).astype(BF16)
    return hi, mid, lo


def _unit_lower_inverse(a, c, tick):
    def bb(x, y, group):
        out = _bbmm(x, y, group)
        tick()
        return out

    shape = (1, c, N_HEADS * c)
    i = lax.broadcasted_iota(jnp.int32, shape, 1)
    j = lax.broadcasted_iota(jnp.int32, shape, 2) % c
    eye = jnp.where(i == j, 1.0, 0.0)
    blockdiag = (i // SOLVE_BLOCK) == (j // SOLVE_BLOCK)
    ad = jnp.where(blockdiag, a, 0.0)
    p = bb(ad, ad, c)
    rinv = eye - ad
    rinv = rinv + bb(rinv, p, c)
    for _ in range(2):
        p = bb(p, p, c)
        rinv = rinv + bb(rinv, p, c)
    if c <= SOLVE_BLOCK:
        return rinv
    mm = bb(rinv, a - ad, c)
    t = eye - mm
    p = bb(mm, mm, c)
    t = t + bb(t, p, c)
    nblk = c // SOLVE_BLOCK
    span = 4
    while span < nblk:
        p = bb(p, p, c)
        t = t + bb(t, p, c)
        span *= 2
    return bb(t, rinv, c)


def _to_square(x, c):
    if c == HEAD_W:
        return x
    return jnp.concatenate([x[..., HEAD_W * hh:HEAD_W * hh + c] for hh in range(N_HEADS)], axis=-1)


def _gdn_kernel(qkv_ref, gate_ref, gab_ref, buf_ref, s0_ref, cw_ref, alog_ref, dtb_ref, ng_ref,
                o_ref, conv_ref, s_ref, prev_scr, s_scr, u_scr, w_scr, qk_scr, qd_scr, kt_scr, gl_scr, *, tg, c, nj):
    t = pl.program_id(0)
    nc = tg // c
    slot = t % 2
    pslot = 1 - slot
    pre_scrs = (u_scr, w_scr, qk_scr, qd_scr, kt_scr, gl_scr)

    @pl.when(t == 0)
    def _():
        s_scr[...] = jnp.zeros_like(s_scr)
        for scr in pre_scrs:
            scr[1] = jnp.zeros(scr.shape[1:], F32)

    @pl.when(t % nj == 0)
    def _():
        prev_scr[...] = jnp.zeros_like(prev_scr)
        prev_scr[8 - (GDN_CONV - 1):8, :] = buf_ref[0]

    @pl.when((t > 0) & ((t - 1) % nj == 0))
    def _():
        for hh in range(N_HEADS):
            s_scr[:, HEAD_W * hh:HEAD_W * (hh + 1)] = s0_ref[0, hh]

    pre = [tuple(scr[pslot, n] for scr in pre_scrs) for n in range(nc)]
    s_old = s_scr[...]
    st = dict(s=s_old, n=0, ws=None, outs=[])

    def tick():
        if st["n"] >= nc:
            return
        u, w, qk, qd, kt, glast = pre[st["n"]]
        if st["ws"] is None:
            st["ws"] = _dot(jnp.concatenate([w, qd], axis=0), _stack4(st["s"], HEAD_W))
        else:
            ws = st["ws"]
            vnew = u - ws[0:c]
            st["outs"].append(ws[c:2 * c] + _bmm(qk, vnew, HEAD_W))
            st["s"] = st["s"] * glast + _diag_sum(_dot_tn(kt, vnew), HEAD_W)
            st["ws"] = None
            st["n"] += 1

    x = qkv_ref[...]
    prev8 = prev_scr[...]
    cw = cw_ref[...]
    y = x * cw[GDN_CONV - 1:GDN_CONV]
    for s in range(1, GDN_CONV):
        y = y + _shift_rows(x, prev8, s) * cw[GDN_CONV - 1 - s:GDN_CONV - s]
    prev_scr[...] = x[tg - 8:tg]
    conv_ref[0] = x[tg - (GDN_CONV - 1):tg]

    y = _silu(y)
    q = y[:, 0:256]
    k = y[:, 256:512]
    v = y[:, 512:768]
    gones = jnp.where(_lane_group((GROUP_W, GROUP_W), HEAD_W)
                      == lax.broadcasted_iota(jnp.int32, (GROUP_W, GROUP_W), 0) // HEAD_W,
                      1.0, 0.0).astype(BF16)

    def head_sum(z):
        z_hi, z_mid, z_lo = _split3(z)
        return _dot(z_hi, gones) + _dot(z_mid, gones) + _dot(z_lo, gones)

    q = q * lax.rsqrt(head_sum(q * q) + 1e-6) * (HEAD_W ** -0.5)
    tick()
    k = k * lax.rsqrt(head_sum(k * k) + 1e-6)
    tick()
    gab = gab_ref[...]
    za = gab[:, 0:256] + dtb_ref[...]
    softplus = jnp.maximum(za, 0.0) + jnp.log(1.0 + jnp.exp(-jnp.abs(za)))
    g = -jnp.exp(alog_ref[...]) * softplus
    beta = 1.0 / (1.0 + jnp.exp(-gab[:, 256:512]))

    sq = (1, c, N_HEADS * c)
    ri = lax.broadcasted_iota(jnp.int32, sq, 1)
    cj = lax.broadcasted_iota(jnp.int32, sq, 2) % c
    tr = lax.broadcasted_iota(jnp.int32, (tg, tg), 0)
    tc = lax.broadcasted_iota(jnp.int32, (tg, tg), 1)
    tri_b = jnp.where((tc <= tr) & (tc // c == tr // c), 1.0, 0.0).astype(BF16)
    g_hi, g_mid, g_lo = _split3(g)
    decay2 = _dot(tri_b, g_hi) + _dot(tri_b, g_mid) + _dot(tri_b, g_lo)
    tick()
    chunked = lambda z: z.reshape(nc, c, z.shape[-1])
    qb, kb_, vb, bb, decay = chunked(q), chunked(k), chunked(v), chunked(beta), chunked(decay2)
    dsq = _to_square(decay, c)
    drow = jnp.sum(jnp.where(ri == cj, dsq, 0.0), axis=1, keepdims=True)
    lm = jnp.where(cj <= ri, jnp.exp(jnp.where(cj <= ri, dsq - drow, 0.0)), 0.0)
    kbeta = kb_ * bb
    kq = _bdot_nt(jnp.concatenate([kbeta, qb], axis=1), _stack4(kb_, HEAD_W))
    tick()
    a_mat = jnp.where(cj < ri, kq[:, 0:c] * lm, 0.0)
    edec = jnp.exp(decay)
    tinv = _unit_lower_inverse(a_mat, c, tick)
    uw = _bdot(tinv, jnp.concatenate([_stack4(vb * bb, HEAD_W), _stack4(kbeta * edec, HEAD_W)], axis=2))
    tick()
    dlast = decay[:, c - 1:c]
    u_scr[slot] = uw[:, :, 0:GROUP_W]
    w_scr[slot] = uw[:, :, GROUP_W:2 * GROUP_W]
    qk_scr[slot] = kq[:, c:2 * c] * lm
    qd_scr[slot] = qb * edec
    kt_scr[slot] = kb_ * jnp.exp(dlast - decay)
    gl_scr[slot] = jnp.exp(dlast)

    while st["n"] < nc:
        tick()
    s_fin = jnp.where(t > 0, st["s"], s_old)
    s_scr[...] = s_fin
    for hh in range(N_HEADS):
        s_ref[0, hh] = s_fin[:, HEAD_W * hh:HEAD_W * (hh + 1)]
    outs = st["outs"]
    o = outs[0] if len(outs) == 1 else jnp.concatenate(outs, axis=0)
    o = o * lax.rsqrt(head_sum(o * o) * (1.0 / HEAD_W) + EPS) * ng_ref[...]
    o_ref[...] = (o * _silu(gate_ref[...])).astype(BF16)


def _gdn(gqkv, ggate, gab, buf, s0, cw, alog, dtb, ng, *, batch, seq, tg, c):
    nj = seq // tg
    nt = batch * nj
    nc = tg // c
    t = batch * seq
    cur_t = lambda i: jnp.minimum(i, nt - 1)
    prv_t = lambda i: jnp.maximum(i - 1, 0)
    cur = lambda wd: pl.BlockSpec((tg, wd), lambda i: (cur_t(i), 0))
    prv = lambda wd: pl.BlockSpec((tg, wd), lambda i: (prv_t(i), 0))
    pre_shape = lambda wd: pltpu.VMEM((2, nc, c, wd), F32)
    return pl.pallas_call(
        functools.partial(_gdn_kernel, tg=tg, c=c, nj=nj),
        grid=(nt + 1,),
        in_specs=[
            cur(768), prv(256), cur(512),
            pl.BlockSpec((1, GDN_CONV - 1, 768), lambda i: (cur_t(i) // nj, 0, 0)),
            pl.BlockSpec((1, N_HEADS, HEAD_W, HEAD_W), lambda i: (prv_t(i) // nj, 0, 0, 0)),
            _const_spec(cw.shape), _const_spec(alog.shape), _const_spec(dtb.shape), _const_spec(ng.shape),
        ],
        out_specs=[
            prv(256),
            pl.BlockSpec((1, GDN_CONV - 1, 768), lambda i: (cur_t(i) // nj, 0, 0)),
            pl.BlockSpec((1, N_HEADS, HEAD_W, HEAD_W), lambda i: (prv_t(i) // nj, 0, 0, 0)),
        ],
        out_shape=[
            jax.ShapeDtypeStruct((t, 256), BF16),
            jax.ShapeDtypeStruct((batch, GDN_CONV - 1, 768), F32),
            jax.ShapeDtypeStruct((batch, N_HEADS, HEAD_W, HEAD_W), F32),
        ],
        scratch_shapes=[pltpu.VMEM((8, 768), F32), pltpu.VMEM((HEAD_W, GROUP_W), F32),
                        pre_shape(GROUP_W), pre_shape(GROUP_W), pre_shape(N_HEADS * c), pre_shape(GROUP_W),
                        pre_shape(GROUP_W), pltpu.VMEM((2, nc, 1, GROUP_W), F32)],
        compiler_params=_cparams(("arbitrary",)), name="gdn",
    )(gqkv, ggate, gab, buf, s0, cw, alog, dtb, ng)


def _ffn_kernel(x_ref, om_ref, od_ref, os_ref, og_ref, buf_ref, wout_ref, g2_ref, wup_ref, cw_ref, wdn_ref,
                fg_ref, y_ref, conv_ref, prev_scr, act_scr, *, tm, d_ff, ft, final):
    j = pl.program_id(1)
    nj = pl.num_programs(1)

    @pl.when(j == 0)
    def _():
        prev_scr[...] = jnp.zeros_like(prev_scr)
        prev_scr[8 - (FFN_CONV - 1):8, :] = buf_ref[0]

    mixed = jnp.concatenate([om_ref[...], od_ref[...], os_ref[...], og_ref[...]], axis=1)
    x1 = x_ref[...] + _dot(mixed, wout_ref[...])
    h2 = (x1 * lax.rsqrt(jnp.mean(x1 * x1, axis=-1, keepdims=True) + EPS) * g2_ref[...]).astype(BF16)

    def conv_cols(off):
        a = _dot(h2, wup_ref[:, off:off + ft])
        prev8 = prev_scr[:, off:off + ft]
        cw = cw_ref[:, off:off + ft]
        y = a * cw[FFN_CONV - 1:FFN_CONV]
        for s in range(1, FFN_CONV):
            y = y + _shift_rows(a, prev8, s) * cw[FFN_CONV - 1 - s:FFN_CONV - s]
        prev_scr[:, off:off + ft] = a[tm - 8:tm]
        return y

    for fi in range(d_ff // ft):
        gate = conv_cols(fi * ft)
        up = conv_cols(d_ff + fi * ft)
        act_scr[:, fi * ft:(fi + 1) * ft] = (_silu(gate) * up).astype(BF16)
    acc = x1 + _dot(act_scr[...], wdn_ref[...])

    if final:
        acc = acc * lax.rsqrt(jnp.mean(acc * acc, axis=-1, keepdims=True) + EPS) * fg_ref[...]
    y_ref[...] = acc

    @pl.when(j == nj - 1)
    def _():
        conv_ref[0] = prev_scr[8 - (FFN_CONV - 1):8, :]


def _ffn(x, om, od, osg, og, buf, wout, g2, wup, cw, wdn, fg, *, batch, seq, tm, ft, final):
    nj = seq // tm
    t, d = x.shape
    d_ff = wdn.shape[0]
    row = lambda wd: pl.BlockSpec((tm, wd), lambda b, j: (b * nj + j, 0))
    return pl.pallas_call(
        functools.partial(_ffn_kernel, tm=tm, d_ff=d_ff, ft=ft, final=final),
        grid=(batch, nj),
        in_specs=[
            row(d), row(256), row(256), row(256), row(256),
            pl.BlockSpec((1, FFN_CONV - 1, 2 * d_ff), lambda b, j: (b, 0, 0)),
            _const_spec(wout.shape), _const_spec(g2.shape), _const_spec(wup.shape),
            _const_spec(cw.shape), _const_spec(wdn.shape), _const_spec(fg.shape),
        ],
        out_specs=[row(d), pl.BlockSpec((1, FFN_CONV - 1, 2 * d_ff), lambda b, j: (b, 0, 0))],
        out_shape=[jax.ShapeDtypeStruct((t, d), F32),
                   jax.ShapeDtypeStruct((batch, FFN_CONV - 1, 2 * d_ff), F32)],
        scratch_shapes=[pltpu.VMEM((8, 2 * d_ff), F32), pltpu.VMEM((tm, d_ff), BF16)],
        compiler_params=_cparams(("parallel", "arbitrary")), name="ffn",
    )(x, om, od, osg, og, buf, wout, g2, wup, cw, wdn, fg)


def _rot_cols(w):
    d = w.shape[0]
    wg = w.reshape(d, -1, 2, MLA_DR // 2)
    return jnp.concatenate([-wg[:, :, 1:2], wg[:, :, 0:1]], axis=2).reshape(d, -1)


def _permute_w_in(w_in):
    d = w_in.shape[0]
    pts = np.cumsum([384, 128, 32, 256, 256, 256, 256, 256, 768, 256, 4, 4])[:-1]
    mq, mlat, mkr, dq, dk, dv, su, sv, gqkv, ggate, ga, gb = jnp.split(w_in, pts, axis=1)
    mq = mq.reshape(d, N_HEADS, MLA_DN + MLA_DR)
    qn = mq[:, :, :MLA_DN].reshape(d, N_HEADS * MLA_DN)
    qr = mq[:, :, MLA_DN:].reshape(d, N_HEADS * MLA_DR)
    kr4 = jnp.tile(mkr, (1, N_HEADS))
    rep = lambda a: jnp.repeat(a, HEAD_W, axis=1)
    w = jnp.concatenate([qn, qr, _rot_cols(qr), mlat, kr4, _rot_cols(kr4),
                         dq, dk, dv, su, sv, gqkv, ggate, rep(ga), rep(gb)], axis=1)
    assert w.shape[1] == C_TOTAL
    wt = jnp.transpose(jnp.concatenate([qn, qr, _rot_cols(qr), mlat, dq, dv, dk, mkr, _rot_cols(mkr)], axis=1))
    assert wt.shape[0] == WT_ROWS
    return w.astype(BF16), wt.astype(BF16)


def _rope_tables(pos):
    inv = ROPE_THETA ** (-jnp.arange(0, MLA_DR, 2, dtype=F32) / MLA_DR)
    ang = pos.astype(F32)[:, None] * inv[None, :]
    cos = jnp.tile(jnp.cos(ang), (1, 2 * N_HEADS))
    sin = jnp.tile(jnp.sin(ang), (1, 2 * N_HEADS))
    return cos, sin


def _layer_params(l, p):
    row = lambda a: a.reshape(1, -1)
    rep = lambda a: jnp.repeat(a, HEAD_W).reshape(1, -1)
    sgu_w = p['sgu_w'][l]
    w_perm, w_t = _permute_w_in(p['w_in'][l])
    return dict(
        w_in=w_perm, w_t=w_t,
        g1=row(p['norm1_g'][l]),
        wukt=jnp.transpose(p['mla_w_uk'][l], (1, 2, 0)).astype(BF16),
        wuk=jnp.transpose(p['mla_w_uk'][l], (1, 0, 2)).astype(BF16),
        wuv=jnp.transpose(p['mla_w_uv'][l], (1, 0, 2)).astype(BF16),
        wuvt=jnp.transpose(p['mla_w_uv'][l], (1, 2, 0)).astype(BF16),
        latg=row(p['mla_lat_g'][l]), latgt=p['mla_lat_g'][l].reshape(-1, 1),
        sgu_w=sgu_w,
        sgu_b=p['sgu_b'][l],
        lng=row(p['sgu_ln_g'][l]), lnb=row(p['sgu_ln_b'][l]),
        lamv=jnp.stack([p['diff_lam_q1'][l], p['diff_lam_k1'][l], p['diff_lam_q2'][l], p['diff_lam_k2'][l]]),
        subg=row(p['diff_sub_g'][l]),
        gcw=p['gdn_conv_w'][l],
        alog=rep(p['gdn_a_log'][l]), dtb=rep(p['gdn_dt_bias'][l]),
        ng=jnp.tile(p['gdn_norm_g'][l], N_HEADS).reshape(1, -1),
        wout=p['w_out'][l].astype(BF16),
        g2=row(p['norm2_g'][l]),
        wup=p['ffn_w_up'][l].astype(BF16),
        fcw=p['ffn_conv_w'][l],
        wdn=p['ffn_w_down'][l].astype(BF16),
        fg=row(p['final_g']),
    )


def _sgu_tables(lp, c):
    w = lp['sgu_w'][:, :c, :c].reshape(N_HEADS * c, c)
    b = jnp.repeat(jnp.transpose(lp['sgu_b'][:, :c]), HEAD_W, axis=1)
    return w, b


def _run_inproj(x, lp, cos, sin, stacked=(), *, tm, sgu_c, prompt, layer=0):
    sw, sb = _sgu_tables(lp, sgu_c)
    return _inproj(x, lp['g1'], lp['w_in'], lp['w_t'], cos, sin, jnp.transpose(cos), jnp.transpose(sin),
                   lp['wukt'], lp['wuk'], lp['latg'], lp['latgt'], sw, sb, lp['lng'], lp['lnb'], tuple(stacked),
                   tm=tm, sgu_c=sgu_c, prompt=prompt, layer=layer)


def kernel(x_prompt, x_sample, cache_mla_latent, cache_mla_krope, cache_diff_k, cache_diff_v, state_gdn_conv, state_gdn_s, state_ffn_conv, t5_table, final_g, norm1_g, w_in, mla_lat_g, mla_w_uk, mla_w_uv, diff_lam_q1, diff_lam_k1, diff_lam_q2, diff_lam_k2, diff_sub_g, sgu_ln_g, sgu_ln_b, sgu_w, sgu_b, gdn_conv_w, gdn_a_log, gdn_dt_bias, gdn_norm_g, w_out, norm2_g, ffn_w_up, ffn_conv_w, ffn_w_down):
    p = dict(final_g=final_g, norm1_g=norm1_g, w_in=w_in, mla_lat_g=mla_lat_g, mla_w_uk=mla_w_uk,
             mla_w_uv=mla_w_uv, diff_lam_q1=diff_lam_q1, diff_lam_k1=diff_lam_k1, diff_lam_q2=diff_lam_q2,
             diff_lam_k2=diff_lam_k2, diff_sub_g=diff_sub_g, sgu_ln_g=sgu_ln_g, sgu_ln_b=sgu_ln_b,
             sgu_w=sgu_w, sgu_b=sgu_b, gdn_conv_w=gdn_conv_w, gdn_a_log=gdn_a_log, gdn_dt_bias=gdn_dt_bias,
             gdn_norm_g=gdn_norm_g, w_out=w_out, norm2_g=norm2_g, ffn_w_up=ffn_w_up, ffn_conv_w=ffn_conv_w,
             ffn_w_down=ffn_w_down)
    depth = w_in.shape[0]
    bp, sp, d = x_prompt.shape
    bs, ls, _ = x_sample.shape
    past = cache_mla_latent.shape[2]
    d_ff = ffn_w_down.shape[1]
    assert past % CHUNK == 0 and ls <= CHUNK

    tm_p = min(512, sp)
    tq = min(512, sp)
    tg_p = min(512, sp)
    tm_f = min(1024, sp)
    ft = 256
    sgu_cp = min(SGU_CHUNK, sp)
    gdn_cp = min(GDN_CHUNK, sp)

    cos_p, sin_p = _rope_tables(jnp.arange(sp, dtype=jnp.int32))
    pos_s = past + jnp.arange(ls, dtype=jnp.int32)
    cos_s, sin_s = _rope_tables(jnp.tile(pos_s, bs))

    assert tq >= DIFF_CORNER and tm_p == tq
    stack_t = lambda b: jnp.transpose(jnp.repeat(b, 2, axis=0), (2, 0, 1)).reshape(b.shape[2], -1)
    bias_p = stack_t(_t5_bias(t5_table, tq, tq, (0,), (0,))[0])
    corner = _t5_bias(t5_table, DIFF_CORNER, DIFF_CORNER, (DIFF_CORNER,), (0,))[0]
    bias_pc = stack_t(jnp.pad(corner, ((0, 0), (0, tq - DIFF_CORNER), (0, 0))))
    bias_sc = _t5_bias(t5_table, ls, past, (past,), (0,))
    bias_sn = _t5_bias(t5_table, ls, ls, (past,), (past,))

    xp = x_prompt.reshape(bp * sp, d)
    xs = x_sample.reshape(bs * ls, d)
    zeros_gconv = jnp.zeros((bp, GDN_CONV - 1, 3 * GROUP_W), F32)
    zeros_gs = jnp.zeros((bp, N_HEADS, HEAD_W, HEAD_W), F32)
    zeros_fconv = jnp.zeros((bp, FFN_CONV - 1, 2 * d_ff), F32)

    stacked = (jnp.zeros((depth, bp * sp, MLA_R), F32),) + tuple(
        jnp.zeros((depth, bp, rows, sp), F32) for rows in (MLA_DR, GROUP_W, GROUP_W))

    outs_p, outs_s = [], []
    for l in range(depth):
        lp = _layer_params(l, p)
        lam_init = 0.8 - 0.6 * math.exp(-0.3 * l)
        final = l == depth - 1

        ip = _run_inproj(xp, lp, cos_p, sin_p, stacked, tm=tm_p, sgu_c=sgu_cp, prompt=True, layer=l)
        stacked = tuple(ip[k] for k in _INPROJ_STACKED)
        osgu, gqkv, ggate, gab = ip['osgu'], ip['gqkv'], ip['ggate'], ip['gab']
        o_mla = _mla_prompt(ip['qmt'], ip['km'], ip['vmt'], lp['wuvt'], batch=bp, seq=sp, tq=tq)
        o_diff = _diff_prompt(ip['dqt'], ip['kd'], ip['vdt'], bias_p, bias_pc, lp['lamv'],
                              jnp.tile(lp['subg'], (1, N_HEADS)).reshape(-1, 1),
                              batch=bp, seq=sp, tq=tq, lam_init=lam_init)
        o_gdn, gconv, gs = _gdn(gqkv, ggate, gab, zeros_gconv, zeros_gs, lp['gcw'], lp['alog'], lp['dtb'],
                                lp['ng'], batch=bp, seq=sp, tg=tg_p, c=gdn_cp)
        xp, fconv = _ffn(xp, o_mla, o_diff, osgu, o_gdn, zeros_fconv, lp['wout'], lp['g2'], lp['wup'],
                         lp['fcw'], lp['wdn'], lp['fg'], batch=bp, seq=sp, tm=tm_f, ft=ft, final=final)
        outs_p.append(dict(gconv=gconv, gs=gs, fconv=fconv))

        ip = _run_inproj(xs, lp, cos_s, sin_s, tm=bs * ls, sgu_c=min(SGU_CHUNK, ls), prompt=False)
        qm, km, vm, lat, kr, dq, dk, dv = (ip[k] for k in ('qm', 'km', 'vm', 'lat', 'kr', 'dq', 'dk', 'dv'))
        osgu, vn, gqkv, ggate, gab = (ip[k] for k in ('osgu', 'vn', 'gqkv', 'ggate', 'gab'))
        kc_m = jnp.concatenate([cache_mla_latent[l], jnp.tile(cache_mla_krope[l], (1, 1, N_HEADS))],
                               axis=-1).astype(BF16)
        o_mla = _mla_sample(qm, kc_m, km, vm, lp['wuv'], batch=bs, lq=ls)
        kc_d = cache_diff_k[l].reshape(bs, past, GROUP_W).astype(BF16)
        vc_d = cache_diff_v[l].reshape(bs, past, GROUP_W).astype(BF16)
        o_diff = _diff_sample(dq, kc_d, vc_d, dk, dv, bias_sc, bias_sn, lp['lamv'], lp['subg'],
                              batch=bs, lq=ls, lam_init=lam_init)
        o_gdn, gconv, gs = _gdn(gqkv, ggate, gab, state_gdn_conv[l], state_gdn_s[l], lp['gcw'], lp['alog'],
                                lp['dtb'], lp['ng'], batch=bs, seq=ls, tg=ls, c=min(GDN_CHUNK, ls))
        xs, fconv = _ffn(xs, o_mla, o_diff, osgu, o_gdn, state_ffn_conv[l], lp['wout'], lp['g2'], lp['wup'],
                         lp['fcw'], lp['wdn'], lp['fg'], batch=bs, seq=ls, tm=ls, ft=ft, final=final)
        outs_s.append(dict(
            lat=lat.reshape(bs, ls, MLA_R), kr=kr.reshape(bs, ls, MLA_DR),
            dk=dk.reshape(bs, ls, N_HEADS, 2, DIFF_DH), dv=dv.reshape(bs, ls, N_HEADS, 2 * DIFF_DH),
            gconv=gconv, gs=gs, fconv=fconv, sv=vn.reshape(bs, ls, GROUP_W)))

    st = lambda lst, key: jnp.stack([o[key] for o in lst])
    lat_p, krt_p, dkt_p, dvt_p = stacked
    kr_p = jnp.transpose(krt_p, (0, 1, 3, 2))
    dk_p = jnp.transpose(dkt_p.reshape(depth, bp, N_HEADS, 2, DIFF_DH, sp), (0, 1, 5, 2, 3, 4))
    dv_p = jnp.transpose(dvt_p.reshape(depth, bp, N_HEADS, 2 * DIFF_DH, sp), (0, 1, 4, 2, 3))
    return (xp.reshape(bp, sp, d), xs.reshape(bs, ls, d),
            lat_p.reshape(depth, bp, sp, MLA_R), kr_p, dk_p, dv_p,
            st(outs_p, 'gconv'), st(outs_p, 'gs'), st(outs_p, 'fconv'),
            st(outs_s, 'lat'), st(outs_s, 'kr'), st(outs_s, 'dk'), st(outs_s, 'dv'),
            st(outs_s, 'gconv'), st(outs_s, 'gs'), st(outs_s, 'fconv'), st(outs_s, 'sv'))
```
